```python
import math
import jax, jax.numpy as jnp
from jax import lax
import numpy as np

D_MODEL = 1024
BATCH = 4
SEQ = 8192
DEPTH = 2
DEC_BATCH = 32
DEC_SEQ = 4
PAST_LEN = 16384
PAGE_SIZE = 128

N_MIXERS = 2
DIL_GROUPS = ((128, 1), (512, 4), (2048, 16))
N_GROUPS = len(DIL_GROUPS)
H_SLOT = 8
HEAD_DIM = 64
A_HEADS = N_GROUPS * H_SLOT
A_WIDTH = H_SLOT * HEAD_DIM
Q_BLOCK = 128
N_BUCKETS = 32
MAX_DISTANCE = 2048
RET_HEADS = 4
RET_DK = 256
RET_DV = 512
RET_CHUNK = 128
ROPE_BASE = 10000.0
RET_QK = RET_HEADS * RET_DK
RET_V = RET_HEADS * RET_DV
D_FF = 2816
N_EXPERTS = 8
TOP_K = 2
D_FF_EXPERT = 3584
LN_EPS = 1e-5
GN_EPS = 1e-5
ALPHA = (2 * DEPTH) ** 0.25
BETA = (8 * DEPTH) ** -0.25

kernel_name = 'hybrid_dilated_retention_step'


def t5_bucket(dist):
    max_exact = N_BUCKETS // 2
    d = np.asarray(dist, dtype=np.int64)
    scaled = np.log(np.maximum(d, 1) / max_exact) / np.log(MAX_DISTANCE / max_exact)
    large = np.minimum(max_exact + (scaled * (N_BUCKETS - max_exact)).astype(np.int32), N_BUCKETS - 1)
    return np.where(d < max_exact, d, large).astype(np.int32)


def group_bias(rel_bias, g, window, dil):
    n_keys = window // dil + 1
    buckets = jnp.asarray(t5_bucket(np.arange(n_keys) * dil))
    b = rel_bias[buckets][:, g * H_SLOT:(g + 1) * H_SLOT]
    return b.T.astype(jnp.float32)


def gathered_attention(q, k_ctx, v_ctx, idx, valid, bias):
    kg = k_ctx[:, idx]
    vg = v_ctx[:, idx]
    logits = jnp.einsum('bqhd,bqjhd->bhqj', q, kg).astype(jnp.float32) * (HEAD_DIM ** -0.5)
    logits = jnp.where(valid[None, None], logits + bias[:, None, :], -jnp.inf)
    lse = jax.nn.logsumexp(logits, axis=-1)
    p = jnp.exp(logits - lse[..., None]).astype(v_ctx.dtype)
    out = jnp.einsum('bhqj,bqjhd->bqhd', p, vg)
    return out, jnp.transpose(lse, (0, 2, 1))


def dilated_group_prompt(q, k, v, window, dil, bias):
    bsz, seq = q.shape[:2]
    n_keys = window // dil + 1
    offs = np.arange(Q_BLOCK)[:, None] - np.arange(n_keys)[None, :] * dil
    idx = jnp.asarray(window + offs, dtype=jnp.int32)
    offs_j = jnp.asarray(offs, dtype=jnp.int32)
    pad = ((0, 0), (window, 0), (0, 0), (0, 0))
    k_pad = jnp.pad(k, pad)
    v_pad = jnp.pad(v, pad)

    def block(start):
        qb = lax.dynamic_slice_in_dim(q, start, Q_BLOCK, axis=1)
        kb = lax.dynamic_slice_in_dim(k_pad, start, Q_BLOCK + window, axis=1)
        vb = lax.dynamic_slice_in_dim(v_pad, start, Q_BLOCK + window, axis=1)
        valid = (start + offs_j) >= 0
        return gathered_attention(qb, kb, vb, idx, valid, bias)

    starts = jnp.arange(seq // Q_BLOCK, dtype=jnp.int32) * Q_BLOCK
    out, lse = lax.map(block, starts)
    out = jnp.moveaxis(out, 0, 1).reshape(bsz, seq, H_SLOT, HEAD_DIM)
    lse = jnp.moveaxis(lse, 0, 1).reshape(bsz, seq, H_SLOT)
    return out, lse


def dilated_group_sample(q, k_ctx, v_ctx, window, dil, bias):
    t_new = q.shape[1]
    past = k_ctx.shape[1] - t_new
    n_keys = window // dil + 1
    rel = past + np.arange(t_new)[:, None] - np.arange(n_keys)[None, :] * dil
    idx = jnp.asarray(np.maximum(rel, 0), dtype=jnp.int32)
    valid = jnp.asarray(rel >= 0)
    return gathered_attention(q, k_ctx, v_ctx, idx, valid, bias)


def dilated_mixer(x, w_in, w_out, rel_bias, caches):
    bsz, t = x.shape[:2]
    proj = (x @ w_in).reshape(bsz, t, N_GROUPS, 3, H_SLOT, HEAD_DIM)
    outs, lses, rows = [], [], []
    for g, (window, dil) in enumerate(DIL_GROUPS):
        q, k, v = proj[:, :, g, 0], proj[:, :, g, 1], proj[:, :, g, 2]
        bias = group_bias(rel_bias, g, window, dil)
        if caches is None:
            o, l = dilated_group_prompt(q, k, v, window, dil, bias)
            keep = min(window, t)
            rows.append(jnp.stack([k[:, t - keep:], v[:, t - keep:]], axis=2))
        else:
            cache = caches[g]
            k_ctx = jnp.concatenate([cache[:, :, 0], k], axis=1)
            v_ctx = jnp.concatenate([cache[:, :, 1], v], axis=1)
            o, l = dilated_group_sample(q, k_ctx, v_ctx, window, dil, bias)
            rows.append(jnp.stack([k, v], axis=2))
        outs.append(o)
        lses.append(l)
    outs = jnp.stack(outs, 0)
    w = jax.nn.softmax(jnp.stack(lses, 0), axis=0).astype(outs.dtype)
    mixed = jnp.einsum('gbth,gbthd->bthd', w, outs)
    return mixed.reshape(bsz, t, A_WIDTH) @ w_out, rows


def rotate(x, pos):
    half = x.shape[-1] // 2
    inv = 1.0 / (ROPE_BASE ** (jnp.arange(half, dtype=jnp.float32) / half))
    ang = pos.astype(jnp.float32)[:, None] * inv[None]
    cos = jnp.cos(ang)[None, :, None, :].astype(x.dtype)
    sin = jnp.sin(ang)[None, :, None, :].astype(x.dtype)
    x1, x2 = x[..., :half], x[..., half:]
    return jnp.concatenate([x1 * cos - x2 * sin, x1 * sin + x2 * cos], axis=-1)


def retention_chunk(q, k, v, state, log_gamma):
    dt = q.dtype
    c = q.shape[1]
    n = jnp.arange(c, dtype=jnp.float32)
    diff = n[:, None] - n[None, :]
    decay = jnp.where(diff[None] >= 0, jnp.exp(jnp.maximum(diff, 0.0)[None] * log_gamma[:, None, None]), 0.0)
    scores = jnp.einsum('bnhk,bmhk->bhnm', q, k) * decay[None].astype(dt)
    inner = jnp.einsum('bhnm,bmhv->bnhv', scores, v)
    q_decay = jnp.exp((n[:, None] + 1.0) * log_gamma[None]).astype(dt)
    cross = jnp.einsum('bnhk,bhkv->bnhv', q, state) * q_decay[None, :, :, None]
    k_decay = jnp.exp((c - 1.0 - n)[:, None] * log_gamma[None]).astype(dt)
    chunk_decay = jnp.exp(c * log_gamma).astype(dt)[None, :, None, None]
    new_state = chunk_decay * state + jnp.einsum('bmhk,bmhv->bhkv', k * k_decay[None, :, :, None], v)
    return inner + cross, new_state


def retention_mixer(x, w_in, gn_g, w_out, pos, state):
    bsz, t = x.shape[:2]
    log_gamma = jnp.log(1.0 - 2.0 ** (-5.0 - jnp.arange(RET_HEADS, dtype=jnp.float32)))
    proj = x @ w_in
    q = proj[..., :RET_QK].reshape(bsz, t, RET_HEADS, RET_DK)
    k = proj[..., RET_QK:2 * RET_QK].reshape(bsz, t, RET_HEADS, RET_DK)
    v = proj[..., 2 * RET_QK:2 * RET_QK + RET_V].reshape(bsz, t, RET_HEADS, RET_DV)
    gate = proj[..., 2 * RET_QK + RET_V:]
    q = rotate(q, pos)
    k = rotate(k, pos) * (RET_DK ** -0.5)
    if state is None:
        n_chunks = t // RET_CHUNK
        split = lambda a: jnp.moveaxis(a.reshape(bsz, n_chunks, RET_CHUNK, *a.shape[2:]), 1, 0)

        def step(s, inp):
            o, s_new = retention_chunk(inp[0], inp[1], inp[2], s, log_gamma)
            return s_new, o

        init = jnp.zeros((bsz, RET_HEADS, RET_DK, RET_DV), x.dtype)
        new_state, o = lax.scan(step, init, (split(q), split(k), split(v)))
        o = jnp.moveaxis(o, 0, 1).reshape(bsz, t, RET_HEADS, RET_DV)
    else:
        o, new_state = retention_chunk(q, k, v, state, log_gamma)
    of = o.astype(jnp.float32)
    mu = jnp.mean(of, axis=-1, keepdims=True)
    var = jnp.mean(jnp.square(of - mu), axis=-1, keepdims=True)
    on = ((of - mu) * lax.rsqrt(var + GN_EPS)).reshape(bsz, t, RET_V).astype(x.dtype) * gn_g
    return (jax.nn.silu(gate) * on) @ w_out, new_state


def layer_norm(x, g, b):
    xf = x.astype(jnp.float32)
    mu = jnp.mean(xf, axis=-1, keepdims=True)
    var = jnp.mean(jnp.square(xf - mu), axis=-1, keepdims=True)
    return ((xf - mu) * lax.rsqrt(var + LN_EPS)).astype(x.dtype) * g + b


def swiglu(x, w_gu, w_down):
    a, b = jnp.split(x @ w_gu, 2, axis=-1)
    return (jax.nn.silu(a) * b) @ w_down


def moe_swiglu(x, w_router, w_gu, w_down):
    logits = (x @ w_router).astype(jnp.float32)
    top_v, top_i = lax.top_k(logits, TOP_K)
    gates = jax.nn.softmax(top_v, axis=-1)
    combine = jnp.sum(jax.nn.one_hot(top_i, N_EXPERTS, dtype=jnp.float32) * gates[..., None], axis=-2)
    combine = combine.astype(x.dtype)
    y = jnp.zeros_like(x)
    for e in range(N_EXPERTS):
        y = y + combine[..., e:e + 1] * swiglu(x, w_gu[e], w_down[e])
    return y


def setup_inputs(seed: int = 0) -> dict:
    key = jax.random.key(seed)
    ks = jax.random.split(key, 20)
    nrm = lambda k, shape, scale: jax.random.normal(k, shape, jnp.float32) * scale
    return {
        'x_prompt': nrm(ks[0], (BATCH, SEQ, D_MODEL), 1.0),
        'x_sample': nrm(ks[1], (DEC_BATCH, DEC_SEQ, D_MODEL), 1.0),
        'cache_kv_w128': nrm(ks[2], (DEC_BATCH, min(DIL_GROUPS[0][0], PAST_LEN), 2, H_SLOT, HEAD_DIM), 1.0),
        'cache_kv_w512': nrm(ks[3], (DEC_BATCH, min(DIL_GROUPS[1][0], PAST_LEN), 2, H_SLOT, HEAD_DIM), 1.0),
        'cache_kv_w2048': nrm(ks[4], (DEC_BATCH, min(DIL_GROUPS[2][0], PAST_LEN), 2, H_SLOT, HEAD_DIM), 1.0),
        'state_ret': nrm(ks[5], (DEC_BATCH, RET_HEADS, RET_DK, RET_DV), 0.5),
        'ln_g': 1.0 + nrm(ks[6], (DEPTH, 2, D_MODEL), 0.02),
        'ln_b': nrm(ks[7], (DEPTH, 2, D_MODEL), 0.02),
        'rel_bias': nrm(ks[8], (N_BUCKETS, A_HEADS), 0.5),
        'w_in_dil': nrm(ks[9], (D_MODEL, N_GROUPS * 3 * A_WIDTH), D_MODEL ** -0.5),
        'w_out_dil': nrm(ks[10], (A_WIDTH, D_MODEL), BETA * A_WIDTH ** -0.5),
        'w_in_ret': nrm(ks[11], (D_MODEL, 2 * RET_QK + 2 * RET_V), D_MODEL ** -0.5),
        'ret_gn_g': 1.0 + nrm(ks[12], (RET_V,), 0.02),
        'w_out_ret': nrm(ks[13], (RET_V, D_MODEL), BETA * RET_V ** -0.5),
        'w_gu_dense': nrm(ks[14], (D_MODEL, 2 * D_FF), D_MODEL ** -0.5),
        'w_down_dense': nrm(ks[15], (D_FF, D_MODEL), BETA * D_FF ** -0.5),
        'w_router': nrm(ks[16], (D_MODEL, N_EXPERTS), D_MODEL ** -0.5),
        'w_gu_moe': nrm(ks[17], (N_EXPERTS, D_MODEL, 2 * D_FF_EXPERT), D_MODEL ** -0.5),
        'w_down_moe': nrm(ks[18], (N_EXPERTS, D_FF_EXPERT, D_MODEL), BETA * D_FF_EXPERT ** -0.5),
    }


def reference(x_prompt, x_sample, cache_kv_w128, cache_kv_w512, cache_kv_w2048, state_ret,
              ln_g, ln_b, rel_bias, w_in_dil, w_out_dil, w_in_ret, ret_gn_g, w_out_ret,
              w_gu_dense, w_down_dense, w_router, w_gu_moe, w_down_moe):
    pos_p = jnp.arange(x_prompt.shape[1], dtype=jnp.int32)
    pos_s = PAST_LEN + jnp.arange(x_sample.shape[1], dtype=jnp.int32)
    hp, hs = x_prompt, x_sample
    for layer in range(DEPTH):
        if layer % N_MIXERS == 0:
            mp, rows_p = dilated_mixer(hp, w_in_dil, w_out_dil, rel_bias, None)
            ms, rows_s = dilated_mixer(hs, w_in_dil, w_out_dil, rel_bias,
                                       (cache_kv_w128, cache_kv_w512, cache_kv_w2048))
        else:
            mp, ret_p = retention_mixer(hp, w_in_ret, ret_gn_g, w_out_ret, pos_p, None)
            ms, ret_s = retention_mixer(hs, w_in_ret, ret_gn_g, w_out_ret, pos_s, state_ret)
        hp = layer_norm(ALPHA * hp + mp, ln_g[layer, 0], ln_b[layer, 0])
        hs = layer_norm(ALPHA * hs + ms, ln_g[layer, 0], ln_b[layer, 0])
        if layer % 2 == 0:
            fp = swiglu(hp, w_gu_dense, w_down_dense)
            fs = swiglu(hs, w_gu_dense, w_down_dense)
        else:
            fp = moe_swiglu(hp, w_router, w_gu_moe, w_down_moe)
            fs = moe_swiglu(hs, w_router, w_gu_moe, w_down_moe)
        hp = layer_norm(ALPHA * hp + fp, ln_g[layer, 1], ln_b[layer, 1])
        hs = layer_norm(ALPHA * hs + fs, ln_g[layer, 1], ln_b[layer, 1])
    kv128_p, kv512_p, kv2048_p = rows_p
    kv128_s, kv512_s, kv2048_s = rows_s
    return (hp, hs, kv128_p, kv512_p, kv2048_p, ret_p, kv128_s, kv512_s, kv2048_s, ret_s)
```

```python
import functools

import jax
import jax.numpy as jnp
import numpy as np
from jax import lax
from jax.experimental import pallas as pl
from jax.experimental.pallas import tpu as pltpu

F32 = jnp.float32
BF16 = jnp.bfloat16

DEPTH = 2
D_MODEL = 1024
PAST_LEN = 16384
DIL_GROUPS = ((128, 1), (512, 4), (2048, 16))
N_GROUPS = 3
H_SLOT = 8
HEAD_DIM = 64
A_WIDTH = H_SLOT * HEAD_DIM
G_COLS = 3 * A_WIDTH
N_BUCKETS = 32
MAX_DISTANCE = 2048
RET_HEADS = 4
RET_DK = 256
RET_DV = 512
RET_CHUNK = 128
ROPE_BASE = 10000.0
RET_QK = RET_HEADS * RET_DK
RET_V = RET_HEADS * RET_DV
N_EXPERTS = 8
LN_EPS = 1e-5
GN_EPS = 1e-5
ALPHA = (2 * DEPTH) ** 0.25
NEG = -1e30

LANES = 128
ATT_BLOCK = 128
SAMPLE_PAD = 16
VMEM_LIMIT = 56 * 1024 * 1024

_NT = (((1,), (1,)), ((), ()))
_TN = (((0,), (0,)), ((), ()))


def _params(n_grid):
    return pltpu.CompilerParams(dimension_semantics=("arbitrary",) * n_grid,
                                vmem_limit_bytes=VMEM_LIMIT)


def _layer_norm(z, g, b):
    mu = jnp.mean(z, axis=-1, keepdims=True)
    zc = z - mu
    var = jnp.mean(zc * zc, axis=-1, keepdims=True)
    return zc * lax.rsqrt(var + LN_EPS) * g + b


def _silu(a):
    return a / (1.0 + jnp.exp(-a))


def _t5_bucket(dist):
    max_exact = N_BUCKETS // 2
    d = np.asarray(dist, dtype=np.int64)
    scaled = np.log(np.maximum(d, 1) / max_exact) / np.log(MAX_DISTANCE / max_exact)
    large = np.minimum(max_exact + (scaled * (N_BUCKETS - max_exact)).astype(np.int32), N_BUCKETS - 1)
    return np.where(d < max_exact, d, large).astype(np.int32)


def _group_bias(rel_bias, g, window, dil):
    n_keys = window // dil + 1
    buckets = jnp.asarray(_t5_bucket(np.arange(n_keys) * dil))
    return rel_bias[buckets][:, g * H_SLOT:(g + 1) * H_SLOT].T.astype(F32)


def _matmul_kernel(x_ref, w_ref, o_ref):
    o_ref[...] = jnp.dot(x_ref[...].astype(BF16), w_ref[...], preferred_element_type=F32)


def _matmul(x, w, tn):
    m, k = x.shape
    n = w.shape[1]
    return pl.pallas_call(
        _matmul_kernel,
        grid=(n // tn,),
        in_specs=[pl.BlockSpec((m, k), lambda j: (0, 0)),
                  pl.BlockSpec((k, tn), lambda j: (0, j))],
        out_specs=pl.BlockSpec((m, tn), lambda j: (0, j)),
        out_shape=jax.ShapeDtypeStruct((m, n), F32),
        compiler_params=_params(1),
        name="matmul_sample",
    )(x, w)


def _proj_dil_kernel(x_ref, w_ref, qkv_ref, kv0_ref, kv1_ref, kv2_ref, *, tm, tpb, keeps):
    g = pl.program_id(1)
    j = pl.program_id(0) % tpb
    acc = jnp.dot(x_ref[...].astype(BF16), w_ref[...], preferred_element_type=F32)
    qkv_ref[:, :A_WIDTH] = (acc[:, :A_WIDTH] * (HEAD_DIM ** -0.5)).astype(BF16)
    qkv_ref[:, A_WIDTH:] = acc[:, A_WIDTH:].astype(BF16)
    for gi, (ref, keep) in enumerate(zip((kv0_ref, kv1_ref, kv2_ref), keeps)):
        if keep >= tm:
            cond = jnp.logical_and(g == gi, j >= tpb - keep // tm)
            row0 = 0
        else:
            cond = jnp.logical_and(g == gi, j == tpb - 1)
            row0 = tm - keep

        @pl.when(cond)
        def _(ref=ref, row0=row0):
            ref[0] = acc[row0:, A_WIDTH:]


def _proj_dil_prompt(x, w, bsz, seq, tm):
    tpb = seq // tm
    keeps = tuple(min(wd, seq) for wd, _ in DIL_GROUPS)

    def kv_spec(keep):
        if keep >= tm:
            first = tpb - keep // tm
            return pl.BlockSpec((1, tm, 2 * A_WIDTH),
                                lambda i, g: (i // tpb, jnp.maximum(i % tpb - first, 0), 0))
        return pl.BlockSpec((1, keep, 2 * A_WIDTH), lambda i, g: (i // tpb, 0, 0))

    return pl.pallas_call(
        functools.partial(_proj_dil_kernel, tm=tm, tpb=tpb, keeps=keeps),
        grid=(bsz * tpb, N_GROUPS),
        in_specs=[pl.BlockSpec((tm, D_MODEL), lambda i, g: (i, 0)),
                  pl.BlockSpec((D_MODEL, G_COLS), lambda i, g: (0, g))],
        out_specs=[pl.BlockSpec((tm, G_COLS), lambda i, g: (i, g))] + [kv_spec(k) for k in keeps],
        out_shape=[jax.ShapeDtypeStruct((bsz * seq, N_GROUPS * G_COLS), BF16)]
        + [jax.ShapeDtypeStruct((bsz, k, 2 * A_WIDTH), F32) for k in keeps],
        compiler_params=_params(2),
        name="proj_dil_prompt",
    )(x, w)


def _attn_prompt_kernel(q_ref, k_ref, v_ref, kp_ref, vp_ref, tbl_ref, o_ref, lse_ref, kbuf, vbuf, *, tq):
    n = pl.program_id(1)
    kbuf[0:ATT_BLOCK, :] = kp_ref[0]
    kbuf[ATT_BLOCK:, :] = k_ref[0]
    vbuf[0:ATT_BLOCK, :] = vp_ref[0]
    vbuf[ATT_BLOCK:, :] = v_ref[0]
    low = lax.broadcasted_iota(jnp.int32, (ATT_BLOCK, LANES), 1) < HEAD_DIM

    def body(m, carry):
        r0 = pl.multiple_of(m * ATT_BLOCK, ATT_BLOCK)
        first = jnp.where(jnp.logical_and(n == 0, m == 0), 1, 0)
        for p in range(A_WIDTH // LANES):
            cols = slice(p * LANES, (p + 1) * LANES)
            qm = q_ref[0, pl.ds(r0, ATT_BLOCK), cols]
            keys = kbuf[pl.ds(r0, 2 * ATT_BLOCK), cols]
            vals = vbuf[pl.ds(r0, 2 * ATT_BLOCK), cols]
            outs, lses = [], []
            for a in range(2):
                qa = jnp.where(low if a == 0 else jnp.logical_not(low), qm, jnp.zeros_like(qm))
                s = lax.dot_general(qa, keys, _NT, preferred_element_type=F32) + tbl_ref[first, 2 * p + a]
                mx = jnp.max(s, axis=-1, keepdims=True)
                e = jnp.exp(s - mx)
                l = jnp.sum(e, axis=-1, keepdims=True)
                o = jnp.dot(e.astype(BF16), vals, preferred_element_type=F32)
                outs.append(o / l)
                lses.append(jnp.broadcast_to(mx + jnp.log(l), (ATT_BLOCK, LANES)))
            o_ref[0, pl.ds(r0, ATT_BLOCK), cols] = jnp.where(low, outs[0], outs[1]).astype(BF16)
            lse_ref[0, pl.ds(r0, ATT_BLOCK), cols] = jnp.where(low, lses[0], lses[1])
        return carry

    lax.fori_loop(0, tq // ATT_BLOCK, body, 0)


def _attn_prompt(qkv, tbl, tq):
    nb, length, _ = qkv.shape
    sub = tq // ATT_BLOCK
    cur = lambda c: pl.BlockSpec((1, tq, A_WIDTH), lambda s, n: (s, n, c))
    prev = lambda c: pl.BlockSpec((1, ATT_BLOCK, A_WIDTH), lambda s, n: (s, jnp.maximum(n * sub - 1, 0), c))
    return pl.pallas_call(
        functools.partial(_attn_prompt_kernel, tq=tq),
        grid=(nb, length // tq),
        in_specs=[cur(0), cur(1), cur(2), prev(1), prev(2),
                  pl.BlockSpec(tbl.shape, lambda s, n: (0, 0, 0, 0))],
        out_specs=[pl.BlockSpec((1, tq, A_WIDTH), lambda s, n: (s, n, 0))] * 2,
        out_shape=[jax.ShapeDtypeStruct((nb, length, A_WIDTH), BF16),
                   jax.ShapeDtypeStruct((nb, length, A_WIDTH), F32)],
        scratch_shapes=[pltpu.VMEM((tq + ATT_BLOCK, A_WIDTH), BF16)] * 2,
        compiler_params=_params(2),
        name="attn_prompt",
    )(qkv, qkv, qkv, qkv, qkv, tbl)


def _prompt_table(rel_bias, g, window, dil):
    bias = _group_bias(rel_bias, g, window, dil)
    i = np.arange(ATT_BLOCK)[:, None]
    c = np.arange(2 * ATT_BLOCK)[None, :]
    dist = i + ATT_BLOCK - c
    valid = (dist >= 0) & (dist <= window // dil)
    tbl = jnp.where(valid[None], bias[:, np.clip(dist, 0, window // dil)], NEG)
    tbl_first = jnp.where((c < ATT_BLOCK)[None], NEG, tbl)
    return jnp.stack([tbl, tbl_first], 0)


def _merge_kernel(o0, o1, o2, l0, l1, l2, out_ref):
    ls = [l0[...], l1[...], l2[...]]
    mx = jnp.maximum(jnp.maximum(ls[0], ls[1]), ls[2])
    es = [jnp.exp(l - mx) for l in ls]
    num = es[0] * o0[...].astype(F32) + es[1] * o1[...].astype(F32) + es[2] * o2[...].astype(F32)
    out_ref[...] = (num / (es[0] + es[1] + es[2])).astype(BF16)


def _merge(os_, ls_, tm):
    m = os_[0].shape[0]
    spec = pl.BlockSpec((tm, A_WIDTH), lambda i: (i, 0))
    return pl.pallas_call(
        _merge_kernel, grid=(m // tm,), in_specs=[spec] * 6, out_specs=spec,
        out_shape=jax.ShapeDtypeStruct((m, A_WIDTH), BF16),
        compiler_params=_params(1), name="merge_groups",
    )(*os_, *ls_)


def _attn_sample_kernel(qkv_ref, c0_ref, c1_ref, c2_ref, tc0, tc1, tc2, tn0, tn1, tn2, out_ref):
    tp = SAMPLE_PAD
    rows = H_SLOT * tp
    head_of_row = lax.broadcasted_iota(jnp.int32, (rows, A_WIDTH), 0) // tp
    head_of_lane = lax.broadcasted_iota(jnp.int32, (rows, A_WIDTH), 1) // HEAD_DIM
    diag = head_of_row == head_of_lane
    qkv = qkv_ref[0]
    scores, values = [], []
    for g, (c_ref, tc, tn) in enumerate(((c0_ref, tc0, tn0), (c1_ref, tc1, tn1), (c2_ref, tc2, tn2))):
        base = g * G_COLS
        q = qkv[:, base:base + A_WIDTH] * (HEAD_DIM ** -0.5)
        q2 = jnp.where(diag, jnp.concatenate([q] * H_SLOT, axis=0), 0.0).astype(BF16)
        kc = c_ref[0, :, :A_WIDTH].astype(BF16)
        vc = c_ref[0, :, A_WIDTH:].astype(BF16)
        zpad = jnp.zeros((LANES - tp, A_WIDTH), F32)
        kn = jnp.concatenate([qkv[:, base + A_WIDTH:base + 2 * A_WIDTH], zpad], axis=0).astype(BF16)
        vn = jnp.concatenate([qkv[:, base + 2 * A_WIDTH:base + 3 * A_WIDTH], zpad], axis=0).astype(BF16)
        scores.append(lax.dot_general(q2, kc, _NT, preferred_element_type=F32) + tc[...])
        values.append(vc)
        scores.append(lax.dot_general(q2, kn, _NT, preferred_element_type=F32) + tn[...])
        values.append(vn)
    mx = functools.reduce(jnp.maximum, [jnp.max(s, axis=-1, keepdims=True) for s in scores])
    den = jnp.zeros((rows, 1), F32)
    acc = jnp.zeros((rows, A_WIDTH), F32)
    for s, v in zip(scores, values):
        e = jnp.exp(s - mx)
        den = den + jnp.sum(e, axis=-1, keepdims=True)
        acc = acc + jnp.dot(e.astype(BF16), v, preferred_element_type=F32)
    r = jnp.where(diag, acc / den, 0.0)
    out = r[0:tp]
    for h in range(1, H_SLOT):
        out = out + r[h * tp:(h + 1) * tp]
    out_ref[0] = out


def _sample_tables(rel_bias, g, window, dil, length, t_new):
    bias = _group_bias(rel_bias, g, window, dil)
    n_keys = window // dil + 1
    t = np.arange(SAMPLE_PAD)[:, None]
    c = np.arange(length)[None, :]
    dist = length + t - c
    valid = (dist % dil == 0) & (dist // dil < n_keys) & (t < t_new)
    tc = jnp.where(valid[None], bias[:, np.clip(dist // dil, 0, n_keys - 1)], NEG)
    c2 = np.arange(LANES)[None, :]
    dist2 = t - c2
    valid2 = (dist2 >= 0) & (dist2 % dil == 0) & (dist2 // dil < n_keys) & (c2 < t_new) & (t < t_new)
    tn = jnp.where(valid2[None], bias[:, np.clip(dist2 // dil, 0, n_keys - 1)], NEG)
    return (tc.reshape(H_SLOT * SAMPLE_PAD, length), tn.reshape(H_SLOT * SAMPLE_PAD, LANES))


def _attn_sample(qkv, caches, tables_c, tables_n):
    bsz = qkv.shape[0]
    const = lambda a: pl.BlockSpec(a.shape, lambda b: (0, 0))
    return pl.pallas_call(
        _attn_sample_kernel,
        grid=(bsz,),
        in_specs=[pl.BlockSpec((1,) + qkv.shape[1:], lambda b: (b, 0, 0))]
        + [pl.BlockSpec((1,) + c.shape[1:], lambda b: (b, 0, 0)) for c in caches]
        + [const(t) for t in tables_c] + [const(t) for t in tables_n],
        out_specs=pl.BlockSpec((1, SAMPLE_PAD, A_WIDTH), lambda b: (b, 0, 0)),
        out_shape=jax.ShapeDtypeStruct((bsz, SAMPLE_PAD, A_WIDTH), F32),
        compiler_params=_params(1),
        name="attn_sample",
    )(qkv, *caches, *tables_c, *tables_n)


def _mm_ln_kernel(y_ref, w_ref, x_ref, g_ref, b_ref, o_ref):
    acc = jnp.dot(y_ref[...].astype(BF16), w_ref[...], preferred_element_type=F32)
    o_ref[...] = _layer_norm(ALPHA * x_ref[...] + acc, g_ref[...], b_ref[...])


def _mm_ln(y, w, x, g, b, tm):
    m, k = y.shape
    vec = pl.BlockSpec((1, D_MODEL), lambda i: (0, 0))
    return pl.pallas_call(
        _mm_ln_kernel, grid=(m // tm,),
        in_specs=[pl.BlockSpec((tm, k), lambda i: (i, 0)), pl.BlockSpec((k, D_MODEL), lambda i: (0, 0)),
                  pl.BlockSpec((tm, D_MODEL), lambda i: (i, 0)), vec, vec],
        out_specs=pl.BlockSpec((tm, D_MODEL), lambda i: (i, 0)),
        out_shape=jax.ShapeDtypeStruct((m, D_MODEL), F32),
        compiler_params=_params(1), name="outproj_ln",
    )(y, w, x, g, b)


def _ffn_kernel(x_ref, wg_ref, wu_ref, wd_ref, g_ref, b_ref, o_ref, acc_ref):
    f = pl.program_id(1)
    xb = x_ref[...].astype(BF16)
    a = jnp.dot(xb, wg_ref[...], preferred_element_type=F32)
    u = jnp.dot(xb, wu_ref[...], preferred_element_type=F32)
    part = jnp.dot((_silu(a) * u).astype(BF16), wd_ref[...], preferred_element_type=F32)

    @pl.when(f == 0)
    def _():
        acc_ref[...] = part

    @pl.when(f > 0)
    def _():
        acc_ref[...] += part

    @pl.when(f == pl.num_programs(1) - 1)
    def _():
        o_ref[...] = _layer_norm(ALPHA * x_ref[...] + acc_ref[...], g_ref[...], b_ref[...])


def _ffn_dense(x, w_gu, w_down, g, b, tm, tf):
    m = x.shape[0]
    d_ff = w_down.shape[0]
    nf = d_ff // tf
    vec = pl.BlockSpec((1, D_MODEL), lambda i, f: (0, 0))
    return pl.pallas_call(
        _ffn_kernel, grid=(m // tm, nf),
        in_specs=[pl.BlockSpec((tm, D_MODEL), lambda i, f: (i, 0)),
                  pl.BlockSpec((D_MODEL, tf), lambda i, f: (0, f)),
                  pl.BlockSpec((D_MODEL, tf), lambda i, f: (0, nf + f)),
                  pl.BlockSpec((tf, D_MODEL), lambda i, f: (f, 0)), vec, vec],
        out_specs=pl.BlockSpec((tm, D_MODEL), lambda i, f: (i, 0)),
        out_shape=jax.ShapeDtypeStruct((m, D_MODEL), F32),
        scratch_shapes=[pltpu.VMEM((tm, D_MODEL), F32)],
        compiler_params=_params(2), name="ffn_dense",
    )(x, w_gu, w_gu, w_down, g, b)


def _proj_ret_kernel(x_ref, w_ref, cos_ref, sin_ref, o_ref, *, scale_k):
    n = pl.program_id(1)
    acc = jnp.dot(x_ref[...].astype(BF16), w_ref[...], preferred_element_type=F32)

    @pl.when(n < 2)
    def _():
        cos = cos_ref[...]
        sin = sin_ref[...]
        scale = jnp.where(n == 1, scale_k, 1.0)
        half = RET_DK // 2
        for h in range(RET_HEADS):
            x1 = acc[:, h * RET_DK:h * RET_DK + half]
            x2 = acc[:, h * RET_DK + half:(h + 1) * RET_DK]
            o_ref[:, h * RET_DK:h * RET_DK + half] = ((x1 * cos - x2 * sin) * scale).astype(o_ref.dtype)
            o_ref[:, h * RET_DK + half:(h + 1) * RET_DK] = ((x1 * sin + x2 * cos) * scale).astype(o_ref.dtype)

    @pl.when(n >= 2)
    def _():
        o_ref[...] = acc.astype(o_ref.dtype)


def _proj_ret(x, w, cos, sin, tm, out_dtype):
    m = x.shape[0]
    n_cols = w.shape[1]
    tn = RET_QK
    pos_tiles = cos.shape[0] // tm
    return pl.pallas_call(
        functools.partial(_proj_ret_kernel, scale_k=RET_DK ** -0.5),
        grid=(m // tm, n_cols // tn),
        in_specs=[pl.BlockSpec((tm, D_MODEL), lambda i, n: (i, 0)),
                  pl.BlockSpec((D_MODEL, tn), lambda i, n: (0, n)),
                  pl.BlockSpec((tm, RET_DK // 2), lambda i, n: (i % pos_tiles, 0)),
                  pl.BlockSpec((tm, RET_DK // 2), lambda i, n: (i % pos_tiles, 0))],
        out_specs=pl.BlockSpec((tm, tn), lambda i, n: (i, n)),
        out_shape=jax.ShapeDtypeStruct((m, n_cols), out_dtype),
        compiler_params=_params(2), name="proj_ret",
    )(x, w, cos, sin)


def _rope_tables(pos):
    half = RET_DK // 2
    inv = 1.0 / (ROPE_BASE ** (jnp.arange(half, dtype=F32) / half))
    ang = pos.astype(F32)[:, None] * inv[None]
    return jnp.cos(ang), jnp.sin(ang)


def _log_gamma():
    return jnp.log(1.0 - 2.0 ** (-5.0 - jnp.arange(RET_HEADS, dtype=F32)))


def _decay_tables(c, rows):
    lg = _log_gamma()
    n = jnp.arange(rows, dtype=F32)
    live = n < c
    diff = n[:, None] - n[None, :]
    decay = jnp.where((diff >= 0)[None] & live[None, None, :],
                      jnp.exp(jnp.maximum(diff, 0.0)[None] * lg[:, None, None]), 0.0)
    q_decay = jnp.exp((n[None, :] + 1.0) * lg[:, None])
    k_decay = jnp.where(live[None], jnp.exp((c - 1.0 - n)[None, :] * lg[:, None]), 0.0)
    chunk_decay = jnp.exp(c * lg)
    return decay, q_decay, k_decay, chunk_decay


def _group_norm_gate(o, gate, gn):
    mu = jnp.mean(o, axis=-1, keepdims=True)
    oc = o - mu
    var = jnp.mean(oc * oc, axis=-1, keepdims=True)
    return _silu(gate) * (oc * lax.rsqrt(var + GN_EPS) * gn)


def _ret_prompt_kernel(q_ref, k_ref, v_ref, gate_ref, dec_ref, qd_ref, kd_ref, cd_ref, gn_ref,
                       y_ref, st_ref, s_ref, *, tb):
    cb = pl.program_id(2)

    @pl.when(cb == 0)
    def _():
        s_ref[...] = jnp.zeros_like(s_ref)

    def body(ci, carry):
        r0 = pl.multiple_of(ci * RET_CHUNK, RET_CHUNK)
        rows = pl.ds(r0, RET_CHUNK)
        q = q_ref[0, rows, :]
        k = k_ref[0, rows, :]
        v = v_ref[0, rows, :]
        state = s_ref[...]
        scores = lax.dot_general(q, k, _NT, preferred_element_type=F32) * dec_ref[0]
        inner = jnp.dot(scores.astype(BF16), v, preferred_element_type=F32)
        cross = jnp.dot(q, state.astype(BF16), preferred_element_type=F32) * qd_ref[0]
        kd = (k.astype(F32) * kd_ref[0]).astype(BF16)
        s_ref[...] = cd_ref[0, 0:1, :] * state + lax.dot_general(kd, v, _TN, preferred_element_type=F32)
        y = _group_norm_gate(inner + cross, gate_ref[0, rows, :].astype(F32), gn_ref[...])
        y_ref[0, rows, :] = y.astype(y_ref.dtype)
        return carry

    lax.fori_loop(0, tb // RET_CHUNK, body, 0)

    @pl.when(cb == pl.num_programs(2) - 1)
    def _():
        st_ref[0, 0] = s_ref[...]


def _ret_prompt(proj, gn, bsz, seq, tb):
    decay, q_decay, k_decay, chunk_decay = _decay_tables(RET_CHUNK, RET_CHUNK)
    qd = jnp.broadcast_to(q_decay[:, :, None], (RET_HEADS, RET_CHUNK, RET_DV))
    kd = jnp.broadcast_to(k_decay[:, :, None], (RET_HEADS, RET_CHUNK, RET_DK))
    cd = jnp.broadcast_to(chunk_decay[:, None, None], (RET_HEADS, 8, RET_DV))
    proj = proj.reshape(bsz, seq, proj.shape[-1])
    k_off = RET_QK // RET_DK
    v_off = 2 * RET_QK // RET_DV
    g_off = v_off + RET_HEADS
    per_head = lambda shape: pl.BlockSpec((1,) + shape, lambda b, h, c: (h, 0, 0))
    y, state = pl.pallas_call(
        functools.partial(_ret_prompt_kernel, tb=tb),
        grid=(bsz, RET_HEADS, seq // tb),
        in_specs=[pl.BlockSpec((1, tb, RET_DK), lambda b, h, c: (b, c, h)),
                  pl.BlockSpec((1, tb, RET_DK), lambda b, h, c: (b, c, k_off + h)),
                  pl.BlockSpec((1, tb, RET_DV), lambda b, h, c: (b, c, v_off + h)),
                  pl.BlockSpec((1, tb, RET_DV), lambda b, h, c: (b, c, g_off + h)),
                  per_head((RET_CHUNK, RET_CHUNK)), per_head((RET_CHUNK, RET_DV)),
                  per_head((RET_CHUNK, RET_DK)), per_head((8, RET_DV)),
                  pl.BlockSpec((1, RET_DV), lambda b, h, c: (0, h))],
        out_specs=[pl.BlockSpec((1, tb, RET_DV), lambda b, h, c: (b, c, h)),
                   pl.BlockSpec((1, 1, RET_DK, RET_DV), lambda b, h, c: (b, h, 0, 0))],
        out_shape=[jax.ShapeDtypeStruct((bsz, seq, RET_V), BF16),
                   jax.ShapeDtypeStruct((bsz, RET_HEADS, RET_DK, RET_DV), F32)],
        scratch_shapes=[pltpu.VMEM((RET_DK, RET_DV), F32)],
        compiler_params=_params(3), name="retention_prompt",
    )(proj, proj, proj, proj, decay, qd, kd, cd, gn)
    return y.reshape(bsz * seq, RET_V), state


def _ret_sample_kernel(p_ref, st_ref, dec_ref, qd_ref, kd_ref, cd_ref, gn_ref, y_ref, ns_ref):
    tp = SAMPLE_PAD
    proj = p_ref[0]
    for h in range(RET_HEADS):
        q = proj[:, h * RET_DK:(h + 1) * RET_DK].astype(BF16)
        k = proj[:, RET_QK + h * RET_DK:RET_QK + (h + 1) * RET_DK]
        v = proj[:, 2 * RET_QK + h * RET_DV:2 * RET_QK + (h + 1) * RET_DV]
        gate = proj[:, 2 * RET_QK + RET_V + h * RET_DV:2 * RET_QK + RET_V + (h + 1) * RET_DV]
        zk = jnp.zeros((LANES - tp, RET_DK), F32)
        zv = jnp.zeros((LANES - tp, RET_DV), F32)
        kp = jnp.concatenate([k, zk], axis=0).astype(BF16)
        kdp = jnp.concatenate([k * kd_ref[h], zk], axis=0).astype(BF16)
        vp = jnp.concatenate([v, zv], axis=0).astype(BF16)
        state = st_ref[0, h]
        scores = lax.dot_general(q, kp, _NT, preferred_element_type=F32) * dec_ref[h]
        inner = jnp.dot(scores.astype(BF16), vp, preferred_element_type=F32)
        cross = jnp.dot(q, state.astype(BF16), preferred_element_type=F32) * qd_ref[h]
        ns_ref[0, h] = cd_ref[h, 0:1, :] * state + lax.dot_general(kdp, vp, _TN, preferred_element_type=F32)
        y = _group_norm_gate(inner + cross, gate, gn_ref[:, h * RET_DV:(h + 1) * RET_DV])
        y_ref[0, :, h * RET_DV:(h + 1) * RET_DV] = y


def _ret_sample(proj, state, gn, t_new):
    bsz = proj.shape[0]
    tp = SAMPLE_PAD
    decay, q_decay, k_decay, chunk_decay = _decay_tables(t_new, tp)
    dec = jnp.pad(decay, ((0, 0), (0, 0), (0, LANES - tp)))
    qd = jnp.broadcast_to(q_decay[:, :, None], (RET_HEADS, tp, RET_DV))
    kd = jnp.broadcast_to(k_decay[:, :, None], (RET_HEADS, tp, RET_DK))
    cd = jnp.broadcast_to(chunk_decay[:, None, None], (RET_HEADS, 8, RET_DV))
    const = lambda a: pl.BlockSpec(a.shape, lambda b: (0,) * a.ndim)
    return pl.pallas_call(
        _ret_sample_kernel, grid=(bsz,),
        in_specs=[pl.BlockSpec((1,) + proj.shape[1:], lambda b: (b, 0, 0)),
                  pl.BlockSpec((1,) + state.shape[1:], lambda b: (b, 0, 0, 0)),
                  const(dec), const(qd), const(kd), const(cd), const(gn)],
        out_specs=[pl.BlockSpec((1, tp, RET_V), lambda b: (b, 0, 0)),
                   pl.BlockSpec((1,) + state.shape[1:], lambda b: (b, 0, 0, 0))],
        out_shape=[jax.ShapeDtypeStruct((bsz, tp, RET_V), F32),
                   jax.ShapeDtypeStruct(state.shape, F32)],
        compiler_params=_params(1), name="retention_sample",
    )(proj, state, dec, qd, kd, cd, gn)


def _router_kernel(x_ref, w_ref, o_ref):
    logits = jnp.dot(x_ref[...], w_ref[...], preferred_element_type=F32, precision=lax.Precision.HIGHEST)
    lane = lax.broadcasted_iota(jnp.int32, logits.shape, 1)
    logits = jnp.where(lane < N_EXPERTS, logits, NEG)
    m1 = jnp.max(logits, axis=-1, keepdims=True)
    i1 = jnp.min(jnp.where(logits == m1, lane, LANES), axis=-1, keepdims=True)
    rest = jnp.where(lane == i1, NEG, logits)
    m2 = jnp.max(rest, axis=-1, keepdims=True)
    i2 = jnp.min(jnp.where(rest == m2, lane, LANES), axis=-1, keepdims=True)
    e2 = jnp.exp(m2 - m1)
    den = 1.0 + e2
    o_ref[...] = jnp.where(lane == i1, 1.0 / den, jnp.where(lane == i2, e2 / den, 0.0))


def _router(x, w_router, tm):
    m = x.shape[0]
    w = jnp.pad(w_router, ((0, 0), (0, LANES - N_EXPERTS)))
    return pl.pallas_call(
        _router_kernel, grid=(m // tm,),
        in_specs=[pl.BlockSpec((tm, D_MODEL), lambda i: (i, 0)), pl.BlockSpec((D_MODEL, LANES), lambda i: (0, 0))],
        out_specs=pl.BlockSpec((tm, LANES), lambda i: (i, 0)),
        out_shape=jax.ShapeDtypeStruct((m, LANES), F32),
        compiler_params=_params(1), name="router",
    )(x, w)


def _moe_kernel(x_ref, c_ref, wg_ref, wu_ref, wd_ref, g_ref, b_ref, o_ref, acc_ref):
    e = pl.program_id(1)
    f = pl.program_id(2)
    xb = x_ref[...].astype(BF16)
    a = jnp.dot(xb, wg_ref[0], preferred_element_type=F32)
    u = jnp.dot(xb, wu_ref[0], preferred_element_type=F32)
    comb = c_ref[...]
    lane = lax.broadcasted_iota(jnp.int32, comb.shape, 1)
    ce = jnp.sum(jnp.where(lane == e, comb, 0.0), axis=-1, keepdims=True)
    part = jnp.dot((_silu(a) * u * ce).astype(BF16), wd_ref[0], preferred_element_type=F32)
    start = jnp.logical_and(e == 0, f == 0)

    @pl.when(start)
    def _():
        acc_ref[...] = part

    @pl.when(jnp.logical_not(start))
    def _():
        acc_ref[...] += part

    @pl.when(jnp.logical_and(e == pl.num_programs(1) - 1, f == pl.num_programs(2) - 1))
    def _():
        o_ref[...] = _layer_norm(ALPHA * x_ref[...] + acc_ref[...], g_ref[...], b_ref[...])


def _moe(x, comb, w_gu, w_down, g, b, tm, tf):
    m = x.shape[0]
    d_ff = w_down.shape[1]
    nf = d_ff // tf
    vec = pl.BlockSpec((1, D_MODEL), lambda i, e, f: (0, 0))
    return pl.pallas_call(
        _moe_kernel, grid=(m // tm, N_EXPERTS, nf),
        in_specs=[pl.BlockSpec((tm, D_MODEL), lambda i, e, f: (i, 0)),
                  pl.BlockSpec((tm, LANES), lambda i, e, f: (i, 0)),
                  pl.BlockSpec((1, D_MODEL, tf), lambda i, e, f: (e, 0, f)),
                  pl.BlockSpec((1, D_MODEL, tf), lambda i, e, f: (e, 0, nf + f)),
                  pl.BlockSpec((1, tf, D_MODEL), lambda i, e, f: (e, f, 0)), vec, vec],
        out_specs=pl.BlockSpec((tm, D_MODEL), lambda i, e, f: (i, 0)),
        out_shape=jax.ShapeDtypeStruct((m, D_MODEL), F32),
        scratch_shapes=[pltpu.VMEM((tm, D_MODEL), F32)],
        compiler_params=_params(3), name="moe",
    )(x, comb, w_gu, w_gu, w_down, g, b)


def _deinterleave(a, bsz, seq, dil):
    c = a.shape[-1]
    return a.reshape(bsz, seq // dil, dil, c).transpose(0, 2, 1, 3).reshape(bsz * dil, seq // dil, c)


def _interleave(a, bsz, seq, dil):
    c = a.shape[-1]
    return a.reshape(bsz, dil, seq // dil, c).transpose(0, 2, 1, 3).reshape(bsz * seq, c)


def kernel(x_prompt, x_sample, cache_kv_w128, cache_kv_w512, cache_kv_w2048, state_ret,
           ln_g, ln_b, rel_bias, w_in_dil, w_out_dil, w_in_ret, ret_gn_g, w_out_ret,
           w_gu_dense, w_down_dense, w_router, w_gu_moe, w_down_moe):
    bsz, seq, _ = x_prompt.shape
    dbsz, t_new, _ = x_sample.shape
    tp = SAMPLE_PAD
    n_p = bsz * seq
    n_s = dbsz * tp
    caches = (cache_kv_w128, cache_kv_w512, cache_kv_w2048)

    w_in_dil_b = w_in_dil.astype(BF16)
    w_out_dil_b = w_out_dil.astype(BF16)
    w_in_ret_b = w_in_ret.astype(BF16)
    w_out_ret_b = w_out_ret.astype(BF16)
    w_gu_dense_b = w_gu_dense.astype(BF16)
    w_down_dense_b = w_down_dense.astype(BF16)
    w_gu_moe_b = w_gu_moe.astype(BF16)
    w_down_moe_b = w_down_moe.astype(BF16)
    lng = ln_g.reshape(DEPTH, 2, 1, D_MODEL)
    lnb = ln_b.reshape(DEPTH, 2, 1, D_MODEL)
    gn = ret_gn_g.reshape(1, RET_V)

    hp = x_prompt.reshape(n_p, D_MODEL)
    hs = jnp.pad(x_sample, ((0, 0), (0, tp - t_new), (0, 0))).reshape(n_s, D_MODEL)

    tm = min(1024, seq)
    qkv_p, kv128_p, kv512_p, kv2048_p = _proj_dil_prompt(hp, w_in_dil_b, bsz, seq, tm)
    outs, lses = [], []
    for g, (window, dil) in enumerate(DIL_GROUPS):
        cols = qkv_p[:, g * G_COLS:(g + 1) * G_COLS]
        qkv_g = _deinterleave(cols, bsz, seq, dil)
        tbl = _prompt_table(rel_bias, g, window, dil)
        o, l = _attn_prompt(qkv_g, tbl, min(512, seq // dil))
        outs.append(_interleave(o, bsz, seq, dil))
        lses.append(_interleave(l, bsz, seq, dil))
    mixed_p = _merge(outs, lses, tm)
    hp = _mm_ln(mixed_p, w_out_dil_b, hp, lng[0, 0], lnb[0, 0], 512)

    qkv_s = _matmul(hs, w_in_dil_b, G_COLS)
    qkv_s3 = qkv_s.reshape(dbsz, tp, N_GROUPS * G_COLS)
    tabs = [_sample_tables(rel_bias, g, window, dil, caches[g].shape[1], t_new)
            for g, (window, dil) in enumerate(DIL_GROUPS)]
    caches2 = [c.reshape(c.shape[0], c.shape[1], 2 * A_WIDTH) for c in caches]
    mixed_s = _attn_sample(qkv_s3, caches2, [t[0] for t in tabs], [t[1] for t in tabs])
    hs = _mm_ln(mixed_s.reshape(n_s, A_WIDTH), w_out_dil_b, hs, lng[0, 0], lnb[0, 0], n_s)
    rows_s = []
    for g in range(N_GROUPS):
        kv = qkv_s3[:, :t_new, g * G_COLS + A_WIDTH:(g + 1) * G_COLS]
        rows_s.append(kv.reshape(dbsz, t_new, 2, H_SLOT, HEAD_DIM))

    hp = _ffn_dense(hp, w_gu_dense_b, w_down_dense_b, lng[0, 1], lnb[0, 1], 512, 1408)
    hs = _ffn_dense(hs, w_gu_dense_b, w_down_dense_b, lng[0, 1], lnb[0, 1], n_s, 1408)

    cos_p, sin_p = _rope_tables(jnp.arange(seq, dtype=jnp.int32))
    pos_s = jnp.tile(PAST_LEN + jnp.arange(tp, dtype=jnp.int32), dbsz)
    cos_s, sin_s = _rope_tables(pos_s)
    proj_p = _proj_ret(hp, w_in_ret_b, cos_p, sin_p, tm, BF16)
    y_p, ret_p = _ret_prompt(proj_p, gn, bsz, seq, min(1024, seq))
    hp = _mm_ln(y_p, w_out_ret_b, hp, lng[1, 0], lnb[1, 0], 512)
    proj_s = _proj_ret(hs, w_in_ret_b, cos_s, sin_s, n_s, F32)
    y_s, ret_s = _ret_sample(proj_s.reshape(dbsz, tp, -1), state_ret, gn, t_new)
    hs = _mm_ln(y_s.reshape(n_s, RET_V), w_out_ret_b, hs, lng[1, 0], lnb[1, 0], n_s)

    comb_p = _router(hp, w_router, 1024)
    hp = _moe(hp, comb_p, w_gu_moe_b, w_down_moe_b, lng[1, 1], lnb[1, 1], 1024, 896)
    comb_s = _router(hs, w_router, n_s)
    hs = _moe(hs, comb_s, w_gu_moe_b, w_down_moe_b, lng[1, 1], lnb[1, 1], n_s, 896)

    y_prompt = hp.reshape(bsz, seq, D_MODEL)
    y_sample = hs.reshape(dbsz, tp, D_MODEL)[:, :t_new]
    shape5 = lambda a: a.reshape(a.shape[0], a.shape[1], 2, H_SLOT, HEAD_DIM)
    return (y_prompt, y_sample, shape5(kv128_p), shape5(kv512_p), shape5(kv2048_p), ret_p,
            rows_s[0], rows_s[1], rows_s[2], ret_s)
```

```python
import functools

import jax
import jax.numpy as jnp
import numpy as np
from jax import lax
from jax.experimental import pallas as pl
from jax.experimental.pallas import tpu as pltpu

F32 = jnp.float32
BF16 = jnp.bfloat16

DEPTH = 2
D_MODEL = 1024
PAST_LEN = 16384
DIL_GROUPS = ((128, 1), (512, 4), (2048, 16))
N_GROUPS = 3
H_SLOT = 8
HEAD_DIM = 64
A_WIDTH = H_SLOT * HEAD_DIM
G_COLS = 3 * A_WIDTH
N_BUCKETS = 32
MAX_DISTANCE = 2048
RET_HEADS = 4
RET_DK = 256
RET_DV = 512
RET_CHUNK = 128
ROPE_BASE = 10000.0
RET_QK = RET_HEADS * RET_DK
RET_V = RET_HEADS * RET_DV
N_EXPERTS = 8
LN_EPS = 1e-5
GN_EPS = 1e-5
ALPHA = (2 * DEPTH) ** 0.25
NEG = -1e30

LANES = 128
ATT_BLOCK = 128
SAMPLE_PAD = 16
VMEM_LIMIT = 56 * 1024 * 1024

_NT = (((1,), (1,)), ((), ()))
_TN = (((0,), (0,)), ((), ()))


def _params(n_grid):
    return pltpu.CompilerParams(dimension_semantics=("arbitrary",) * n_grid,
                                vmem_limit_bytes=VMEM_LIMIT)


def _layer_norm(z, g, b):
    mu = jnp.mean(z, axis=-1, keepdims=True)
    zc = z - mu
    var = jnp.mean(zc * zc, axis=-1, keepdims=True)
    return zc * lax.rsqrt(var + LN_EPS) * g + b


def _silu(a):
    return a / (1.0 + jnp.exp(-a))


def _t5_bucket(dist):
    max_exact = N_BUCKETS // 2
    d = np.asarray(dist, dtype=np.int64)
    scaled = np.log(np.maximum(d, 1) / max_exact) / np.log(MAX_DISTANCE / max_exact)
    large = np.minimum(max_exact + (scaled * (N_BUCKETS - max_exact)).astype(np.int32), N_BUCKETS - 1)
    return np.where(d < max_exact, d, large).astype(np.int32)


def _group_bias(rel_bias, g, window, dil):
    n_keys = window // dil + 1
    buckets = jnp.asarray(_t5_bucket(np.arange(n_keys) * dil))
    return rel_bias[buckets][:, g * H_SLOT:(g + 1) * H_SLOT].T.astype(F32)


def _matmul_kernel(x_ref, w_ref, o_ref):
    o_ref[...] = jnp.dot(x_ref[...].astype(BF16), w_ref[...], preferred_element_type=F32)


def _matmul(x, w, tn):
    m, k = x.shape
    n = w.shape[1]
    return pl.pallas_call(
        _matmul_kernel,
        grid=(n // tn,),
        in_specs=[pl.BlockSpec((m, k), lambda j: (0, 0)),
                  pl.BlockSpec((k, tn), lambda j: (0, j))],
        out_specs=pl.BlockSpec((m, tn), lambda j: (0, j)),
        out_shape=jax.ShapeDtypeStruct((m, n), F32),
        compiler_params=_params(1),
        name="matmul_sample",
    )(x, w)


def _proj_dil_kernel(x_ref, w_ref, qkv_ref, kv0_ref, kv1_ref, kv2_ref, *, tm, tpb, keeps):
    g = pl.program_id(1)
    j = pl.program_id(0) % tpb
    acc = jnp.dot(x_ref[...].astype(BF16), w_ref[...], preferred_element_type=F32)
    qkv_ref[:, :A_WIDTH] = (acc[:, :A_WIDTH] * (HEAD_DIM ** -0.5)).astype(BF16)
    qkv_ref[:, A_WIDTH:] = acc[:, A_WIDTH:].astype(BF16)
    for gi, (ref, keep) in enumerate(zip((kv0_ref, kv1_ref, kv2_ref), keeps)):
        if keep >= tm:
            cond = jnp.logical_and(g == gi, j >= tpb - keep // tm)
            row0 = 0
        else:
            cond = jnp.logical_and(g == gi, j == tpb - 1)
            row0 = tm - keep

        @pl.when(cond)
        def _(ref=ref, row0=row0):
            ref[0] = acc[row0:, A_WIDTH:]


def _proj_dil_prompt(x, w, bsz, seq, tm):
    tpb = seq // tm
    keeps = tuple(min(wd, seq) for wd, _ in DIL_GROUPS)

    def kv_spec(keep):
        if keep >= tm:
            first = tpb - keep // tm
            return pl.BlockSpec((1, tm, 2 * A_WIDTH),
                                lambda i, g: (i // tpb, jnp.maximum(i % tpb - first, 0), 0))
        return pl.BlockSpec((1, keep, 2 * A_WIDTH), lambda i, g: (i // tpb, 0, 0))

    return pl.pallas_call(
        functools.partial(_proj_dil_kernel, tm=tm, tpb=tpb, keeps=keeps),
        grid=(bsz * tpb, N_GROUPS),
        in_specs=[pl.BlockSpec((tm, D_MODEL), lambda i, g: (i, 0)),
                  pl.BlockSpec((D_MODEL, G_COLS), lambda i, g: (0, g))],
        out_specs=[pl.BlockSpec((tm, G_COLS), lambda i, g: (i, g))] + [kv_spec(k) for k in keeps],
        out_shape=[jax.ShapeDtypeStruct((bsz * seq, N_GROUPS * G_COLS), BF16)]
        + [jax.ShapeDtypeStruct((bsz, k, 2 * A_WIDTH), F32) for k in keeps],
        compiler_params=_params(2),
        name="proj_dil_prompt",
    )(x, w)


def _attn_prompt_kernel(q_ref, k_ref, v_ref, kp_ref, vp_ref, tbl_ref, o_ref, lse_ref, kbuf, vbuf, *, tq):
    n = pl.program_id(1)
    kbuf[0:ATT_BLOCK, :] = kp_ref[0]
    kbuf[ATT_BLOCK:, :] = k_ref[0]
    vbuf[0:ATT_BLOCK, :] = vp_ref[0]
    vbuf[ATT_BLOCK:, :] = v_ref[0]
    low = lax.broadcasted_iota(jnp.int32, (ATT_BLOCK, LANES), 1) < HEAD_DIM

    def body(m, carry):
        r0 = pl.multiple_of(m * ATT_BLOCK, ATT_BLOCK)
        first = jnp.where(jnp.logical_and(n == 0, m == 0), 1, 0)
        for p in range(A_WIDTH // LANES):
            cols = slice(p * LANES, (p + 1) * LANES)
            qm = q_ref[0, pl.ds(r0, ATT_BLOCK), cols]
            keys = kbuf[pl.ds(r0, 2 * ATT_BLOCK), cols]
            vals = vbuf[pl.ds(r0, 2 * ATT_BLOCK), cols]
            outs, lses = [], []
            for a in range(2):
                qa = jnp.where(low if a == 0 else jnp.logical_not(low), qm, jnp.zeros_like(qm))
                s = lax.dot_general(qa, keys, _NT, preferred_element_type=F32) + tbl_ref[first, 2 * p + a]
                mx = jnp.max(s, axis=-1, keepdims=True)
                e = jnp.exp(s - mx)
                l = jnp.sum(e, axis=-1, keepdims=True)
                o = jnp.dot(e.astype(BF16), vals, preferred_element_type=F32)
                outs.append(o / l)
                lses.append(jnp.broadcast_to(mx + jnp.log(l), (ATT_BLOCK, LANES)))
            o_ref[0, pl.ds(r0, ATT_BLOCK), cols] = jnp.where(low, outs[0], outs[1]).astype(BF16)
            lse_ref[0, pl.ds(r0, ATT_BLOCK), cols] = jnp.where(low, lses[0], lses[1])
        return carry

    lax.fori_loop(0, tq // ATT_BLOCK, body, 0)


def _attn_prompt(qkv, tbl, tq):
    nb, length, _ = qkv.shape
    sub = tq // ATT_BLOCK
    cur = lambda c: pl.BlockSpec((1, tq, A_WIDTH), lambda s, n: (s, n, c))
    prev = lambda c: pl.BlockSpec((1, ATT_BLOCK, A_WIDTH), lambda s, n: (s, jnp.maximum(n * sub - 1, 0), c))
    return pl.pallas_call(
        functools.partial(_attn_prompt_kernel, tq=tq),
        grid=(nb, length // tq),
        in_specs=[cur(0), cur(1), cur(2), prev(1), prev(2),
                  pl.BlockSpec(tbl.shape, lambda s, n: (0, 0, 0, 0))],
        out_specs=[pl.BlockSpec((1, tq, A_WIDTH), lambda s, n: (s, n, 0))] * 2,
        out_shape=[jax.ShapeDtypeStruct((nb, length, A_WIDTH), BF16),
                   jax.ShapeDtypeStruct((nb, length, A_WIDTH), F32)],
        scratch_shapes=[pltpu.VMEM((tq + ATT_BLOCK, A_WIDTH), BF16)] * 2,
        compiler_params=_params(2),
        name="attn_prompt",
    )(qkv, qkv, qkv, qkv, qkv, tbl)


def _prompt_table(rel_bias, g, window, dil):
    bias = _group_bias(rel_bias, g, window, dil)
    i = np.arange(ATT_BLOCK)[:, None]
    c = np.arange(2 * ATT_BLOCK)[None, :]
    dist = i + ATT_BLOCK - c
    valid = (dist >= 0) & (dist <= window // dil)
    tbl = jnp.where(valid[None], bias[:, np.clip(dist, 0, window // dil)], NEG)
    tbl_first = jnp.where((c < ATT_BLOCK)[None], NEG, tbl)
    return jnp.stack([tbl, tbl_first], 0)


def _merge_kernel(o0, o1, o2, l0, l1, l2, out_ref):
    ls = [l0[...], l1[...], l2[...]]
    mx = jnp.maximum(jnp.maximum(ls[0], ls[1]), ls[2])
    es = [jnp.exp(l - mx) for l in ls]
    num = es[0] * o0[...].astype(F32) + es[1] * o1[...].astype(F32) + es[2] * o2[...].astype(F32)
    out_ref[...] = (num / (es[0] + es[1] + es[2])).astype(BF16)


def _merge(os_, ls_, tm):
    m = os_[0].shape[0]
    spec = pl.BlockSpec((tm, A_WIDTH), lambda i: (i, 0))
    return pl.pallas_call(
        _merge_kernel, grid=(m // tm,), in_specs=[spec] * 6, out_specs=spec,
        out_shape=jax.ShapeDtypeStruct((m, A_WIDTH), BF16),
        compiler_params=_params(1), name="merge_groups",
    )(*os_, *ls_)


def _attn_sample_kernel(qkv_ref, c0_ref, c1_ref, c2_ref, tc0, tc1, tc2, tn0, tn1, tn2, out_ref):
    tp = SAMPLE_PAD
    rows = H_SLOT * tp
    head_of_row = lax.broadcasted_iota(jnp.int32, (rows, A_WIDTH), 0) // tp
    head_of_lane = lax.broadcasted_iota(jnp.int32, (rows, A_WIDTH), 1) // HEAD_DIM
    diag = head_of_row == head_of_lane
    qkv = qkv_ref[0]
    scores, values = [], []
    for g, (c_ref, tc, tn) in enumerate(((c0_ref, tc0, tn0), (c1_ref, tc1, tn1), (c2_ref, tc2, tn2))):
        base = g * G_COLS
        q = qkv[:, base:base + A_WIDTH] * (HEAD_DIM ** -0.5)
        q2 = jnp.where(diag, jnp.concatenate([q] * H_SLOT, axis=0), 0.0).astype(BF16)
        kc = c_ref[0, :, :A_WIDTH].astype(BF16)
        vc = c_ref[0, :, A_WIDTH:].astype(BF16)
        zpad = jnp.zeros((LANES - tp, A_WIDTH), F32)
        kn = jnp.concatenate([qkv[:, base + A_WIDTH:base + 2 * A_WIDTH], zpad], axis=0).astype(BF16)
        vn = jnp.concatenate([qkv[:, base + 2 * A_WIDTH:base + 3 * A_WIDTH], zpad], axis=0).astype(BF16)
        scores.append(lax.dot_general(q2, kc, _NT, preferred_element_type=F32) + tc[...])
        values.append(vc)
        scores.append(lax.dot_general(q2, kn, _NT, preferred_element_type=F32) + tn[...])
        values.append(vn)
    mx = functools.reduce(jnp.maximum, [jnp.max(s, axis=-1, keepdims=True) for s in scores])
    den = jnp.zeros((rows, 1), F32)
    acc = jnp.zeros((rows, A_WIDTH), F32)
    for s, v in zip(scores, values):
        e = jnp.exp(s - mx)
        den = den + jnp.sum(e, axis=-1, keepdims=True)
        acc = acc + jnp.dot(e.astype(BF16), v, preferred_element_type=F32)
    r = jnp.where(diag, acc / den, 0.0)
    out = r[0:tp]
    for h in range(1, H_SLOT):
        out = out + r[h * tp:(h + 1) * tp]
    out_ref[0] = out


def _sample_tables(rel_bias, g, window, dil, length, t_new):
    bias = _group_bias(rel_bias, g, window, dil)
    n_keys = window // dil + 1
    t = np.arange(SAMPLE_PAD)[:, None]
    c = np.arange(length)[None, :]
    dist = length + t - c
    valid = (dist % dil == 0) & (dist // dil < n_keys) & (t < t_new)
    tc = jnp.where(valid[None], bias[:, np.clip(dist // dil, 0, n_keys - 1)], NEG)
    c2 = np.arange(LANES)[None, :]
    dist2 = t - c2
    valid2 = (dist2 >= 0) & (dist2 % dil == 0) & (dist2 // dil < n_keys) & (c2 < t_new) & (t < t_new)
    tn = jnp.where(valid2[None], bias[:, np.clip(dist2 // dil, 0, n_keys - 1)], NEG)
    return (tc.reshape(H_SLOT * SAMPLE_PAD, length), tn.reshape(H_SLOT * SAMPLE_PAD, LANES))


def _attn_sample(qkv, caches, tables_c, tables_n):
    bsz = qkv.shape[0]
    const = lambda a: pl.BlockSpec(a.shape, lambda b: (0, 0))
    return pl.pallas_call(
        _attn_sample_kernel,
        grid=(bsz,),
        in_specs=[pl.BlockSpec((1,) + qkv.shape[1:], lambda b: (b, 0, 0))]
        + [pl.BlockSpec((1,) + c.shape[1:], lambda b: (b, 0, 0)) for c in caches]
        + [const(t) for t in tables_c] + [const(t) for t in tables_n],
        out_specs=pl.BlockSpec((1, SAMPLE_PAD, A_WIDTH), lambda b: (b, 0, 0)),
        out_shape=jax.ShapeDtypeStruct((bsz, SAMPLE_PAD, A_WIDTH), F32),
        compiler_params=_params(1),
        name="attn_sample",
    )(qkv, *caches, *tables_c, *tables_n)


def _mm_ln_kernel(y_ref, w_ref, x_ref, g_ref, b_ref, o_ref):
    acc = jnp.dot(y_ref[...].astype(BF16), w_ref[...], preferred_element_type=F32)
    o_ref[...] = _layer_norm(ALPHA * x_ref[...] + acc, g_ref[...], b_ref[...])


def _mm_ln(y, w, x, g, b, tm):
    m, k = y.shape
    vec = pl.BlockSpec((1, D_MODEL), lambda i: (0, 0))
    return pl.pallas_call(
        _mm_ln_kernel, grid=(m // tm,),
        in_specs=[pl.BlockSpec((tm, k), lambda i: (i, 0)), pl.BlockSpec((k, D_MODEL), lambda i: (0, 0)),
                  pl.BlockSpec((tm, D_MODEL), lambda i: (i, 0)), vec, vec],
        out_specs=pl.BlockSpec((tm, D_MODEL), lambda i: (i, 0)),
        out_shape=jax.ShapeDtypeStruct((m, D_MODEL), F32),
        compiler_params=_params(1), name="outproj_ln",
    )(y, w, x, g, b)


def _ffn_kernel(x_ref, wg_ref, wu_ref, wd_ref, g_ref, b_ref, o_ref, acc_ref):
    f = pl.program_id(1)
    xb = x_ref[...].astype(BF16)
    a = jnp.dot(xb, wg_ref[...], preferred_element_type=F32)
    u = jnp.dot(xb, wu_ref[...], preferred_element_type=F32)
    part = jnp.dot((_silu(a) * u).astype(BF16), wd_ref[...], preferred_element_type=F32)

    @pl.when(f == 0)
    def _():
        acc_ref[...] = part

    @pl.when(f > 0)
    def _():
        acc_ref[...] += part

    @pl.when(f == pl.num_programs(1) - 1)
    def _():
        o_ref[...] = _layer_norm(ALPHA * x_ref[...] + acc_ref[...], g_ref[...], b_ref[...])


def _ffn_dense(x, w_gu, w_down, g, b, tm, tf):
    m = x.shape[0]
    d_ff = w_down.shape[0]
    nf = d_ff // tf
    vec = pl.BlockSpec((1, D_MODEL), lambda i, f: (0, 0))
    return pl.pallas_call(
        _ffn_kernel, grid=(m // tm, nf),
        in_specs=[pl.BlockSpec((tm, D_MODEL), lambda i, f: (i, 0)),
                  pl.BlockSpec((D_MODEL, tf), lambda i, f: (0, f)),
                  pl.BlockSpec((D_MODEL, tf), lambda i, f: (0, nf + f)),
                  pl.BlockSpec((tf, D_MODEL), lambda i, f: (f, 0)), vec, vec],
        out_specs=pl.BlockSpec((tm, D_MODEL), lambda i, f: (i, 0)),
        out_shape=jax.ShapeDtypeStruct((m, D_MODEL), F32),
        scratch_shapes=[pltpu.VMEM((tm, D_MODEL), F32)],
        compiler_params=_params(2), name="ffn_dense",
    )(x, w_gu, w_gu, w_down, g, b)


def _proj_ret_kernel(x_ref, w_ref, cos_ref, sin_ref, o_ref, *, scale_k):
    n = pl.program_id(1)
    acc = jnp.dot(x_ref[...].astype(BF16), w_ref[...], preferred_element_type=F32)

    @pl.when(n < 2)
    def _():
        cos = cos_ref[...]
        sin = sin_ref[...]
        scale = jnp.where(n == 1, scale_k, 1.0)
        half = RET_DK // 2
        for h in range(RET_HEADS):
            x1 = acc[:, h * RET_DK:h * RET_DK + half]
            x2 = acc[:, h * RET_DK + half:(h + 1) * RET_DK]
            o_ref[:, h * RET_DK:h * RET_DK + half] = ((x1 * cos - x2 * sin) * scale).astype(o_ref.dtype)
            o_ref[:, h * RET_DK + half:(h + 1) * RET_DK] = ((x1 * sin + x2 * cos) * scale).astype(o_ref.dtype)

    @pl.when(n >= 2)
    def _():
        o_ref[...] = acc.astype(o_ref.dtype)


def _proj_ret(x, w, cos, sin, tm, out_dtype):
    m = x.shape[0]
    n_cols = w.shape[1]
    tn = RET_QK
    pos_tiles = cos.shape[0] // tm
    return pl.pallas_call(
        functools.partial(_proj_ret_kernel, scale_k=RET_DK ** -0.5),
        grid=(m // tm, n_cols // tn),
        in_specs=[pl.BlockSpec((tm, D_MODEL), lambda i, n: (i, 0)),
                  pl.BlockSpec((D_MODEL, tn), lambda i, n: (0, n)),
                  pl.BlockSpec((tm, RET_DK // 2), lambda i, n: (i % pos_tiles, 0)),
                  pl.BlockSpec((tm, RET_DK // 2), lambda i, n: (i % pos_tiles, 0))],
        out_specs=pl.BlockSpec((tm, tn), lambda i, n: (i, n)),
        out_shape=jax.ShapeDtypeStruct((m, n_cols), out_dtype),
        compiler_params=_params(2), name="proj_ret",
    )(x, w, cos, sin)


def _rope_tables(pos):
    half = RET_DK // 2
    inv = 1.0 / (ROPE_BASE ** (jnp.arange(half, dtype=F32) / half))
    ang = pos.astype(F32)[:, None] * inv[None]
    return jnp.cos(ang), jnp.sin(ang)


def _log_gamma():
    return jnp.log(1.0 - 2.0 ** (-5.0 - jnp.arange(RET_HEADS, dtype=F32)))


def _decay_tables(c, rows):
    lg = _log_gamma()
    n = jnp.arange(rows, dtype=F32)
    live = n < c
    diff = n[:, None] - n[None, :]
    decay = jnp.where((diff >= 0)[None] & live[None, None, :],
                      jnp.exp(jnp.maximum(diff, 0.0)[None] * lg[:, None, None]), 0.0)
    q_decay = jnp.exp((n[None, :] + 1.0) * lg[:, None])
    k_decay = jnp.where(live[None], jnp.exp((c - 1.0 - n)[None, :] * lg[:, None]), 0.0)
    chunk_decay = jnp.exp(c * lg)
    return decay, q_decay, k_decay, chunk_decay


def _group_norm_gate(o, gate, gn):
    mu = jnp.mean(o, axis=-1, keepdims=True)
    oc = o - mu
    var = jnp.mean(oc * oc, axis=-1, keepdims=True)
    return _silu(gate) * (oc * lax.rsqrt(var + GN_EPS) * gn)


def _ret_prompt_kernel(q_ref, k_ref, v_ref, gate_ref, dec_ref, qd_ref, kd_ref, cd_ref, gn_ref,
                       y_ref, st_ref, s_ref, *, tb):
    cb = pl.program_id(2)

    @pl.when(cb == 0)
    def _():
        s_ref[...] = jnp.zeros_like(s_ref)

    def body(ci, carry):
        r0 = pl.multiple_of(ci * RET_CHUNK, RET_CHUNK)
        rows = pl.ds(r0, RET_CHUNK)
        q = q_ref[0, rows, :]
        k = k_ref[0, rows, :]
        v = v_ref[0, rows, :]
        state = s_ref[...]
        scores = lax.dot_general(q, k, _NT, preferred_element_type=F32) * dec_ref[0]
        inner = jnp.dot(scores.astype(BF16), v, preferred_element_type=F32)
        cross = jnp.dot(q, state.astype(BF16), preferred_element_type=F32) * qd_ref[0]
        kd = (k.astype(F32) * kd_ref[0]).astype(BF16)
        s_ref[...] = cd_ref[0, 0:1, :] * state + lax.dot_general(kd, v, _TN, preferred_element_type=F32)
        y = _group_norm_gate(inner + cross, gate_ref[0, rows, :].astype(F32), gn_ref[...])
        y_ref[0, rows, :] = y.astype(y_ref.dtype)
        return carry

    lax.fori_loop(0, tb // RET_CHUNK, body, 0)

    @pl.when(cb == pl.num_programs(2) - 1)
    def _():
        st_ref[0, 0] = s_ref[...]


def _ret_prompt(proj, gn, bsz, seq, tb):
    decay, q_decay, k_decay, chunk_decay = _decay_tables(RET_CHUNK, RET_CHUNK)
    qd = jnp.broadcast_to(q_decay[:, :, None], (RET_HEADS, RET_CHUNK, RET_DV))
    kd = jnp.broadcast_to(k_decay[:, :, None], (RET_HEADS, RET_CHUNK, RET_DK))
    cd = jnp.broadcast_to(chunk_decay[:, None, None], (RET_HEADS, 8, RET_DV))
    proj = proj.reshape(bsz, seq, proj.shape[-1])
    k_off = RET_QK // RET_DK
    v_off = 2 * RET_QK // RET_DV
    g_off = v_off + RET_HEADS
    per_head = lambda shape: pl.BlockSpec((1,) + shape, lambda b, h, c: (h, 0, 0))
    y, state = pl.pallas_call(
        functools.partial(_ret_prompt_kernel, tb=tb),
        grid=(bsz, RET_HEADS, seq // tb),
        in_specs=[pl.BlockSpec((1, tb, RET_DK), lambda b, h, c: (b, c, h)),
                  pl.BlockSpec((1, tb, RET_DK), lambda b, h, c: (b, c, k_off + h)),
                  pl.BlockSpec((1, tb, RET_DV), lambda b, h, c: (b, c, v_off + h)),
                  pl.BlockSpec((1, tb, RET_DV), lambda b, h, c: (b, c, g_off + h)),
                  per_head((RET_CHUNK, RET_CHUNK)), per_head((RET_CHUNK, RET_DV)),
                  per_head((RET_CHUNK, RET_DK)), per_head((8, RET_DV)),
                  pl.BlockSpec((1, RET_DV), lambda b, h, c: (0, h))],
        out_specs=[pl.BlockSpec((1, tb, RET_DV), lambda b, h, c: (b, c, h)),
                   pl.BlockSpec((1, 1, RET_DK, RET_DV), lambda b, h, c: (b, h, 0, 0))],
        out_shape=[jax.ShapeDtypeStruct((bsz, seq, RET_V), BF16),
                   jax.ShapeDtypeStruct((bsz, RET_HEADS, RET_DK, RET_DV), F32)],
        scratch_shapes=[pltpu.VMEM((RET_DK, RET_DV), F32)],
        compiler_params=_params(3), name="retention_prompt",
    )(proj, proj, proj, proj, decay, qd, kd, cd, gn)
    return y.reshape(bsz * seq, RET_V), state


def _ret_sample_kernel(p_ref, st_ref, dec_ref, qd_ref, kd_ref, cd_ref, gn_ref, y_ref, ns_ref):
    tp = SAMPLE_PAD
    proj = p_ref[0]
    for h in range(RET_HEADS):
        q = proj[:, h * RET_DK:(h + 1) * RET_DK].astype(BF16)
        k = proj[:, RET_QK + h * RET_DK:RET_QK + (h + 1) * RET_DK]
        v = proj[:, 2 * RET_QK + h * RET_DV:2 * RET_QK + (h + 1) * RET_DV]
        gate = proj[:, 2 * RET_QK + RET_V + h * RET_DV:2 * RET_QK + RET_V + (h + 1) * RET_DV]
        zk = jnp.zeros((LANES - tp, RET_DK), F32)
        zv = jnp.zeros((LANES - tp, RET_DV), F32)
        kp = jnp.concatenate([k, zk], axis=0).astype(BF16)
        kdp = jnp.concatenate([k * kd_ref[h], zk], axis=0).astype(BF16)
        vp = jnp.concatenate([v, zv], axis=0).astype(BF16)
        state = st_ref[0, h]
        scores = lax.dot_general(q, kp, _NT, preferred_element_type=F32) * dec_ref[h]
        inner = jnp.dot(scores.astype(BF16), vp, preferred_element_type=F32)
        cross = jnp.dot(q, state.astype(BF16), preferred_element_type=F32) * qd_ref[h]
        ns_ref[0, h] = cd_ref[h, 0:1, :] * state + lax.dot_general(kdp, vp, _TN, preferred_element_type=F32)
        y = _group_norm_gate(inner + cross, gate, gn_ref[:, h * RET_DV:(h + 1) * RET_DV])
        y_ref[0, :, h * RET_DV:(h + 1) * RET_DV] = y


def _ret_sample(proj, state, gn, t_new):
    bsz = proj.shape[0]
    tp = SAMPLE_PAD
    decay, q_decay, k_decay, chunk_decay = _decay_tables(t_new, tp)
    dec = jnp.pad(decay, ((0, 0), (0, 0), (0, LANES - tp)))
    qd = jnp.broadcast_to(q_decay[:, :, None], (RET_HEADS, tp, RET_DV))
    kd = jnp.broadcast_to(k_decay[:, :, None], (RET_HEADS, tp, RET_DK))
    cd = jnp.broadcast_to(chunk_decay[:, None, None], (RET_HEADS, 8, RET_DV))
    const = lambda a: pl.BlockSpec(a.shape, lambda b: (0,) * a.ndim)
    return pl.pallas_call(
        _ret_sample_kernel, grid=(bsz,),
        in_specs=[pl.BlockSpec((1,) + proj.shape[1:], lambda b: (b, 0, 0)),
                  pl.BlockSpec((1,) + state.shape[1:], lambda b: (b, 0, 0, 0)),
                  const(dec), const(qd), const(kd), const(cd), const(gn)],
        out_specs=[pl.BlockSpec((1, tp, RET_V), lambda b: (b, 0, 0)),
                   pl.BlockSpec((1,) + state.shape[1:], lambda b: (b, 0, 0, 0))],
        out_shape=[jax.ShapeDtypeStruct((bsz, tp, RET_V), F32),
                   jax.ShapeDtypeStruct(state.shape, F32)],
        compiler_params=_params(1), name="retention_sample",
    )(proj, state, dec, qd, kd, cd, gn)


def _router_kernel(x_ref, w_ref, o_ref):
    logits = jnp.dot(x_ref[...], w_ref[...], preferred_element_type=F32, precision=lax.Precision.HIGHEST)
    lane = lax.broadcasted_iota(jnp.int32, logits.shape, 1)
    logits = jnp.where(lane < N_EXPERTS, logits, NEG)
    m1 = jnp.max(logits, axis=-1, keepdims=True)
    i1 = jnp.min(jnp.where(logits == m1, lane, LANES), axis=-1, keepdims=True)
    rest = jnp.where(lane == i1, NEG, logits)
    m2 = jnp.max(rest, axis=-1, keepdims=True)
    i2 = jnp.min(jnp.where(rest == m2, lane, LANES), axis=-1, keepdims=True)
    e2 = jnp.exp(m2 - m1)
    den = 1.0 + e2
    o_ref[...] = jnp.where(lane == 0, 1.0 / den,
                           jnp.where(lane == 1, e2 / den,
                                     jnp.where(lane == 2, i1.astype(F32),
                                               jnp.where(lane == 3, i2.astype(F32), 0.0))))


def _router(x, w_router, tm):
    m = x.shape[0]
    w = jnp.pad(w_router, ((0, 0), (0, LANES - N_EXPERTS)))
    return pl.pallas_call(
        _router_kernel, grid=(m // tm,),
        in_specs=[pl.BlockSpec((tm, D_MODEL), lambda i: (i, 0)), pl.BlockSpec((D_MODEL, LANES), lambda i: (0, 0))],
        out_specs=pl.BlockSpec((tm, LANES), lambda i: (i, 0)),
        out_shape=jax.ShapeDtypeStruct((m, LANES), F32),
        compiler_params=_params(1), name="router",
    )(x, w)


def _route_plan(e1, e2, tile):
    n_tok = e1.shape[0]
    e = jnp.concatenate([e1, e2])
    onehot = (e[:, None] == jnp.arange(N_EXPERTS, dtype=jnp.int32)[None]).astype(jnp.int32)
    csum = jnp.cumsum(onehot, axis=0)
    rank = jnp.take_along_axis(csum, e[:, None], axis=1)[:, 0] - 1
    tiles_per_expert = (csum[-1] + tile - 1) // tile
    tile_end = jnp.cumsum(tiles_per_expert)
    pos = ((tile_end - tiles_per_expert) * tile)[e] + rank
    n_tiles = (2 * n_tok) // tile + N_EXPERTS
    tok = jnp.tile(jnp.arange(n_tok, dtype=jnp.int32), 2)
    row_token = jnp.zeros((n_tiles * tile,), jnp.int32).at[pos].set(tok)
    n_used = tile_end[-1]
    tile_ids = jnp.arange(n_tiles, dtype=jnp.int32)
    tile_expert = jnp.minimum(jnp.searchsorted(tile_end, tile_ids, side="right"), N_EXPERTS - 1)
    tile_expert = jnp.where(tile_ids < n_used, tile_expert, tile_expert[n_used - 1])
    meta = jnp.concatenate([tile_expert.astype(jnp.int32), n_used[None].astype(jnp.int32)])
    return meta, row_token.reshape(n_tiles, 1, tile), pos[:n_tok], pos[n_tok:], n_tiles


def _row_copy(src_hbm, dst, sem, src_row, dst_row):
    return pltpu.make_async_copy(src_hbm.at[pl.ds(src_row, 1), :], dst.at[pl.ds(dst_row, 1), :], sem)


def _gather_rows(idx_ref, src_hbm, dst, sem, n_rows):
    def issue(r, carry):
        _row_copy(src_hbm, dst, sem, idx_ref[0, 0, r], r).start()
        return carry

    lax.fori_loop(0, n_rows, issue, 0)
    pltpu.make_async_copy(src_hbm.at[pl.ds(0, n_rows), :], dst, sem).wait()


def _dispatch_kernel(meta_ref, idx_ref, x_hbm, o_ref, buf, sem, *, tile, n_tiles):
    used = pl.program_id(0) < meta_ref[n_tiles]

    @pl.when(used)
    def _():
        _gather_rows(idx_ref, x_hbm, buf, sem, tile)
        o_ref[...] = buf[...].astype(BF16)

    @pl.when(jnp.logical_not(used))
    def _():
        o_ref[...] = jnp.zeros_like(o_ref)


def _dispatch(meta, row_token, x, tile, n_tiles):
    return pl.pallas_call(
        functools.partial(_dispatch_kernel, tile=tile, n_tiles=n_tiles),
        grid_spec=pltpu.PrefetchScalarGridSpec(
            num_scalar_prefetch=1, grid=(n_tiles,),
            in_specs=[pl.BlockSpec((1, 1, tile), lambda j, meta: (j, 0, 0), memory_space=pltpu.SMEM),
                      pl.BlockSpec(memory_space=pl.ANY)],
            out_specs=pl.BlockSpec((tile, D_MODEL), lambda j, meta: (j, 0)),
            scratch_shapes=[pltpu.VMEM((tile, D_MODEL), F32), pltpu.SemaphoreType.DMA(())]),
        out_shape=jax.ShapeDtypeStruct((n_tiles * tile, D_MODEL), BF16),
        compiler_params=_params(1), name="moe_dispatch",
    )(meta, row_token, x)


def _expert_kernel(meta_ref, xs_ref, wg_ref, wu_ref, wd_ref, o_ref, *, n_tiles):
    f = pl.program_id(1)
    used = pl.program_id(0) < meta_ref[n_tiles]

    @pl.when(jnp.logical_and(jnp.logical_not(used), f == 0))
    def _():
        o_ref[...] = jnp.zeros_like(o_ref)

    @pl.when(used)
    def _():
        xb = xs_ref[...]
        a = jnp.dot(xb, wg_ref[0], preferred_element_type=F32)
        u = jnp.dot(xb, wu_ref[0], preferred_element_type=F32)
        part = jnp.dot((_silu(a) * u).astype(BF16), wd_ref[0], preferred_element_type=F32)

        @pl.when(f == 0)
        def _():
            o_ref[...] = part

        @pl.when(f > 0)
        def _():
            o_ref[...] += part


def _experts(meta, xs, w_gu, w_down, tile, n_tiles, tf):
    d_ff = w_down.shape[1]
    nf = d_ff // tf
    fcol = lambda j, f, meta: jnp.where(j < meta[n_tiles], f, nf - 1)
    return pl.pallas_call(
        functools.partial(_expert_kernel, n_tiles=n_tiles),
        grid_spec=pltpu.PrefetchScalarGridSpec(
            num_scalar_prefetch=1, grid=(n_tiles, nf),
            in_specs=[pl.BlockSpec((tile, D_MODEL), lambda j, f, meta: (j, 0)),
                      pl.BlockSpec((1, D_MODEL, tf), lambda j, f, meta: (meta[j], 0, fcol(j, f, meta))),
                      pl.BlockSpec((1, D_MODEL, tf), lambda j, f, meta: (meta[j], 0, nf + fcol(j, f, meta))),
                      pl.BlockSpec((1, tf, D_MODEL), lambda j, f, meta: (meta[j], fcol(j, f, meta), 0))],
            out_specs=pl.BlockSpec((tile, D_MODEL), lambda j, f, meta: (j, 0))),
        out_shape=jax.ShapeDtypeStruct((n_tiles * tile, D_MODEL), F32),
        compiler_params=_params(2), name="moe_experts",
    )(meta, xs, w_gu, w_gu, w_down)


def _combine_kernel(p1_ref, p2_ref, rows_hbm, x_ref, r_ref, g_ref, b_ref, o_ref, buf1, buf2, sem1, sem2, *, tm):
    _gather_rows(p1_ref, rows_hbm, buf1, sem1, tm)
    _gather_rows(p2_ref, rows_hbm, buf2, sem2, tm)
    route = r_ref[...]
    y = route[:, 0:1] * buf1[...] + route[:, 1:2] * buf2[...]
    o_ref[...] = _layer_norm(ALPHA * x_ref[...] + y, g_ref[...], b_ref[...])


def _combine(pos1, pos2, rows, x, route, g, b, tm):
    m = x.shape[0]
    idx = lambda: pl.BlockSpec((1, 1, tm), lambda i: (i, 0, 0), memory_space=pltpu.SMEM)
    vec = pl.BlockSpec((1, D_MODEL), lambda i: (0, 0))
    return pl.pallas_call(
        functools.partial(_combine_kernel, tm=tm), grid=(m // tm,),
        in_specs=[idx(), idx(), pl.BlockSpec(memory_space=pl.ANY),
                  pl.BlockSpec((tm, D_MODEL), lambda i: (i, 0)),
                  pl.BlockSpec((tm, LANES), lambda i: (i, 0)), vec, vec],
        out_specs=pl.BlockSpec((tm, D_MODEL), lambda i: (i, 0)),
        out_shape=jax.ShapeDtypeStruct((m, D_MODEL), F32),
        scratch_shapes=[pltpu.VMEM((tm, D_MODEL), F32), pltpu.VMEM((tm, D_MODEL), F32),
                        pltpu.SemaphoreType.DMA(()), pltpu.SemaphoreType.DMA(())],
        compiler_params=_params(1), name="moe_combine",
    )(pos1.reshape(m // tm, 1, tm), pos2.reshape(m // tm, 1, tm), rows, x, route, g, b)


def _moe(x, w_router, w_gu, w_down, g, b, tm, tile, tf):
    route = _router(x, w_router, tm)
    e1 = route[:, 2].astype(jnp.int32)
    e2 = route[:, 3].astype(jnp.int32)
    meta, row_token, pos1, pos2, n_tiles = _route_plan(e1, e2, tile)
    xs = _dispatch(meta, row_token, x, tile, n_tiles)
    rows = _experts(meta, xs, w_gu, w_down, tile, n_tiles, tf)
    return _combine(pos1, pos2, rows, x, route, g, b, tm)


def _deinterleave(a, bsz, seq, dil):
    c = a.shape[-1]
    return a.reshape(bsz, seq // dil, dil, c).transpose(0, 2, 1, 3).reshape(bsz * dil, seq // dil, c)


def _interleave(a, bsz, seq, dil):
    c = a.shape[-1]
    return a.reshape(bsz, dil, seq // dil, c).transpose(0, 2, 1, 3).reshape(bsz * seq, c)


def kernel(x_prompt, x_sample, cache_kv_w128, cache_kv_w512, cache_kv_w2048, state_ret,
           ln_g, ln_b, rel_bias, w_in_dil, w_out_dil, w_in_ret, ret_gn_g, w_out_ret,
           w_gu_dense, w_down_dense, w_router, w_gu_moe, w_down_moe):
    bsz, seq, _ = x_prompt.shape
    dbsz, t_new, _ = x_sample.shape
    tp = SAMPLE_PAD
    n_p = bsz * seq
    n_s = dbsz * tp
    caches = (cache_kv_w128, cache_kv_w512, cache_kv_w2048)

    w_in_dil_b = w_in_dil.astype(BF16)
    w_out_dil_b = w_out_dil.astype(BF16)
    w_in_ret_b = w_in_ret.astype(BF16)
    w_out_ret_b = w_out_ret.astype(BF16)
    w_gu_dense_b = w_gu_dense.astype(BF16)
    w_down_dense_b = w_down_dense.astype(BF16)
    w_gu_moe_b = w_gu_moe.astype(BF16)
    w_down_moe_b = w_down_moe.astype(BF16)
    lng = ln_g.reshape(DEPTH, 2, 1, D_MODEL)
    lnb = ln_b.reshape(DEPTH, 2, 1, D_MODEL)
    gn = ret_gn_g.reshape(1, RET_V)

    hp = x_prompt.reshape(n_p, D_MODEL)
    hs = jnp.pad(x_sample, ((0, 0), (0, tp - t_new), (0, 0))).reshape(n_s, D_MODEL)

    tm = min(1024, seq)
    qkv_p, kv128_p, kv512_p, kv2048_p = _proj_dil_prompt(hp, w_in_dil_b, bsz, seq, tm)
    outs, lses = [], []
    for g, (window, dil) in enumerate(DIL_GROUPS):
        cols = qkv_p[:, g * G_COLS:(g + 1) * G_COLS]
        qkv_g = _deinterleave(cols, bsz, seq, dil)
        tbl = _prompt_table(rel_bias, g, window, dil)
        o, l = _attn_prompt(qkv_g, tbl, min(512, seq // dil))
        outs.append(_interleave(o, bsz, seq, dil))
        lses.append(_interleave(l, bsz, seq, dil))
    mixed_p = _merge(outs, lses, tm)
    hp = _mm_ln(mixed_p, w_out_dil_b, hp, lng[0, 0], lnb[0, 0], 512)

    qkv_s = _matmul(hs, w_in_dil_b, G_COLS)
    qkv_s3 = qkv_s.reshape(dbsz, tp, N_GROUPS * G_COLS)
    tabs = [_sample_tables(rel_bias, g, window, dil, caches[g].shape[1], t_new)
            for g, (window, dil) in enumerate(DIL_GROUPS)]
    caches2 = [c.reshape(c.shape[0], c.shape[1], 2 * A_WIDTH) for c in caches]
    mixed_s = _attn_sample(qkv_s3, caches2, [t[0] for t in tabs], [t[1] for t in tabs])
    hs = _mm_ln(mixed_s.reshape(n_s, A_WIDTH), w_out_dil_b, hs, lng[0, 0], lnb[0, 0], n_s)
    rows_s = []
    for g in range(N_GROUPS):
        kv = qkv_s3[:, :t_new, g * G_COLS + A_WIDTH:(g + 1) * G_COLS]
        rows_s.append(kv.reshape(dbsz, t_new, 2, H_SLOT, HEAD_DIM))

    hp = _ffn_dense(hp, w_gu_dense_b, w_down_dense_b, lng[0, 1], lnb[0, 1], 512, 1408)
    hs = _ffn_dense(hs, w_gu_dense_b, w_down_dense_b, lng[0, 1], lnb[0, 1], n_s, 1408)

    cos_p, sin_p = _rope_tables(jnp.arange(seq, dtype=jnp.int32))
    pos_s = jnp.tile(PAST_LEN + jnp.arange(tp, dtype=jnp.int32), dbsz)
    cos_s, sin_s = _rope_tables(pos_s)
    proj_p = _proj_ret(hp, w_in_ret_b, cos_p, sin_p, tm, BF16)
    y_p, ret_p = _ret_prompt(proj_p, gn, bsz, seq, min(1024, seq))
    hp = _mm_ln(y_p, w_out_ret_b, hp, lng[1, 0], lnb[1, 0], 512)
    proj_s = _proj_ret(hs, w_in_ret_b, cos_s, sin_s, n_s, F32)
    y_s, ret_s = _ret_sample(proj_s.reshape(dbsz, tp, -1), state_ret, gn, t_new)
    hs = _mm_ln(y_s.reshape(n_s, RET_V), w_out_ret_b, hs, lng[1, 0], lnb[1, 0], n_s)

    h_all = jnp.concatenate([hp, hs], axis=0)
    h_all = _moe(h_all, w_router, w_gu_moe_b, w_down_moe_b, lng[1, 1], lnb[1, 1], 512, 1024, 896)

    y_prompt = h_all[:n_p].reshape(bsz, seq, D_MODEL)
    y_sample = h_all[n_p:].reshape(dbsz, tp, D_MODEL)[:, :t_new]
    shape5 = lambda a: a.reshape(a.shape[0], a.shape[1], 2, H_SLOT, HEAD_DIM)
    return (y_prompt, y_sample, shape5(kv128_p), shape5(kv512_p), shape5(kv2048_p), ret_p,
            rows_s[0], rows_s[1], rows_s[2], ret_s)
```

```python
import functools

import jax
import jax.numpy as jnp
import numpy as np
from jax import lax
from jax.experimental import pallas as pl
from jax.experimental.pallas import tpu as pltpu

F32 = jnp.float32
BF16 = jnp.bfloat16

DEPTH = 2
D_MODEL = 1024
PAST_LEN = 16384
DIL_GROUPS = ((128, 1), (512, 4), (2048, 16))
N_GROUPS = 3
H_SLOT = 8
HEAD_DIM = 64
A_WIDTH = H_SLOT * HEAD_DIM
G_COLS = 3 * A_WIDTH
N_BUCKETS = 32
MAX_DISTANCE = 2048
RET_HEADS = 4
RET_DK = 256
RET_DV = 512
RET_CHUNK = 128
ROPE_BASE = 10000.0
RET_QK = RET_HEADS * RET_DK
RET_V = RET_HEADS * RET_DV
N_EXPERTS = 8
LN_EPS = 1e-5
GN_EPS = 1e-5
ALPHA = (2 * DEPTH) ** 0.25
NEG = -1e30

LANES = 128
ATT_BLOCK = 128
PERM = 256
SAMPLE_PAD = 16
VMEM_LIMIT = 56 * 1024 * 1024

_NT = (((1,), (1,)), ((), ()))
_TN = (((0,), (0,)), ((), ()))


def _params(n_grid):
    return pltpu.CompilerParams(dimension_semantics=("arbitrary",) * n_grid,
                                vmem_limit_bytes=VMEM_LIMIT)


def _layer_norm(z, g, b):
    mu = jnp.mean(z, axis=-1, keepdims=True)
    zc = z - mu
    var = jnp.mean(zc * zc, axis=-1, keepdims=True)
    return zc * lax.rsqrt(var + LN_EPS) * g + b


def _silu(a):
    return a / (1.0 + jnp.exp(-a))


def _t5_bucket(dist):
    max_exact = N_BUCKETS // 2
    d = np.asarray(dist, dtype=np.int64)
    scaled = np.log(np.maximum(d, 1) / max_exact) / np.log(MAX_DISTANCE / max_exact)
    large = np.minimum(max_exact + (scaled * (N_BUCKETS - max_exact)).astype(np.int32), N_BUCKETS - 1)
    return np.where(d < max_exact, d, large).astype(np.int32)


def _group_bias(rel_bias, g, window, dil):
    n_keys = window // dil + 1
    buckets = jnp.asarray(_t5_bucket(np.arange(n_keys) * dil))
    return rel_bias[buckets][:, g * H_SLOT:(g + 1) * H_SLOT].T.astype(F32)


def _matmul_kernel(x_ref, w_ref, o_ref):
    o_ref[...] = jnp.dot(x_ref[...].astype(BF16), w_ref[...], preferred_element_type=F32)


def _matmul(x, w, tn):
    m, k = x.shape
    n = w.shape[1]
    return pl.pallas_call(
        _matmul_kernel,
        grid=(n // tn,),
        in_specs=[pl.BlockSpec((m, k), lambda j: (0, 0)),
                  pl.BlockSpec((k, tn), lambda j: (0, j))],
        out_specs=pl.BlockSpec((m, tn), lambda j: (0, j)),
        out_shape=jax.ShapeDtypeStruct((m, n), F32),
        compiler_params=_params(1),
        name="matmul_sample",
    )(x, w)


def _deinterleave_matrix(dil):
    p = np.zeros((PERM, PERM), np.float32)
    rows = PERM // dil
    for r in range(dil):
        for m in range(rows):
            p[r * rows + m, m * dil + r] = 1.0
    return p


def _proj_dil_kernel(x_ref, w_ref, p1_ref, p2_ref, q0_ref, q1_ref, q2_ref, kv0_ref, kv1_ref, kv2_ref, xperm,
                     *, tm, tpb, keeps):
    g = pl.program_id(1)
    j = pl.program_id(0) % tpb
    q_refs = (q0_ref, q1_ref, q2_ref)
    kv_refs = (kv0_ref, kv1_ref, kv2_ref)
    perms = (None, p1_ref, p2_ref)

    def store_q(ref, r, row0, n_rows, acc, src0):
        ref[0, r, row0:row0 + n_rows, :A_WIDTH] = (acc[src0:src0 + n_rows, :A_WIDTH] * (HEAD_DIM ** -0.5)).astype(BF16)
        ref[0, r, row0:row0 + n_rows, A_WIDTH:] = acc[src0:src0 + n_rows, A_WIDTH:].astype(BF16)

    for gi, (_, dil) in enumerate(DIL_GROUPS):
        keep = keeps[gi]
        in_tail = (j >= tpb - keep // tm) if keep >= tm else (j == tpb - 1)
        row0 = 0 if keep >= tm else tm - keep

        @pl.when(g == gi)
        def _(gi=gi, dil=dil, in_tail=in_tail, row0=row0):
            xb = x_ref[...].astype(BF16)
            if dil == 1:
                acc = jnp.dot(xb, w_ref[...], preferred_element_type=F32)
                store_q(q_refs[gi], 0, 0, tm, acc, 0)
            else:
                for sub in range(tm // PERM):
                    blk = xb[sub * PERM:(sub + 1) * PERM]
                    xperm[sub * PERM:(sub + 1) * PERM, :] = jnp.dot(
                        perms[gi][...], blk, preferred_element_type=F32).astype(BF16)
                acc = jnp.dot(xperm[...], w_ref[...], preferred_element_type=F32)
                rows = PERM // dil
                for sub in range(tm // PERM):
                    for r in range(dil):
                        store_q(q_refs[gi], r, sub * rows, rows, acc, sub * PERM + r * rows)

            @pl.when(in_tail)
            def _():
                if dil == 1:
                    kv_refs[gi][0] = acc[row0:, A_WIDTH:]
                else:
                    kv_refs[gi][0] = jnp.dot(xb[row0:], w_ref[:, A_WIDTH:], preferred_element_type=F32)


def _proj_dil_prompt(x, w, bsz, seq, tm):
    tpb = seq // tm
    keeps = tuple(min(wd, seq) for wd, _ in DIL_GROUPS)

    def kv_spec(keep):
        if keep >= tm:
            first = tpb - keep // tm
            return pl.BlockSpec((1, tm, 2 * A_WIDTH),
                                lambda i, g: (i // tpb, jnp.maximum(i % tpb - first, 0), 0))
        return pl.BlockSpec((1, keep, 2 * A_WIDTH), lambda i, g: (i // tpb, 0, 0))

    perm = lambda d: jnp.asarray(_deinterleave_matrix(d), BF16)
    const = pl.BlockSpec((PERM, PERM), lambda i, g: (0, 0))
    outs = pl.pallas_call(
        functools.partial(_proj_dil_kernel, tm=tm, tpb=tpb, keeps=keeps),
        grid=(bsz * tpb, N_GROUPS),
        in_specs=[pl.BlockSpec((tm, D_MODEL), lambda i, g: (i, 0)),
                  pl.BlockSpec((D_MODEL, G_COLS), lambda i, g: (0, g)), const, const],
        out_specs=[pl.BlockSpec((1, d, tm // d, G_COLS), lambda i, g: (i // tpb, 0, i % tpb, 0))
                   for _, d in DIL_GROUPS] + [kv_spec(k) for k in keeps],
        out_shape=[jax.ShapeDtypeStruct((bsz, d, seq // d, G_COLS), BF16) for _, d in DIL_GROUPS]
        + [jax.ShapeDtypeStruct((bsz, k, 2 * A_WIDTH), F32) for k in keeps],
        scratch_shapes=[pltpu.VMEM((tm, D_MODEL), BF16)],
        compiler_params=_params(2),
        name="proj_dil_prompt",
    )(x, w, perm(DIL_GROUPS[1][1]), perm(DIL_GROUPS[2][1]))
    return outs[:3], outs[3:]


def _attn_prompt_kernel(q_ref, k_ref, v_ref, kp_ref, vp_ref, tbl_ref, o_ref, lse_ref, kbuf, vbuf, *, tq):
    n = pl.program_id(1)
    kbuf[0:ATT_BLOCK, :] = kp_ref[0]
    kbuf[ATT_BLOCK:, :] = k_ref[0]
    vbuf[0:ATT_BLOCK, :] = vp_ref[0]
    vbuf[ATT_BLOCK:, :] = v_ref[0]
    low = lax.broadcasted_iota(jnp.int32, (ATT_BLOCK, LANES), 1) < HEAD_DIM

    def body(m, carry):
        r0 = pl.multiple_of(m * ATT_BLOCK, ATT_BLOCK)
        first = jnp.where(jnp.logical_and(n == 0, m == 0), 1, 0)
        for p in range(A_WIDTH // LANES):
            cols = slice(p * LANES, (p + 1) * LANES)
            qm = q_ref[0, pl.ds(r0, ATT_BLOCK), cols]
            keys = kbuf[pl.ds(r0, 2 * ATT_BLOCK), cols]
            vals = vbuf[pl.ds(r0, 2 * ATT_BLOCK), cols]
            outs, lses = [], []
            for a in range(2):
                qa = jnp.where(low if a == 0 else jnp.logical_not(low), qm, jnp.zeros_like(qm))
                s = lax.dot_general(qa, keys, _NT, preferred_element_type=F32) + tbl_ref[first, 2 * p + a]
                mx = jnp.max(s, axis=-1, keepdims=True)
                e = jnp.exp(s - mx)
                l = jnp.sum(e, axis=-1, keepdims=True)
                o = jnp.dot(e.astype(BF16), vals, preferred_element_type=F32)
                outs.append(o / l)
                lses.append(jnp.broadcast_to(mx + jnp.log(l), (ATT_BLOCK, LANES)))
            o_ref[0, pl.ds(r0, ATT_BLOCK), cols] = jnp.where(low, outs[0], outs[1]).astype(BF16)
            lse_ref[0, pl.ds(r0, ATT_BLOCK), cols] = jnp.where(low, lses[0], lses[1])
        return carry

    lax.fori_loop(0, tq // ATT_BLOCK, body, 0)


def _attn_prompt(qkv, tbl, tq):
    nb, length, _ = qkv.shape
    sub = tq // ATT_BLOCK
    cur = lambda c: pl.BlockSpec((1, tq, A_WIDTH), lambda s, n: (s, n, c))
    prev = lambda c: pl.BlockSpec((1, ATT_BLOCK, A_WIDTH), lambda s, n: (s, jnp.maximum(n * sub - 1, 0), c))
    return pl.pallas_call(
        functools.partial(_attn_prompt_kernel, tq=tq),
        grid=(nb, length // tq),
        in_specs=[cur(0), cur(1), cur(2), prev(1), prev(2),
                  pl.BlockSpec(tbl.shape, lambda s, n: (0, 0, 0, 0))],
        out_specs=[pl.BlockSpec((1, tq, A_WIDTH), lambda s, n: (s, n, 0))] * 2,
        out_shape=[jax.ShapeDtypeStruct((nb, length, A_WIDTH), BF16),
                   jax.ShapeDtypeStruct((nb, length, A_WIDTH), F32)],
        scratch_shapes=[pltpu.VMEM((tq + ATT_BLOCK, A_WIDTH), BF16)] * 2,
        compiler_params=_params(2),
        name="attn_prompt",
    )(qkv, qkv, qkv, qkv, qkv, tbl)


def _prompt_table(rel_bias, g, window, dil):
    bias = _group_bias(rel_bias, g, window, dil)
    i = np.arange(ATT_BLOCK)[:, None]
    c = np.arange(2 * ATT_BLOCK)[None, :]
    dist = i + ATT_BLOCK - c
    valid = (dist >= 0) & (dist <= window // dil)
    tbl = jnp.where(valid[None], bias[:, np.clip(dist, 0, window // dil)], NEG)
    tbl_first = jnp.where((c < ATT_BLOCK)[None], NEG, tbl)
    return jnp.stack([tbl, tbl_first], 0)


def _split3(x):
    hi = x.astype(BF16)
    r1 = x - hi.astype(F32)
    mid = r1.astype(BF16)
    lo = (r1 - mid.astype(F32)).astype(BF16)
    return hi, mid, lo


def _merge_out_kernel(o0, o1, o2, l0, l1, l2, pt1_ref, pt2_ref, w_ref, x_ref, g_ref, b_ref, out_ref, *, tm):
    pts = (None, pt1_ref, pt2_ref)

    def natural(ref, gi, exact_f32):
        dil = DIL_GROUPS[gi][1]
        if dil == 1:
            return ref[0, 0].astype(F32)
        rows = PERM // dil
        blocks = []
        for sub in range(tm // PERM):
            piece = jnp.concatenate([ref[0, r, sub * rows:(sub + 1) * rows, :] for r in range(dil)], axis=0)
            parts = _split3(piece) if exact_f32 else (piece,)
            blocks.append(sum(jnp.dot(pts[gi][...], p, preferred_element_type=F32) for p in parts))
        return jnp.concatenate(blocks, axis=0)

    os_ = [natural(r, gi, False) for gi, r in enumerate((o0, o1, o2))]
    ls = [natural(r, gi, True) for gi, r in enumerate((l0, l1, l2))]
    mx = jnp.maximum(jnp.maximum(ls[0], ls[1]), ls[2])
    es = [jnp.exp(l - mx) for l in ls]
    mixed = (es[0] * os_[0] + es[1] * os_[1] + es[2] * os_[2]) / (es[0] + es[1] + es[2])
    acc = jnp.dot(mixed.astype(BF16), w_ref[...], preferred_element_type=F32)
    out_ref[...] = _layer_norm(ALPHA * x_ref[...] + acc, g_ref[...], b_ref[...])


def _merge_out(os_, ls_, w, x, g, b, tm):
    bsz, _, seq, _ = os_[0].shape
    tpb = seq // tm
    grp = lambda d: pl.BlockSpec((1, d, tm // d, A_WIDTH), lambda i: (i // tpb, 0, i % tpb, 0))
    specs = [grp(d) for _, d in DIL_GROUPS]
    pt = lambda d: jnp.asarray(_deinterleave_matrix(d).T, BF16)
    const = pl.BlockSpec((PERM, PERM), lambda i: (0, 0))
    vec = pl.BlockSpec((1, D_MODEL), lambda i: (0, 0))
    return pl.pallas_call(
        functools.partial(_merge_out_kernel, tm=tm), grid=(bsz * tpb,),
        in_specs=specs + specs + [const, const, pl.BlockSpec((A_WIDTH, D_MODEL), lambda i: (0, 0)),
                                  pl.BlockSpec((tm, D_MODEL), lambda i: (i, 0)), vec, vec],
        out_specs=pl.BlockSpec((tm, D_MODEL), lambda i: (i, 0)),
        out_shape=jax.ShapeDtypeStruct((bsz * seq, D_MODEL), F32),
        compiler_params=_params(1), name="merge_outproj_ln",
    )(*os_, *ls_, pt(DIL_GROUPS[1][1]), pt(DIL_GROUPS[2][1]), w, x, g, b)


def _attn_sample_kernel(qkv_ref, c0_ref, c1_ref, c2_ref, tc0, tc1, tc2, tn0, tn1, tn2, out_ref):
    tp = SAMPLE_PAD
    rows = H_SLOT * tp
    head_of_row = lax.broadcasted_iota(jnp.int32, (rows, A_WIDTH), 0) // tp
    head_of_lane = lax.broadcasted_iota(jnp.int32, (rows, A_WIDTH), 1) // HEAD_DIM
    diag = head_of_row == head_of_lane
    qkv = qkv_ref[0]
    scores, values = [], []
    for g, (c_ref, tc, tn) in enumerate(((c0_ref, tc0, tn0), (c1_ref, tc1, tn1), (c2_ref, tc2, tn2))):
        base = g * G_COLS
        q = qkv[:, base:base + A_WIDTH] * (HEAD_DIM ** -0.5)
        q2 = jnp.where(diag, jnp.concatenate([q] * H_SLOT, axis=0), 0.0).astype(BF16)
        kc = c_ref[0, :, :A_WIDTH].astype(BF16)
        vc = c_ref[0, :, A_WIDTH:].astype(BF16)
        zpad = jnp.zeros((LANES - tp, A_WIDTH), F32)
        kn = jnp.concatenate([qkv[:, base + A_WIDTH:base + 2 * A_WIDTH], zpad], axis=0).astype(BF16)
        vn = jnp.concatenate([qkv[:, base + 2 * A_WIDTH:base + 3 * A_WIDTH], zpad], axis=0).astype(BF16)
        scores.append(lax.dot_general(q2, kc, _NT, preferred_element_type=F32) + tc[...])
        values.append(vc)
        scores.append(lax.dot_general(q2, kn, _NT, preferred_element_type=F32) + tn[...])
        values.append(vn)
    mx = functools.reduce(jnp.maximum, [jnp.max(s, axis=-1, keepdims=True) for s in scores])
    den = jnp.zeros((rows, 1), F32)
    acc = jnp.zeros((rows, A_WIDTH), F32)
    for s, v in zip(scores, values):
        e = jnp.exp(s - mx)
        den = den + jnp.sum(e, axis=-1, keepdims=True)
        acc = acc + jnp.dot(e.astype(BF16), v, preferred_element_type=F32)
    r = jnp.where(diag, acc / den, 0.0)
    out = r[0:tp]
    for h in range(1, H_SLOT):
        out = out + r[h * tp:(h + 1) * tp]
    out_ref[0] = out


def _sample_tables(rel_bias, g, window, dil, length, t_new):
    bias = _group_bias(rel_bias, g, window, dil)
    n_keys = window // dil + 1
    t = np.arange(SAMPLE_PAD)[:, None]
    c = np.arange(length)[None, :]
    dist = length + t - c
    valid = (dist % dil == 0) & (dist // dil < n_keys) & (t < t_new)
    tc = jnp.where(valid[None], bias[:, np.clip(dist // dil, 0, n_keys - 1)], NEG)
    c2 = np.arange(LANES)[None, :]
    dist2 = t - c2
    valid2 = (dist2 >= 0) & (dist2 % dil == 0) & (dist2 // dil < n_keys) & (c2 < t_new) & (t < t_new)
    tn = jnp.where(valid2[None], bias[:, np.clip(dist2 // dil, 0, n_keys - 1)], NEG)
    return (tc.reshape(H_SLOT * SAMPLE_PAD, length), tn.reshape(H_SLOT * SAMPLE_PAD, LANES))


def _attn_sample(qkv, caches, tables_c, tables_n):
    bsz = qkv.shape[0]
    const = lambda a: pl.BlockSpec(a.shape, lambda b: (0, 0))
    return pl.pallas_call(
        _attn_sample_kernel,
        grid=(bsz,),
        in_specs=[pl.BlockSpec((1,) + qkv.shape[1:], lambda b: (b, 0, 0))]
        + [pl.BlockSpec((1,) + c.shape[1:], lambda b: (b, 0, 0)) for c in caches]
        + [const(t) for t in tables_c] + [const(t) for t in tables_n],
        out_specs=pl.BlockSpec((1, SAMPLE_PAD, A_WIDTH), lambda b: (b, 0, 0)),
        out_shape=jax.ShapeDtypeStruct((bsz, SAMPLE_PAD, A_WIDTH), F32),
        compiler_params=_params(1),
        name="attn_sample",
    )(qkv, *caches, *tables_c, *tables_n)


def _mm_ln_kernel(y_ref, w_ref, x_ref, g_ref, b_ref, o_ref):
    acc = jnp.dot(y_ref[...].astype(BF16), w_ref[...], preferred_element_type=F32)
    o_ref[...] = _layer_norm(ALPHA * x_ref[...] + acc, g_ref[...], b_ref[...])


def _mm_ln(y, w, x, g, b, tm):
    m, k = y.shape
    vec = pl.BlockSpec((1, D_MODEL), lambda i: (0, 0))
    return pl.pallas_call(
        _mm_ln_kernel, grid=(m // tm,),
        in_specs=[pl.BlockSpec((tm, k), lambda i: (i, 0)), pl.BlockSpec((k, D_MODEL), lambda i: (0, 0)),
                  pl.BlockSpec((tm, D_MODEL), lambda i: (i, 0)), vec, vec],
        out_specs=pl.BlockSpec((tm, D_MODEL), lambda i: (i, 0)),
        out_shape=jax.ShapeDtypeStruct((m, D_MODEL), F32),
        compiler_params=_params(1), name="outproj_ln",
    )(y, w, x, g, b)


def _mm_ln_pair_kernel(yp_ref, ys_ref, w_ref, xp_ref, xs_ref, g_ref, b_ref, o_ref, *, prompt_tiles):
    def run(y_ref, x_ref):
        acc = jnp.dot(y_ref[...].astype(BF16), w_ref[...], preferred_element_type=F32)
        o_ref[...] = _layer_norm(ALPHA * x_ref[...] + acc, g_ref[...], b_ref[...])

    pl.when(pl.program_id(0) < prompt_tiles)(lambda: run(yp_ref, xp_ref))
    pl.when(pl.program_id(0) >= prompt_tiles)(lambda: run(ys_ref, xs_ref))


def _mm_ln_pair(y_p, y_s, w, x_p, x_s, g, b, tm):
    n_p, k = y_p.shape
    n_s = y_s.shape[0]
    pt = n_p // tm
    first = lambda width: pl.BlockSpec((tm, width), lambda i: (jnp.minimum(i, pt - 1), 0))
    second = lambda width: pl.BlockSpec((tm, width), lambda i: (jnp.maximum(i - pt, 0), 0))
    vec = pl.BlockSpec((1, D_MODEL), lambda i: (0, 0))
    return pl.pallas_call(
        functools.partial(_mm_ln_pair_kernel, prompt_tiles=pt), grid=((n_p + n_s) // tm,),
        in_specs=[first(k), second(k), pl.BlockSpec((k, D_MODEL), lambda i: (0, 0)),
                  first(D_MODEL), second(D_MODEL), vec, vec],
        out_specs=pl.BlockSpec((tm, D_MODEL), lambda i: (i, 0)),
        out_shape=jax.ShapeDtypeStruct((n_p + n_s, D_MODEL), F32),
        compiler_params=_params(1), name="outproj_ln_pair",
    )(y_p, y_s, w, x_p, x_s, g, b)


def _ffn_kernel(x_ref, wg_ref, wu_ref, wd_ref, g_ref, b_ref, o_ref, acc_ref):
    f = pl.program_id(1)
    xb = x_ref[...].astype(BF16)
    a = jnp.dot(xb, wg_ref[...], preferred_element_type=F32)
    u = jnp.dot(xb, wu_ref[...], preferred_element_type=F32)
    part = jnp.dot((_silu(a) * u).astype(BF16), wd_ref[...], preferred_element_type=F32)

    @pl.when(f == 0)
    def _():
        acc_ref[...] = part

    @pl.when(f > 0)
    def _():
        acc_ref[...] += part

    @pl.when(f == pl.num_programs(1) - 1)
    def _():
        o_ref[...] = _layer_norm(ALPHA * x_ref[...] + acc_ref[...], g_ref[...], b_ref[...])


def _ffn_dense(x, w_gu, w_down, g, b, tm, tf):
    m = x.shape[0]
    d_ff = w_down.shape[0]
    nf = d_ff // tf
    vec = pl.BlockSpec((1, D_MODEL), lambda i, f: (0, 0))
    return pl.pallas_call(
        _ffn_kernel, grid=(m // tm, nf),
        in_specs=[pl.BlockSpec((tm, D_MODEL), lambda i, f: (i, 0)),
                  pl.BlockSpec((D_MODEL, tf), lambda i, f: (0, f)),
                  pl.BlockSpec((D_MODEL, tf), lambda i, f: (0, nf + f)),
                  pl.BlockSpec((tf, D_MODEL), lambda i, f: (f, 0)), vec, vec],
        out_specs=pl.BlockSpec((tm, D_MODEL), lambda i, f: (i, 0)),
        out_shape=jax.ShapeDtypeStruct((m, D_MODEL), F32),
        scratch_shapes=[pltpu.VMEM((tm, D_MODEL), F32)],
        compiler_params=_params(2), name="ffn_dense",
    )(x, w_gu, w_gu, w_down, g, b)


def _proj_ret_kernel(x_ref, w_ref, cos_ref, sin_ref, o_ref, *, scale_k):
    n = pl.program_id(1)
    acc = jnp.dot(x_ref[...].astype(BF16), w_ref[...], preferred_element_type=F32)

    @pl.when(n < 2)
    def _():
        cos = cos_ref[...]
        sin = sin_ref[...]
        scale = jnp.where(n == 1, scale_k, 1.0)
        half = RET_DK // 2
        for h in range(RET_HEADS):
            x1 = acc[:, h * RET_DK:h * RET_DK + half]
            x2 = acc[:, h * RET_DK + half:(h + 1) * RET_DK]
            o_ref[:, h * RET_DK:h * RET_DK + half] = ((x1 * cos - x2 * sin) * scale).astype(o_ref.dtype)
            o_ref[:, h * RET_DK + half:(h + 1) * RET_DK] = ((x1 * sin + x2 * cos) * scale).astype(o_ref.dtype)

    @pl.when(n >= 2)
    def _():
        o_ref[...] = acc.astype(o_ref.dtype)


def _proj_ret(x, w, cos, sin, tm, out_dtype):
    m = x.shape[0]
    n_cols = w.shape[1]
    tn = RET_QK
    pos_tiles = cos.shape[0] // tm
    return pl.pallas_call(
        functools.partial(_proj_ret_kernel, scale_k=RET_DK ** -0.5),
        grid=(m // tm, n_cols // tn),
        in_specs=[pl.BlockSpec((tm, D_MODEL), lambda i, n: (i, 0)),
                  pl.BlockSpec((D_MODEL, tn), lambda i, n: (0, n)),
                  pl.BlockSpec((tm, RET_DK // 2), lambda i, n: (i % pos_tiles, 0)),
                  pl.BlockSpec((tm, RET_DK // 2), lambda i, n: (i % pos_tiles, 0))],
        out_specs=pl.BlockSpec((tm, tn), lambda i, n: (i, n)),
        out_shape=jax.ShapeDtypeStruct((m, n_cols), out_dtype),
        compiler_params=_params(2), name="proj_ret",
    )(x, w, cos, sin)


def _rope_tables(pos):
    half = RET_DK // 2
    inv = 1.0 / (ROPE_BASE ** (jnp.arange(half, dtype=F32) / half))
    ang = pos.astype(F32)[:, None] * inv[None]
    return jnp.cos(ang), jnp.sin(ang)


def _log_gamma():
    return jnp.log(1.0 - 2.0 ** (-5.0 - jnp.arange(RET_HEADS, dtype=F32)))


def _decay_tables(c, rows):
    lg = _log_gamma()
    n = jnp.arange(rows, dtype=F32)
    live = n < c
    diff = n[:, None] - n[None, :]
    decay = jnp.where((diff >= 0)[None] & live[None, None, :],
                      jnp.exp(jnp.maximum(diff, 0.0)[None] * lg[:, None, None]), 0.0)
    q_decay = jnp.exp((n[None, :] + 1.0) * lg[:, None])
    k_decay = jnp.where(live[None], jnp.exp((c - 1.0 - n)[None, :] * lg[:, None]), 0.0)
    chunk_decay = jnp.exp(c * lg)
    return decay, q_decay, k_decay, chunk_decay


def _group_norm_gate(o, gate, gn):
    mu = jnp.mean(o, axis=-1, keepdims=True)
    oc = o - mu
    var = jnp.mean(oc * oc, axis=-1, keepdims=True)
    return _silu(gate) * (oc * lax.rsqrt(var + GN_EPS) * gn)


def _ret_prompt_kernel(q_ref, k_ref, v_ref, gate_ref, dec_ref, qd_ref, kd_ref, cd_ref, gn_ref,
                       y_ref, st_ref, s_ref, *, tb):
    cb = pl.program_id(2)

    @pl.when(cb == 0)
    def _():
        s_ref[...] = jnp.zeros_like(s_ref)

    def body(ci, carry):
        r0 = pl.multiple_of(ci * RET_CHUNK, RET_CHUNK)
        rows = pl.ds(r0, RET_CHUNK)
        q = q_ref[0, rows, :]
        k = k_ref[0, rows, :]
        v = v_ref[0, rows, :]
        state = s_ref[...]
        scores = lax.dot_general(q, k, _NT, preferred_element_type=F32) * dec_ref[0]
        inner = jnp.dot(scores.astype(BF16), v, preferred_element_type=F32)
        cross = jnp.dot(q, state.astype(BF16), preferred_element_type=F32) * qd_ref[0]
        kd = (k.astype(F32) * kd_ref[0]).astype(BF16)
        s_ref[...] = cd_ref[0, 0:1, :] * state + lax.dot_general(kd, v, _TN, preferred_element_type=F32)
        y = _group_norm_gate(inner + cross, gate_ref[0, rows, :].astype(F32), gn_ref[...])
        y_ref[0, rows, :] = y.astype(y_ref.dtype)
        return carry

    lax.fori_loop(0, tb // RET_CHUNK, body, 0)

    @pl.when(cb == pl.num_programs(2) - 1)
    def _():
        st_ref[0, 0] = s_ref[...]


def _ret_prompt(proj, gn, bsz, seq, tb):
    decay, q_decay, k_decay, chunk_decay = _decay_tables(RET_CHUNK, RET_CHUNK)
    qd = jnp.broadcast_to(q_decay[:, :, None], (RET_HEADS, RET_CHUNK, RET_DV))
    kd = jnp.broadcast_to(k_decay[:, :, None], (RET_HEADS, RET_CHUNK, RET_DK))
    cd = jnp.broadcast_to(chunk_decay[:, None, None], (RET_HEADS, 8, RET_DV))
    proj = proj.reshape(bsz, seq, proj.shape[-1])
    k_off = RET_QK // RET_DK
    v_off = 2 * RET_QK // RET_DV
    g_off = v_off + RET_HEADS
    per_head = lambda shape: pl.BlockSpec((1,) + shape, lambda b, h, c: (h, 0, 0))
    y, state = pl.pallas_call(
        functools.partial(_ret_prompt_kernel, tb=tb),
        grid=(bsz, RET_HEADS, seq // tb),
        in_specs=[pl.BlockSpec((1, tb, RET_DK), lambda b, h, c: (b, c, h)),
                  pl.BlockSpec((1, tb, RET_DK), lambda b, h, c: (b, c, k_off + h)),
                  pl.BlockSpec((1, tb, RET_DV), lambda b, h, c: (b, c, v_off + h)),
                  pl.BlockSpec((1, tb, RET_DV), lambda b, h, c: (b, c, g_off + h)),
                  per_head((RET_CHUNK, RET_CHUNK)), per_head((RET_CHUNK, RET_DV)),
                  per_head((RET_CHUNK, RET_DK)), per_head((8, RET_DV)),
                  pl.BlockSpec((1, RET_DV), lambda b, h, c: (0, h))],
        out_specs=[pl.BlockSpec((1, tb, RET_DV), lambda b, h, c: (b, c, h)),
                   pl.BlockSpec((1, 1, RET_DK, RET_DV), lambda b, h, c: (b, h, 0, 0))],
        out_shape=[jax.ShapeDtypeStruct((bsz, seq, RET_V), BF16),
                   jax.ShapeDtypeStruct((bsz, RET_HEADS, RET_DK, RET_DV), F32)],
        scratch_shapes=[pltpu.VMEM((RET_DK, RET_DV), F32)],
        compiler_params=_params(3), name="retention_prompt",
    )(proj, proj, proj, proj, decay, qd, kd, cd, gn)
    return y.reshape(bsz * seq, RET_V), state


def _ret_sample_kernel(p_ref, st_ref, dec_ref, qd_ref, kd_ref, cd_ref, gn_ref, y_ref, ns_ref):
    tp = SAMPLE_PAD
    proj = p_ref[0]
    for h in range(RET_HEADS):
        q = proj[:, h * RET_DK:(h + 1) * RET_DK].astype(BF16)
        k = proj[:, RET_QK + h * RET_DK:RET_QK + (h + 1) * RET_DK]
        v = proj[:, 2 * RET_QK + h * RET_DV:2 * RET_QK + (h + 1) * RET_DV]
        gate = proj[:, 2 * RET_QK + RET_V + h * RET_DV:2 * RET_QK + RET_V + (h + 1) * RET_DV]
        zk = jnp.zeros((LANES - tp, RET_DK), F32)
        zv = jnp.zeros((LANES - tp, RET_DV), F32)
        kp = jnp.concatenate([k, zk], axis=0).astype(BF16)
        kdp = jnp.concatenate([k * kd_ref[h], zk], axis=0).astype(BF16)
        vp = jnp.concatenate([v, zv], axis=0).astype(BF16)
        state = st_ref[0, h]
        scores = lax.dot_general(q, kp, _NT, preferred_element_type=F32) * dec_ref[h]
        inner = jnp.dot(scores.astype(BF16), vp, preferred_element_type=F32)
        cross = jnp.dot(q, state.astype(BF16), preferred_element_type=F32) * qd_ref[h]
        ns_ref[0, h] = cd_ref[h, 0:1, :] * state + lax.dot_general(kdp, vp, _TN, preferred_element_type=F32)
        y = _group_norm_gate(inner + cross, gate, gn_ref[:, h * RET_DV:(h + 1) * RET_DV])
        y_ref[0, :, h * RET_DV:(h + 1) * RET_DV] = y


def _ret_sample(proj, state, gn, t_new):
    bsz = proj.shape[0]
    tp = SAMPLE_PAD
    decay, q_decay, k_decay, chunk_decay = _decay_tables(t_new, tp)
    dec = jnp.pad(decay, ((0, 0), (0, 0), (0, LANES - tp)))
    qd = jnp.broadcast_to(q_decay[:, :, None], (RET_HEADS, tp, RET_DV))
    kd = jnp.broadcast_to(k_decay[:, :, None], (RET_HEADS, tp, RET_DK))
    cd = jnp.broadcast_to(chunk_decay[:, None, None], (RET_HEADS, 8, RET_DV))
    const = lambda a: pl.BlockSpec(a.shape, lambda b: (0,) * a.ndim)
    return pl.pallas_call(
        _ret_sample_kernel, grid=(bsz,),
        in_specs=[pl.BlockSpec((1,) + proj.shape[1:], lambda b: (b, 0, 0)),
                  pl.BlockSpec((1,) + state.shape[1:], lambda b: (b, 0, 0, 0)),
                  const(dec), const(qd), const(kd), const(cd), const(gn)],
        out_specs=[pl.BlockSpec((1, tp, RET_V), lambda b: (b, 0, 0)),
                   pl.BlockSpec((1,) + state.shape[1:], lambda b: (b, 0, 0, 0))],
        out_shape=[jax.ShapeDtypeStruct((bsz, tp, RET_V), F32),
                   jax.ShapeDtypeStruct(state.shape, F32)],
        compiler_params=_params(1), name="retention_sample",
    )(proj, state, dec, qd, kd, cd, gn)


def _router_kernel(x_ref, w_ref, o_ref):
    logits = jnp.dot(x_ref[...], w_ref[...], preferred_element_type=F32, precision=lax.Precision.HIGHEST)
    lane = lax.broadcasted_iota(jnp.int32, logits.shape, 1)
    logits = jnp.where(lane < N_EXPERTS, logits, NEG)
    m1 = jnp.max(logits, axis=-1, keepdims=True)
    i1 = jnp.min(jnp.where(logits == m1, lane, LANES), axis=-1, keepdims=True)
    rest = jnp.where(lane == i1, NEG, logits)
    m2 = jnp.max(rest, axis=-1, keepdims=True)
    i2 = jnp.min(jnp.where(rest == m2, lane, LANES), axis=-1, keepdims=True)
    e2 = jnp.exp(m2 - m1)
    den = 1.0 + e2
    o_ref[...] = jnp.where(lane == 0, 1.0 / den,
                           jnp.where(lane == 1, e2 / den,
                                     jnp.where(lane == 2, i1.astype(F32),
                                               jnp.where(lane == 3, i2.astype(F32), 0.0))))


def _router(x, w_router, tm):
    m = x.shape[0]
    w = jnp.pad(w_router, ((0, 0), (0, LANES - N_EXPERTS)))
    return pl.pallas_call(
        _router_kernel, grid=(m // tm,),
        in_specs=[pl.BlockSpec((tm, D_MODEL), lambda i: (i, 0)), pl.BlockSpec((D_MODEL, LANES), lambda i: (0, 0))],
        out_specs=pl.BlockSpec((tm, LANES), lambda i: (i, 0)),
        out_shape=jax.ShapeDtypeStruct((m, LANES), F32),
        compiler_params=_params(1), name="router",
    )(x, w)


def _route_plan(e1, e2, tile):
    n_tok = e1.shape[0]
    e = jnp.concatenate([e1, e2])
    onehot = (e[:, None] == jnp.arange(N_EXPERTS, dtype=jnp.int32)[None]).astype(jnp.int32)
    csum = jnp.cumsum(onehot, axis=0)
    rank = jnp.take_along_axis(csum, e[:, None], axis=1)[:, 0] - 1
    tiles_per_expert = (csum[-1] + tile - 1) // tile
    tile_end = jnp.cumsum(tiles_per_expert)
    pos = ((tile_end - tiles_per_expert) * tile)[e] + rank
    n_tiles = (2 * n_tok) // tile + N_EXPERTS
    tok = jnp.tile(jnp.arange(n_tok, dtype=jnp.int32), 2)
    row_token = jnp.zeros((n_tiles * tile,), jnp.int32).at[pos].set(tok)
    n_used = tile_end[-1]
    tile_ids = jnp.arange(n_tiles, dtype=jnp.int32)
    tile_expert = jnp.minimum(jnp.searchsorted(tile_end, tile_ids, side="right"), N_EXPERTS - 1)
    tile_expert = jnp.where(tile_ids < n_used, tile_expert, tile_expert[n_used - 1])
    meta = jnp.concatenate([tile_expert.astype(jnp.int32), n_used[None].astype(jnp.int32)])
    return meta, row_token.reshape(n_tiles, 1, tile), pos[:n_tok], pos[n_tok:], n_tiles


def _row_copy(src_hbm, dst, sem, src_row, dst_row):
    return pltpu.make_async_copy(src_hbm.at[pl.ds(src_row, 1), :], dst.at[pl.ds(dst_row, 1), :], sem)


def _gather_rows(idx_ref, src_hbm, dst, sem, n_rows):
    def issue(r, carry):
        _row_copy(src_hbm, dst, sem, idx_ref[0, 0, r], r).start()
        return carry

    lax.fori_loop(0, n_rows, issue, 0)
    pltpu.make_async_copy(src_hbm.at[pl.ds(0, n_rows), :], dst, sem).wait()


def _dispatch_kernel(meta_ref, idx_ref, x_hbm, o_ref, buf, sem, *, tile, n_tiles):
    used = pl.program_id(0) < meta_ref[n_tiles]

    @pl.when(used)
    def _():
        _gather_rows(idx_ref, x_hbm, buf, sem, tile)
        o_ref[...] = buf[...].astype(BF16)

    @pl.when(jnp.logical_not(used))
    def _():
        o_ref[...] = jnp.zeros_like(o_ref)


def _dispatch(meta, row_token, x, tile, n_tiles):
    return pl.pallas_call(
        functools.partial(_dispatch_kernel, tile=tile, n_tiles=n_tiles),
        grid_spec=pltpu.PrefetchScalarGridSpec(
            num_scalar_prefetch=1, grid=(n_tiles,),
            in_specs=[pl.BlockSpec((1, 1, tile), lambda j, meta: (j, 0, 0), memory_space=pltpu.SMEM),
                      pl.BlockSpec(memory_space=pl.ANY)],
            out_specs=pl.BlockSpec((tile, D_MODEL), lambda j, meta: (j, 0)),
            scratch_shapes=[pltpu.VMEM((tile, D_MODEL), F32), pltpu.SemaphoreType.DMA(())]),
        out_shape=jax.ShapeDtypeStruct((n_tiles * tile, D_MODEL), BF16),
        compiler_params=_params(1), name="moe_dispatch",
    )(meta, row_token, x)


def _expert_kernel(meta_ref, xs_ref, wg_ref, wu_ref, wd_ref, o_ref, *, n_tiles):
    f = pl.program_id(1)
    used = pl.program_id(0) < meta_ref[n_tiles]

    @pl.when(jnp.logical_and(jnp.logical_not(used), f == 0))
    def _():
        o_ref[...] = jnp.zeros_like(o_ref)

    @pl.when(used)
    def _():
        xb = xs_ref[...]
        a = jnp.dot(xb, wg_ref[0], preferred_element_type=F32)
        u = jnp.dot(xb, wu_ref[0], preferred_element_type=F32)
        part = jnp.dot((_silu(a) * u).astype(BF16), wd_ref[0], preferred_element_type=F32)

        @pl.when(f == 0)
        def _():
            o_ref[...] = part

        @pl.when(f > 0)
        def _():
            o_ref[...] += part


def _experts(meta, xs, w_gu, w_down, tile, n_tiles, tf):
    d_ff = w_down.shape[1]
    nf = d_ff // tf
    fcol = lambda j, f, meta: jnp.where(j < meta[n_tiles], f, nf - 1)
    return pl.pallas_call(
        functools.partial(_expert_kernel, n_tiles=n_tiles),
        grid_spec=pltpu.PrefetchScalarGridSpec(
            num_scalar_prefetch=1, grid=(n_tiles, nf),
            in_specs=[pl.BlockSpec((tile, D_MODEL), lambda j, f, meta: (j, 0)),
                      pl.BlockSpec((1, D_MODEL, tf), lambda j, f, meta: (meta[j], 0, fcol(j, f, meta))),
                      pl.BlockSpec((1, D_MODEL, tf), lambda j, f, meta: (meta[j], 0, nf + fcol(j, f, meta))),
                      pl.BlockSpec((1, tf, D_MODEL), lambda j, f, meta: (meta[j], fcol(j, f, meta), 0))],
            out_specs=pl.BlockSpec((tile, D_MODEL), lambda j, f, meta: (j, 0))),
        out_shape=jax.ShapeDtypeStruct((n_tiles * tile, D_MODEL), F32),
        compiler_params=_params(2), name="moe_experts",
    )(meta, xs, w_gu, w_gu, w_down)


def _combine_kernel(p1_ref, p2_ref, rows_hbm, x_ref, r_ref, g_ref, b_ref, op_ref, os_ref, buf1, buf2, sem1, sem2,
                    *, tm, prompt_tiles):
    _gather_rows(p1_ref, rows_hbm, buf1, sem1, tm)
    _gather_rows(p2_ref, rows_hbm, buf2, sem2, tm)
    route = r_ref[...]
    y = route[:, 0:1] * buf1[...] + route[:, 1:2] * buf2[...]
    res = _layer_norm(ALPHA * x_ref[...] + y, g_ref[...], b_ref[...])

    @pl.when(pl.program_id(0) < prompt_tiles)
    def _():
        op_ref[...] = res

    @pl.when(pl.program_id(0) >= prompt_tiles)
    def _():
        os_ref[...] = res


def _combine(pos1, pos2, rows, x, route, g, b, tm, n_prompt):
    m = x.shape[0]
    pt = n_prompt // tm
    idx = lambda: pl.BlockSpec((1, 1, tm), lambda i: (i, 0, 0), memory_space=pltpu.SMEM)
    vec = pl.BlockSpec((1, D_MODEL), lambda i: (0, 0))
    return pl.pallas_call(
        functools.partial(_combine_kernel, tm=tm, prompt_tiles=pt), grid=(m // tm,),
        in_specs=[idx(), idx(), pl.BlockSpec(memory_space=pl.ANY),
                  pl.BlockSpec((tm, D_MODEL), lambda i: (i, 0)),
                  pl.BlockSpec((tm, LANES), lambda i: (i, 0)), vec, vec],
        out_specs=[pl.BlockSpec((tm, D_MODEL), lambda i: (jnp.minimum(i, pt - 1), 0)),
                   pl.BlockSpec((tm, D_MODEL), lambda i: (jnp.maximum(i - pt, 0), 0))],
        out_shape=[jax.ShapeDtypeStruct((n_prompt, D_MODEL), F32),
                   jax.ShapeDtypeStruct((m - n_prompt, D_MODEL), F32)],
        scratch_shapes=[pltpu.VMEM((tm, D_MODEL), F32), pltpu.VMEM((tm, D_MODEL), F32),
                        pltpu.SemaphoreType.DMA(()), pltpu.SemaphoreType.DMA(())],
        compiler_params=_params(1), name="moe_combine",
    )(pos1.reshape(m // tm, 1, tm), pos2.reshape(m // tm, 1, tm), rows, x, route, g, b)


def _moe(x, n_prompt, w_router, w_gu, w_down, g, b, tm, tile, tf):
    route = _router(x, w_router, tm)
    e1 = route[:, 2].astype(jnp.int32)
    e2 = route[:, 3].astype(jnp.int32)
    meta, row_token, pos1, pos2, n_tiles = _route_plan(e1, e2, tile)
    xs = _dispatch(meta, row_token, x, tile, n_tiles)
    rows = _experts(meta, xs, w_gu, w_down, tile, n_tiles, tf)
    return _combine(pos1, pos2, rows, x, route, g, b, tm, n_prompt)


def kernel(x_prompt, x_sample, cache_kv_w128, cache_kv_w512, cache_kv_w2048, state_ret,
           ln_g, ln_b, rel_bias, w_in_dil, w_out_dil, w_in_ret, ret_gn_g, w_out_ret,
           w_gu_dense, w_down_dense, w_router, w_gu_moe, w_down_moe):
    bsz, seq, _ = x_prompt.shape
    dbsz, t_new, _ = x_sample.shape
    tp = SAMPLE_PAD
    n_p = bsz * seq
    n_s = dbsz * tp
    caches = (cache_kv_w128, cache_kv_w512, cache_kv_w2048)

    w_in_dil_b = w_in_dil.astype(BF16)
    w_out_dil_b = w_out_dil.astype(BF16)
    w_in_ret_b = w_in_ret.astype(BF16)
    w_out_ret_b = w_out_ret.astype(BF16)
    w_gu_dense_b = w_gu_dense.astype(BF16)
    w_down_dense_b = w_down_dense.astype(BF16)
    w_gu_moe_b = w_gu_moe.astype(BF16)
    w_down_moe_b = w_down_moe.astype(BF16)
    lng = ln_g.reshape(DEPTH, 2, 1, D_MODEL)
    lnb = ln_b.reshape(DEPTH, 2, 1, D_MODEL)
    gn = ret_gn_g.reshape(1, RET_V)

    hp = x_prompt.reshape(n_p, D_MODEL)
    hs = jnp.pad(x_sample, ((0, 0), (0, tp - t_new), (0, 0))).reshape(n_s, D_MODEL)

    tm = min(1024, seq)
    qkv_groups, (kv128_p, kv512_p, kv2048_p) = _proj_dil_prompt(hp, w_in_dil_b, bsz, seq, tm)
    outs, lses = [], []
    for g, (window, dil) in enumerate(DIL_GROUPS):
        qkv_g = qkv_groups[g].reshape(bsz * dil, seq // dil, G_COLS)
        tbl = _prompt_table(rel_bias, g, window, dil)
        o, l = _attn_prompt(qkv_g, tbl, min(512, seq // dil))
        outs.append(o.reshape(bsz, dil, seq // dil, A_WIDTH))
        lses.append(l.reshape(bsz, dil, seq // dil, A_WIDTH))
    hp = _merge_out(outs, lses, w_out_dil_b, hp, lng[0, 0], lnb[0, 0], 512)

    qkv_s = _matmul(hs, w_in_dil_b, G_COLS)
    qkv_s3 = qkv_s.reshape(dbsz, tp, N_GROUPS * G_COLS)
    tabs = [_sample_tables(rel_bias, g, window, dil, caches[g].shape[1], t_new)
            for g, (window, dil) in enumerate(DIL_GROUPS)]
    caches2 = [c.reshape(c.shape[0], c.shape[1], 2 * A_WIDTH) for c in caches]
    mixed_s = _attn_sample(qkv_s3, caches2, [t[0] for t in tabs], [t[1] for t in tabs])
    hs = _mm_ln(mixed_s.reshape(n_s, A_WIDTH), w_out_dil_b, hs, lng[0, 0], lnb[0, 0], n_s)
    rows_s = []
    for g in range(N_GROUPS):
        kv = qkv_s3[:, :t_new, g * G_COLS + A_WIDTH:(g + 1) * G_COLS]
        rows_s.append(kv.reshape(dbsz, t_new, 2, H_SLOT, HEAD_DIM))

    hp = _ffn_dense(hp, w_gu_dense_b, w_down_dense_b, lng[0, 1], lnb[0, 1], 512, 1408)
    hs = _ffn_dense(hs, w_gu_dense_b, w_down_dense_b, lng[0, 1], lnb[0, 1], n_s, 1408)

    cos_p, sin_p = _rope_tables(jnp.arange(seq, dtype=jnp.int32))
    pos_s = jnp.tile(PAST_LEN + jnp.arange(tp, dtype=jnp.int32), dbsz)
    cos_s, sin_s = _rope_tables(pos_s)
    proj_p = _proj_ret(hp, w_in_ret_b, cos_p, sin_p, tm, BF16)
    y_p, ret_p = _ret_prompt(proj_p, gn, bsz, seq, min(1024, seq))
    proj_s = _proj_ret(hs, w_in_ret_b, cos_s, sin_s, n_s, F32)
    y_s, ret_s = _ret_sample(proj_s.reshape(dbsz, tp, -1), state_ret, gn, t_new)
    h_all = _mm_ln_pair(y_p, y_s.reshape(n_s, RET_V), w_out_ret_b, hp, hs, lng[1, 0], lnb[1, 0], 512)

    out_p, out_s = _moe(h_all, n_p, w_router, w_gu_moe_b, w_down_moe_b, lng[1, 1], lnb[1, 1], 512, 1024, 896)

    y_prompt = out_p.reshape(bsz, seq, D_MODEL)
    y_sample = out_s.reshape(dbsz, tp, D_MODEL)[:, :t_new]
    shape5 = lambda a: a.reshape(a.shape[0], a.shape[1], 2, H_SLOT, HEAD_DIM)
    return (y_prompt, y_sample, shape5(kv128_p), shape5(kv512_p), shape5(kv2048_p), ret_p,
            rows_s[0], rows_s[1], rows_s[2], ret_s)
```

```python
import functools

import jax
import jax.numpy as jnp
import numpy as np
from jax import lax
from jax.experimental import pallas as pl
from jax.experimental.pallas import tpu as pltpu

F32 = jnp.float32
BF16 = jnp.bfloat16

DEPTH = 2
D_MODEL = 1024
PAST_LEN = 16384
DIL_GROUPS = ((128, 1), (512, 4), (2048, 16))
N_GROUPS = 3
H_SLOT = 8
HEAD_DIM = 64
A_WIDTH = H_SLOT * HEAD_DIM
G_COLS = 3 * A_WIDTH
N_BUCKETS = 32
MAX_DISTANCE = 2048
RET_HEADS = 4
RET_DK = 256
RET_DV = 512
RET_CHUNK = 128
RET_HPS = 2
ROPE_BASE = 10000.0
RET_QK = RET_HEADS * RET_DK
RET_V = RET_HEADS * RET_DV
N_EXPERTS = 8
LN_EPS = 1e-5
GN_EPS = 1e-5
ALPHA = (2 * DEPTH) ** 0.25
NEG = -1e30

LANES = 128
ATT_BLOCK = 128
PERM = 256
SAMPLE_PAD = 16
VMEM_LIMIT = 56 * 1024 * 1024

_NT = (((1,), (1,)), ((), ()))
_TN = (((0,), (0,)), ((), ()))


def _params(n_grid):
    return pltpu.CompilerParams(dimension_semantics=("arbitrary",) * n_grid,
                                vmem_limit_bytes=VMEM_LIMIT)


def _layer_norm(z, g, b):
    mu = jnp.mean(z, axis=-1, keepdims=True)
    zc = z - mu
    var = jnp.mean(zc * zc, axis=-1, keepdims=True)
    return zc * lax.rsqrt(var + LN_EPS) * g + b


def _silu(a):
    return a / (1.0 + jnp.exp(-a))


def _t5_bucket(dist):
    max_exact = N_BUCKETS // 2
    d = np.asarray(dist, dtype=np.int64)
    scaled = np.log(np.maximum(d, 1) / max_exact) / np.log(MAX_DISTANCE / max_exact)
    large = np.minimum(max_exact + (scaled * (N_BUCKETS - max_exact)).astype(np.int32), N_BUCKETS - 1)
    return np.where(d < max_exact, d, large).astype(np.int32)


def _group_bias(rel_bias, g, window, dil):
    n_keys = window // dil + 1
    buckets = jnp.asarray(_t5_bucket(np.arange(n_keys) * dil))
    return rel_bias[buckets][:, g * H_SLOT:(g + 1) * H_SLOT].T.astype(F32)


def _matmul_kernel(x_ref, w_ref, o_ref):
    o_ref[...] = jnp.dot(x_ref[...].astype(BF16), w_ref[...], preferred_element_type=F32)


def _matmul(x, w, tn):
    m, k = x.shape
    n = w.shape[1]
    return pl.pallas_call(
        _matmul_kernel,
        grid=(n // tn,),
        in_specs=[pl.BlockSpec((m, k), lambda j: (0, 0)),
                  pl.BlockSpec((k, tn), lambda j: (0, j))],
        out_specs=pl.BlockSpec((m, tn), lambda j: (0, j)),
        out_shape=jax.ShapeDtypeStruct((m, n), F32),
        compiler_params=_params(1),
        name="matmul_sample",
    )(x, w)


def _deinterleave_matrix(dil):
    p = np.zeros((PERM, PERM), np.float32)
    rows = PERM // dil
    for r in range(dil):
        for m in range(rows):
            p[r * rows + m, m * dil + r] = 1.0
    return p


def _proj_dil_kernel(x_ref, w_ref, p1_ref, p2_ref, q0_ref, q1_ref, q2_ref, kv0_ref, kv1_ref, kv2_ref, xperm,
                     *, tm, tpb, keeps):
    g = pl.program_id(1)
    j = pl.program_id(0) % tpb
    q_refs = (q0_ref, q1_ref, q2_ref)
    kv_refs = (kv0_ref, kv1_ref, kv2_ref)
    perms = (None, p1_ref, p2_ref)

    def store_q(ref, r, row0, n_rows, acc, src0):
        ref[0, r, row0:row0 + n_rows, :A_WIDTH] = (acc[src0:src0 + n_rows, :A_WIDTH] * (HEAD_DIM ** -0.5)).astype(BF16)
        ref[0, r, row0:row0 + n_rows, A_WIDTH:] = acc[src0:src0 + n_rows, A_WIDTH:].astype(BF16)

    for gi, (_, dil) in enumerate(DIL_GROUPS):
        keep = keeps[gi]
        in_tail = (j >= tpb - keep // tm) if keep >= tm else (j == tpb - 1)
        row0 = 0 if keep >= tm else tm - keep

        @pl.when(g == gi)
        def _(gi=gi, dil=dil, in_tail=in_tail, row0=row0):
            xb = x_ref[...].astype(BF16)
            if dil == 1:
                acc = jnp.dot(xb, w_ref[...], preferred_element_type=F32)
                store_q(q_refs[gi], 0, 0, tm, acc, 0)
            else:
                for sub in range(tm // PERM):
                    blk = xb[sub * PERM:(sub + 1) * PERM]
                    xperm[sub * PERM:(sub + 1) * PERM, :] = jnp.dot(
                        perms[gi][...], blk, preferred_element_type=F32).astype(BF16)
                acc = jnp.dot(xperm[...], w_ref[...], preferred_element_type=F32)
                rows = PERM // dil
                for sub in range(tm // PERM):
                    for r in range(dil):
                        store_q(q_refs[gi], r, sub * rows, rows, acc, sub * PERM + r * rows)

            @pl.when(in_tail)
            def _():
                if dil == 1:
                    kv_refs[gi][0] = acc[row0:, A_WIDTH:]
                else:
                    kv_refs[gi][0] = jnp.dot(xb[row0:], w_ref[:, A_WIDTH:], preferred_element_type=F32)


def _proj_dil_prompt(x, w, bsz, seq, tm):
    tpb = seq // tm
    keeps = tuple(min(wd, seq) for wd, _ in DIL_GROUPS)

    def kv_spec(keep):
        if keep >= tm:
            first = tpb - keep // tm
            return pl.BlockSpec((1, tm, 2 * A_WIDTH),
                                lambda i, g: (i // tpb, jnp.maximum(i % tpb - first, 0), 0))
        return pl.BlockSpec((1, keep, 2 * A_WIDTH), lambda i, g: (i // tpb, 0, 0))

    perm = lambda d: jnp.asarray(_deinterleave_matrix(d), BF16)
    const = pl.BlockSpec((PERM, PERM), lambda i, g: (0, 0))
    outs = pl.pallas_call(
        functools.partial(_proj_dil_kernel, tm=tm, tpb=tpb, keeps=keeps),
        grid=(bsz * tpb, N_GROUPS),
        in_specs=[pl.BlockSpec((tm, D_MODEL), lambda i, g: (i, 0)),
                  pl.BlockSpec((D_MODEL, G_COLS), lambda i, g: (0, g)), const, const],
        out_specs=[pl.BlockSpec((1, d, tm // d, G_COLS), lambda i, g: (i // tpb, 0, i % tpb, 0))
                   for _, d in DIL_GROUPS] + [kv_spec(k) for k in keeps],
        out_shape=[jax.ShapeDtypeStruct((bsz, d, seq // d, G_COLS), BF16) for _, d in DIL_GROUPS]
        + [jax.ShapeDtypeStruct((bsz, k, 2 * A_WIDTH), F32) for k in keeps],
        scratch_shapes=[pltpu.VMEM((tm, D_MODEL), BF16)],
        compiler_params=_params(2),
        name="proj_dil_prompt",
    )(x, w, perm(DIL_GROUPS[1][1]), perm(DIL_GROUPS[2][1]))
    return outs[:3], outs[3:]


def _attn_prompt_kernel(q_ref, k_ref, v_ref, kp_ref, vp_ref, tbl_ref, o_ref, lse_ref, kbuf, vbuf, *, tq):
    n = pl.program_id(1)
    kbuf[0:ATT_BLOCK, :] = kp_ref[0]
    kbuf[ATT_BLOCK:, :] = k_ref[0]
    vbuf[0:ATT_BLOCK, :] = vp_ref[0]
    vbuf[ATT_BLOCK:, :] = v_ref[0]
    low = lax.broadcasted_iota(jnp.int32, (ATT_BLOCK, LANES), 1) < HEAD_DIM

    def body(m, carry):
        r0 = pl.multiple_of(m * ATT_BLOCK, ATT_BLOCK)
        first = jnp.where(jnp.logical_and(n == 0, m == 0), 1, 0)
        for p in range(A_WIDTH // LANES):
            cols = slice(p * LANES, (p + 1) * LANES)
            qm = q_ref[0, pl.ds(r0, ATT_BLOCK), cols]
            keys = kbuf[pl.ds(r0, 2 * ATT_BLOCK), cols]
            vals = vbuf[pl.ds(r0, 2 * ATT_BLOCK), cols]
            outs, lses = [], []
            for a in range(2):
                qa = jnp.where(low if a == 0 else jnp.logical_not(low), qm, jnp.zeros_like(qm))
                s = lax.dot_general(qa, keys, _NT, preferred_element_type=F32) + tbl_ref[first, 2 * p + a]
                mx = jnp.max(s, axis=-1, keepdims=True)
                e = jnp.exp(s - mx)
                l = jnp.sum(e, axis=-1, keepdims=True)
                o = jnp.dot(e.astype(BF16), vals, preferred_element_type=F32)
                outs.append(o / l)
                lses.append(jnp.broadcast_to(mx + jnp.log(l), (ATT_BLOCK, LANES)))
            o_ref[0, pl.ds(r0, ATT_BLOCK), cols] = jnp.where(low, outs[0], outs[1]).astype(BF16)
            lse_ref[0, pl.ds(r0, ATT_BLOCK), cols] = jnp.where(low, lses[0], lses[1])
        return carry

    lax.fori_loop(0, tq // ATT_BLOCK, body, 0)


def _attn_prompt(qkv, tbl, tq):
    nb, length, _ = qkv.shape
    sub = tq // ATT_BLOCK
    cur = lambda c: pl.BlockSpec((1, tq, A_WIDTH), lambda s, n: (s, n, c))
    prev = lambda c: pl.BlockSpec((1, ATT_BLOCK, A_WIDTH), lambda s, n: (s, jnp.maximum(n * sub - 1, 0), c))
    return pl.pallas_call(
        functools.partial(_attn_prompt_kernel, tq=tq),
        grid=(nb, length // tq),
        in_specs=[cur(0), cur(1), cur(2), prev(1), prev(2),
                  pl.BlockSpec(tbl.shape, lambda s, n: (0, 0, 0, 0))],
        out_specs=[pl.BlockSpec((1, tq, A_WIDTH), lambda s, n: (s, n, 0))] * 2,
        out_shape=[jax.ShapeDtypeStruct((nb, length, A_WIDTH), BF16),
                   jax.ShapeDtypeStruct((nb, length, A_WIDTH), F32)],
        scratch_shapes=[pltpu.VMEM((tq + ATT_BLOCK, A_WIDTH), BF16)] * 2,
        compiler_params=_params(2),
        name="attn_prompt",
    )(qkv, qkv, qkv, qkv, qkv, tbl)


def _toeplitz(u, n_rows, n_cols):
    h, n = u.shape
    assert n == n_rows + n_cols - 1
    up = jnp.pad(u, ((0, 0), (0, 1)))
    w = jnp.tile(up, (1, n_rows))[:, :n_rows * n].reshape(h, n_rows, n)
    return w[:, :, n_rows - 1:n_rows - 1 + n_cols]


def _bias_by_offset(bias, dist, valid):
    return jnp.where(valid[None], bias[:, np.clip(dist, 0, bias.shape[1] - 1)], NEG)


def _prompt_table(rel_bias, g, window, dil):
    bias = _group_bias(rel_bias, g, window, dil)
    k = np.arange(3 * ATT_BLOCK - 1)
    dist = 2 * ATT_BLOCK - 1 - k
    u = _bias_by_offset(bias, dist, (dist >= 0) & (dist <= window // dil))
    tbl = _toeplitz(u, ATT_BLOCK, 2 * ATT_BLOCK)
    c = np.arange(2 * ATT_BLOCK)[None, None, :]
    tbl_first = jnp.where(c < ATT_BLOCK, NEG, tbl)
    return jnp.stack([tbl, tbl_first], 0)


def _split3(x):
    hi = x.astype(BF16)
    r1 = x - hi.astype(F32)
    mid = r1.astype(BF16)
    lo = (r1 - mid.astype(F32)).astype(BF16)
    return hi, mid, lo


def _merge_out_kernel(o0, o1, o2, l0, l1, l2, pt1_ref, pt2_ref, w_ref, x_ref, g_ref, b_ref, out_ref, *, tm):
    pts = (None, pt1_ref, pt2_ref)

    def natural(ref, gi, exact_f32):
        dil = DIL_GROUPS[gi][1]
        if dil == 1:
            return ref[0, 0].astype(F32)
        rows = PERM // dil
        blocks = []
        for sub in range(tm // PERM):
            piece = jnp.concatenate([ref[0, r, sub * rows:(sub + 1) * rows, :] for r in range(dil)], axis=0)
            parts = _split3(piece) if exact_f32 else (piece,)
            blocks.append(sum(jnp.dot(pts[gi][...], p, preferred_element_type=F32) for p in parts))
        return jnp.concatenate(blocks, axis=0)

    os_ = [natural(r, gi, False) for gi, r in enumerate((o0, o1, o2))]
    ls = [natural(r, gi, True) for gi, r in enumerate((l0, l1, l2))]
    mx = jnp.maximum(jnp.maximum(ls[0], ls[1]), ls[2])
    es = [jnp.exp(l - mx) for l in ls]
    mixed = (es[0] * os_[0] + es[1] * os_[1] + es[2] * os_[2]) / (es[0] + es[1] + es[2])
    acc = jnp.dot(mixed.astype(BF16), w_ref[...], preferred_element_type=F32)
    out_ref[...] = _layer_norm(ALPHA * x_ref[...] + acc, g_ref[...], b_ref[...])


def _merge_out(os_, ls_, w, x, g, b, tm):
    bsz, _, seq, _ = os_[0].shape
    tpb = seq // tm
    grp = lambda d: pl.BlockSpec((1, d, tm // d, A_WIDTH), lambda i: (i // tpb, 0, i % tpb, 0))
    specs = [grp(d) for _, d in DIL_GROUPS]
    pt = lambda d: jnp.asarray(_deinterleave_matrix(d).T, BF16)
    const = pl.BlockSpec((PERM, PERM), lambda i: (0, 0))
    vec = pl.BlockSpec((1, D_MODEL), lambda i: (0, 0))
    return pl.pallas_call(
        functools.partial(_merge_out_kernel, tm=tm), grid=(bsz * tpb,),
        in_specs=specs + specs + [const, const, pl.BlockSpec((A_WIDTH, D_MODEL), lambda i: (0, 0)),
                                  pl.BlockSpec((tm, D_MODEL), lambda i: (i, 0)), vec, vec],
        out_specs=pl.BlockSpec((tm, D_MODEL), lambda i: (i, 0)),
        out_shape=jax.ShapeDtypeStruct((bsz * seq, D_MODEL), F32),
        compiler_params=_params(1), name="merge_outproj_ln",
    )(*os_, *ls_, pt(DIL_GROUPS[1][1]), pt(DIL_GROUPS[2][1]), w, x, g, b)


def _attn_sample_kernel(qkv_ref, c0_ref, c1_ref, c2_ref, tc0, tc1, tc2, tn0, tn1, tn2, out_ref):
    tp = SAMPLE_PAD
    rows = H_SLOT * tp
    head_of_row = lax.broadcasted_iota(jnp.int32, (rows, A_WIDTH), 0) // tp
    head_of_lane = lax.broadcasted_iota(jnp.int32, (rows, A_WIDTH), 1) // HEAD_DIM
    diag = head_of_row == head_of_lane
    qkv = qkv_ref[0]
    scores, values = [], []
    for g, (c_ref, tc, tn) in enumerate(((c0_ref, tc0, tn0), (c1_ref, tc1, tn1), (c2_ref, tc2, tn2))):
        base = g * G_COLS
        q = qkv[:, base:base + A_WIDTH] * (HEAD_DIM ** -0.5)
        q2 = jnp.where(diag, jnp.concatenate([q] * H_SLOT, axis=0), 0.0).astype(BF16)
        kc = c_ref[0, :, :A_WIDTH].astype(BF16)
        vc = c_ref[0, :, A_WIDTH:].astype(BF16)
        zpad = jnp.zeros((LANES - tp, A_WIDTH), F32)
        kn = jnp.concatenate([qkv[:, base + A_WIDTH:base + 2 * A_WIDTH], zpad], axis=0).astype(BF16)
        vn = jnp.concatenate([qkv[:, base + 2 * A_WIDTH:base + 3 * A_WIDTH], zpad], axis=0).astype(BF16)
        scores.append(lax.dot_general(q2, kc, _NT, preferred_element_type=F32) + tc[...])
        values.append(vc)
        scores.append(lax.dot_general(q2, kn, _NT, preferred_element_type=F32) + tn[...])
        values.append(vn)
    mx = functools.reduce(jnp.maximum, [jnp.max(s, axis=-1, keepdims=True) for s in scores])
    den = jnp.zeros((rows, 1), F32)
    acc = jnp.zeros((rows, A_WIDTH), F32)
    for s, v in zip(scores, values):
        e = jnp.exp(s - mx)
        den = den + jnp.sum(e, axis=-1, keepdims=True)
        acc = acc + jnp.dot(e.astype(BF16), v, preferred_element_type=F32)
    r = jnp.where(diag, acc / den, 0.0)
    out = r[0:tp]
    for h in range(1, H_SLOT):
        out = out + r[h * tp:(h + 1) * tp]
    out_ref[0] = out


def _sample_tables(rel_bias, g, window, dil, length, t_new):
    bias = _group_bias(rel_bias, g, window, dil)
    n_keys = window // dil + 1
    tp = SAMPLE_PAD
    live_row = (np.arange(tp) < t_new)[None, :, None]
    k = np.arange(tp + length - 1)
    dist = length + tp - 1 - k
    u = _bias_by_offset(bias, dist // dil, (dist % dil == 0) & (dist // dil < n_keys))
    tc = jnp.where(live_row, _toeplitz(u, tp, length), NEG)
    k2 = np.arange(tp + LANES - 1)
    dist2 = tp - 1 - k2
    u2 = _bias_by_offset(bias, dist2 // dil, (dist2 >= 0) & (dist2 % dil == 0) & (dist2 // dil < n_keys))
    live_col = (np.arange(LANES) < t_new)[None, None, :]
    tn = jnp.where(live_row & live_col, _toeplitz(u2, tp, LANES), NEG)
    return (tc.reshape(H_SLOT * tp, length), tn.reshape(H_SLOT * tp, LANES))


def _attn_sample(qkv, caches, tables_c, tables_n):
    bsz = qkv.shape[0]
    const = lambda a: pl.BlockSpec(a.shape, lambda b: (0, 0))
    return pl.pallas_call(
        _attn_sample_kernel,
        grid=(bsz,),
        in_specs=[pl.BlockSpec((1,) + qkv.shape[1:], lambda b: (b, 0, 0))]
        + [pl.BlockSpec((1,) + c.shape[1:], lambda b: (b, 0, 0)) for c in caches]
        + [const(t) for t in tables_c] + [const(t) for t in tables_n],
        out_specs=pl.BlockSpec((1, SAMPLE_PAD, A_WIDTH), lambda b: (b, 0, 0)),
        out_shape=jax.ShapeDtypeStruct((bsz, SAMPLE_PAD, A_WIDTH), F32),
        compiler_params=_params(1),
        name="attn_sample",
    )(qkv, *caches, *tables_c, *tables_n)


def _mm_ln_kernel(y_ref, w_ref, x_ref, g_ref, b_ref, o_ref):
    acc = jnp.dot(y_ref[...].astype(BF16), w_ref[...], preferred_element_type=F32)
    o_ref[...] = _layer_norm(ALPHA * x_ref[...] + acc, g_ref[...], b_ref[...])


def _mm_ln(y, w, x, g, b, tm):
    m, k = y.shape
    vec = pl.BlockSpec((1, D_MODEL), lambda i: (0, 0))
    return pl.pallas_call(
        _mm_ln_kernel, grid=(m // tm,),
        in_specs=[pl.BlockSpec((tm, k), lambda i: (i, 0)), pl.BlockSpec((k, D_MODEL), lambda i: (0, 0)),
                  pl.BlockSpec((tm, D_MODEL), lambda i: (i, 0)), vec, vec],
        out_specs=pl.BlockSpec((tm, D_MODEL), lambda i: (i, 0)),
        out_shape=jax.ShapeDtypeStruct((m, D_MODEL), F32),
        compiler_params=_params(1), name="outproj_ln",
    )(y, w, x, g, b)


def _mm_ln_pair_kernel(yp_ref, ys_ref, w_ref, xp_ref, xs_ref, g_ref, b_ref, o_ref, *, prompt_tiles):
    def run(y_ref, x_ref):
        acc = jnp.dot(y_ref[...].astype(BF16), w_ref[...], preferred_element_type=F32)
        o_ref[...] = _layer_norm(ALPHA * x_ref[...] + acc, g_ref[...], b_ref[...])

    pl.when(pl.program_id(0) < prompt_tiles)(lambda: run(yp_ref, xp_ref))
    pl.when(pl.program_id(0) >= prompt_tiles)(lambda: run(ys_ref, xs_ref))


def _mm_ln_pair(y_p, y_s, w, x_p, x_s, g, b, tm):
    n_p, k = y_p.shape
    n_s = y_s.shape[0]
    pt = n_p // tm
    first = lambda width: pl.BlockSpec((tm, width), lambda i: (jnp.minimum(i, pt - 1), 0))
    second = lambda width: pl.BlockSpec((tm, width), lambda i: (jnp.maximum(i - pt, 0), 0))
    vec = pl.BlockSpec((1, D_MODEL), lambda i: (0, 0))
    return pl.pallas_call(
        functools.partial(_mm_ln_pair_kernel, prompt_tiles=pt), grid=((n_p + n_s) // tm,),
        in_specs=[first(k), second(k), pl.BlockSpec((k, D_MODEL), lambda i: (0, 0)),
                  first(D_MODEL), second(D_MODEL), vec, vec],
        out_specs=pl.BlockSpec((tm, D_MODEL), lambda i: (i, 0)),
        out_shape=jax.ShapeDtypeStruct((n_p + n_s, D_MODEL), F32),
        compiler_params=_params(1), name="outproj_ln_pair",
    )(y_p, y_s, w, x_p, x_s, g, b)


def _ffn_kernel(x_ref, wg_ref, wu_ref, wd_ref, g_ref, b_ref, o_ref, acc_ref):
    f = pl.program_id(1)
    xb = x_ref[...].astype(BF16)
    a = jnp.dot(xb, wg_ref[...], preferred_element_type=F32)
    u = jnp.dot(xb, wu_ref[...], preferred_element_type=F32)
    part = jnp.dot((_silu(a) * u).astype(BF16), wd_ref[...], preferred_element_type=F32)

    @pl.when(f == 0)
    def _():
        acc_ref[...] = part

    @pl.when(f > 0)
    def _():
        acc_ref[...] += part

    @pl.when(f == pl.num_programs(1) - 1)
    def _():
        o_ref[...] = _layer_norm(ALPHA * x_ref[...] + acc_ref[...], g_ref[...], b_ref[...])


def _ffn_dense(x, w_gu, w_down, g, b, tm, tf):
    m = x.shape[0]
    d_ff = w_down.shape[0]
    nf = d_ff // tf
    vec = pl.BlockSpec((1, D_MODEL), lambda i, f: (0, 0))
    return pl.pallas_call(
        _ffn_kernel, grid=(m // tm, nf),
        in_specs=[pl.BlockSpec((tm, D_MODEL), lambda i, f: (i, 0)),
                  pl.BlockSpec((D_MODEL, tf), lambda i, f: (0, f)),
                  pl.BlockSpec((D_MODEL, tf), lambda i, f: (0, nf + f)),
                  pl.BlockSpec((tf, D_MODEL), lambda i, f: (f, 0)), vec, vec],
        out_specs=pl.BlockSpec((tm, D_MODEL), lambda i, f: (i, 0)),
        out_shape=jax.ShapeDtypeStruct((m, D_MODEL), F32),
        scratch_shapes=[pltpu.VMEM((tm, D_MODEL), F32)],
        compiler_params=_params(2), name="ffn_dense",
    )(x, w_gu, w_gu, w_down, g, b)


def _proj_ret_kernel(x_ref, w_ref, cos_ref, sin_ref, o_ref, *, scale_k):
    n = pl.program_id(1)
    acc = jnp.dot(x_ref[...].astype(BF16), w_ref[...], preferred_element_type=F32)

    @pl.when(n < 2)
    def _():
        cos = cos_ref[...]
        sin = sin_ref[...]
        scale = jnp.where(n == 1, scale_k, 1.0)
        half = RET_DK // 2
        for h in range(RET_HEADS):
            x1 = acc[:, h * RET_DK:h * RET_DK + half]
            x2 = acc[:, h * RET_DK + half:(h + 1) * RET_DK]
            o_ref[:, h * RET_DK:h * RET_DK + half] = ((x1 * cos - x2 * sin) * scale).astype(o_ref.dtype)
            o_ref[:, h * RET_DK + half:(h + 1) * RET_DK] = ((x1 * sin + x2 * cos) * scale).astype(o_ref.dtype)

    @pl.when(n >= 2)
    def _():
        o_ref[...] = acc.astype(o_ref.dtype)


def _proj_ret(x, w, cos, sin, tm, out_dtype):
    m = x.shape[0]
    n_cols = w.shape[1]
    tn = RET_QK
    pos_tiles = cos.shape[0] // tm
    return pl.pallas_call(
        functools.partial(_proj_ret_kernel, scale_k=RET_DK ** -0.5),
        grid=(m // tm, n_cols // tn),
        in_specs=[pl.BlockSpec((tm, D_MODEL), lambda i, n: (i, 0)),
                  pl.BlockSpec((D_MODEL, tn), lambda i, n: (0, n)),
                  pl.BlockSpec((tm, RET_DK // 2), lambda i, n: (i % pos_tiles, 0)),
                  pl.BlockSpec((tm, RET_DK // 2), lambda i, n: (i % pos_tiles, 0))],
        out_specs=pl.BlockSpec((tm, tn), lambda i, n: (i, n)),
        out_shape=jax.ShapeDtypeStruct((m, n_cols), out_dtype),
        compiler_params=_params(2), name="proj_ret",
    )(x, w, cos, sin)


def _rope_tables(pos):
    half = RET_DK // 2
    inv = 1.0 / (ROPE_BASE ** (jnp.arange(half, dtype=F32) / half))
    ang = pos.astype(F32)[:, None] * inv[None]
    return jnp.cos(ang), jnp.sin(ang)


def _log_gamma():
    return jnp.log(1.0 - 2.0 ** (-5.0 - jnp.arange(RET_HEADS, dtype=F32)))


def _decay_tables(c, rows):
    lg = _log_gamma()
    n = jnp.arange(rows, dtype=F32)
    live = n < c
    diff = n[:, None] - n[None, :]
    decay = jnp.where((diff >= 0)[None] & live[None, None, :],
                      jnp.exp(jnp.maximum(diff, 0.0)[None] * lg[:, None, None]), 0.0)
    q_decay = jnp.exp((n[None, :] + 1.0) * lg[:, None])
    k_decay = jnp.where(live[None], jnp.exp((c - 1.0 - n)[None, :] * lg[:, None]), 0.0)
    chunk_decay = jnp.exp(c * lg)
    return decay, q_decay, k_decay, chunk_decay


def _group_norm_gate(o, gate, gn):
    mu = jnp.mean(o, axis=-1, keepdims=True)
    oc = o - mu
    var = jnp.mean(oc * oc, axis=-1, keepdims=True)
    return _silu(gate) * (oc * lax.rsqrt(var + GN_EPS) * gn)


def _ret_prompt_kernel(q_ref, k_ref, v_ref, gate_ref, dec_ref, qd_ref, kd_ref, cd_ref, gn_ref,
                       y_ref, st_ref, s_ref, *, tb):
    cb = pl.program_id(2)

    @pl.when(cb == 0)
    def _():
        s_ref[...] = jnp.zeros_like(s_ref)

    def body(ci, carry):
        r0 = pl.multiple_of(ci * RET_CHUNK, RET_CHUNK)
        rows = pl.ds(r0, RET_CHUNK)
        for j in range(RET_HPS):
            kc = slice(j * RET_DK, (j + 1) * RET_DK)
            vc = slice(j * RET_DV, (j + 1) * RET_DV)
            q = q_ref[0, rows, kc]
            k = k_ref[0, rows, kc]
            v = v_ref[0, rows, vc]
            state = s_ref[j]
            scores = lax.dot_general(q, k, _NT, preferred_element_type=F32) * dec_ref[j]
            inner = jnp.dot(scores.astype(BF16), v, preferred_element_type=F32)
            cross = jnp.dot(q, state.astype(BF16), preferred_element_type=F32) * qd_ref[j]
            kd = (k.astype(F32) * kd_ref[j]).astype(BF16)
            s_ref[j] = cd_ref[j, 0:1, :] * state + lax.dot_general(kd, v, _TN, preferred_element_type=F32)
            y = _group_norm_gate(inner + cross, gate_ref[0, rows, vc].astype(F32), gn_ref[:, vc])
            y_ref[0, rows, vc] = y.astype(y_ref.dtype)
        return carry

    lax.fori_loop(0, tb // RET_CHUNK, body, 0)

    @pl.when(cb == pl.num_programs(2) - 1)
    def _():
        st_ref[0] = s_ref[...]


def _ret_prompt(proj, gn, bsz, seq, tb):
    decay, q_decay, k_decay, chunk_decay = _decay_tables(RET_CHUNK, RET_CHUNK)
    qd = jnp.broadcast_to(q_decay[:, :, None], (RET_HEADS, RET_CHUNK, RET_DV))
    kd = jnp.broadcast_to(k_decay[:, :, None], (RET_HEADS, RET_CHUNK, RET_DK))
    cd = jnp.broadcast_to(chunk_decay[:, None, None], (RET_HEADS, 8, RET_DV))
    proj = proj.reshape(bsz, seq, proj.shape[-1])
    hps = RET_HPS
    k_off = RET_QK // (hps * RET_DK)
    v_off = 2 * RET_QK // (hps * RET_DV)
    g_off = v_off + RET_HEADS // hps
    per_head = lambda shape: pl.BlockSpec((hps,) + shape, lambda b, h, c: (h, 0, 0))
    y, state = pl.pallas_call(
        functools.partial(_ret_prompt_kernel, tb=tb),
        grid=(bsz, RET_HEADS // hps, seq // tb),
        in_specs=[pl.BlockSpec((1, tb, hps * RET_DK), lambda b, h, c: (b, c, h)),
                  pl.BlockSpec((1, tb, hps * RET_DK), lambda b, h, c: (b, c, k_off + h)),
                  pl.BlockSpec((1, tb, hps * RET_DV), lambda b, h, c: (b, c, v_off + h)),
                  pl.BlockSpec((1, tb, hps * RET_DV), lambda b, h, c: (b, c, g_off + h)),
                  per_head((RET_CHUNK, RET_CHUNK)), per_head((RET_CHUNK, RET_DV)),
                  per_head((RET_CHUNK, RET_DK)), per_head((8, RET_DV)),
                  pl.BlockSpec((1, hps * RET_DV), lambda b, h, c: (0, h))],
        out_specs=[pl.BlockSpec((1, tb, hps * RET_DV), lambda b, h, c: (b, c, h)),
                   pl.BlockSpec((1, hps, RET_DK, RET_DV), lambda b, h, c: (b, h, 0, 0))],
        out_shape=[jax.ShapeDtypeStruct((bsz, seq, RET_V), BF16),
                   jax.ShapeDtypeStruct((bsz, RET_HEADS, RET_DK, RET_DV), F32)],
        scratch_shapes=[pltpu.VMEM((hps, RET_DK, RET_DV), F32)],
        compiler_params=_params(3), name="retention_prompt",
    )(proj, proj, proj, proj, decay, qd, kd, cd, gn)
    return y.reshape(bsz * seq, RET_V), state


def _ret_sample_kernel(p_ref, st_ref, dec_ref, qd_ref, kd_ref, cd_ref, gn_ref, y_ref, ns_ref):
    tp = SAMPLE_PAD
    proj = p_ref[0]
    for h in range(RET_HEADS):
        q = proj[:, h * RET_DK:(h + 1) * RET_DK].astype(BF16)
        k = proj[:, RET_QK + h * RET_DK:RET_QK + (h + 1) * RET_DK]
        v = proj[:, 2 * RET_QK + h * RET_DV:2 * RET_QK + (h + 1) * RET_DV]
        gate = proj[:, 2 * RET_QK + RET_V + h * RET_DV:2 * RET_QK + RET_V + (h + 1) * RET_DV]
        zk = jnp.zeros((LANES - tp, RET_DK), F32)
        zv = jnp.zeros((LANES - tp, RET_DV), F32)
        kp = jnp.concatenate([k, zk], axis=0).astype(BF16)
        kdp = jnp.concatenate([k * kd_ref[h], zk], axis=0).astype(BF16)
        vp = jnp.concatenate([v, zv], axis=0).astype(BF16)
        state = st_ref[0, h]
        scores = lax.dot_general(q, kp, _NT, preferred_element_type=F32) * dec_ref[h]
        inner = jnp.dot(scores.astype(BF16), vp, preferred_element_type=F32)
        cross = jnp.dot(q, state.astype(BF16), preferred_element_type=F32) * qd_ref[h]
        ns_ref[0, h] = cd_ref[h, 0:1, :] * state + lax.dot_general(kdp, vp, _TN, preferred_element_type=F32)
        y = _group_norm_gate(inner + cross, gate, gn_ref[:, h * RET_DV:(h + 1) * RET_DV])
        y_ref[0, :, h * RET_DV:(h + 1) * RET_DV] = y


def _ret_sample(proj, state, gn, t_new):
    bsz = proj.shape[0]
    tp = SAMPLE_PAD
    decay, q_decay, k_decay, chunk_decay = _decay_tables(t_new, tp)
    dec = jnp.pad(decay, ((0, 0), (0, 0), (0, LANES - tp)))
    qd = jnp.broadcast_to(q_decay[:, :, None], (RET_HEADS, tp, RET_DV))
    kd = jnp.broadcast_to(k_decay[:, :, None], (RET_HEADS, tp, RET_DK))
    cd = jnp.broadcast_to(chunk_decay[:, None, None], (RET_HEADS, 8, RET_DV))
    const = lambda a: pl.BlockSpec(a.shape, lambda b: (0,) * a.ndim)
    return pl.pallas_call(
        _ret_sample_kernel, grid=(bsz,),
        in_specs=[pl.BlockSpec((1,) + proj.shape[1:], lambda b: (b, 0, 0)),
                  pl.BlockSpec((1,) + state.shape[1:], lambda b: (b, 0, 0, 0)),
                  const(dec), const(qd), const(kd), const(cd), const(gn)],
        out_specs=[pl.BlockSpec((1, tp, RET_V), lambda b: (b, 0, 0)),
                   pl.BlockSpec((1,) + state.shape[1:], lambda b: (b, 0, 0, 0))],
        out_shape=[jax.ShapeDtypeStruct((bsz, tp, RET_V), F32),
                   jax.ShapeDtypeStruct(state.shape, F32)],
        compiler_params=_params(1), name="retention_sample",
    )(proj, state, dec, qd, kd, cd, gn)


def _router_kernel(x_ref, w_ref, o_ref):
    logits = jnp.dot(x_ref[...], w_ref[...], preferred_element_type=F32, precision=lax.Precision.HIGHEST)
    lane = lax.broadcasted_iota(jnp.int32, logits.shape, 1)
    logits = jnp.where(lane < N_EXPERTS, logits, NEG)
    m1 = jnp.max(logits, axis=-1, keepdims=True)
    i1 = jnp.min(jnp.where(logits == m1, lane, LANES), axis=-1, keepdims=True)
    rest = jnp.where(lane == i1, NEG, logits)
    m2 = jnp.max(rest, axis=-1, keepdims=True)
    i2 = jnp.min(jnp.where(rest == m2, lane, LANES), axis=-1, keepdims=True)
    e2 = jnp.exp(m2 - m1)
    den = 1.0 + e2
    o_ref[...] = jnp.where(lane == 0, 1.0 / den,
                           jnp.where(lane == 1, e2 / den,
                                     jnp.where(lane == 2, i1.astype(F32),
                                               jnp.where(lane == 3, i2.astype(F32), 0.0))))


def _router(x, w_router, tm):
    m = x.shape[0]
    w = jnp.pad(w_router, ((0, 0), (0, LANES - N_EXPERTS)))
    return pl.pallas_call(
        _router_kernel, grid=(m // tm,),
        in_specs=[pl.BlockSpec((tm, D_MODEL), lambda i: (i, 0)), pl.BlockSpec((D_MODEL, LANES), lambda i: (0, 0))],
        out_specs=pl.BlockSpec((tm, LANES), lambda i: (i, 0)),
        out_shape=jax.ShapeDtypeStruct((m, LANES), F32),
        compiler_params=_params(1), name="router",
    )(x, w)


def _route_plan(e1, e2, tile):
    n_tok = e1.shape[0]
    e = jnp.concatenate([e1, e2])
    onehot = (e[:, None] == jnp.arange(N_EXPERTS, dtype=jnp.int32)[None]).astype(jnp.int32)
    csum = jnp.cumsum(onehot, axis=0)
    rank = jnp.take_along_axis(csum, e[:, None], axis=1)[:, 0] - 1
    tiles_per_expert = (csum[-1] + tile - 1) // tile
    tile_end = jnp.cumsum(tiles_per_expert)
    pos = ((tile_end - tiles_per_expert) * tile)[e] + rank
    n_tiles = (2 * n_tok) // tile + N_EXPERTS
    tok = jnp.tile(jnp.arange(n_tok, dtype=jnp.int32), 2)
    row_token = jnp.zeros((n_tiles * tile,), jnp.int32).at[pos].set(tok)
    n_used = tile_end[-1]
    tile_ids = jnp.arange(n_tiles, dtype=jnp.int32)
    tile_expert = jnp.minimum(jnp.searchsorted(tile_end, tile_ids, side="right"), N_EXPERTS - 1)
    tile_expert = jnp.where(tile_ids < n_used, tile_expert, tile_expert[n_used - 1])
    meta = jnp.concatenate([tile_expert.astype(jnp.int32), n_used[None].astype(jnp.int32)])
    return meta, row_token.reshape(n_tiles, 1, tile), pos[:n_tok], pos[n_tok:], n_tiles


def _row_copy(src_hbm, dst, sem, src_row, dst_row):
    return pltpu.make_async_copy(src_hbm.at[pl.ds(src_row, 1), :], dst.at[pl.ds(dst_row, 1), :], sem)


def _gather_rows(idx_ref, src_hbm, dst, sem, n_rows):
    def issue(r, carry):
        _row_copy(src_hbm, dst, sem, idx_ref[0, 0, r], r).start()
        return carry

    lax.fori_loop(0, n_rows, issue, 0)
    pltpu.make_async_copy(src_hbm.at[pl.ds(0, n_rows), :], dst, sem).wait()


def _dispatch_kernel(meta_ref, idx_ref, x_hbm, o_ref, buf, sem, *, tile, n_tiles):
    used = pl.program_id(0) < meta_ref[n_tiles]

    @pl.when(used)
    def _():
        _gather_rows(idx_ref, x_hbm, buf, sem, tile)
        o_ref[...] = buf[...].astype(BF16)

    @pl.when(jnp.logical_not(used))
    def _():
        o_ref[...] = jnp.zeros_like(o_ref)


def _dispatch(meta, row_token, x, tile, n_tiles):
    return pl.pallas_call(
        functools.partial(_dispatch_kernel, tile=tile, n_tiles=n_tiles),
        grid_spec=pltpu.PrefetchScalarGridSpec(
            num_scalar_prefetch=1, grid=(n_tiles,),
            in_specs=[pl.BlockSpec((1, 1, tile), lambda j, meta: (j, 0, 0), memory_space=pltpu.SMEM),
                      pl.BlockSpec(memory_space=pl.ANY)],
            out_specs=pl.BlockSpec((tile, D_MODEL), lambda j, meta: (j, 0)),
            scratch_shapes=[pltpu.VMEM((tile, D_MODEL), F32), pltpu.SemaphoreType.DMA(())]),
        out_shape=jax.ShapeDtypeStruct((n_tiles * tile, D_MODEL), BF16),
        compiler_params=_params(1), name="moe_dispatch",
    )(meta, row_token, x)


def _expert_kernel(meta_ref, xs_ref, wg_ref, wu_ref, wd_ref, o_ref, *, n_tiles):
    f = pl.program_id(1)
    used = pl.program_id(0) < meta_ref[n_tiles]

    @pl.when(jnp.logical_and(jnp.logical_not(used), f == 0))
    def _():
        o_ref[...] = jnp.zeros_like(o_ref)

    @pl.when(used)
    def _():
        xb = xs_ref[...]
        a = jnp.dot(xb, wg_ref[0], preferred_element_type=F32)
        u = jnp.dot(xb, wu_ref[0], preferred_element_type=F32)
        part = jnp.dot((_silu(a) * u).astype(BF16), wd_ref[0], preferred_element_type=F32)

        @pl.when(f == 0)
        def _():
            o_ref[...] = part

        @pl.when(f > 0)
        def _():
            o_ref[...] += part


def _experts(meta, xs, w_gu, w_down, tile, n_tiles, tf):
    d_ff = w_down.shape[1]
    nf = d_ff // tf
    fcol = lambda j, f, meta: jnp.where(j < meta[n_tiles], f, nf - 1)
    return pl.pallas_call(
        functools.partial(_expert_kernel, n_tiles=n_tiles),
        grid_spec=pltpu.PrefetchScalarGridSpec(
            num_scalar_prefetch=1, grid=(n_tiles, nf),
            in_specs=[pl.BlockSpec((tile, D_MODEL), lambda j, f, meta: (j, 0)),
                      pl.BlockSpec((1, D_MODEL, tf), lambda j, f, meta: (meta[j], 0, fcol(j, f, meta))),
                      pl.BlockSpec((1, D_MODEL, tf), lambda j, f, meta: (meta[j], 0, nf + fcol(j, f, meta))),
                      pl.BlockSpec((1, tf, D_MODEL), lambda j, f, meta: (meta[j], fcol(j, f, meta), 0))],
            out_specs=pl.BlockSpec((tile, D_MODEL), lambda j, f, meta: (j, 0))),
        out_shape=jax.ShapeDtypeStruct((n_tiles * tile, D_MODEL), F32),
        compiler_params=_params(2), name="moe_experts",
    )(meta, xs, w_gu, w_gu, w_down)


def _combine_kernel(p1_ref, p2_ref, rows_hbm, x_ref, r_ref, g_ref, b_ref, op_ref, os_ref, buf1, buf2, sem1, sem2,
                    *, tm, prompt_tiles):
    _gather_rows(p1_ref, rows_hbm, buf1, sem1, tm)
    _gather_rows(p2_ref, rows_hbm, buf2, sem2, tm)
    route = r_ref[...]
    y = route[:, 0:1] * buf1[...] + route[:, 1:2] * buf2[...]
    res = _layer_norm(ALPHA * x_ref[...] + y, g_ref[...], b_ref[...])

    @pl.when(pl.program_id(0) < prompt_tiles)
    def _():
        op_ref[...] = res

    @pl.when(pl.program_id(0) >= prompt_tiles)
    def _():
        os_ref[...] = res


def _combine(pos1, pos2, rows, x, route, g, b, tm, n_prompt):
    m = x.shape[0]
    pt = n_prompt // tm
    idx = lambda: pl.BlockSpec((1, 1, tm), lambda i: (i, 0, 0), memory_space=pltpu.SMEM)
    vec = pl.BlockSpec((1, D_MODEL), lambda i: (0, 0))
    return pl.pallas_call(
        functools.partial(_combine_kernel, tm=tm, prompt_tiles=pt), grid=(m // tm,),
        in_specs=[idx(), idx(), pl.BlockSpec(memory_space=pl.ANY),
                  pl.BlockSpec((tm, D_MODEL), lambda i: (i, 0)),
                  pl.BlockSpec((tm, LANES), lambda i: (i, 0)), vec, vec],
        out_specs=[pl.BlockSpec((tm, D_MODEL), lambda i: (jnp.minimum(i, pt - 1), 0)),
                   pl.BlockSpec((tm, D_MODEL), lambda i: (jnp.maximum(i - pt, 0), 0))],
        out_shape=[jax.ShapeDtypeStruct((n_prompt, D_MODEL), F32),
                   jax.ShapeDtypeStruct((m - n_prompt, D_MODEL), F32)],
        scratch_shapes=[pltpu.VMEM((tm, D_MODEL), F32), pltpu.VMEM((tm, D_MODEL), F32),
                        pltpu.SemaphoreType.DMA(()), pltpu.SemaphoreType.DMA(())],
        compiler_params=_params(1), name="moe_combine",
    )(pos1.reshape(m // tm, 1, tm), pos2.reshape(m // tm, 1, tm), rows, x, route, g, b)


def _moe(x, n_prompt, w_router, w_gu, w_down, g, b, tm, tile, tf):
    route = _router(x, w_router, tm)
    e1 = route[:, 2].astype(jnp.int32)
    e2 = route[:, 3].astype(jnp.int32)
    meta, row_token, pos1, pos2, n_tiles = _route_plan(e1, e2, tile)
    xs = _dispatch(meta, row_token, x, tile, n_tiles)
    rows = _experts(meta, xs, w_gu, w_down, tile, n_tiles, tf)
    return _combine(pos1, pos2, rows, x, route, g, b, tm, n_prompt)


def kernel(x_prompt, x_sample, cache_kv_w128, cache_kv_w512, cache_kv_w2048, state_ret,
           ln_g, ln_b, rel_bias, w_in_dil, w_out_dil, w_in_ret, ret_gn_g, w_out_ret,
           w_gu_dense, w_down_dense, w_router, w_gu_moe, w_down_moe):
    bsz, seq, _ = x_prompt.shape
    dbsz, t_new, _ = x_sample.shape
    tp = SAMPLE_PAD
    n_p = bsz * seq
    n_s = dbsz * tp
    caches = (cache_kv_w128, cache_kv_w512, cache_kv_w2048)

    w_in_dil_b = w_in_dil.astype(BF16)
    w_out_dil_b = w_out_dil.astype(BF16)
    w_in_ret_b = w_in_ret.astype(BF16)
    w_out_ret_b = w_out_ret.astype(BF16)
    w_gu_dense_b = w_gu_dense.astype(BF16)
    w_down_dense_b = w_down_dense.astype(BF16)
    w_gu_moe_b = w_gu_moe.astype(BF16)
    w_down_moe_b = w_down_moe.astype(BF16)
    lng = ln_g.reshape(DEPTH, 2, 1, D_MODEL)
    lnb = ln_b.reshape(DEPTH, 2, 1, D_MODEL)
    gn = ret_gn_g.reshape(1, RET_V)

    hp = x_prompt.reshape(n_p, D_MODEL)
    hs = jnp.pad(x_sample, ((0, 0), (0, tp - t_new), (0, 0))).reshape(n_s, D_MODEL)

    tm = min(1024, seq)
    qkv_groups, (kv128_p, kv512_p, kv2048_p) = _proj_dil_prompt(hp, w_in_dil_b, bsz, seq, tm)
    outs, lses = [], []
    for g, (window, dil) in enumerate(DIL_GROUPS):
        qkv_g = qkv_groups[g].reshape(bsz * dil, seq // dil, G_COLS)
        tbl = _prompt_table(rel_bias, g, window, dil)
        o, l = _attn_prompt(qkv_g, tbl, min(512, seq // dil))
        outs.append(o.reshape(bsz, dil, seq // dil, A_WIDTH))
        lses.append(l.reshape(bsz, dil, seq // dil, A_WIDTH))
    hp = _merge_out(outs, lses, w_out_dil_b, hp, lng[0, 0], lnb[0, 0], 512)

    qkv_s = _matmul(hs, w_in_dil_b, G_COLS)
    qkv_s3 = qkv_s.reshape(dbsz, tp, N_GROUPS * G_COLS)
    tabs = [_sample_tables(rel_bias, g, window, dil, caches[g].shape[1], t_new)
            for g, (window, dil) in enumerate(DIL_GROUPS)]
    caches2 = [c.reshape(c.shape[0], c.shape[1], 2 * A_WIDTH) for c in caches]
    mixed_s = _attn_sample(qkv_s3, caches2, [t[0] for t in tabs], [t[1] for t in tabs])
    hs = _mm_ln(mixed_s.reshape(n_s, A_WIDTH), w_out_dil_b, hs, lng[0, 0], lnb[0, 0], n_s)
    rows_s = []
    for g in range(N_GROUPS):
        kv = qkv_s3[:, :t_new, g * G_COLS + A_WIDTH:(g + 1) * G_COLS]
        rows_s.append(kv.reshape(dbsz, t_new, 2, H_SLOT, HEAD_DIM))

    hp = _ffn_dense(hp, w_gu_dense_b, w_down_dense_b, lng[0, 1], lnb[0, 1], 512, 1408)
    hs = _ffn_dense(hs, w_gu_dense_b, w_down_dense_b, lng[0, 1], lnb[0, 1], n_s, 1408)

    cos_p, sin_p = _rope_tables(jnp.arange(seq, dtype=jnp.int32))
    pos_s = jnp.tile(PAST_LEN + jnp.arange(tp, dtype=jnp.int32), dbsz)
    cos_s, sin_s = _rope_tables(pos_s)
    proj_p = _proj_ret(hp, w_in_ret_b, cos_p, sin_p, tm, BF16)
    y_p, ret_p = _ret_prompt(proj_p, gn, bsz, seq, min(1024, seq))
    proj_s = _proj_ret(hs, w_in_ret_b, cos_s, sin_s, n_s, F32)
    y_s, ret_s = _ret_sample(proj_s.reshape(dbsz, tp, -1), state_ret, gn, t_new)
    h_all = _mm_ln_pair(y_p, y_s.reshape(n_s, RET_V), w_out_ret_b, hp, hs, lng[1, 0], lnb[1, 0], 512)

    out_p, out_s = _moe(h_all, n_p, w_router, w_gu_moe_b, w_down_moe_b, lng[1, 1], lnb[1, 1], 512, 1024, 896)

    y_prompt = out_p.reshape(bsz, seq, D_MODEL)
    y_sample = out_s.reshape(dbsz, tp, D_MODEL)[:, :t_new]
    shape5 = lambda a: a.reshape(a.shape[0], a.shape[1], 2, H_SLOT, HEAD_DIM)
    return (y_prompt, y_sample, shape5(kv128_p), shape5(kv512_p), shape5(kv2048_p), ret_p,
            rows_s[0], rows_s[1], rows_s[2], ret_s)
```

```python
import functools

import jax
import jax.numpy as jnp
import numpy as np
from jax import lax
from jax.experimental import pallas as pl
from jax.experimental.pallas import tpu as pltpu

F32 = jnp.float32
BF16 = jnp.bfloat16

DEPTH = 2
D_MODEL = 1024
PAST_LEN = 16384
DIL_GROUPS = ((128, 1), (512, 4), (2048, 16))
N_GROUPS = 3
H_SLOT = 8
HEAD_DIM = 64
A_WIDTH = H_SLOT * HEAD_DIM
G_COLS = 3 * A_WIDTH
N_BUCKETS = 32
MAX_DISTANCE = 2048
RET_HEADS = 4
RET_DK = 256
RET_DV = 512
RET_CHUNK = 128
RET_HPS = 2
ROPE_BASE = 10000.0
RET_QK = RET_HEADS * RET_DK
RET_V = RET_HEADS * RET_DV
N_EXPERTS = 8
LN_EPS = 1e-5
GN_EPS = 1e-5
ALPHA = (2 * DEPTH) ** 0.25
NEG = -1e30

LANES = 128
ATT_BLOCK = 128
PERM = 256
SAMPLE_PAD = 16
VMEM_LIMIT = 56 * 1024 * 1024

_NT = (((1,), (1,)), ((), ()))
_TN = (((0,), (0,)), ((), ()))


def _params(n_grid):
    return pltpu.CompilerParams(dimension_semantics=("arbitrary",) * n_grid,
                                vmem_limit_bytes=VMEM_LIMIT)


def _layer_norm(z, g, b):
    mu = jnp.mean(z, axis=-1, keepdims=True)
    zc = z - mu
    var = jnp.mean(zc * zc, axis=-1, keepdims=True)
    return zc * lax.rsqrt(var + LN_EPS) * g + b


def _silu(a):
    return a / (1.0 + jnp.exp(-a))


def _t5_bucket(dist):
    max_exact = N_BUCKETS // 2
    d = np.asarray(dist, dtype=np.int64)
    scaled = np.log(np.maximum(d, 1) / max_exact) / np.log(MAX_DISTANCE / max_exact)
    large = np.minimum(max_exact + (scaled * (N_BUCKETS - max_exact)).astype(np.int32), N_BUCKETS - 1)
    return np.where(d < max_exact, d, large).astype(np.int32)


def _group_bias(rel_bias, g, window, dil):
    n_keys = window // dil + 1
    buckets = jnp.asarray(_t5_bucket(np.arange(n_keys) * dil))
    return rel_bias[buckets][:, g * H_SLOT:(g + 1) * H_SLOT].T.astype(F32)


def _matmul_kernel(x_ref, w_ref, o_ref):
    o_ref[...] = jnp.dot(x_ref[...].astype(BF16), w_ref[...], preferred_element_type=F32)


def _matmul(x, w, tn):
    m, k = x.shape
    n = w.shape[1]
    return pl.pallas_call(
        _matmul_kernel,
        grid=(n // tn,),
        in_specs=[pl.BlockSpec((m, k), lambda j: (0, 0)),
                  pl.BlockSpec((k, tn), lambda j: (0, j))],
        out_specs=pl.BlockSpec((m, tn), lambda j: (0, j)),
        out_shape=jax.ShapeDtypeStruct((m, n), F32),
        compiler_params=_params(1),
        name="matmul_sample",
    )(x, w)


def _deinterleave_matrix(dil):
    p = np.zeros((PERM, PERM), np.float32)
    rows = PERM // dil
    for r in range(dil):
        for m in range(rows):
            p[r * rows + m, m * dil + r] = 1.0
    return p


def _proj_dil_kernel(x_ref, w_ref, p1_ref, p2_ref, q0_ref, q1_ref, q2_ref, kv0_ref, kv1_ref, kv2_ref, xperm,
                     *, tm, tpb, keeps):
    g = pl.program_id(1)
    j = pl.program_id(0) % tpb
    q_refs = (q0_ref, q1_ref, q2_ref)
    kv_refs = (kv0_ref, kv1_ref, kv2_ref)
    perms = (None, p1_ref, p2_ref)

    def store_q(ref, r, row0, n_rows, acc, src0):
        ref[0, r, row0:row0 + n_rows, :A_WIDTH] = (acc[src0:src0 + n_rows, :A_WIDTH] * (HEAD_DIM ** -0.5)).astype(BF16)
        ref[0, r, row0:row0 + n_rows, A_WIDTH:] = acc[src0:src0 + n_rows, A_WIDTH:].astype(BF16)

    for gi, (_, dil) in enumerate(DIL_GROUPS):
        keep = keeps[gi]
        in_tail = (j >= tpb - keep // tm) if keep >= tm else (j == tpb - 1)
        row0 = 0 if keep >= tm else tm - keep

        @pl.when(g == gi)
        def _(gi=gi, dil=dil, in_tail=in_tail, row0=row0):
            xb = x_ref[...].astype(BF16)
            if dil == 1:
                acc = jnp.dot(xb, w_ref[...], preferred_element_type=F32)
                store_q(q_refs[gi], 0, 0, tm, acc, 0)
            else:
                for sub in range(tm // PERM):
                    blk = xb[sub * PERM:(sub + 1) * PERM]
                    xperm[sub * PERM:(sub + 1) * PERM, :] = jnp.dot(
                        perms[gi][...], blk, preferred_element_type=F32).astype(BF16)
                acc = jnp.dot(xperm[...], w_ref[...], preferred_element_type=F32)
                rows = PERM // dil
                for sub in range(tm // PERM):
                    for r in range(dil):
                        store_q(q_refs[gi], r, sub * rows, rows, acc, sub * PERM + r * rows)

            @pl.when(in_tail)
            def _():
                if dil == 1:
                    kv_refs[gi][0] = acc[row0:, A_WIDTH:]
                else:
                    kv_refs[gi][0] = jnp.dot(xb[row0:], w_ref[:, A_WIDTH:], preferred_element_type=F32)


def _proj_dil_prompt(x, w, bsz, seq, tm):
    tpb = seq // tm
    keeps = tuple(min(wd, seq) for wd, _ in DIL_GROUPS)

    def kv_spec(keep):
        if keep >= tm:
            first = tpb - keep // tm
            return pl.BlockSpec((1, tm, 2 * A_WIDTH),
                                lambda i, g: (i // tpb, jnp.maximum(i % tpb - first, 0), 0))
        return pl.BlockSpec((1, keep, 2 * A_WIDTH), lambda i, g: (i // tpb, 0, 0))

    perm = lambda d: jnp.asarray(_deinterleave_matrix(d), BF16)
    const = pl.BlockSpec((PERM, PERM), lambda i, g: (0, 0))
    outs = pl.pallas_call(
        functools.partial(_proj_dil_kernel, tm=tm, tpb=tpb, keeps=keeps),
        grid=(bsz * tpb, N_GROUPS),
        in_specs=[pl.BlockSpec((tm, D_MODEL), lambda i, g: (i, 0)),
                  pl.BlockSpec((D_MODEL, G_COLS), lambda i, g: (0, g)), const, const],
        out_specs=[pl.BlockSpec((1, d, tm // d, G_COLS), lambda i, g: (i // tpb, 0, i % tpb, 0))
                   for _, d in DIL_GROUPS] + [kv_spec(k) for k in keeps],
        out_shape=[jax.ShapeDtypeStruct((bsz, d, seq // d, G_COLS), BF16) for _, d in DIL_GROUPS]
        + [jax.ShapeDtypeStruct((bsz, k, 2 * A_WIDTH), F32) for k in keeps],
        scratch_shapes=[pltpu.VMEM((tm, D_MODEL), BF16)],
        compiler_params=_params(2),
        name="proj_dil_prompt",
    )(x, w, perm(DIL_GROUPS[1][1]), perm(DIL_GROUPS[2][1]))
    return outs[:3], outs[3:]


def _attn_prompt_kernel(q_ref, k_ref, v_ref, kp_ref, vp_ref, tbl_ref, o_ref, lse_ref, kbuf, vbuf, *, tq):
    n = pl.program_id(1)
    kbuf[0:ATT_BLOCK, :] = kp_ref[0]
    kbuf[ATT_BLOCK:, :] = k_ref[0]
    vbuf[0:ATT_BLOCK, :] = vp_ref[0]
    vbuf[ATT_BLOCK:, :] = v_ref[0]
    low = lax.broadcasted_iota(jnp.int32, (ATT_BLOCK, LANES), 1) < HEAD_DIM

    def body(m, carry):
        r0 = pl.multiple_of(m * ATT_BLOCK, ATT_BLOCK)
        first = jnp.where(jnp.logical_and(n == 0, m == 0), 1, 0)
        for p in range(A_WIDTH // LANES):
            cols = slice(p * LANES, (p + 1) * LANES)
            qm = q_ref[0, pl.ds(r0, ATT_BLOCK), cols]
            keys = kbuf[pl.ds(r0, 2 * ATT_BLOCK), cols]
            vals = vbuf[pl.ds(r0, 2 * ATT_BLOCK), cols]
            outs, lses = [], []
            for a in range(2):
                qa = jnp.where(low if a == 0 else jnp.logical_not(low), qm, jnp.zeros_like(qm))
                s = lax.dot_general(qa, keys, _NT, preferred_element_type=F32) + tbl_ref[first, 2 * p + a]
                mx = jnp.max(s, axis=-1, keepdims=True)
                e = jnp.exp(s - mx)
                l = jnp.sum(e, axis=-1, keepdims=True)
                o = jnp.dot(e.astype(BF16), vals, preferred_element_type=F32)
                outs.append(o / l)
                lses.append(jnp.broadcast_to(mx + jnp.log(l), (ATT_BLOCK, LANES)))
            o_ref[0, pl.ds(r0, ATT_BLOCK), cols] = jnp.where(low, outs[0], outs[1]).astype(BF16)
            lse_ref[0, pl.ds(r0, ATT_BLOCK), cols] = jnp.where(low, lses[0], lses[1])
        return carry

    lax.fori_loop(0, tq // ATT_BLOCK, body, 0)


def _attn_prompt(qkv, tbl, tq):
    nb, length, _ = qkv.shape
    sub = tq // ATT_BLOCK
    cur = lambda c: pl.BlockSpec((1, tq, A_WIDTH), lambda s, n: (s, n, c))
    prev = lambda c: pl.BlockSpec((1, ATT_BLOCK, A_WIDTH), lambda s, n: (s, jnp.maximum(n * sub - 1, 0), c))
    return pl.pallas_call(
        functools.partial(_attn_prompt_kernel, tq=tq),
        grid=(nb, length // tq),
        in_specs=[cur(0), cur(1), cur(2), prev(1), prev(2),
                  pl.BlockSpec(tbl.shape, lambda s, n: (0, 0, 0, 0))],
        out_specs=[pl.BlockSpec((1, tq, A_WIDTH), lambda s, n: (s, n, 0))] * 2,
        out_shape=[jax.ShapeDtypeStruct((nb, length, A_WIDTH), BF16),
                   jax.ShapeDtypeStruct((nb, length, A_WIDTH), F32)],
        scratch_shapes=[pltpu.VMEM((tq + ATT_BLOCK, A_WIDTH), BF16)] * 2,
        compiler_params=_params(2),
        name="attn_prompt",
    )(qkv, qkv, qkv, qkv, qkv, tbl)


def _toeplitz(u, n_rows, n_cols):
    h, n = u.shape
    assert n == n_rows + n_cols - 1
    up = jnp.pad(u, ((0, 0), (0, 1)))
    w = jnp.tile(up, (1, n_rows))[:, :n_rows * n].reshape(h, n_rows, n)
    return w[:, :, n_rows - 1:n_rows - 1 + n_cols]


def _bias_by_offset(bias, dist, valid):
    return jnp.where(valid[None], bias[:, np.clip(dist, 0, bias.shape[1] - 1)], NEG)


def _prompt_table(rel_bias, g, window, dil):
    bias = _group_bias(rel_bias, g, window, dil)
    k = np.arange(3 * ATT_BLOCK - 1)
    dist = 2 * ATT_BLOCK - 1 - k
    u = _bias_by_offset(bias, dist, (dist >= 0) & (dist <= window // dil))
    tbl = _toeplitz(u, ATT_BLOCK, 2 * ATT_BLOCK)
    c = np.arange(2 * ATT_BLOCK)[None, None, :]
    tbl_first = jnp.where(c < ATT_BLOCK, NEG, tbl)
    return jnp.stack([tbl, tbl_first], 0)


def _split3(x):
    hi = x.astype(BF16)
    r1 = x - hi.astype(F32)
    mid = r1.astype(BF16)
    lo = (r1 - mid.astype(F32)).astype(BF16)
    return hi, mid, lo


def _merge_out_kernel(o0, o1, o2, l0, l1, l2, pt1_ref, pt2_ref, w_ref, x_ref, g_ref, b_ref, out_ref, *, tm):
    pts = (None, pt1_ref, pt2_ref)

    def natural(ref, gi, exact_f32):
        dil = DIL_GROUPS[gi][1]
        if dil == 1:
            return ref[0, 0].astype(F32)
        rows = PERM // dil
        blocks = []
        for sub in range(tm // PERM):
            piece = jnp.concatenate([ref[0, r, sub * rows:(sub + 1) * rows, :] for r in range(dil)], axis=0)
            parts = _split3(piece) if exact_f32 else (piece,)
            blocks.append(sum(jnp.dot(pts[gi][...], p, preferred_element_type=F32) for p in parts))
        return jnp.concatenate(blocks, axis=0)

    os_ = [natural(r, gi, False) for gi, r in enumerate((o0, o1, o2))]
    ls = [natural(r, gi, True) for gi, r in enumerate((l0, l1, l2))]
    mx = jnp.maximum(jnp.maximum(ls[0], ls[1]), ls[2])
    es = [jnp.exp(l - mx) for l in ls]
    mixed = (es[0] * os_[0] + es[1] * os_[1] + es[2] * os_[2]) / (es[0] + es[1] + es[2])
    acc = jnp.dot(mixed.astype(BF16), w_ref[...], preferred_element_type=F32)
    out_ref[...] = _layer_norm(ALPHA * x_ref[...] + acc, g_ref[...], b_ref[...])


def _merge_out(os_, ls_, w, x, g, b, tm):
    bsz, _, seq, _ = os_[0].shape
    tpb = seq // tm
    grp = lambda d: pl.BlockSpec((1, d, tm // d, A_WIDTH), lambda i: (i // tpb, 0, i % tpb, 0))
    specs = [grp(d) for _, d in DIL_GROUPS]
    pt = lambda d: jnp.asarray(_deinterleave_matrix(d).T, BF16)
    const = pl.BlockSpec((PERM, PERM), lambda i: (0, 0))
    vec = pl.BlockSpec((1, D_MODEL), lambda i: (0, 0))
    return pl.pallas_call(
        functools.partial(_merge_out_kernel, tm=tm), grid=(bsz * tpb,),
        in_specs=specs + specs + [const, const, pl.BlockSpec((A_WIDTH, D_MODEL), lambda i: (0, 0)),
                                  pl.BlockSpec((tm, D_MODEL), lambda i: (i, 0)), vec, vec],
        out_specs=pl.BlockSpec((tm, D_MODEL), lambda i: (i, 0)),
        out_shape=jax.ShapeDtypeStruct((bsz * seq, D_MODEL), F32),
        compiler_params=_params(1), name="merge_outproj_ln",
    )(*os_, *ls_, pt(DIL_GROUPS[1][1]), pt(DIL_GROUPS[2][1]), w, x, g, b)


def _attn_sample_kernel(qkv_ref, c0_ref, c1_ref, c2_ref, tc0, tc1, tc2, tn0, tn1, tn2, out_ref):
    tp = SAMPLE_PAD
    rows = H_SLOT * tp
    head_of_row = lax.broadcasted_iota(jnp.int32, (rows, A_WIDTH), 0) // tp
    head_of_lane = lax.broadcasted_iota(jnp.int32, (rows, A_WIDTH), 1) // HEAD_DIM
    diag = head_of_row == head_of_lane
    qkv = qkv_ref[0]
    scores, values = [], []
    for g, (c_ref, tc, tn) in enumerate(((c0_ref, tc0, tn0), (c1_ref, tc1, tn1), (c2_ref, tc2, tn2))):
        base = g * G_COLS
        q = qkv[:, base:base + A_WIDTH] * (HEAD_DIM ** -0.5)
        q2 = jnp.where(diag, jnp.concatenate([q] * H_SLOT, axis=0), 0.0).astype(BF16)
        kc = c_ref[0, :, :A_WIDTH].astype(BF16)
        vc = c_ref[0, :, A_WIDTH:].astype(BF16)
        zpad = jnp.zeros((LANES - tp, A_WIDTH), F32)
        kn = jnp.concatenate([qkv[:, base + A_WIDTH:base + 2 * A_WIDTH], zpad], axis=0).astype(BF16)
        vn = jnp.concatenate([qkv[:, base + 2 * A_WIDTH:base + 3 * A_WIDTH], zpad], axis=0).astype(BF16)
        scores.append(lax.dot_general(q2, kc, _NT, preferred_element_type=F32) + tc[...])
        values.append(vc)
        scores.append(lax.dot_general(q2, kn, _NT, preferred_element_type=F32) + tn[...])
        values.append(vn)
    mx = functools.reduce(jnp.maximum, [jnp.max(s, axis=-1, keepdims=True) for s in scores])
    den = jnp.zeros((rows, 1), F32)
    acc = jnp.zeros((rows, A_WIDTH), F32)
    for s, v in zip(scores, values):
        e = jnp.exp(s - mx)
        den = den + jnp.sum(e, axis=-1, keepdims=True)
        acc = acc + jnp.dot(e.astype(BF16), v, preferred_element_type=F32)
    r = jnp.where(diag, acc / den, 0.0)
    out = r[0:tp]
    for h in range(1, H_SLOT):
        out = out + r[h * tp:(h + 1) * tp]
    out_ref[0] = out


def _sample_tables(rel_bias, g, window, dil, length, t_new):
    bias = _group_bias(rel_bias, g, window, dil)
    n_keys = window // dil + 1
    tp = SAMPLE_PAD
    live_row = (np.arange(tp) < t_new)[None, :, None]
    k = np.arange(tp + length - 1)
    dist = length + tp - 1 - k
    u = _bias_by_offset(bias, dist // dil, (dist % dil == 0) & (dist // dil < n_keys))
    tc = jnp.where(live_row, _toeplitz(u, tp, length), NEG)
    k2 = np.arange(tp + LANES - 1)
    dist2 = tp - 1 - k2
    u2 = _bias_by_offset(bias, dist2 // dil, (dist2 >= 0) & (dist2 % dil == 0) & (dist2 // dil < n_keys))
    live_col = (np.arange(LANES) < t_new)[None, None, :]
    tn = jnp.where(live_row & live_col, _toeplitz(u2, tp, LANES), NEG)
    return (tc.reshape(H_SLOT * tp, length), tn.reshape(H_SLOT * tp, LANES))


def _attn_sample(qkv, caches, tables_c, tables_n):
    bsz = qkv.shape[0]
    const = lambda a: pl.BlockSpec(a.shape, lambda b: (0, 0))
    return pl.pallas_call(
        _attn_sample_kernel,
        grid=(bsz,),
        in_specs=[pl.BlockSpec((1,) + qkv.shape[1:], lambda b: (b, 0, 0))]
        + [pl.BlockSpec((1,) + c.shape[1:], lambda b: (b, 0, 0)) for c in caches]
        + [const(t) for t in tables_c] + [const(t) for t in tables_n],
        out_specs=pl.BlockSpec((1, SAMPLE_PAD, A_WIDTH), lambda b: (b, 0, 0)),
        out_shape=jax.ShapeDtypeStruct((bsz, SAMPLE_PAD, A_WIDTH), F32),
        compiler_params=_params(1),
        name="attn_sample",
    )(qkv, *caches, *tables_c, *tables_n)


def _mm_ln_kernel(y_ref, w_ref, x_ref, g_ref, b_ref, o_ref):
    acc = jnp.dot(y_ref[...].astype(BF16), w_ref[...], preferred_element_type=F32)
    o_ref[...] = _layer_norm(ALPHA * x_ref[...] + acc, g_ref[...], b_ref[...])


def _mm_ln(y, w, x, g, b, tm):
    m, k = y.shape
    vec = pl.BlockSpec((1, D_MODEL), lambda i: (0, 0))
    return pl.pallas_call(
        _mm_ln_kernel, grid=(m // tm,),
        in_specs=[pl.BlockSpec((tm, k), lambda i: (i, 0)), pl.BlockSpec((k, D_MODEL), lambda i: (0, 0)),
                  pl.BlockSpec((tm, D_MODEL), lambda i: (i, 0)), vec, vec],
        out_specs=pl.BlockSpec((tm, D_MODEL), lambda i: (i, 0)),
        out_shape=jax.ShapeDtypeStruct((m, D_MODEL), F32),
        compiler_params=_params(1), name="outproj_ln",
    )(y, w, x, g, b)


def _mm_ln_pair_kernel(yp_ref, ys_ref, w_ref, xp_ref, xs_ref, g_ref, b_ref, o_ref, *, prompt_tiles):
    def run(y_ref, x_ref):
        acc = jnp.dot(y_ref[...].astype(BF16), w_ref[...], preferred_element_type=F32)
        o_ref[...] = _layer_norm(ALPHA * x_ref[...] + acc, g_ref[...], b_ref[...])

    pl.when(pl.program_id(0) < prompt_tiles)(lambda: run(yp_ref, xp_ref))
    pl.when(pl.program_id(0) >= prompt_tiles)(lambda: run(ys_ref, xs_ref))


def _mm_ln_pair(y_p, y_s, w, x_p, x_s, g, b, tm):
    n_p, k = y_p.shape
    n_s = y_s.shape[0]
    pt = n_p // tm
    first = lambda width: pl.BlockSpec((tm, width), lambda i: (jnp.minimum(i, pt - 1), 0))
    second = lambda width: pl.BlockSpec((tm, width), lambda i: (jnp.maximum(i - pt, 0), 0))
    vec = pl.BlockSpec((1, D_MODEL), lambda i: (0, 0))
    return pl.pallas_call(
        functools.partial(_mm_ln_pair_kernel, prompt_tiles=pt), grid=((n_p + n_s) // tm,),
        in_specs=[first(k), second(k), pl.BlockSpec((k, D_MODEL), lambda i: (0, 0)),
                  first(D_MODEL), second(D_MODEL), vec, vec],
        out_specs=pl.BlockSpec((tm, D_MODEL), lambda i: (i, 0)),
        out_shape=jax.ShapeDtypeStruct((n_p + n_s, D_MODEL), F32),
        compiler_params=_params(1), name="outproj_ln_pair",
    )(y_p, y_s, w, x_p, x_s, g, b)


def _ffn_kernel(x_ref, wg_ref, wu_ref, wd_ref, g_ref, b_ref, o_ref, acc_ref):
    f = pl.program_id(1)
    xb = x_ref[...].astype(BF16)
    a = jnp.dot(xb, wg_ref[...], preferred_element_type=F32)
    u = jnp.dot(xb, wu_ref[...], preferred_element_type=F32)
    part = jnp.dot((_silu(a) * u).astype(BF16), wd_ref[...], preferred_element_type=F32)

    @pl.when(f == 0)
    def _():
        acc_ref[...] = part

    @pl.when(f > 0)
    def _():
        acc_ref[...] += part

    @pl.when(f == pl.num_programs(1) - 1)
    def _():
        o_ref[...] = _layer_norm(ALPHA * x_ref[...] + acc_ref[...], g_ref[...], b_ref[...])


def _ffn_dense(x, w_gu, w_down, g, b, tm, tf):
    m = x.shape[0]
    d_ff = w_down.shape[0]
    nf = d_ff // tf
    vec = pl.BlockSpec((1, D_MODEL), lambda i, f: (0, 0))
    return pl.pallas_call(
        _ffn_kernel, grid=(m // tm, nf),
        in_specs=[pl.BlockSpec((tm, D_MODEL), lambda i, f: (i, 0)),
                  pl.BlockSpec((D_MODEL, tf), lambda i, f: (0, f)),
                  pl.BlockSpec((D_MODEL, tf), lambda i, f: (0, nf + f)),
                  pl.BlockSpec((tf, D_MODEL), lambda i, f: (f, 0)), vec, vec],
        out_specs=pl.BlockSpec((tm, D_MODEL), lambda i, f: (i, 0)),
        out_shape=jax.ShapeDtypeStruct((m, D_MODEL), F32),
        scratch_shapes=[pltpu.VMEM((tm, D_MODEL), F32)],
        compiler_params=_params(2), name="ffn_dense",
    )(x, w_gu, w_gu, w_down, g, b)


def _proj_ret_kernel(x_ref, w_ref, cos_ref, sin_ref, o_ref, *, scale_k):
    n = pl.program_id(1)
    acc = jnp.dot(x_ref[...].astype(BF16), w_ref[...], preferred_element_type=F32)

    @pl.when(n < 2)
    def _():
        cos = cos_ref[...]
        sin = sin_ref[...]
        scale = jnp.where(n == 1, scale_k, 1.0)
        half = RET_DK // 2
        for h in range(RET_HEADS):
            x1 = acc[:, h * RET_DK:h * RET_DK + half]
            x2 = acc[:, h * RET_DK + half:(h + 1) * RET_DK]
            o_ref[:, h * RET_DK:h * RET_DK + half] = ((x1 * cos - x2 * sin) * scale).astype(o_ref.dtype)
            o_ref[:, h * RET_DK + half:(h + 1) * RET_DK] = ((x1 * sin + x2 * cos) * scale).astype(o_ref.dtype)

    @pl.when(n >= 2)
    def _():
        o_ref[...] = acc.astype(o_ref.dtype)


def _proj_ret(x, w, cos, sin, tm, out_dtype):
    m = x.shape[0]
    n_cols = w.shape[1]
    tn = RET_QK
    pos_tiles = cos.shape[0] // tm
    return pl.pallas_call(
        functools.partial(_proj_ret_kernel, scale_k=RET_DK ** -0.5),
        grid=(m // tm, n_cols // tn),
        in_specs=[pl.BlockSpec((tm, D_MODEL), lambda i, n: (i, 0)),
                  pl.BlockSpec((D_MODEL, tn), lambda i, n: (0, n)),
                  pl.BlockSpec((tm, RET_DK // 2), lambda i, n: (i % pos_tiles, 0)),
                  pl.BlockSpec((tm, RET_DK // 2), lambda i, n: (i % pos_tiles, 0))],
        out_specs=pl.BlockSpec((tm, tn), lambda i, n: (i, n)),
        out_shape=jax.ShapeDtypeStruct((m, n_cols), out_dtype),
        compiler_params=_params(2), name="proj_ret",
    )(x, w, cos, sin)


def _rope_tables(pos):
    half = RET_DK // 2
    inv = 1.0 / (ROPE_BASE ** (jnp.arange(half, dtype=F32) / half))
    ang = pos.astype(F32)[:, None] * inv[None]
    return jnp.cos(ang), jnp.sin(ang)


def _log_gamma():
    return jnp.log(1.0 - 2.0 ** (-5.0 - jnp.arange(RET_HEADS, dtype=F32)))


def _decay_tables(c, rows):
    lg = _log_gamma()
    n = jnp.arange(rows, dtype=F32)
    live = n < c
    diff = n[:, None] - n[None, :]
    decay = jnp.where((diff >= 0)[None] & live[None, None, :],
                      jnp.exp(jnp.maximum(diff, 0.0)[None] * lg[:, None, None]), 0.0)
    q_decay = jnp.exp((n[None, :] + 1.0) * lg[:, None])
    k_decay = jnp.where(live[None], jnp.exp((c - 1.0 - n)[None, :] * lg[:, None]), 0.0)
    chunk_decay = jnp.exp(c * lg)
    return decay, q_decay, k_decay, chunk_decay


def _group_norm_gate(o, gate, gn):
    mu = jnp.mean(o, axis=-1, keepdims=True)
    oc = o - mu
    var = jnp.mean(oc * oc, axis=-1, keepdims=True)
    return _silu(gate) * (oc * lax.rsqrt(var + GN_EPS) * gn)


def _ret_prompt_kernel(q_ref, k_ref, v_ref, gate_ref, dec_ref, qd_ref, kd_ref, cd_ref, gn_ref,
                       y_ref, st_ref, s_ref, *, tb):
    cb = pl.program_id(2)

    @pl.when(cb == 0)
    def _():
        s_ref[...] = jnp.zeros_like(s_ref)

    def body(ci, carry):
        r0 = pl.multiple_of(ci * RET_CHUNK, RET_CHUNK)
        rows = pl.ds(r0, RET_CHUNK)
        for j in range(RET_HPS):
            kc = slice(j * RET_DK, (j + 1) * RET_DK)
            vc = slice(j * RET_DV, (j + 1) * RET_DV)
            q = q_ref[0, rows, kc]
            k = k_ref[0, rows, kc]
            v = v_ref[0, rows, vc]
            state = s_ref[j]
            scores = lax.dot_general(q, k, _NT, preferred_element_type=F32) * dec_ref[j]
            inner = jnp.dot(scores.astype(BF16), v, preferred_element_type=F32)
            cross = jnp.dot(q, state.astype(BF16), preferred_element_type=F32) * qd_ref[j]
            kd = (k.astype(F32) * kd_ref[j]).astype(BF16)
            s_ref[j] = cd_ref[j, 0:1, :] * state + lax.dot_general(kd, v, _TN, preferred_element_type=F32)
            y = _group_norm_gate(inner + cross, gate_ref[0, rows, vc].astype(F32), gn_ref[:, vc])
            y_ref[0, rows, vc] = y.astype(y_ref.dtype)
        return carry

    lax.fori_loop(0, tb // RET_CHUNK, body, 0)

    @pl.when(cb == pl.num_programs(2) - 1)
    def _():
        st_ref[0] = s_ref[...]


def _ret_prompt(proj, gn, bsz, seq, tb):
    decay, q_decay, k_decay, chunk_decay = _decay_tables(RET_CHUNK, RET_CHUNK)
    qd = jnp.broadcast_to(q_decay[:, :, None], (RET_HEADS, RET_CHUNK, RET_DV))
    kd = jnp.broadcast_to(k_decay[:, :, None], (RET_HEADS, RET_CHUNK, RET_DK))
    cd = jnp.broadcast_to(chunk_decay[:, None, None], (RET_HEADS, 8, RET_DV))
    proj = proj.reshape(bsz, seq, proj.shape[-1])
    hps = RET_HPS
    k_off = RET_QK // (hps * RET_DK)
    v_off = 2 * RET_QK // (hps * RET_DV)
    g_off = v_off + RET_HEADS // hps
    per_head = lambda shape: pl.BlockSpec((hps,) + shape, lambda b, h, c: (h, 0, 0))
    y, state = pl.pallas_call(
        functools.partial(_ret_prompt_kernel, tb=tb),
        grid=(bsz, RET_HEADS // hps, seq // tb),
        in_specs=[pl.BlockSpec((1, tb, hps * RET_DK), lambda b, h, c: (b, c, h)),
                  pl.BlockSpec((1, tb, hps * RET_DK), lambda b, h, c: (b, c, k_off + h)),
                  pl.BlockSpec((1, tb, hps * RET_DV), lambda b, h, c: (b, c, v_off + h)),
                  pl.BlockSpec((1, tb, hps * RET_DV), lambda b, h, c: (b, c, g_off + h)),
                  per_head((RET_CHUNK, RET_CHUNK)), per_head((RET_CHUNK, RET_DV)),
                  per_head((RET_CHUNK, RET_DK)), per_head((8, RET_DV)),
                  pl.BlockSpec((1, hps * RET_DV), lambda b, h, c: (0, h))],
        out_specs=[pl.BlockSpec((1, tb, hps * RET_DV), lambda b, h, c: (b, c, h)),
                   pl.BlockSpec((1, hps, RET_DK, RET_DV), lambda b, h, c: (b, h, 0, 0))],
        out_shape=[jax.ShapeDtypeStruct((bsz, seq, RET_V), BF16),
                   jax.ShapeDtypeStruct((bsz, RET_HEADS, RET_DK, RET_DV), F32)],
        scratch_shapes=[pltpu.VMEM((hps, RET_DK, RET_DV), F32)],
        compiler_params=_params(3), name="retention_prompt",
    )(proj, proj, proj, proj, decay, qd, kd, cd, gn)
    return y.reshape(bsz * seq, RET_V), state


def _ret_sample_kernel(p_ref, st_ref, dec_ref, qd_ref, kd_ref, cd_ref, gn_ref, y_ref, ns_ref):
    tp = SAMPLE_PAD
    proj = p_ref[0]
    for h in range(RET_HEADS):
        q = proj[:, h * RET_DK:(h + 1) * RET_DK].astype(BF16)
        k = proj[:, RET_QK + h * RET_DK:RET_QK + (h + 1) * RET_DK]
        v = proj[:, 2 * RET_QK + h * RET_DV:2 * RET_QK + (h + 1) * RET_DV]
        gate = proj[:, 2 * RET_QK + RET_V + h * RET_DV:2 * RET_QK + RET_V + (h + 1) * RET_DV]
        zk = jnp.zeros((LANES - tp, RET_DK), F32)
        zv = jnp.zeros((LANES - tp, RET_DV), F32)
        kp = jnp.concatenate([k, zk], axis=0).astype(BF16)
        kdp = jnp.concatenate([k * kd_ref[h], zk], axis=0).astype(BF16)
        vp = jnp.concatenate([v, zv], axis=0).astype(BF16)
        state = st_ref[0, h]
        scores = lax.dot_general(q, kp, _NT, preferred_element_type=F32) * dec_ref[h]
        inner = jnp.dot(scores.astype(BF16), vp, preferred_element_type=F32)
        cross = jnp.dot(q, state.astype(BF16), preferred_element_type=F32) * qd_ref[h]
        ns_ref[0, h] = cd_ref[h, 0:1, :] * state + lax.dot_general(kdp, vp, _TN, preferred_element_type=F32)
        y = _group_norm_gate(inner + cross, gate, gn_ref[:, h * RET_DV:(h + 1) * RET_DV])
        y_ref[0, :, h * RET_DV:(h + 1) * RET_DV] = y


def _ret_sample(proj, state, gn, t_new):
    bsz = proj.shape[0]
    tp = SAMPLE_PAD
    decay, q_decay, k_decay, chunk_decay = _decay_tables(t_new, tp)
    dec = jnp.pad(decay, ((0, 0), (0, 0), (0, LANES - tp)))
    qd = jnp.broadcast_to(q_decay[:, :, None], (RET_HEADS, tp, RET_DV))
    kd = jnp.broadcast_to(k_decay[:, :, None], (RET_HEADS, tp, RET_DK))
    cd = jnp.broadcast_to(chunk_decay[:, None, None], (RET_HEADS, 8, RET_DV))
    const = lambda a: pl.BlockSpec(a.shape, lambda b: (0,) * a.ndim)
    return pl.pallas_call(
        _ret_sample_kernel, grid=(bsz,),
        in_specs=[pl.BlockSpec((1,) + proj.shape[1:], lambda b: (b, 0, 0)),
                  pl.BlockSpec((1,) + state.shape[1:], lambda b: (b, 0, 0, 0)),
                  const(dec), const(qd), const(kd), const(cd), const(gn)],
        out_specs=[pl.BlockSpec((1, tp, RET_V), lambda b: (b, 0, 0)),
                   pl.BlockSpec((1,) + state.shape[1:], lambda b: (b, 0, 0, 0))],
        out_shape=[jax.ShapeDtypeStruct((bsz, tp, RET_V), F32),
                   jax.ShapeDtypeStruct(state.shape, F32)],
        compiler_params=_params(1), name="retention_sample",
    )(proj, state, dec, qd, kd, cd, gn)


def _router_kernel(x_ref, w_ref, o_ref):
    logits = jnp.dot(x_ref[...], w_ref[...], preferred_element_type=F32, precision=lax.Precision.HIGHEST)
    lane = lax.broadcasted_iota(jnp.int32, logits.shape, 1)
    logits = jnp.where(lane < N_EXPERTS, logits, NEG)
    m1 = jnp.max(logits, axis=-1, keepdims=True)
    i1 = jnp.min(jnp.where(logits == m1, lane, LANES), axis=-1, keepdims=True)
    rest = jnp.where(lane == i1, NEG, logits)
    m2 = jnp.max(rest, axis=-1, keepdims=True)
    i2 = jnp.min(jnp.where(rest == m2, lane, LANES), axis=-1, keepdims=True)
    e2 = jnp.exp(m2 - m1)
    den = 1.0 + e2
    o_ref[...] = jnp.where(lane == 0, 1.0 / den,
                           jnp.where(lane == 1, e2 / den,
                                     jnp.where(lane == 2, i1.astype(F32),
                                               jnp.where(lane == 3, i2.astype(F32), 0.0))))


def _router(x, w_router, tm):
    m = x.shape[0]
    w = jnp.pad(w_router, ((0, 0), (0, LANES - N_EXPERTS)))
    return pl.pallas_call(
        _router_kernel, grid=(m // tm,),
        in_specs=[pl.BlockSpec((tm, D_MODEL), lambda i: (i, 0)), pl.BlockSpec((D_MODEL, LANES), lambda i: (0, 0))],
        out_specs=pl.BlockSpec((tm, LANES), lambda i: (i, 0)),
        out_shape=jax.ShapeDtypeStruct((m, LANES), F32),
        compiler_params=_params(1), name="router",
    )(x, w)


def _route_plan(e1, e2, tile):
    n_tok = e1.shape[0]
    e = jnp.concatenate([e1, e2])
    onehot = (e[:, None] == jnp.arange(N_EXPERTS, dtype=jnp.int32)[None]).astype(jnp.int32)
    csum = jnp.cumsum(onehot, axis=0)
    rank = jnp.take_along_axis(csum, e[:, None], axis=1)[:, 0] - 1
    tiles_per_expert = (csum[-1] + tile - 1) // tile
    tile_end = jnp.cumsum(tiles_per_expert)
    pos = ((tile_end - tiles_per_expert) * tile)[e] + rank
    n_tiles = (2 * n_tok) // tile + N_EXPERTS
    n_used = tile_end[-1]
    tile_ids = jnp.arange(n_tiles, dtype=jnp.int32)
    tile_expert = jnp.minimum(jnp.searchsorted(tile_end, tile_ids, side="right"), N_EXPERTS - 1)
    tile_expert = jnp.where(tile_ids < n_used, tile_expert, tile_expert[n_used - 1])
    meta = jnp.concatenate([tile_expert, n_used[None], tile_end]).astype(jnp.int32)
    return meta, pos[:n_tok], pos[n_tok:], n_tiles


ROW_DMA_UNROLL = 8


def _row_copy(src, dst, sem, src_row, dst_row):
    return pltpu.make_async_copy(src.at[pl.ds(src_row, 1), :], dst.at[pl.ds(dst_row, 1), :], sem)


def _start_row_gather(idx_ref, src_hbm, dst, sem, n_rows):
    def issue(r, carry):
        _row_copy(src_hbm, dst, sem, idx_ref[0, 0, r], r).start()
        return carry

    lax.fori_loop(0, n_rows, issue, 0, unroll=ROW_DMA_UNROLL)


def _start_row_scatter(idx_ref, src, dst_hbm, sem, n_rows):
    def issue(r, carry):
        _row_copy(src, dst_hbm, sem, r, idx_ref[0, 0, r]).start()
        return carry

    lax.fori_loop(0, n_rows, issue, 0, unroll=ROW_DMA_UNROLL)


def _wait_rows(hbm, vmem, sem):
    pltpu.make_async_copy(hbm.at[pl.ds(0, vmem.shape[0]), :], vmem, sem).wait()


def _dispatch_kernel(meta_ref, p1_ref, p2_ref, x_ref, xs_hbm, zbuf, sem, zsem, *, tm, tile, n_tiles):
    @pl.when(pl.program_id(0) == 0)
    def _():
        zbuf[...] = jnp.zeros_like(zbuf)
        zero_tile = lambda t: pltpu.make_async_copy(zbuf, xs_hbm.at[pl.ds(t * tile, tile), :], zsem)
        for e in range(N_EXPERTS):
            zero_tile(jnp.maximum(meta_ref[n_tiles + 1 + e] - 1, 0)).start()
        for e in range(N_EXPERTS):
            zero_tile(0).wait()

        def zero_unused(t, carry):
            zero_tile(t).start()
            zero_tile(t).wait()
            return carry

        lax.fori_loop(meta_ref[n_tiles], n_tiles, zero_unused, 0)

    _start_row_scatter(p1_ref, x_ref, xs_hbm, sem, tm)
    _start_row_scatter(p2_ref, x_ref, xs_hbm, sem, tm)
    for _ in range(2):
        pltpu.make_async_copy(x_ref, xs_hbm.at[pl.ds(0, tm), :], sem).wait()


def _dispatch(meta, pos1, pos2, x, tm, tile, n_tiles):
    m = x.shape[0]
    idx = lambda: pl.BlockSpec((1, 1, tm), lambda i, meta: (i, 0, 0), memory_space=pltpu.SMEM)
    return pl.pallas_call(
        functools.partial(_dispatch_kernel, tm=tm, tile=tile, n_tiles=n_tiles),
        grid_spec=pltpu.PrefetchScalarGridSpec(
            num_scalar_prefetch=1, grid=(m // tm,),
            in_specs=[idx(), idx(), pl.BlockSpec((tm, D_MODEL), lambda i, meta: (i, 0))],
            out_specs=pl.BlockSpec(memory_space=pl.ANY),
            scratch_shapes=[pltpu.VMEM((tile, D_MODEL), F32),
                            pltpu.SemaphoreType.DMA(()), pltpu.SemaphoreType.DMA(())]),
        out_shape=jax.ShapeDtypeStruct((n_tiles * tile, D_MODEL), F32),
        compiler_params=_params(1), name="moe_dispatch",
    )(meta, pos1.reshape(m // tm, 1, tm), pos2.reshape(m // tm, 1, tm), x)


def _expert_kernel(meta_ref, xs_ref, wg_ref, wu_ref, wd_ref, o_ref, xb_ref, *, n_tiles):
    f = pl.program_id(1)
    used = pl.program_id(0) < meta_ref[n_tiles]

    @pl.when(jnp.logical_and(jnp.logical_not(used), f == 0))
    def _():
        o_ref[...] = jnp.zeros_like(o_ref)

    @pl.when(jnp.logical_and(used, f == 0))
    def _():
        xb_ref[...] = xs_ref[...].astype(BF16)

    @pl.when(used)
    def _():
        xb = xb_ref[...]
        a = jnp.dot(xb, wg_ref[0], preferred_element_type=F32)
        u = jnp.dot(xb, wu_ref[0], preferred_element_type=F32)
        part = jnp.dot((_silu(a) * u).astype(BF16), wd_ref[0], preferred_element_type=F32)

        @pl.when(f == 0)
        def _():
            o_ref[...] = part

        @pl.when(f > 0)
        def _():
            o_ref[...] += part


def _experts(meta, xs, w_gu, w_down, tile, n_tiles, tf):
    d_ff = w_down.shape[1]
    nf = d_ff // tf
    fcol = lambda j, f, meta: jnp.where(j < meta[n_tiles], f, nf - 1)
    return pl.pallas_call(
        functools.partial(_expert_kernel, n_tiles=n_tiles),
        grid_spec=pltpu.PrefetchScalarGridSpec(
            num_scalar_prefetch=1, grid=(n_tiles, nf),
            in_specs=[pl.BlockSpec((tile, D_MODEL), lambda j, f, meta: (j, 0)),
                      pl.BlockSpec((1, D_MODEL, tf), lambda j, f, meta: (meta[j], 0, fcol(j, f, meta))),
                      pl.BlockSpec((1, D_MODEL, tf), lambda j, f, meta: (meta[j], 0, nf + fcol(j, f, meta))),
                      pl.BlockSpec((1, tf, D_MODEL), lambda j, f, meta: (meta[j], fcol(j, f, meta), 0))],
            out_specs=pl.BlockSpec((tile, D_MODEL), lambda j, f, meta: (j, 0)),
            scratch_shapes=[pltpu.VMEM((tile, D_MODEL), BF16)]),
        out_shape=jax.ShapeDtypeStruct((n_tiles * tile, D_MODEL), F32),
        compiler_params=_params(2), name="moe_experts",
    )(meta, xs, w_gu, w_gu, w_down)


def _combine_kernel(p1_ref, p2_ref, n1_ref, n2_ref, rows_hbm, x_ref, r_ref, g_ref, b_ref, op_ref, os_ref,
                    buf1, buf2, sem1, sem2, *, tm, prompt_tiles):
    i = pl.program_id(0)
    slot = i % 2

    def start(pa_ref, pb_ref, s):
        _start_row_gather(pa_ref, rows_hbm, buf1.at[s], sem1.at[s], tm)
        _start_row_gather(pb_ref, rows_hbm, buf2.at[s], sem2.at[s], tm)

    @pl.when(i == 0)
    def _():
        start(p1_ref, p2_ref, 0)

    @pl.when(i + 1 < pl.num_programs(0))
    def _():
        start(n1_ref, n2_ref, 1 - slot)

    _wait_rows(rows_hbm, buf1.at[slot], sem1.at[slot])
    _wait_rows(rows_hbm, buf2.at[slot], sem2.at[slot])
    route = r_ref[...]
    y = route[:, 0:1] * buf1[slot] + route[:, 1:2] * buf2[slot]
    res = _layer_norm(ALPHA * x_ref[...] + y, g_ref[...], b_ref[...])

    @pl.when(i < prompt_tiles)
    def _():
        op_ref[...] = res

    @pl.when(i >= prompt_tiles)
    def _():
        os_ref[...] = res


def _combine(pos1, pos2, rows, x, route, g, b, tm, n_prompt):
    m = x.shape[0]
    pt = n_prompt // tm
    nt = m // tm
    idx = lambda: pl.BlockSpec((1, 1, tm), lambda i: (i, 0, 0), memory_space=pltpu.SMEM)
    nxt = lambda: pl.BlockSpec((1, 1, tm), lambda i: (jnp.minimum(i + 1, nt - 1), 0, 0), memory_space=pltpu.SMEM)
    vec = pl.BlockSpec((1, D_MODEL), lambda i: (0, 0))
    p1 = pos1.reshape(nt, 1, tm)
    p2 = pos2.reshape(nt, 1, tm)
    return pl.pallas_call(
        functools.partial(_combine_kernel, tm=tm, prompt_tiles=pt), grid=(nt,),
        in_specs=[idx(), idx(), nxt(), nxt(), pl.BlockSpec(memory_space=pl.ANY),
                  pl.BlockSpec((tm, D_MODEL), lambda i: (i, 0)),
                  pl.BlockSpec((tm, LANES), lambda i: (i, 0)), vec, vec],
        out_specs=[pl.BlockSpec((tm, D_MODEL), lambda i: (jnp.minimum(i, pt - 1), 0)),
                   pl.BlockSpec((tm, D_MODEL), lambda i: (jnp.maximum(i - pt, 0), 0))],
        out_shape=[jax.ShapeDtypeStruct((n_prompt, D_MODEL), F32),
                   jax.ShapeDtypeStruct((m - n_prompt, D_MODEL), F32)],
        scratch_shapes=[pltpu.VMEM((2, tm, D_MODEL), F32), pltpu.VMEM((2, tm, D_MODEL), F32),
                        pltpu.SemaphoreType.DMA((2,)), pltpu.SemaphoreType.DMA((2,))],
        compiler_params=_params(1), name="moe_combine",
    )(p1, p2, p1, p2, rows, x, route, g, b)


def _moe(x, n_prompt, w_router, w_gu, w_down, g, b, tm, tile, tf):
    route = _router(x, w_router, tm)
    e1 = route[:, 2].astype(jnp.int32)
    e2 = route[:, 3].astype(jnp.int32)
    meta, pos1, pos2, n_tiles = _route_plan(e1, e2, tile)
    xs = _dispatch(meta, pos1, pos2, x, tm, tile, n_tiles)
    rows = _experts(meta, xs, w_gu, w_down, tile, n_tiles, tf)
    return _combine(pos1, pos2, rows, x, route, g, b, tm, n_prompt)


def kernel(x_prompt, x_sample, cache_kv_w128, cache_kv_w512, cache_kv_w2048, state_ret,
           ln_g, ln_b, rel_bias, w_in_dil, w_out_dil, w_in_ret, ret_gn_g, w_out_ret,
           w_gu_dense, w_down_dense, w_router, w_gu_moe, w_down_moe):
    bsz, seq, _ = x_prompt.shape
    dbsz, t_new, _ = x_sample.shape
    tp = SAMPLE_PAD
    n_p = bsz * seq
    n_s = dbsz * tp
    caches = (cache_kv_w128, cache_kv_w512, cache_kv_w2048)

    w_in_dil_b = w_in_dil.astype(BF16)
    w_out_dil_b = w_out_dil.astype(BF16)
    w_in_ret_b = w_in_ret.astype(BF16)
    w_out_ret_b = w_out_ret.astype(BF16)
    w_gu_dense_b = w_gu_dense.astype(BF16)
    w_down_dense_b = w_down_dense.astype(BF16)
    w_gu_moe_b = w_gu_moe.astype(BF16)
    w_down_moe_b = w_down_moe.astype(BF16)
    lng = ln_g.reshape(DEPTH, 2, 1, D_MODEL)
    lnb = ln_b.reshape(DEPTH, 2, 1, D_MODEL)
    gn = ret_gn_g.reshape(1, RET_V)

    hp = x_prompt.reshape(n_p, D_MODEL)
    hs = jnp.pad(x_sample, ((0, 0), (0, tp - t_new), (0, 0))).reshape(n_s, D_MODEL)

    tm = min(1024, seq)
    qkv_groups, (kv128_p, kv512_p, kv2048_p) = _proj_dil_prompt(hp, w_in_dil_b, bsz, seq, tm)
    outs, lses = [], []
    for g, (window, dil) in enumerate(DIL_GROUPS):
        qkv_g = qkv_groups[g].reshape(bsz * dil, seq // dil, G_COLS)
        tbl = _prompt_table(rel_bias, g, window, dil)
        o, l = _attn_prompt(qkv_g, tbl, min(512, seq // dil))
        outs.append(o.reshape(bsz, dil, seq // dil, A_WIDTH))
        lses.append(l.reshape(bsz, dil, seq // dil, A_WIDTH))
    hp = _merge_out(outs, lses, w_out_dil_b, hp, lng[0, 0], lnb[0, 0], 512)

    qkv_s = _matmul(hs, w_in_dil_b, G_COLS)
    qkv_s3 = qkv_s.reshape(dbsz, tp, N_GROUPS * G_COLS)
    tabs = [_sample_tables(rel_bias, g, window, dil, caches[g].shape[1], t_new)
            for g, (window, dil) in enumerate(DIL_GROUPS)]
    caches2 = [c.reshape(c.shape[0], c.shape[1], 2 * A_WIDTH) for c in caches]
    mixed_s = _attn_sample(qkv_s3, caches2, [t[0] for t in tabs], [t[1] for t in tabs])
    hs = _mm_ln(mixed_s.reshape(n_s, A_WIDTH), w_out_dil_b, hs, lng[0, 0], lnb[0, 0], n_s)
    rows_s = []
    for g in range(N_GROUPS):
        kv = qkv_s3[:, :t_new, g * G_COLS + A_WIDTH:(g + 1) * G_COLS]
        rows_s.append(kv.reshape(dbsz, t_new, 2, H_SLOT, HEAD_DIM))

    hp = _ffn_dense(hp, w_gu_dense_b, w_down_dense_b, lng[0, 1], lnb[0, 1], 512, 1408)
    hs = _ffn_dense(hs, w_gu_dense_b, w_down_dense_b, lng[0, 1], lnb[0, 1], n_s, 1408)

    cos_p, sin_p = _rope_tables(jnp.arange(seq, dtype=jnp.int32))
    pos_s = jnp.tile(PAST_LEN + jnp.arange(tp, dtype=jnp.int32), dbsz)
    cos_s, sin_s = _rope_tables(pos_s)
    proj_p = _proj_ret(hp, w_in_ret_b, cos_p, sin_p, tm, BF16)
    y_p, ret_p = _ret_prompt(proj_p, gn, bsz, seq, min(1024, seq))
    proj_s = _proj_ret(hs, w_in_ret_b, cos_s, sin_s, n_s, F32)
    y_s, ret_s = _ret_sample(proj_s.reshape(dbsz, tp, -1), state_ret, gn, t_new)
    h_all = _mm_ln_pair(y_p, y_s.reshape(n_s, RET_V), w_out_ret_b, hp, hs, lng[1, 0], lnb[1, 0], 512)

    out_p, out_s = _moe(h_all, n_p, w_router, w_gu_moe_b, w_down_moe_b, lng[1, 1], lnb[1, 1], 512, 1024, 896)

    y_prompt = out_p.reshape(bsz, seq, D_MODEL)
    y_sample = out_s.reshape(dbsz, tp, D_MODEL)[:, :t_new]
    shape5 = lambda a: a.reshape(a.shape[0], a.shape[1], 2, H_SLOT, HEAD_DIM)
    return (y_prompt, y_sample, shape5(kv128_p), shape5(kv512_p), shape5(kv2048_p), ret_p,
            rows_s[0], rows_s[1], rows_s[2], ret_s)
```

```python
import functools

import jax
import jax.numpy as jnp
import numpy as np
from jax import lax
from jax.experimental import pallas as pl
from jax.experimental.pallas import tpu as pltpu

F32 = jnp.float32
BF16 = jnp.bfloat16

DEPTH = 2
D_MODEL = 1024
PAST_LEN = 16384
DIL_GROUPS = ((128, 1), (512, 4), (2048, 16))
N_GROUPS = 3
H_SLOT = 8
HEAD_DIM = 64
A_WIDTH = H_SLOT * HEAD_DIM
G_COLS = 3 * A_WIDTH
N_BUCKETS = 32
MAX_DISTANCE = 2048
RET_HEADS = 4
RET_DK = 256
RET_DV = 512
RET_CHUNK = 128
RET_HPS = 2
ROPE_BASE = 10000.0
RET_QK = RET_HEADS * RET_DK
RET_V = RET_HEADS * RET_DV
N_EXPERTS = 8
LN_EPS = 1e-5
GN_EPS = 1e-5
ALPHA = (2 * DEPTH) ** 0.25
NEG = -1e30

LANES = 128
ATT_BLOCK = 128
PERM = 256
FF_CHUNK = 512
SAMPLE_PAD = 16
VMEM_LIMIT = 56 * 1024 * 1024

_NT = (((1,), (1,)), ((), ()))
_TN = (((0,), (0,)), ((), ()))


def _params(n_grid):
    return pltpu.CompilerParams(dimension_semantics=("arbitrary",) * n_grid,
                                vmem_limit_bytes=VMEM_LIMIT)


def _layer_norm(z, g, b):
    mu = jnp.mean(z, axis=-1, keepdims=True)
    zc = z - mu
    var = jnp.mean(zc * zc, axis=-1, keepdims=True)
    return zc * lax.rsqrt(var + LN_EPS) * g + b


def _silu(a):
    return a / (1.0 + jnp.exp(-a))


def _t5_bucket(dist):
    max_exact = N_BUCKETS // 2
    d = np.asarray(dist, dtype=np.int64)
    scaled = np.log(np.maximum(d, 1) / max_exact) / np.log(MAX_DISTANCE / max_exact)
    large = np.minimum(max_exact + (scaled * (N_BUCKETS - max_exact)).astype(np.int32), N_BUCKETS - 1)
    return np.where(d < max_exact, d, large).astype(np.int32)


def _group_bias(rel_bias, g, window, dil):
    n_keys = window // dil + 1
    buckets = jnp.asarray(_t5_bucket(np.arange(n_keys) * dil))
    return rel_bias[buckets][:, g * H_SLOT:(g + 1) * H_SLOT].T.astype(F32)


def _matmul_kernel(x_ref, w_ref, o_ref):
    o_ref[...] = jnp.dot(x_ref[...].astype(BF16), w_ref[...], preferred_element_type=F32)


def _matmul(x, w, tn):
    m, k = x.shape
    n = w.shape[1]
    return pl.pallas_call(
        _matmul_kernel,
        grid=(n // tn,),
        in_specs=[pl.BlockSpec((m, k), lambda j: (0, 0)),
                  pl.BlockSpec((k, tn), lambda j: (0, j))],
        out_specs=pl.BlockSpec((m, tn), lambda j: (0, j)),
        out_shape=jax.ShapeDtypeStruct((m, n), F32),
        compiler_params=_params(1),
        name="matmul_sample",
    )(x, w)


def _deinterleave_matrix(dil):
    p = np.zeros((PERM, PERM), np.float32)
    rows = PERM // dil
    for r in range(dil):
        for m in range(rows):
            p[r * rows + m, m * dil + r] = 1.0
    return p


def _proj_dil_kernel(x_ref, w_ref, p1_ref, p2_ref, q0_ref, q1_ref, q2_ref, kv0_ref, kv1_ref, kv2_ref, xperm,
                     *, tm, tpb, keeps):
    g = pl.program_id(1)
    j = pl.program_id(0) % tpb
    q_refs = (q0_ref, q1_ref, q2_ref)
    kv_refs = (kv0_ref, kv1_ref, kv2_ref)
    perms = (None, p1_ref, p2_ref)

    def store_q(ref, r, row0, n_rows, acc, src0):
        ref[0, r, row0:row0 + n_rows, :A_WIDTH] = (acc[src0:src0 + n_rows, :A_WIDTH] * (HEAD_DIM ** -0.5)).astype(BF16)
        ref[0, r, row0:row0 + n_rows, A_WIDTH:] = acc[src0:src0 + n_rows, A_WIDTH:].astype(BF16)

    for gi, (_, dil) in enumerate(DIL_GROUPS):
        keep = keeps[gi]
        in_tail = (j >= tpb - keep // tm) if keep >= tm else (j == tpb - 1)
        row0 = 0 if keep >= tm else tm - keep

        @pl.when(g == gi)
        def _(gi=gi, dil=dil, in_tail=in_tail, row0=row0):
            xb = x_ref[...].astype(BF16)
            if dil == 1:
                acc = jnp.dot(xb, w_ref[...], preferred_element_type=F32)
                store_q(q_refs[gi], 0, 0, tm, acc, 0)
            else:
                for sub in range(tm // PERM):
                    blk = xb[sub * PERM:(sub + 1) * PERM]
                    xperm[sub * PERM:(sub + 1) * PERM, :] = jnp.dot(
                        perms[gi][...], blk, preferred_element_type=F32).astype(BF16)
                acc = jnp.dot(xperm[...], w_ref[...], preferred_element_type=F32)
                rows = PERM // dil
                for sub in range(tm // PERM):
                    for r in range(dil):
                        store_q(q_refs[gi], r, sub * rows, rows, acc, sub * PERM + r * rows)

            @pl.when(in_tail)
            def _():
                if dil == 1:
                    kv_refs[gi][0] = acc[row0:, A_WIDTH:]
                else:
                    kv_refs[gi][0] = jnp.dot(xb[row0:], w_ref[:, A_WIDTH:], preferred_element_type=F32)


def _proj_dil_prompt(x, w, bsz, seq, tm):
    tpb = seq // tm
    keeps = tuple(min(wd, seq) for wd, _ in DIL_GROUPS)

    def kv_spec(keep):
        if keep >= tm:
            first = tpb - keep // tm
            return pl.BlockSpec((1, tm, 2 * A_WIDTH),
                                lambda i, g: (i // tpb, jnp.maximum(i % tpb - first, 0), 0))
        return pl.BlockSpec((1, keep, 2 * A_WIDTH), lambda i, g: (i // tpb, 0, 0))

    perm = lambda d: jnp.asarray(_deinterleave_matrix(d), BF16)
    const = pl.BlockSpec((PERM, PERM), lambda i, g: (0, 0))
    outs = pl.pallas_call(
        functools.partial(_proj_dil_kernel, tm=tm, tpb=tpb, keeps=keeps),
        grid=(bsz * tpb, N_GROUPS),
        in_specs=[pl.BlockSpec((tm, D_MODEL), lambda i, g: (i, 0)),
                  pl.BlockSpec((D_MODEL, G_COLS), lambda i, g: (0, g)), const, const],
        out_specs=[pl.BlockSpec((1, d, tm // d, G_COLS), lambda i, g: (i // tpb, 0, i % tpb, 0))
                   for _, d in DIL_GROUPS] + [kv_spec(k) for k in keeps],
        out_shape=[jax.ShapeDtypeStruct((bsz, d, seq // d, G_COLS), BF16) for _, d in DIL_GROUPS]
        + [jax.ShapeDtypeStruct((bsz, k, 2 * A_WIDTH), F32) for k in keeps],
        scratch_shapes=[pltpu.VMEM((tm, D_MODEL), BF16)],
        compiler_params=_params(2),
        name="proj_dil_prompt",
    )(x, w, perm(DIL_GROUPS[1][1]), perm(DIL_GROUPS[2][1]))
    return outs[:3], outs[3:]


def _attn_prompt_kernel(q_ref, k_ref, v_ref, kp_ref, vp_ref, tbl_ref, o_ref, lse_ref, kbuf, vbuf, *, tq):
    n = pl.program_id(1)
    kbuf[0:ATT_BLOCK, :] = kp_ref[0]
    kbuf[ATT_BLOCK:, :] = k_ref[0]
    vbuf[0:ATT_BLOCK, :] = vp_ref[0]
    vbuf[ATT_BLOCK:, :] = v_ref[0]
    low = lax.broadcasted_iota(jnp.int32, (ATT_BLOCK, LANES), 1) < HEAD_DIM

    def body(m, carry):
        r0 = pl.multiple_of(m * ATT_BLOCK, ATT_BLOCK)
        first = jnp.where(jnp.logical_and(n == 0, m == 0), 1, 0)
        for p in range(A_WIDTH // LANES):
            cols = slice(p * LANES, (p + 1) * LANES)
            qm = q_ref[0, pl.ds(r0, ATT_BLOCK), cols]
            keys = kbuf[pl.ds(r0, 2 * ATT_BLOCK), cols]
            vals = vbuf[pl.ds(r0, 2 * ATT_BLOCK), cols]
            outs, lses = [], []
            for a in range(2):
                qa = jnp.where(low if a == 0 else jnp.logical_not(low), qm, jnp.zeros_like(qm))
                s = lax.dot_general(qa, keys, _NT, preferred_element_type=F32) + tbl_ref[first, 2 * p + a]
                mx = jnp.max(s, axis=-1, keepdims=True)
                e = jnp.exp(s - mx)
                l = jnp.sum(e, axis=-1, keepdims=True)
                o = jnp.dot(e.astype(BF16), vals, preferred_element_type=F32)
                outs.append(o / l)
                lses.append(jnp.broadcast_to(mx + jnp.log(l), (ATT_BLOCK, LANES)))
            o_ref[0, pl.ds(r0, ATT_BLOCK), cols] = jnp.where(low, outs[0], outs[1]).astype(BF16)
            lse_ref[0, pl.ds(r0, ATT_BLOCK), cols] = jnp.where(low, lses[0], lses[1])
        return carry

    lax.fori_loop(0, tq // ATT_BLOCK, body, 0)


def _attn_prompt(qkv, tbl, tq):
    nb, length, _ = qkv.shape
    sub = tq // ATT_BLOCK
    cur = lambda c: pl.BlockSpec((1, tq, A_WIDTH), lambda s, n: (s, n, c))
    prev = lambda c: pl.BlockSpec((1, ATT_BLOCK, A_WIDTH), lambda s, n: (s, jnp.maximum(n * sub - 1, 0), c))
    return pl.pallas_call(
        functools.partial(_attn_prompt_kernel, tq=tq),
        grid=(nb, length // tq),
        in_specs=[cur(0), cur(1), cur(2), prev(1), prev(2),
                  pl.BlockSpec(tbl.shape, lambda s, n: (0, 0, 0, 0))],
        out_specs=[pl.BlockSpec((1, tq, A_WIDTH), lambda s, n: (s, n, 0))] * 2,
        out_shape=[jax.ShapeDtypeStruct((nb, length, A_WIDTH), BF16),
                   jax.ShapeDtypeStruct((nb, length, A_WIDTH), F32)],
        scratch_shapes=[pltpu.VMEM((tq + ATT_BLOCK, A_WIDTH), BF16)] * 2,
        compiler_params=_params(2),
        name="attn_prompt",
    )(qkv, qkv, qkv, qkv, qkv, tbl)


def _toeplitz(u, n_rows, n_cols):
    h, n = u.shape
    assert n == n_rows + n_cols - 1
    up = jnp.pad(u, ((0, 0), (0, 1)))
    w = jnp.tile(up, (1, n_rows))[:, :n_rows * n].reshape(h, n_rows, n)
    return w[:, :, n_rows - 1:n_rows - 1 + n_cols]


def _bias_by_offset(bias, dist, valid):
    return jnp.where(valid[None], bias[:, np.clip(dist, 0, bias.shape[1] - 1)], NEG)


def _prompt_table(rel_bias, g, window, dil):
    bias = _group_bias(rel_bias, g, window, dil)
    k = np.arange(3 * ATT_BLOCK - 1)
    dist = 2 * ATT_BLOCK - 1 - k
    u = _bias_by_offset(bias, dist, (dist >= 0) & (dist <= window // dil))
    tbl = _toeplitz(u, ATT_BLOCK, 2 * ATT_BLOCK)
    c = np.arange(2 * ATT_BLOCK)[None, None, :]
    tbl_first = jnp.where(c < ATT_BLOCK, NEG, tbl)
    return jnp.stack([tbl, tbl_first], 0)


def _split3(x):
    hi = x.astype(BF16)
    r1 = x - hi.astype(F32)
    mid = r1.astype(BF16)
    lo = (r1 - mid.astype(F32)).astype(BF16)
    return hi, mid, lo


def _merge_out_kernel(o0, o1, o2, l0, l1, l2, pt1_ref, pt2_ref, w_ref, x_ref, g_ref, b_ref, out_ref, *, tm):
    pts = (None, pt1_ref, pt2_ref)

    def natural(ref, gi, exact_f32):
        dil = DIL_GROUPS[gi][1]
        if dil == 1:
            return ref[0, 0].astype(F32)
        rows = PERM // dil
        blocks = []
        for sub in range(tm // PERM):
            piece = jnp.concatenate([ref[0, r, sub * rows:(sub + 1) * rows, :] for r in range(dil)], axis=0)
            parts = _split3(piece) if exact_f32 else (piece,)
            blocks.append(sum(jnp.dot(pts[gi][...], p, preferred_element_type=F32) for p in parts))
        return jnp.concatenate(blocks, axis=0)

    os_ = [natural(r, gi, False) for gi, r in enumerate((o0, o1, o2))]
    ls = [natural(r, gi, True) for gi, r in enumerate((l0, l1, l2))]
    mx = jnp.maximum(jnp.maximum(ls[0], ls[1]), ls[2])
    es = [jnp.exp(l - mx) for l in ls]
    mixed = (es[0] * os_[0] + es[1] * os_[1] + es[2] * os_[2]) / (es[0] + es[1] + es[2])
    acc = jnp.dot(mixed.astype(BF16), w_ref[...], preferred_element_type=F32)
    out_ref[...] = _layer_norm(ALPHA * x_ref[...] + acc, g_ref[...], b_ref[...])


def _merge_out(os_, ls_, w, x, g, b, tm):
    bsz, _, seq, _ = os_[0].shape
    tpb = seq // tm
    grp = lambda d: pl.BlockSpec((1, d, tm // d, A_WIDTH), lambda i: (i // tpb, 0, i % tpb, 0))
    specs = [grp(d) for _, d in DIL_GROUPS]
    pt = lambda d: jnp.asarray(_deinterleave_matrix(d).T, BF16)
    const = pl.BlockSpec((PERM, PERM), lambda i: (0, 0))
    vec = pl.BlockSpec((1, D_MODEL), lambda i: (0, 0))
    return pl.pallas_call(
        functools.partial(_merge_out_kernel, tm=tm), grid=(bsz * tpb,),
        in_specs=specs + specs + [const, const, pl.BlockSpec((A_WIDTH, D_MODEL), lambda i: (0, 0)),
                                  pl.BlockSpec((tm, D_MODEL), lambda i: (i, 0)), vec, vec],
        out_specs=pl.BlockSpec((tm, D_MODEL), lambda i: (i, 0)),
        out_shape=jax.ShapeDtypeStruct((bsz * seq, D_MODEL), F32),
        compiler_params=_params(1), name="merge_outproj_ln",
    )(*os_, *ls_, pt(DIL_GROUPS[1][1]), pt(DIL_GROUPS[2][1]), w, x, g, b)


def _attn_sample_kernel(qkv_ref, c0_ref, c1_ref, c2_ref, tc0, tc1, tc2, tn0, tn1, tn2, out_ref):
    tp = SAMPLE_PAD
    rows = H_SLOT * tp
    head_of_row = lax.broadcasted_iota(jnp.int32, (rows, A_WIDTH), 0) // tp
    head_of_lane = lax.broadcasted_iota(jnp.int32, (rows, A_WIDTH), 1) // HEAD_DIM
    diag = head_of_row == head_of_lane
    qkv = qkv_ref[0]
    scores, values = [], []
    for g, (c_ref, tc, tn) in enumerate(((c0_ref, tc0, tn0), (c1_ref, tc1, tn1), (c2_ref, tc2, tn2))):
        base = g * G_COLS
        q = qkv[:, base:base + A_WIDTH] * (HEAD_DIM ** -0.5)
        q2 = jnp.where(diag, jnp.concatenate([q] * H_SLOT, axis=0), 0.0).astype(BF16)
        kc = c_ref[0, :, :A_WIDTH].astype(BF16)
        vc = c_ref[0, :, A_WIDTH:].astype(BF16)
        zpad = jnp.zeros((LANES - tp, A_WIDTH), F32)
        kn = jnp.concatenate([qkv[:, base + A_WIDTH:base + 2 * A_WIDTH], zpad], axis=0).astype(BF16)
        vn = jnp.concatenate([qkv[:, base + 2 * A_WIDTH:base + 3 * A_WIDTH], zpad], axis=0).astype(BF16)
        scores.append(lax.dot_general(q2, kc, _NT, preferred_element_type=F32) + tc[...])
        values.append(vc)
        scores.append(lax.dot_general(q2, kn, _NT, preferred_element_type=F32) + tn[...])
        values.append(vn)
    mx = functools.reduce(jnp.maximum, [jnp.max(s, axis=-1, keepdims=True) for s in scores])
    den = jnp.zeros((rows, 1), F32)
    acc = jnp.zeros((rows, A_WIDTH), F32)
    for s, v in zip(scores, values):
        e = jnp.exp(s - mx)
        den = den + jnp.sum(e, axis=-1, keepdims=True)
        acc = acc + jnp.dot(e.astype(BF16), v, preferred_element_type=F32)
    r = jnp.where(diag, acc / den, 0.0)
    out = r[0:tp]
    for h in range(1, H_SLOT):
        out = out + r[h * tp:(h + 1) * tp]
    out_ref[0] = out


def _sample_tables(rel_bias, g, window, dil, length, t_new):
    bias = _group_bias(rel_bias, g, window, dil)
    n_keys = window // dil + 1
    tp = SAMPLE_PAD
    live_row = (np.arange(tp) < t_new)[None, :, None]
    k = np.arange(tp + length - 1)
    dist = length + tp - 1 - k
    u = _bias_by_offset(bias, dist // dil, (dist % dil == 0) & (dist // dil < n_keys))
    tc = jnp.where(live_row, _toeplitz(u, tp, length), NEG)
    k2 = np.arange(tp + LANES - 1)
    dist2 = tp - 1 - k2
    u2 = _bias_by_offset(bias, dist2 // dil, (dist2 >= 0) & (dist2 % dil == 0) & (dist2 // dil < n_keys))
    live_col = (np.arange(LANES) < t_new)[None, None, :]
    tn = jnp.where(live_row & live_col, _toeplitz(u2, tp, LANES), NEG)
    return (tc.reshape(H_SLOT * tp, length), tn.reshape(H_SLOT * tp, LANES))


def _attn_sample(qkv, caches, tables_c, tables_n):
    bsz = qkv.shape[0]
    const = lambda a: pl.BlockSpec(a.shape, lambda b: (0, 0))
    return pl.pallas_call(
        _attn_sample_kernel,
        grid=(bsz,),
        in_specs=[pl.BlockSpec((1,) + qkv.shape[1:], lambda b: (b, 0, 0))]
        + [pl.BlockSpec((1,) + c.shape[1:], lambda b: (b, 0, 0)) for c in caches]
        + [const(t) for t in tables_c] + [const(t) for t in tables_n],
        out_specs=pl.BlockSpec((1, SAMPLE_PAD, A_WIDTH), lambda b: (b, 0, 0)),
        out_shape=jax.ShapeDtypeStruct((bsz, SAMPLE_PAD, A_WIDTH), F32),
        compiler_params=_params(1),
        name="attn_sample",
    )(qkv, *caches, *tables_c, *tables_n)


def _mm_ln_kernel(y_ref, w_ref, x_ref, g_ref, b_ref, o_ref):
    acc = jnp.dot(y_ref[...].astype(BF16), w_ref[...], preferred_element_type=F32)
    o_ref[...] = _layer_norm(ALPHA * x_ref[...] + acc, g_ref[...], b_ref[...])


def _mm_ln(y, w, x, g, b, tm):
    m, k = y.shape
    vec = pl.BlockSpec((1, D_MODEL), lambda i: (0, 0))
    return pl.pallas_call(
        _mm_ln_kernel, grid=(m // tm,),
        in_specs=[pl.BlockSpec((tm, k), lambda i: (i, 0)), pl.BlockSpec((k, D_MODEL), lambda i: (0, 0)),
                  pl.BlockSpec((tm, D_MODEL), lambda i: (i, 0)), vec, vec],
        out_specs=pl.BlockSpec((tm, D_MODEL), lambda i: (i, 0)),
        out_shape=jax.ShapeDtypeStruct((m, D_MODEL), F32),
        compiler_params=_params(1), name="outproj_ln",
    )(y, w, x, g, b)


def _mm_ln_pair_kernel(yp_ref, ys_ref, w_ref, xp_ref, xs_ref, g_ref, b_ref, wr_ref, o_ref, route_ref, *, prompt_tiles):
    def run(y_ref, x_ref):
        acc = jnp.dot(y_ref[...].astype(BF16), w_ref[...], preferred_element_type=F32)
        h = _layer_norm(ALPHA * x_ref[...] + acc, g_ref[...], b_ref[...])
        o_ref[...] = h
        route_ref[...] = _top2_route(jnp.dot(h.astype(BF16), wr_ref[...], preferred_element_type=F32))

    pl.when(pl.program_id(0) < prompt_tiles)(lambda: run(yp_ref, xp_ref))
    pl.when(pl.program_id(0) >= prompt_tiles)(lambda: run(ys_ref, xs_ref))


def _mm_ln_pair(y_p, y_s, w, x_p, x_s, g, b, w_router, tm):
    n_p, k = y_p.shape
    n_s = y_s.shape[0]
    pt = n_p // tm
    first = lambda width: pl.BlockSpec((tm, width), lambda i: (jnp.minimum(i, pt - 1), 0))
    second = lambda width: pl.BlockSpec((tm, width), lambda i: (jnp.maximum(i - pt, 0), 0))
    vec = pl.BlockSpec((1, D_MODEL), lambda i: (0, 0))
    wr = jnp.pad(w_router, ((0, 0), (0, LANES - N_EXPERTS))).astype(BF16)
    return pl.pallas_call(
        functools.partial(_mm_ln_pair_kernel, prompt_tiles=pt), grid=((n_p + n_s) // tm,),
        in_specs=[first(k), second(k), pl.BlockSpec((k, D_MODEL), lambda i: (0, 0)),
                  first(D_MODEL), second(D_MODEL), vec, vec, pl.BlockSpec((D_MODEL, LANES), lambda i: (0, 0))],
        out_specs=[pl.BlockSpec((tm, D_MODEL), lambda i: (i, 0)), pl.BlockSpec((tm, LANES), lambda i: (i, 0))],
        out_shape=[jax.ShapeDtypeStruct((n_p + n_s, D_MODEL), F32),
                   jax.ShapeDtypeStruct((n_p + n_s, LANES), F32)],
        compiler_params=_params(1), name="outproj_ln_route",
    )(y_p, y_s, w, x_p, x_s, g, b, wr)


def _swiglu_chunks(xb, wg_ref, wu_ref, wd_ref, lead, width, chunk):
    acc = None
    for c0 in range(0, width, chunk):
        cs = slice(c0, min(c0 + chunk, width))
        a = jnp.dot(xb, wg_ref[lead + (slice(None), cs)], preferred_element_type=F32)
        u = jnp.dot(xb, wu_ref[lead + (slice(None), cs)], preferred_element_type=F32)
        part = jnp.dot((_silu(a) * u).astype(BF16), wd_ref[lead + (cs, slice(None))], preferred_element_type=F32)
        acc = part if acc is None else acc + part
    return acc


def _ffn_kernel(x_ref, wg_ref, wu_ref, wd_ref, g_ref, b_ref, o_ref, *, d_ff):
    x = x_ref[...]
    acc = _swiglu_chunks(x.astype(BF16), wg_ref, wu_ref, wd_ref, (), d_ff, FF_CHUNK)
    o_ref[...] = _layer_norm(ALPHA * x + acc, g_ref[...], b_ref[...])


def _ffn_dense(x, w_gu, w_down, g, b, tm):
    m = x.shape[0]
    d_ff = w_down.shape[0]
    vec = pl.BlockSpec((1, D_MODEL), lambda i: (0, 0))
    return pl.pallas_call(
        functools.partial(_ffn_kernel, d_ff=d_ff), grid=(m // tm,),
        in_specs=[pl.BlockSpec((tm, D_MODEL), lambda i: (i, 0)),
                  pl.BlockSpec((D_MODEL, d_ff), lambda i: (0, 0)),
                  pl.BlockSpec((D_MODEL, d_ff), lambda i: (0, 1)),
                  pl.BlockSpec((d_ff, D_MODEL), lambda i: (0, 0)), vec, vec],
        out_specs=pl.BlockSpec((tm, D_MODEL), lambda i: (i, 0)),
        out_shape=jax.ShapeDtypeStruct((m, D_MODEL), F32),
        compiler_params=_params(1), name="ffn_dense",
    )(x, w_gu, w_gu, w_down, g, b)


def _proj_ret_kernel(x_ref, w_ref, cos_ref, sin_ref, o_ref, *, scale_k):
    n = pl.program_id(1)
    xb = x_ref[...].astype(BF16)
    half = RET_DK // 2

    @pl.when(n < 2)
    def _():
        cos = cos_ref[...]
        sin = sin_ref[...]
        scale = jnp.where(n == 1, scale_k, 1.0)
        for h in range(RET_HEADS):
            acc = jnp.dot(xb, w_ref[:, h * RET_DK:(h + 1) * RET_DK], preferred_element_type=F32)
            x1 = acc[:, :half]
            x2 = acc[:, half:]
            o_ref[:, h * RET_DK:h * RET_DK + half] = ((x1 * cos - x2 * sin) * scale).astype(o_ref.dtype)
            o_ref[:, h * RET_DK + half:(h + 1) * RET_DK] = ((x1 * sin + x2 * cos) * scale).astype(o_ref.dtype)

    @pl.when(n >= 2)
    def _():
        for h in range(RET_HEADS):
            cs = slice(h * RET_DK, (h + 1) * RET_DK)
            o_ref[:, cs] = jnp.dot(xb, w_ref[:, cs], preferred_element_type=F32).astype(o_ref.dtype)


def _proj_ret(x, w, cos, sin, tm, out_dtype):
    m = x.shape[0]
    n_cols = w.shape[1]
    tn = RET_QK
    pos_tiles = cos.shape[0] // tm
    return pl.pallas_call(
        functools.partial(_proj_ret_kernel, scale_k=RET_DK ** -0.5),
        grid=(m // tm, n_cols // tn),
        in_specs=[pl.BlockSpec((tm, D_MODEL), lambda i, n: (i, 0)),
                  pl.BlockSpec((D_MODEL, tn), lambda i, n: (0, n)),
                  pl.BlockSpec((tm, RET_DK // 2), lambda i, n: (i % pos_tiles, 0)),
                  pl.BlockSpec((tm, RET_DK // 2), lambda i, n: (i % pos_tiles, 0))],
        out_specs=pl.BlockSpec((tm, tn), lambda i, n: (i, n)),
        out_shape=jax.ShapeDtypeStruct((m, n_cols), out_dtype),
        compiler_params=_params(2), name="proj_ret",
    )(x, w, cos, sin)


def _rope_tables(pos):
    half = RET_DK // 2
    inv = 1.0 / (ROPE_BASE ** (jnp.arange(half, dtype=F32) / half))
    ang = pos.astype(F32)[:, None] * inv[None]
    return jnp.cos(ang), jnp.sin(ang)


def _log_gamma():
    return jnp.log(1.0 - 2.0 ** (-5.0 - jnp.arange(RET_HEADS, dtype=F32)))


def _decay_tables(c, rows):
    lg = _log_gamma()
    n = jnp.arange(rows, dtype=F32)
    live = n < c
    diff = n[:, None] - n[None, :]
    decay = jnp.where((diff >= 0)[None] & live[None, None, :],
                      jnp.exp(jnp.maximum(diff, 0.0)[None] * lg[:, None, None]), 0.0)
    q_decay = jnp.exp((n[None, :] + 1.0) * lg[:, None])
    k_decay = jnp.where(live[None], jnp.exp((c - 1.0 - n)[None, :] * lg[:, None]), 0.0)
    chunk_decay = jnp.exp(c * lg)
    return decay, q_decay, k_decay, chunk_decay


def _group_norm_gate(o, gate, gn):
    mu = jnp.mean(o, axis=-1, keepdims=True)
    oc = o - mu
    var = jnp.mean(oc * oc, axis=-1, keepdims=True)
    return _silu(gate) * (oc * lax.rsqrt(var + GN_EPS) * gn)


def _ret_prompt_kernel(q_ref, k_ref, v_ref, gate_ref, dec_ref, qd_ref, kd_ref, cd_ref, gn_ref,
                       y_ref, st_ref, s_ref, *, tb):
    cb = pl.program_id(2)

    @pl.when(cb == 0)
    def _():
        s_ref[...] = jnp.zeros_like(s_ref)

    def body(ci, carry):
        r0 = pl.multiple_of(ci * RET_CHUNK, RET_CHUNK)
        rows = pl.ds(r0, RET_CHUNK)
        for j in range(RET_HPS):
            kc = slice(j * RET_DK, (j + 1) * RET_DK)
            vc = slice(j * RET_DV, (j + 1) * RET_DV)
            q = q_ref[0, rows, kc]
            k = k_ref[0, rows, kc]
            v = v_ref[0, rows, vc]
            state = s_ref[j]
            scores = lax.dot_general(q, k, _NT, preferred_element_type=F32) * dec_ref[j]
            inner = jnp.dot(scores.astype(BF16), v, preferred_element_type=F32)
            cross = jnp.dot(q, state.astype(BF16), preferred_element_type=F32) * qd_ref[j]
            kd = (k.astype(F32) * kd_ref[j]).astype(BF16)
            s_ref[j] = cd_ref[j, 0:1, :] * state + lax.dot_general(kd, v, _TN, preferred_element_type=F32)
            y = _group_norm_gate(inner + cross, gate_ref[0, rows, vc].astype(F32), gn_ref[:, vc])
            y_ref[0, rows, vc] = y.astype(y_ref.dtype)
        return carry

    lax.fori_loop(0, tb // RET_CHUNK, body, 0)

    @pl.when(cb == pl.num_programs(2) - 1)
    def _():
        st_ref[0] = s_ref[...]


def _ret_prompt(proj, gn, bsz, seq, tb):
    decay, q_decay, k_decay, chunk_decay = _decay_tables(RET_CHUNK, RET_CHUNK)
    qd = jnp.broadcast_to(q_decay[:, :, None], (RET_HEADS, RET_CHUNK, RET_DV))
    kd = jnp.broadcast_to(k_decay[:, :, None], (RET_HEADS, RET_CHUNK, RET_DK))
    cd = jnp.broadcast_to(chunk_decay[:, None, None], (RET_HEADS, 8, RET_DV))
    proj = proj.reshape(bsz, seq, proj.shape[-1])
    hps = RET_HPS
    k_off = RET_QK // (hps * RET_DK)
    v_off = 2 * RET_QK // (hps * RET_DV)
    g_off = v_off + RET_HEADS // hps
    per_head = lambda shape: pl.BlockSpec((hps,) + shape, lambda b, h, c: (h, 0, 0))
    y, state = pl.pallas_call(
        functools.partial(_ret_prompt_kernel, tb=tb),
        grid=(bsz, RET_HEADS // hps, seq // tb),
        in_specs=[pl.BlockSpec((1, tb, hps * RET_DK), lambda b, h, c: (b, c, h)),
                  pl.BlockSpec((1, tb, hps * RET_DK), lambda b, h, c: (b, c, k_off + h)),
                  pl.BlockSpec((1, tb, hps * RET_DV), lambda b, h, c: (b, c, v_off + h)),
                  pl.BlockSpec((1, tb, hps * RET_DV), lambda b, h, c: (b, c, g_off + h)),
                  per_head((RET_CHUNK, RET_CHUNK)), per_head((RET_CHUNK, RET_DV)),
                  per_head((RET_CHUNK, RET_DK)), per_head((8, RET_DV)),
                  pl.BlockSpec((1, hps * RET_DV), lambda b, h, c: (0, h))],
        out_specs=[pl.BlockSpec((1, tb, hps * RET_DV), lambda b, h, c: (b, c, h)),
                   pl.BlockSpec((1, hps, RET_DK, RET_DV), lambda b, h, c: (b, h, 0, 0))],
        out_shape=[jax.ShapeDtypeStruct((bsz, seq, RET_V), BF16),
                   jax.ShapeDtypeStruct((bsz, RET_HEADS, RET_DK, RET_DV), F32)],
        scratch_shapes=[pltpu.VMEM((hps, RET_DK, RET_DV), F32)],
        compiler_params=_params(3), name="retention_prompt",
    )(proj, proj, proj, proj, decay, qd, kd, cd, gn)
    return y.reshape(bsz * seq, RET_V), state


def _ret_sample_kernel(p_ref, st_ref, dec_ref, qd_ref, kd_ref, cd_ref, gn_ref, y_ref, ns_ref):
    tp = SAMPLE_PAD
    proj = p_ref[0]
    for h in range(RET_HEADS):
        q = proj[:, h * RET_DK:(h + 1) * RET_DK].astype(BF16)
        k = proj[:, RET_QK + h * RET_DK:RET_QK + (h + 1) * RET_DK]
        v = proj[:, 2 * RET_QK + h * RET_DV:2 * RET_QK + (h + 1) * RET_DV]
        gate = proj[:, 2 * RET_QK + RET_V + h * RET_DV:2 * RET_QK + RET_V + (h + 1) * RET_DV]
        zk = jnp.zeros((LANES - tp, RET_DK), F32)
        zv = jnp.zeros((LANES - tp, RET_DV), F32)
        kp = jnp.concatenate([k, zk], axis=0).astype(BF16)
        kdp = jnp.concatenate([k * kd_ref[h], zk], axis=0).astype(BF16)
        vp = jnp.concatenate([v, zv], axis=0).astype(BF16)
        state = st_ref[0, h]
        scores = lax.dot_general(q, kp, _NT, preferred_element_type=F32) * dec_ref[h]
        inner = jnp.dot(scores.astype(BF16), vp, preferred_element_type=F32)
        cross = jnp.dot(q, state.astype(BF16), preferred_element_type=F32) * qd_ref[h]
        ns_ref[0, h] = cd_ref[h, 0:1, :] * state + lax.dot_general(kdp, vp, _TN, preferred_element_type=F32)
        y = _group_norm_gate(inner + cross, gate, gn_ref[:, h * RET_DV:(h + 1) * RET_DV])
        y_ref[0, :, h * RET_DV:(h + 1) * RET_DV] = y


def _ret_sample(proj, state, gn, t_new):
    bsz = proj.shape[0]
    tp = SAMPLE_PAD
    decay, q_decay, k_decay, chunk_decay = _decay_tables(t_new, tp)
    dec = jnp.pad(decay, ((0, 0), (0, 0), (0, LANES - tp)))
    qd = jnp.broadcast_to(q_decay[:, :, None], (RET_HEADS, tp, RET_DV))
    kd = jnp.broadcast_to(k_decay[:, :, None], (RET_HEADS, tp, RET_DK))
    cd = jnp.broadcast_to(chunk_decay[:, None, None], (RET_HEADS, 8, RET_DV))
    const = lambda a: pl.BlockSpec(a.shape, lambda b: (0,) * a.ndim)
    return pl.pallas_call(
        _ret_sample_kernel, grid=(bsz,),
        in_specs=[pl.BlockSpec((1,) + proj.shape[1:], lambda b: (b, 0, 0)),
                  pl.BlockSpec((1,) + state.shape[1:], lambda b: (b, 0, 0, 0)),
                  const(dec), const(qd), const(kd), const(cd), const(gn)],
        out_specs=[pl.BlockSpec((1, tp, RET_V), lambda b: (b, 0, 0)),
                   pl.BlockSpec((1,) + state.shape[1:], lambda b: (b, 0, 0, 0))],
        out_shape=[jax.ShapeDtypeStruct((bsz, tp, RET_V), F32),
                   jax.ShapeDtypeStruct(state.shape, F32)],
        compiler_params=_params(1), name="retention_sample",
    )(proj, state, dec, qd, kd, cd, gn)


def _top2_route(logits):
    lane = lax.broadcasted_iota(jnp.int32, logits.shape, 1)
    logits = jnp.where(lane < N_EXPERTS, logits, NEG)
    m1 = jnp.max(logits, axis=-1, keepdims=True)
    i1 = jnp.min(jnp.where(logits == m1, lane, LANES), axis=-1, keepdims=True)
    rest = jnp.where(lane == i1, NEG, logits)
    m2 = jnp.max(rest, axis=-1, keepdims=True)
    i2 = jnp.min(jnp.where(rest == m2, lane, LANES), axis=-1, keepdims=True)
    e2 = jnp.exp(m2 - m1)
    den = 1.0 + e2
    return jnp.where(lane == 0, 1.0 / den,
                     jnp.where(lane == 1, e2 / den,
                               jnp.where(lane == 2, i1.astype(F32),
                                         jnp.where(lane == 3, i2.astype(F32), 0.0))))


def _route_plan(e1, e2, tile):
    n_tok = e1.shape[0]
    e = jnp.concatenate([e1, e2])
    onehot = (e[:, None] == jnp.arange(N_EXPERTS, dtype=jnp.int32)[None]).astype(jnp.int32)
    csum = jnp.cumsum(onehot, axis=0)
    rank = jnp.take_along_axis(csum, e[:, None], axis=1)[:, 0] - 1
    tiles_per_expert = (csum[-1] + tile - 1) // tile
    tile_end = jnp.cumsum(tiles_per_expert)
    pos = ((tile_end - tiles_per_expert) * tile)[e] + rank
    n_tiles = (2 * n_tok) // tile + N_EXPERTS
    n_used = tile_end[-1]
    tile_ids = jnp.arange(n_tiles, dtype=jnp.int32)
    tile_expert = jnp.minimum(jnp.searchsorted(tile_end, tile_ids, side="right"), N_EXPERTS - 1)
    tile_expert = jnp.where(tile_ids < n_used, tile_expert, tile_expert[n_used - 1])
    meta = jnp.concatenate([tile_expert, n_used[None], tile_end]).astype(jnp.int32)
    return meta, pos[:n_tok], pos[n_tok:], n_tiles


ROW_DMA_UNROLL = 8


def _row_copy(src, dst, sem, src_row, dst_row):
    return pltpu.make_async_copy(src.at[pl.ds(src_row, 1), :], dst.at[pl.ds(dst_row, 1), :], sem)


def _start_row_gather(idx_ref, src_hbm, dst, sem, n_rows):
    def issue(r, carry):
        _row_copy(src_hbm, dst, sem, idx_ref[0, 0, r], r).start()
        return carry

    lax.fori_loop(0, n_rows, issue, 0, unroll=ROW_DMA_UNROLL)


def _start_row_scatter(idx_ref, src, dst_hbm, sem, n_rows):
    def issue(r, carry):
        _row_copy(src, dst_hbm, sem, r, idx_ref[0, 0, r]).start()
        return carry

    lax.fori_loop(0, n_rows, issue, 0, unroll=ROW_DMA_UNROLL)


def _wait_rows(hbm, vmem, sem):
    pltpu.make_async_copy(hbm.at[pl.ds(0, vmem.shape[0]), :], vmem, sem).wait()


def _dispatch_kernel(meta_ref, p1_ref, p2_ref, x_ref, xs_hbm, zbuf, sem, zsem, *, tm, tile, n_tiles):
    @pl.when(pl.program_id(0) == 0)
    def _():
        zbuf[...] = jnp.zeros_like(zbuf)
        zero_tile = lambda t: pltpu.make_async_copy(zbuf, xs_hbm.at[pl.ds(t * tile, tile), :], zsem)
        for e in range(N_EXPERTS):
            zero_tile(jnp.maximum(meta_ref[n_tiles + 1 + e] - 1, 0)).start()
        for e in range(N_EXPERTS):
            zero_tile(0).wait()

        def zero_unused(t, carry):
            zero_tile(t).start()
            zero_tile(t).wait()
            return carry

        lax.fori_loop(meta_ref[n_tiles], n_tiles, zero_unused, 0)

    _start_row_scatter(p1_ref, x_ref, xs_hbm, sem, tm)
    _start_row_scatter(p2_ref, x_ref, xs_hbm, sem, tm)
    for _ in range(2):
        pltpu.make_async_copy(x_ref, xs_hbm.at[pl.ds(0, tm), :], sem).wait()


def _dispatch(meta, pos1, pos2, x, tm, tile, n_tiles):
    m = x.shape[0]
    idx = lambda: pl.BlockSpec((1, 1, tm), lambda i, meta: (i, 0, 0), memory_space=pltpu.SMEM)
    return pl.pallas_call(
        functools.partial(_dispatch_kernel, tm=tm, tile=tile, n_tiles=n_tiles),
        grid_spec=pltpu.PrefetchScalarGridSpec(
            num_scalar_prefetch=1, grid=(m // tm,),
            in_specs=[idx(), idx(), pl.BlockSpec((tm, D_MODEL), lambda i, meta: (i, 0))],
            out_specs=pl.BlockSpec(memory_space=pl.ANY),
            scratch_shapes=[pltpu.VMEM((tile, D_MODEL), F32),
                            pltpu.SemaphoreType.DMA(()), pltpu.SemaphoreType.DMA(())]),
        out_shape=jax.ShapeDtypeStruct((n_tiles * tile, D_MODEL), F32),
        compiler_params=_params(1), name="moe_dispatch",
    )(meta, pos1.reshape(m // tm, 1, tm), pos2.reshape(m // tm, 1, tm), x)


def _expert_kernel(meta_ref, xs_ref, wg_ref, wu_ref, wd_ref, o_ref, xb_ref, *, n_tiles):
    f = pl.program_id(1)
    used = pl.program_id(0) < meta_ref[n_tiles]

    @pl.when(jnp.logical_and(jnp.logical_not(used), f == 0))
    def _():
        o_ref[...] = jnp.zeros_like(o_ref)

    @pl.when(jnp.logical_and(used, f == 0))
    def _():
        xb_ref[...] = xs_ref[...].astype(BF16)

    @pl.when(used)
    def _():
        part = _swiglu_chunks(xb_ref[...], wg_ref, wu_ref, wd_ref, (0,), wd_ref.shape[1], FF_CHUNK)

        @pl.when(f == 0)
        def _():
            o_ref[...] = part

        @pl.when(f > 0)
        def _():
            o_ref[...] += part


def _experts(meta, xs, w_gu, w_down, tile, n_tiles, tf):
    d_ff = w_down.shape[1]
    nf = d_ff // tf
    fcol = lambda j, f, meta: jnp.where(j < meta[n_tiles], f, nf - 1)
    return pl.pallas_call(
        functools.partial(_expert_kernel, n_tiles=n_tiles),
        grid_spec=pltpu.PrefetchScalarGridSpec(
            num_scalar_prefetch=1, grid=(n_tiles, nf),
            in_specs=[pl.BlockSpec((tile, D_MODEL), lambda j, f, meta: (j, 0)),
                      pl.BlockSpec((1, D_MODEL, tf), lambda j, f, meta: (meta[j], 0, fcol(j, f, meta))),
                      pl.BlockSpec((1, D_MODEL, tf), lambda j, f, meta: (meta[j], 0, nf + fcol(j, f, meta))),
                      pl.BlockSpec((1, tf, D_MODEL), lambda j, f, meta: (meta[j], fcol(j, f, meta), 0))],
            out_specs=pl.BlockSpec((tile, D_MODEL), lambda j, f, meta: (j, 0)),
            scratch_shapes=[pltpu.VMEM((tile, D_MODEL), BF16)]),
        out_shape=jax.ShapeDtypeStruct((n_tiles * tile, D_MODEL), F32),
        compiler_params=_params(2), name="moe_experts",
    )(meta, xs, w_gu, w_gu, w_down)


def _combine_kernel(p1_ref, p2_ref, n1_ref, n2_ref, rows_hbm, x_ref, r_ref, g_ref, b_ref, op_ref, os_ref,
                    buf1, buf2, sem1, sem2, *, tm, prompt_tiles):
    i = pl.program_id(0)
    slot = i % 2

    def start(pa_ref, pb_ref, s):
        _start_row_gather(pa_ref, rows_hbm, buf1.at[s], sem1.at[s], tm)
        _start_row_gather(pb_ref, rows_hbm, buf2.at[s], sem2.at[s], tm)

    @pl.when(i == 0)
    def _():
        start(p1_ref, p2_ref, 0)

    @pl.when(i + 1 < pl.num_programs(0))
    def _():
        start(n1_ref, n2_ref, 1 - slot)

    _wait_rows(rows_hbm, buf1.at[slot], sem1.at[slot])
    _wait_rows(rows_hbm, buf2.at[slot], sem2.at[slot])
    route = r_ref[...]
    y = route[:, 0:1] * buf1[slot] + route[:, 1:2] * buf2[slot]
    res = _layer_norm(ALPHA * x_ref[...] + y, g_ref[...], b_ref[...])

    @pl.when(i < prompt_tiles)
    def _():
        op_ref[...] = res

    @pl.when(i >= prompt_tiles)
    def _():
        os_ref[...] = res


def _combine(pos1, pos2, rows, x, route, g, b, tm, n_prompt):
    m = x.shape[0]
    pt = n_prompt // tm
    nt = m // tm
    idx = lambda: pl.BlockSpec((1, 1, tm), lambda i: (i, 0, 0), memory_space=pltpu.SMEM)
    nxt = lambda: pl.BlockSpec((1, 1, tm), lambda i: (jnp.minimum(i + 1, nt - 1), 0, 0), memory_space=pltpu.SMEM)
    vec = pl.BlockSpec((1, D_MODEL), lambda i: (0, 0))
    p1 = pos1.reshape(nt, 1, tm)
    p2 = pos2.reshape(nt, 1, tm)
    return pl.pallas_call(
        functools.partial(_combine_kernel, tm=tm, prompt_tiles=pt), grid=(nt,),
        in_specs=[idx(), idx(), nxt(), nxt(), pl.BlockSpec(memory_space=pl.ANY),
                  pl.BlockSpec((tm, D_MODEL), lambda i: (i, 0)),
                  pl.BlockSpec((tm, LANES), lambda i: (i, 0)), vec, vec],
        out_specs=[pl.BlockSpec((tm, D_MODEL), lambda i: (jnp.minimum(i, pt - 1), 0)),
                   pl.BlockSpec((tm, D_MODEL), lambda i: (jnp.maximum(i - pt, 0), 0))],
        out_shape=[jax.ShapeDtypeStruct((n_prompt, D_MODEL), F32),
                   jax.ShapeDtypeStruct((m - n_prompt, D_MODEL), F32)],
        scratch_shapes=[pltpu.VMEM((2, tm, D_MODEL), F32), pltpu.VMEM((2, tm, D_MODEL), F32),
                        pltpu.SemaphoreType.DMA((2,)), pltpu.SemaphoreType.DMA((2,))],
        compiler_params=_params(1), name="moe_combine",
    )(p1, p2, p1, p2, rows, x, route, g, b)


def _moe(x, route, n_prompt, w_gu, w_down, g, b, tm, tile, tf):
    e1 = route[:, 2].astype(jnp.int32)
    e2 = route[:, 3].astype(jnp.int32)
    meta, pos1, pos2, n_tiles = _route_plan(e1, e2, tile)
    xs = _dispatch(meta, pos1, pos2, x, tm, tile, n_tiles)
    rows = _experts(meta, xs, w_gu, w_down, tile, n_tiles, tf)
    return _combine(pos1, pos2, rows, x, route, g, b, tm, n_prompt)


def kernel(x_prompt, x_sample, cache_kv_w128, cache_kv_w512, cache_kv_w2048, state_ret,
           ln_g, ln_b, rel_bias, w_in_dil, w_out_dil, w_in_ret, ret_gn_g, w_out_ret,
           w_gu_dense, w_down_dense, w_router, w_gu_moe, w_down_moe):
    bsz, seq, _ = x_prompt.shape
    dbsz, t_new, _ = x_sample.shape
    tp = SAMPLE_PAD
    n_p = bsz * seq
    n_s = dbsz * tp
    caches = (cache_kv_w128, cache_kv_w512, cache_kv_w2048)

    w_in_dil_b = w_in_dil.astype(BF16)
    w_out_dil_b = w_out_dil.astype(BF16)
    w_in_ret_b = w_in_ret.astype(BF16)
    w_out_ret_b = w_out_ret.astype(BF16)
    w_gu_dense_b = w_gu_dense.astype(BF16)
    w_down_dense_b = w_down_dense.astype(BF16)
    w_gu_moe_b = w_gu_moe.astype(BF16)
    w_down_moe_b = w_down_moe.astype(BF16)
    lng = ln_g.reshape(DEPTH, 2, 1, D_MODEL)
    lnb = ln_b.reshape(DEPTH, 2, 1, D_MODEL)
    gn = ret_gn_g.reshape(1, RET_V)

    hp = x_prompt.reshape(n_p, D_MODEL)
    hs = jnp.pad(x_sample, ((0, 0), (0, tp - t_new), (0, 0))).reshape(n_s, D_MODEL)

    tm = min(1024, seq)
    qkv_groups, (kv128_p, kv512_p, kv2048_p) = _proj_dil_prompt(hp, w_in_dil_b, bsz, seq, tm)
    outs, lses = [], []
    for g, (window, dil) in enumerate(DIL_GROUPS):
        qkv_g = qkv_groups[g].reshape(bsz * dil, seq // dil, G_COLS)
        tbl = _prompt_table(rel_bias, g, window, dil)
        o, l = _attn_prompt(qkv_g, tbl, min(512, seq // dil))
        outs.append(o.reshape(bsz, dil, seq // dil, A_WIDTH))
        lses.append(l.reshape(bsz, dil, seq // dil, A_WIDTH))
    hp = _merge_out(outs, lses, w_out_dil_b, hp, lng[0, 0], lnb[0, 0], 512)

    qkv_s = _matmul(hs, w_in_dil_b, G_COLS)
    qkv_s3 = qkv_s.reshape(dbsz, tp, N_GROUPS * G_COLS)
    tabs = [_sample_tables(rel_bias, g, window, dil, caches[g].shape[1], t_new)
            for g, (window, dil) in enumerate(DIL_GROUPS)]
    caches2 = [c.reshape(c.shape[0], c.shape[1], 2 * A_WIDTH) for c in caches]
    mixed_s = _attn_sample(qkv_s3, caches2, [t[0] for t in tabs], [t[1] for t in tabs])
    hs = _mm_ln(mixed_s.reshape(n_s, A_WIDTH), w_out_dil_b, hs, lng[0, 0], lnb[0, 0], n_s)
    rows_s = []
    for g in range(N_GROUPS):
        kv = qkv_s3[:, :t_new, g * G_COLS + A_WIDTH:(g + 1) * G_COLS]
        rows_s.append(kv.reshape(dbsz, t_new, 2, H_SLOT, HEAD_DIM))

    hp = _ffn_dense(hp, w_gu_dense_b, w_down_dense_b, lng[0, 1], lnb[0, 1], 512)
    hs = _ffn_dense(hs, w_gu_dense_b, w_down_dense_b, lng[0, 1], lnb[0, 1], n_s)

    cos_p, sin_p = _rope_tables(jnp.arange(seq, dtype=jnp.int32))
    pos_s = jnp.tile(PAST_LEN + jnp.arange(tp, dtype=jnp.int32), dbsz)
    cos_s, sin_s = _rope_tables(pos_s)
    proj_p = _proj_ret(hp, w_in_ret_b, cos_p, sin_p, tm, BF16)
    y_p, ret_p = _ret_prompt(proj_p, gn, bsz, seq, min(1024, seq))
    proj_s = _proj_ret(hs, w_in_ret_b, cos_s, sin_s, n_s, F32)
    y_s, ret_s = _ret_sample(proj_s.reshape(dbsz, tp, -1), state_ret, gn, t_new)
    h_all, route = _mm_ln_pair(y_p, y_s.reshape(n_s, RET_V), w_out_ret_b, hp, hs, lng[1, 0], lnb[1, 0],
                               w_router, 512)

    out_p, out_s = _moe(h_all, route, n_p, w_gu_moe_b, w_down_moe_b, lng[1, 1], lnb[1, 1], 512, 1024,
                        w_down_moe.shape[1] // 2)

    y_prompt = out_p.reshape(bsz, seq, D_MODEL)
    y_sample = out_s.reshape(dbsz, tp, D_MODEL)[:, :t_new]
    shape5 = lambda a: a.reshape(a.shape[0], a.shape[1], 2, H_SLOT, HEAD_DIM)
    return (y_prompt, y_sample, shape5(kv128_p), shape5(kv512_p), shape5(kv2048_p), ret_p,
            rows_s[0], rows_s[1], rows_s[2], ret_s)
```

```python
import functools

import jax
import jax.numpy as jnp
import numpy as np
from jax import lax
from jax.experimental import pallas as pl
from jax.experimental.pallas import tpu as pltpu

F32 = jnp.float32
BF16 = jnp.bfloat16

DEPTH = 2
D_MODEL = 1024
PAST_LEN = 16384
DIL_GROUPS = ((128, 1), (512, 4), (2048, 16))
N_GROUPS = 3
H_SLOT = 8
HEAD_DIM = 64
A_WIDTH = H_SLOT * HEAD_DIM
G_COLS = 3 * A_WIDTH
N_BUCKETS = 32
MAX_DISTANCE = 2048
RET_HEADS = 4
RET_DK = 256
RET_DV = 512
RET_CHUNK = 128
RET_HPS = 4
ROPE_BASE = 10000.0
RET_QK = RET_HEADS * RET_DK
RET_V = RET_HEADS * RET_DV
N_EXPERTS = 8
LN_EPS = 1e-5
GN_EPS = 1e-5
ALPHA = (2 * DEPTH) ** 0.25
NEG = -1e30

LANES = 128
ATT_BLOCK = 128
PERM = 256
FF_CHUNK = 512
SAMPLE_PAD = 16
VMEM_LIMIT = 56 * 1024 * 1024

_NT = (((1,), (1,)), ((), ()))
_TN = (((0,), (0,)), ((), ()))


def _params(n_grid):
    return pltpu.CompilerParams(dimension_semantics=("arbitrary",) * n_grid,
                                vmem_limit_bytes=VMEM_LIMIT)


def _layer_norm(z, g, b):
    mu = jnp.mean(z, axis=-1, keepdims=True)
    zc = z - mu
    var = jnp.mean(zc * zc, axis=-1, keepdims=True)
    return zc * lax.rsqrt(var + LN_EPS) * g + b


def _silu(a):
    return a / (1.0 + jnp.exp(-a))


def _t5_bucket(dist):
    max_exact = N_BUCKETS // 2
    d = np.asarray(dist, dtype=np.int64)
    scaled = np.log(np.maximum(d, 1) / max_exact) / np.log(MAX_DISTANCE / max_exact)
    large = np.minimum(max_exact + (scaled * (N_BUCKETS - max_exact)).astype(np.int32), N_BUCKETS - 1)
    return np.where(d < max_exact, d, large).astype(np.int32)


def _group_bias(rel_bias, g, window, dil):
    n_keys = window // dil + 1
    buckets = jnp.asarray(_t5_bucket(np.arange(n_keys) * dil))
    return rel_bias[buckets][:, g * H_SLOT:(g + 1) * H_SLOT].T.astype(F32)


def _matmul_kernel(x_ref, w_ref, o_ref):
    o_ref[...] = jnp.dot(x_ref[...].astype(BF16), w_ref[...], preferred_element_type=F32)


def _matmul(x, w, tn):
    m, k = x.shape
    n = w.shape[1]
    return pl.pallas_call(
        _matmul_kernel,
        grid=(n // tn,),
        in_specs=[pl.BlockSpec((m, k), lambda j: (0, 0)),
                  pl.BlockSpec((k, tn), lambda j: (0, j))],
        out_specs=pl.BlockSpec((m, tn), lambda j: (0, j)),
        out_shape=jax.ShapeDtypeStruct((m, n), F32),
        compiler_params=_params(1),
        name="matmul_sample",
    )(x, w)


def _deinterleave_matrix(dil):
    p = np.zeros((PERM, PERM), np.float32)
    rows = PERM // dil
    for r in range(dil):
        for m in range(rows):
            p[r * rows + m, m * dil + r] = 1.0
    return p


def _proj_dil_kernel(x_ref, w_ref, p1_ref, p2_ref, q0_ref, q1_ref, q2_ref, kv0_ref, kv1_ref, kv2_ref, xperm, xb_ref,
                     *, tm, tpb, keeps):
    g = pl.program_id(1)
    j = pl.program_id(0) % tpb

    @pl.when(g == 0)
    def _():
        xb_ref[...] = x_ref[...].astype(BF16)
    q_refs = (q0_ref, q1_ref, q2_ref)
    kv_refs = (kv0_ref, kv1_ref, kv2_ref)
    perms = (None, p1_ref, p2_ref)

    def store_q(ref, r, row0, n_rows, acc, src0):
        ref[0, r, row0:row0 + n_rows, :A_WIDTH] = (acc[src0:src0 + n_rows, :A_WIDTH] * (HEAD_DIM ** -0.5)).astype(BF16)
        ref[0, r, row0:row0 + n_rows, A_WIDTH:] = acc[src0:src0 + n_rows, A_WIDTH:].astype(BF16)

    for gi, (_, dil) in enumerate(DIL_GROUPS):
        keep = keeps[gi]
        in_tail = (j >= tpb - keep // tm) if keep >= tm else (j == tpb - 1)
        row0 = 0 if keep >= tm else tm - keep

        @pl.when(g == gi)
        def _(gi=gi, dil=dil, in_tail=in_tail, row0=row0):
            xb = xb_ref[...]
            if dil == 1:
                acc = jnp.dot(xb, w_ref[...], preferred_element_type=F32)
                store_q(q_refs[gi], 0, 0, tm, acc, 0)
            else:
                for sub in range(tm // PERM):
                    blk = xb[sub * PERM:(sub + 1) * PERM]
                    xperm[sub * PERM:(sub + 1) * PERM, :] = jnp.dot(
                        perms[gi][...], blk, preferred_element_type=F32).astype(BF16)
                acc = jnp.dot(xperm[...], w_ref[...], preferred_element_type=F32)
                rows = PERM // dil
                for sub in range(tm // PERM):
                    for r in range(dil):
                        store_q(q_refs[gi], r, sub * rows, rows, acc, sub * PERM + r * rows)

            @pl.when(in_tail)
            def _():
                if dil == 1:
                    kv_refs[gi][0] = acc[row0:, A_WIDTH:]
                else:
                    kv_refs[gi][0] = jnp.dot(xb[row0:], w_ref[:, A_WIDTH:], preferred_element_type=F32)


def _proj_dil_prompt(x, w, bsz, seq, tm):
    tpb = seq // tm
    keeps = tuple(min(wd, seq) for wd, _ in DIL_GROUPS)

    def kv_spec(keep):
        if keep >= tm:
            first = tpb - keep // tm
            return pl.BlockSpec((1, tm, 2 * A_WIDTH),
                                lambda i, g: (i // tpb, jnp.maximum(i % tpb - first, 0), 0))
        return pl.BlockSpec((1, keep, 2 * A_WIDTH), lambda i, g: (i // tpb, 0, 0))

    perm = lambda d: jnp.asarray(_deinterleave_matrix(d), BF16)
    const = pl.BlockSpec((PERM, PERM), lambda i, g: (0, 0))
    outs = pl.pallas_call(
        functools.partial(_proj_dil_kernel, tm=tm, tpb=tpb, keeps=keeps),
        grid=(bsz * tpb, N_GROUPS),
        in_specs=[pl.BlockSpec((tm, D_MODEL), lambda i, g: (i, 0)),
                  pl.BlockSpec((D_MODEL, G_COLS), lambda i, g: (0, g)), const, const],
        out_specs=[pl.BlockSpec((1, d, tm // d, G_COLS), lambda i, g: (i // tpb, 0, i % tpb, 0))
                   for _, d in DIL_GROUPS] + [kv_spec(k) for k in keeps],
        out_shape=[jax.ShapeDtypeStruct((bsz, d, seq // d, G_COLS), BF16) for _, d in DIL_GROUPS]
        + [jax.ShapeDtypeStruct((bsz, k, 2 * A_WIDTH), F32) for k in keeps],
        scratch_shapes=[pltpu.VMEM((tm, D_MODEL), BF16), pltpu.VMEM((tm, D_MODEL), BF16)],
        compiler_params=_params(2),
        name="proj_dil_prompt",
    )(x, w, perm(DIL_GROUPS[1][1]), perm(DIL_GROUPS[2][1]))
    return outs[:3], outs[3:]


def _attn_prompt_kernel(q_ref, k_ref, v_ref, kp_ref, vp_ref, tbl_ref, o_ref, lse_ref, kbuf, vbuf, *, tq):
    n = pl.program_id(1)
    kbuf[0:ATT_BLOCK, :] = kp_ref[0]
    kbuf[ATT_BLOCK:, :] = k_ref[0]
    vbuf[0:ATT_BLOCK, :] = vp_ref[0]
    vbuf[ATT_BLOCK:, :] = v_ref[0]
    low = lax.broadcasted_iota(jnp.int32, (ATT_BLOCK, LANES), 1) < HEAD_DIM

    def body(m, carry):
        r0 = pl.multiple_of(m * ATT_BLOCK, ATT_BLOCK)
        first = jnp.where(jnp.logical_and(n == 0, m == 0), 1, 0)
        for p in range(A_WIDTH // LANES):
            cols = slice(p * LANES, (p + 1) * LANES)
            qm = q_ref[0, pl.ds(r0, ATT_BLOCK), cols]
            keys = kbuf[pl.ds(r0, 2 * ATT_BLOCK), cols]
            vals = vbuf[pl.ds(r0, 2 * ATT_BLOCK), cols]
            outs, lses = [], []
            for a in range(2):
                qa = jnp.where(low if a == 0 else jnp.logical_not(low), qm, jnp.zeros_like(qm))
                s = lax.dot_general(qa, keys, _NT, preferred_element_type=F32) + tbl_ref[first, 2 * p + a]
                mx = jnp.max(s, axis=-1, keepdims=True)
                e = jnp.exp(s - mx)
                l = jnp.sum(e, axis=-1, keepdims=True)
                o = jnp.dot(e.astype(BF16), vals, preferred_element_type=F32)
                outs.append(o / l)
                lses.append(jnp.broadcast_to(mx + jnp.log(l), (ATT_BLOCK, LANES)))
            o_ref[0, pl.ds(r0, ATT_BLOCK), cols] = jnp.where(low, outs[0], outs[1]).astype(BF16)
            lse_ref[0, pl.ds(r0, ATT_BLOCK), cols] = jnp.where(low, lses[0], lses[1])
        return carry

    lax.fori_loop(0, tq // ATT_BLOCK, body, 0)


def _attn_prompt(qkv, tbl, tq):
    nb, length, _ = qkv.shape
    sub = tq // ATT_BLOCK
    cur = lambda c: pl.BlockSpec((1, tq, A_WIDTH), lambda s, n: (s, n, c))
    prev = lambda c: pl.BlockSpec((1, ATT_BLOCK, A_WIDTH), lambda s, n: (s, jnp.maximum(n * sub - 1, 0), c))
    return pl.pallas_call(
        functools.partial(_attn_prompt_kernel, tq=tq),
        grid=(nb, length // tq),
        in_specs=[cur(0), cur(1), cur(2), prev(1), prev(2),
                  pl.BlockSpec(tbl.shape, lambda s, n: (0, 0, 0, 0))],
        out_specs=[pl.BlockSpec((1, tq, A_WIDTH), lambda s, n: (s, n, 0))] * 2,
        out_shape=[jax.ShapeDtypeStruct((nb, length, A_WIDTH), BF16),
                   jax.ShapeDtypeStruct((nb, length, A_WIDTH), F32)],
        scratch_shapes=[pltpu.VMEM((tq + ATT_BLOCK, A_WIDTH), BF16)] * 2,
        compiler_params=_params(2),
        name="attn_prompt",
    )(qkv, qkv, qkv, qkv, qkv, tbl)


def _toeplitz(u, n_rows, n_cols):
    h, n = u.shape
    assert n == n_rows + n_cols - 1
    up = jnp.pad(u, ((0, 0), (0, 1)))
    w = jnp.tile(up, (1, n_rows))[:, :n_rows * n].reshape(h, n_rows, n)
    return w[:, :, n_rows - 1:n_rows - 1 + n_cols]


def _bias_by_offset(bias, dist, valid):
    return jnp.where(valid[None], bias[:, np.clip(dist, 0, bias.shape[1] - 1)], NEG)


def _prompt_table(rel_bias, g, window, dil):
    bias = _group_bias(rel_bias, g, window, dil)
    k = np.arange(3 * ATT_BLOCK - 1)
    dist = 2 * ATT_BLOCK - 1 - k
    u = _bias_by_offset(bias, dist, (dist >= 0) & (dist <= window // dil))
    tbl = _toeplitz(u, ATT_BLOCK, 2 * ATT_BLOCK)
    c = np.arange(2 * ATT_BLOCK)[None, None, :]
    tbl_first = jnp.where(c < ATT_BLOCK, NEG, tbl)
    return jnp.stack([tbl, tbl_first], 0)


def _split3(x):
    hi = x.astype(BF16)
    r1 = x - hi.astype(F32)
    mid = r1.astype(BF16)
    lo = (r1 - mid.astype(F32)).astype(BF16)
    return hi, mid, lo


def _merge_out_kernel(o0, o1, o2, l0, l1, l2, pt1_ref, pt2_ref, w_ref, x_ref, g_ref, b_ref, out_ref, *, tm):
    pts = (None, pt1_ref, pt2_ref)

    def natural(ref, gi, exact_f32):
        dil = DIL_GROUPS[gi][1]
        if dil == 1:
            return ref[0, 0].astype(F32)
        rows = PERM // dil
        blocks = []
        for sub in range(tm // PERM):
            piece = jnp.concatenate([ref[0, r, sub * rows:(sub + 1) * rows, :] for r in range(dil)], axis=0)
            parts = _split3(piece) if exact_f32 else (piece,)
            blocks.append(sum(jnp.dot(pts[gi][...], p, preferred_element_type=F32) for p in parts))
        return jnp.concatenate(blocks, axis=0)

    os_ = [natural(r, gi, False) for gi, r in enumerate((o0, o1, o2))]
    ls = [natural(r, gi, True) for gi, r in enumerate((l0, l1, l2))]
    mx = jnp.maximum(jnp.maximum(ls[0], ls[1]), ls[2])
    es = [jnp.exp(l - mx) for l in ls]
    mixed = (es[0] * os_[0] + es[1] * os_[1] + es[2] * os_[2]) / (es[0] + es[1] + es[2])
    acc = jnp.dot(mixed.astype(BF16), w_ref[...], preferred_element_type=F32)
    out_ref[...] = _layer_norm(ALPHA * x_ref[...] + acc, g_ref[...], b_ref[...])


def _merge_out(os_, ls_, w, x, g, b, tm):
    bsz, _, seq, _ = os_[0].shape
    tpb = seq // tm
    grp = lambda d: pl.BlockSpec((1, d, tm // d, A_WIDTH), lambda i: (i // tpb, 0, i % tpb, 0))
    specs = [grp(d) for _, d in DIL_GROUPS]
    pt = lambda d: jnp.asarray(_deinterleave_matrix(d).T, BF16)
    const = pl.BlockSpec((PERM, PERM), lambda i: (0, 0))
    vec = pl.BlockSpec((1, D_MODEL), lambda i: (0, 0))
    return pl.pallas_call(
        functools.partial(_merge_out_kernel, tm=tm), grid=(bsz * tpb,),
        in_specs=specs + specs + [const, const, pl.BlockSpec((A_WIDTH, D_MODEL), lambda i: (0, 0)),
                                  pl.BlockSpec((tm, D_MODEL), lambda i: (i, 0)), vec, vec],
        out_specs=pl.BlockSpec((tm, D_MODEL), lambda i: (i, 0)),
        out_shape=jax.ShapeDtypeStruct((bsz * seq, D_MODEL), F32),
        compiler_params=_params(1), name="merge_outproj_ln",
    )(*os_, *ls_, pt(DIL_GROUPS[1][1]), pt(DIL_GROUPS[2][1]), w, x, g, b)


def _attn_sample_kernel(qkv_ref, c0_ref, c1_ref, c2_ref, tb0, tb1, tb2, tn0, tn1, tn2, out_ref, *, t_new):
    lane = lax.broadcasted_iota(jnp.int32, (H_SLOT, LANES), 1)
    groups = ((c0_ref, tb0, tn0), (c1_ref, tb1, tn1), (c2_ref, tb2, tn2))
    out_ref[0] = jnp.zeros(out_ref.shape[1:], F32)

    def column(tile, m):
        return jnp.sum(jnp.where(lane == m, tile, 0.0), axis=-1, keepdims=True)

    for t in range(t_new):
        scores = []
        for g, (c_ref, tb, tn) in enumerate(groups):
            r = t % DIL_GROUPS[g][1]
            q = qkv_ref[0, t, 3 * g] * (HEAD_DIM ** -0.5)

            def qk(m, s, c_ref=c_ref, r=r, q=q):
                return jnp.where(lane == m, jnp.sum(c_ref[0, m, r, 0] * q, axis=-1, keepdims=True), s)

            s_c = lax.fori_loop(0, ATT_BLOCK, qk, jnp.zeros((H_SLOT, LANES), F32), unroll=True)
            s_n = jnp.zeros((H_SLOT, LANES), F32)
            for t2 in range(t_new):
                s_n = jnp.where(lane == t2, jnp.sum(qkv_ref[0, t2, 3 * g + 1] * q, axis=-1, keepdims=True), s_n)
            scores += [s_c + tb[t], s_n + tn[t]]
        mx = functools.reduce(jnp.maximum, [jnp.max(s, axis=-1, keepdims=True) for s in scores])
        es = [jnp.exp(s - mx) for s in scores]
        den = functools.reduce(jnp.add, [jnp.sum(e, axis=-1, keepdims=True) for e in es])
        acc = jnp.zeros((H_SLOT, HEAD_DIM), F32)
        for g, (c_ref, _, _) in enumerate(groups):
            r = t % DIL_GROUPS[g][1]
            e_c, e_n = es[2 * g], es[2 * g + 1]

            def pv(m, a, c_ref=c_ref, r=r, e_c=e_c):
                return a + column(e_c, m) * c_ref[0, m, r, 1]

            acc = lax.fori_loop(0, ATT_BLOCK, pv, acc, unroll=True)
            for t2 in range(t_new):
                acc = acc + column(e_n, t2) * qkv_ref[0, t2, 3 * g + 2]
        out_ref[0, t] = acc / den


def _sample_tables(rel_bias, g, window, dil, length, t_new):
    bias = _group_bias(rel_bias, g, window, dil)
    n_keys = window // dil + 1
    t = np.arange(t_new)[:, None]
    m = np.arange(ATT_BLOCK)[None, :]
    dist = length + t - (t % dil + dil * m)
    tb = _bias_by_offset(bias, (dist // dil).reshape(-1), ((dist // dil < n_keys) & (dist > 0)).reshape(-1))
    t2 = np.arange(LANES)[None, :]
    dist2 = t - t2
    ok2 = (dist2 >= 0) & (dist2 % dil == 0) & (dist2 // dil < n_keys) & (t2 < t_new)
    tn = _bias_by_offset(bias, (dist2 // dil).reshape(-1), ok2.reshape(-1))
    shape = lambda a: a.reshape(H_SLOT, t_new, LANES).transpose(1, 0, 2)
    return shape(tb), shape(tn)


def _attn_sample(qkv, caches, tables_c, tables_n, t_new):
    bsz = qkv.shape[0]
    views, specs = [], []
    for c, (_, dil) in zip(caches, DIL_GROUPS):
        assert c.shape[1] == ATT_BLOCK * dil and t_new <= ATT_BLOCK
        views.append(c.reshape(bsz, ATT_BLOCK, dil, 2, H_SLOT, HEAD_DIM))
        specs.append(pl.BlockSpec((1, ATT_BLOCK, min(t_new, dil), 2, H_SLOT, HEAD_DIM),
                                  lambda b: (b, 0, 0, 0, 0, 0)))
    const = lambda a: pl.BlockSpec(a.shape, lambda b: (0, 0, 0))
    return pl.pallas_call(
        functools.partial(_attn_sample_kernel, t_new=t_new),
        grid=(bsz,),
        in_specs=[pl.BlockSpec((1,) + qkv.shape[1:], lambda b: (b, 0, 0, 0, 0))] + specs
        + [const(t) for t in tables_c] + [const(t) for t in tables_n],
        out_specs=pl.BlockSpec((1, SAMPLE_PAD, H_SLOT, HEAD_DIM), lambda b: (b, 0, 0, 0)),
        out_shape=jax.ShapeDtypeStruct((bsz, SAMPLE_PAD, H_SLOT, HEAD_DIM), F32),
        compiler_params=_params(1),
        name="attn_sample",
    )(qkv, *views, *tables_c, *tables_n)


def _mm_ln_kernel(y_ref, w_ref, x_ref, g_ref, b_ref, o_ref):
    acc = jnp.dot(y_ref[...].astype(BF16), w_ref[...], preferred_element_type=F32)
    o_ref[...] = _layer_norm(ALPHA * x_ref[...] + acc, g_ref[...], b_ref[...])


def _mm_ln(y, w, x, g, b, tm):
    m, k = y.shape
    vec = pl.BlockSpec((1, D_MODEL), lambda i: (0, 0))
    return pl.pallas_call(
        _mm_ln_kernel, grid=(m // tm,),
        in_specs=[pl.BlockSpec((tm, k), lambda i: (i, 0)), pl.BlockSpec((k, D_MODEL), lambda i: (0, 0)),
                  pl.BlockSpec((tm, D_MODEL), lambda i: (i, 0)), vec, vec],
        out_specs=pl.BlockSpec((tm, D_MODEL), lambda i: (i, 0)),
        out_shape=jax.ShapeDtypeStruct((m, D_MODEL), F32),
        compiler_params=_params(1), name="outproj_ln",
    )(y, w, x, g, b)


def _mm_ln_pair_kernel(yp_ref, ys_ref, w_ref, xp_ref, xs_ref, g_ref, b_ref, wr_ref, o_ref, route_ref, *, prompt_tiles):
    def run(y_ref, x_ref):
        acc = jnp.dot(y_ref[...].astype(BF16), w_ref[...], preferred_element_type=F32)
        h = _layer_norm(ALPHA * x_ref[...] + acc, g_ref[...], b_ref[...])
        o_ref[...] = h
        route_ref[...] = _top2_route(jnp.dot(h.astype(BF16), wr_ref[...], preferred_element_type=F32))

    pl.when(pl.program_id(0) < prompt_tiles)(lambda: run(yp_ref, xp_ref))
    pl.when(pl.program_id(0) >= prompt_tiles)(lambda: run(ys_ref, xs_ref))


def _mm_ln_pair(y_p, y_s, w, x_p, x_s, g, b, w_router, tm):
    n_p, k = y_p.shape
    n_s = y_s.shape[0]
    pt = n_p // tm
    first = lambda width: pl.BlockSpec((tm, width), lambda i: (jnp.minimum(i, pt - 1), 0))
    second = lambda width: pl.BlockSpec((tm, width), lambda i: (jnp.maximum(i - pt, 0), 0))
    vec = pl.BlockSpec((1, D_MODEL), lambda i: (0, 0))
    wr = jnp.pad(w_router, ((0, 0), (0, LANES - N_EXPERTS))).astype(BF16)
    return pl.pallas_call(
        functools.partial(_mm_ln_pair_kernel, prompt_tiles=pt), grid=((n_p + n_s) // tm,),
        in_specs=[first(k), second(k), pl.BlockSpec((k, D_MODEL), lambda i: (0, 0)),
                  first(D_MODEL), second(D_MODEL), vec, vec, pl.BlockSpec((D_MODEL, LANES), lambda i: (0, 0))],
        out_specs=[pl.BlockSpec((tm, D_MODEL), lambda i: (i, 0)), pl.BlockSpec((tm, LANES), lambda i: (i, 0))],
        out_shape=[jax.ShapeDtypeStruct((n_p + n_s, D_MODEL), F32),
                   jax.ShapeDtypeStruct((n_p + n_s, LANES), F32)],
        compiler_params=_params(1), name="outproj_ln_route",
    )(y_p, y_s, w, x_p, x_s, g, b, wr)


def _swiglu_chunks(xb, wg_ref, wu_ref, wd_ref, lead, width, chunk):
    acc = None
    for c0 in range(0, width, chunk):
        cs = slice(c0, min(c0 + chunk, width))
        a = jnp.dot(xb, wg_ref[lead + (slice(None), cs)], preferred_element_type=F32)
        u = jnp.dot(xb, wu_ref[lead + (slice(None), cs)], preferred_element_type=F32)
        part = jnp.dot((_silu(a) * u).astype(BF16), wd_ref[lead + (cs, slice(None))], preferred_element_type=F32)
        acc = part if acc is None else acc + part
    return acc


def _ffn_kernel(x_ref, wg_ref, wu_ref, wd_ref, g_ref, b_ref, o_ref, *, d_ff):
    x = x_ref[...]
    acc = _swiglu_chunks(x.astype(BF16), wg_ref, wu_ref, wd_ref, (), d_ff, FF_CHUNK)
    o_ref[...] = _layer_norm(ALPHA * x + acc, g_ref[...], b_ref[...])


def _ffn_dense(x, w_gu, w_down, g, b, tm):
    m = x.shape[0]
    d_ff = w_down.shape[0]
    vec = pl.BlockSpec((1, D_MODEL), lambda i: (0, 0))
    return pl.pallas_call(
        functools.partial(_ffn_kernel, d_ff=d_ff), grid=(m // tm,),
        in_specs=[pl.BlockSpec((tm, D_MODEL), lambda i: (i, 0)),
                  pl.BlockSpec((D_MODEL, d_ff), lambda i: (0, 0)),
                  pl.BlockSpec((D_MODEL, d_ff), lambda i: (0, 1)),
                  pl.BlockSpec((d_ff, D_MODEL), lambda i: (0, 0)), vec, vec],
        out_specs=pl.BlockSpec((tm, D_MODEL), lambda i: (i, 0)),
        out_shape=jax.ShapeDtypeStruct((m, D_MODEL), F32),
        compiler_params=_params(1), name="ffn_dense",
    )(x, w_gu, w_gu, w_down, g, b)


def _proj_ret_kernel(x_ref, w_ref, cos_ref, sin_ref, o_ref, xb_ref, *, scale_k):
    n = pl.program_id(1)

    @pl.when(n == 0)
    def _():
        xb_ref[...] = x_ref[...].astype(BF16)

    xb = xb_ref[...]
    half = RET_DK // 2

    @pl.when(n < 2)
    def _():
        cos = cos_ref[...]
        sin = sin_ref[...]
        scale = jnp.where(n == 1, scale_k, 1.0)
        for h in range(RET_HEADS):
            acc = jnp.dot(xb, w_ref[:, h * RET_DK:(h + 1) * RET_DK], preferred_element_type=F32)
            x1 = acc[:, :half]
            x2 = acc[:, half:]
            o_ref[:, h * RET_DK:h * RET_DK + half] = ((x1 * cos - x2 * sin) * scale).astype(o_ref.dtype)
            o_ref[:, h * RET_DK + half:(h + 1) * RET_DK] = ((x1 * sin + x2 * cos) * scale).astype(o_ref.dtype)

    @pl.when(n >= 2)
    def _():
        for h in range(RET_HEADS):
            cs = slice(h * RET_DK, (h + 1) * RET_DK)
            o_ref[:, cs] = jnp.dot(xb, w_ref[:, cs], preferred_element_type=F32).astype(o_ref.dtype)


def _proj_ret(x, w, cos, sin, tm, out_dtype):
    m = x.shape[0]
    n_cols = w.shape[1]
    tn = RET_QK
    pos_tiles = cos.shape[0] // tm
    return pl.pallas_call(
        functools.partial(_proj_ret_kernel, scale_k=RET_DK ** -0.5),
        grid=(m // tm, n_cols // tn),
        in_specs=[pl.BlockSpec((tm, D_MODEL), lambda i, n: (i, 0)),
                  pl.BlockSpec((D_MODEL, tn), lambda i, n: (0, n)),
                  pl.BlockSpec((tm, RET_DK // 2), lambda i, n: (i % pos_tiles, 0)),
                  pl.BlockSpec((tm, RET_DK // 2), lambda i, n: (i % pos_tiles, 0))],
        out_specs=pl.BlockSpec((tm, tn), lambda i, n: (i, n)),
        out_shape=jax.ShapeDtypeStruct((m, n_cols), out_dtype),
        scratch_shapes=[pltpu.VMEM((tm, D_MODEL), BF16)],
        compiler_params=_params(2), name="proj_ret",
    )(x, w, cos, sin)


def _rope_tables(pos):
    half = RET_DK // 2
    inv = 1.0 / (ROPE_BASE ** (jnp.arange(half, dtype=F32) / half))
    ang = pos.astype(F32)[:, None] * inv[None]
    return jnp.cos(ang), jnp.sin(ang)


def _log_gamma():
    return jnp.log(1.0 - 2.0 ** (-5.0 - jnp.arange(RET_HEADS, dtype=F32)))


def _decay_tables(c, rows):
    lg = _log_gamma()
    n = jnp.arange(rows, dtype=F32)
    live = n < c
    diff = n[:, None] - n[None, :]
    decay = jnp.where((diff >= 0)[None] & live[None, None, :],
                      jnp.exp(jnp.maximum(diff, 0.0)[None] * lg[:, None, None]), 0.0)
    q_decay = jnp.exp((n[None, :] + 1.0) * lg[:, None])
    k_decay = jnp.where(live[None], jnp.exp((c - 1.0 - n)[None, :] * lg[:, None]), 0.0)
    chunk_decay = jnp.exp(c * lg)
    return decay, q_decay, k_decay, chunk_decay


def _group_norm_gate(o, gate, gn):
    mu = jnp.mean(o, axis=-1, keepdims=True)
    oc = o - mu
    var = jnp.mean(oc * oc, axis=-1, keepdims=True)
    return _silu(gate) * (oc * lax.rsqrt(var + GN_EPS) * gn)


def _ret_prompt_kernel(q_ref, k_ref, v_ref, gate_ref, dec_ref, qd_ref, kd_ref, cd_ref, gn_ref,
                       y_ref, st_ref, s_ref, *, tb):
    cb = pl.program_id(2)

    @pl.when(cb == 0)
    def _():
        s_ref[...] = jnp.zeros_like(s_ref)

    def body(ci, carry):
        r0 = pl.multiple_of(ci * RET_CHUNK, RET_CHUNK)
        rows = pl.ds(r0, RET_CHUNK)
        for j in range(RET_HPS):
            kc = slice(j * RET_DK, (j + 1) * RET_DK)
            vc = slice(j * RET_DV, (j + 1) * RET_DV)
            q = q_ref[0, rows, kc]
            k = k_ref[0, rows, kc]
            v = v_ref[0, rows, vc]
            state = s_ref[j]
            scores = lax.dot_general(q, k, _NT, preferred_element_type=F32) * dec_ref[j]
            inner = jnp.dot(scores.astype(BF16), v, preferred_element_type=F32)
            cross = jnp.dot(q, state.astype(BF16), preferred_element_type=F32) * qd_ref[j]
            kd = (k.astype(F32) * kd_ref[j]).astype(BF16)
            s_ref[j] = cd_ref[j, 0:1, :] * state + lax.dot_general(kd, v, _TN, preferred_element_type=F32)
            y = _group_norm_gate(inner + cross, gate_ref[0, rows, vc].astype(F32), gn_ref[:, vc])
            y_ref[0, rows, vc] = y.astype(y_ref.dtype)
        return carry

    lax.fori_loop(0, tb // RET_CHUNK, body, 0)

    @pl.when(cb == pl.num_programs(2) - 1)
    def _():
        st_ref[0] = s_ref[...]


def _ret_prompt(proj, gn, bsz, seq, tb):
    decay, q_decay, k_decay, chunk_decay = _decay_tables(RET_CHUNK, RET_CHUNK)
    qd = jnp.broadcast_to(q_decay[:, :, None], (RET_HEADS, RET_CHUNK, RET_DV))
    kd = jnp.broadcast_to(k_decay[:, :, None], (RET_HEADS, RET_CHUNK, RET_DK))
    cd = jnp.broadcast_to(chunk_decay[:, None, None], (RET_HEADS, 8, RET_DV))
    proj = proj.reshape(bsz, seq, proj.shape[-1])
    hps = RET_HPS
    k_off = RET_QK // (hps * RET_DK)
    v_off = 2 * RET_QK // (hps * RET_DV)
    g_off = v_off + RET_HEADS // hps
    per_head = lambda shape: pl.BlockSpec((hps,) + shape, lambda b, h, c: (h, 0, 0))
    y, state = pl.pallas_call(
        functools.partial(_ret_prompt_kernel, tb=tb),
        grid=(bsz, RET_HEADS // hps, seq // tb),
        in_specs=[pl.BlockSpec((1, tb, hps * RET_DK), lambda b, h, c: (b, c, h)),
                  pl.BlockSpec((1, tb, hps * RET_DK), lambda b, h, c: (b, c, k_off + h)),
                  pl.BlockSpec((1, tb, hps * RET_DV), lambda b, h, c: (b, c, v_off + h)),
                  pl.BlockSpec((1, tb, hps * RET_DV), lambda b, h, c: (b, c, g_off + h)),
                  per_head((RET_CHUNK, RET_CHUNK)), per_head((RET_CHUNK, RET_DV)),
                  per_head((RET_CHUNK, RET_DK)), per_head((8, RET_DV)),
                  pl.BlockSpec((1, hps * RET_DV), lambda b, h, c: (0, h))],
        out_specs=[pl.BlockSpec((1, tb, hps * RET_DV), lambda b, h, c: (b, c, h)),
                   pl.BlockSpec((1, hps, RET_DK, RET_DV), lambda b, h, c: (b, h, 0, 0))],
        out_shape=[jax.ShapeDtypeStruct((bsz, seq, RET_V), BF16),
                   jax.ShapeDtypeStruct((bsz, RET_HEADS, RET_DK, RET_DV), F32)],
        scratch_shapes=[pltpu.VMEM((hps, RET_DK, RET_DV), F32)],
        compiler_params=_params(3), name="retention_prompt",
    )(proj, proj, proj, proj, decay, qd, kd, cd, gn)
    return y.reshape(bsz * seq, RET_V), state


def _ret_sample_kernel(p_ref, st_ref, dec_ref, qd_ref, kd_ref, cd_ref, gn_ref, y_ref, ns_ref):
    tp = SAMPLE_PAD
    proj = p_ref[0]
    for h in range(RET_HEADS):
        q = proj[:, h * RET_DK:(h + 1) * RET_DK].astype(BF16)
        k = proj[:, RET_QK + h * RET_DK:RET_QK + (h + 1) * RET_DK]
        v = proj[:, 2 * RET_QK + h * RET_DV:2 * RET_QK + (h + 1) * RET_DV]
        gate = proj[:, 2 * RET_QK + RET_V + h * RET_DV:2 * RET_QK + RET_V + (h + 1) * RET_DV]
        zk = jnp.zeros((LANES - tp, RET_DK), F32)
        zv = jnp.zeros((LANES - tp, RET_DV), F32)
        kp = jnp.concatenate([k, zk], axis=0).astype(BF16)
        kdp = jnp.concatenate([k * kd_ref[h], zk], axis=0).astype(BF16)
        vp = jnp.concatenate([v, zv], axis=0).astype(BF16)
        state = st_ref[0, h]
        scores = lax.dot_general(q, kp, _NT, preferred_element_type=F32) * dec_ref[h]
        inner = jnp.dot(scores.astype(BF16), vp, preferred_element_type=F32)
        cross = jnp.dot(q, state.astype(BF16), preferred_element_type=F32) * qd_ref[h]
        ns_ref[0, h] = cd_ref[h, 0:1, :] * state + lax.dot_general(kdp, vp, _TN, preferred_element_type=F32)
        y = _group_norm_gate(inner + cross, gate, gn_ref[:, h * RET_DV:(h + 1) * RET_DV])
        y_ref[0, :, h * RET_DV:(h + 1) * RET_DV] = y


def _ret_sample(proj, state, gn, t_new):
    bsz = proj.shape[0]
    tp = SAMPLE_PAD
    decay, q_decay, k_decay, chunk_decay = _decay_tables(t_new, tp)
    dec = jnp.pad(decay, ((0, 0), (0, 0), (0, LANES - tp)))
    qd = jnp.broadcast_to(q_decay[:, :, None], (RET_HEADS, tp, RET_DV))
    kd = jnp.broadcast_to(k_decay[:, :, None], (RET_HEADS, tp, RET_DK))
    cd = jnp.broadcast_to(chunk_decay[:, None, None], (RET_HEADS, 8, RET_DV))
    const = lambda a: pl.BlockSpec(a.shape, lambda b: (0,) * a.ndim)
    return pl.pallas_call(
        _ret_sample_kernel, grid=(bsz,),
        in_specs=[pl.BlockSpec((1,) + proj.shape[1:], lambda b: (b, 0, 0)),
                  pl.BlockSpec((1,) + state.shape[1:], lambda b: (b, 0, 0, 0)),
                  const(dec), const(qd), const(kd), const(cd), const(gn)],
        out_specs=[pl.BlockSpec((1, tp, RET_V), lambda b: (b, 0, 0)),
                   pl.BlockSpec((1,) + state.shape[1:], lambda b: (b, 0, 0, 0))],
        out_shape=[jax.ShapeDtypeStruct((bsz, tp, RET_V), F32),
                   jax.ShapeDtypeStruct(state.shape, F32)],
        compiler_params=_params(1), name="retention_sample",
    )(proj, state, dec, qd, kd, cd, gn)


def _top2_route(logits):
    lane = lax.broadcasted_iota(jnp.int32, logits.shape, 1)
    logits = jnp.where(lane < N_EXPERTS, logits, NEG)
    m1 = jnp.max(logits, axis=-1, keepdims=True)
    i1 = jnp.min(jnp.where(logits == m1, lane, LANES), axis=-1, keepdims=True)
    rest = jnp.where(lane == i1, NEG, logits)
    m2 = jnp.max(rest, axis=-1, keepdims=True)
    i2 = jnp.min(jnp.where(rest == m2, lane, LANES), axis=-1, keepdims=True)
    e2 = jnp.exp(m2 - m1)
    den = 1.0 + e2
    return jnp.where(lane == 0, 1.0 / den,
                     jnp.where(lane == 1, e2 / den,
                               jnp.where(lane == 2, i1.astype(F32),
                                         jnp.where(lane == 3, i2.astype(F32), 0.0))))


def _route_plan(e1, e2, tile):
    n_tok = e1.shape[0]
    e = jnp.concatenate([e1, e2])
    onehot = (e[:, None] == jnp.arange(N_EXPERTS, dtype=jnp.int32)[None]).astype(jnp.int32)
    csum = jnp.cumsum(onehot, axis=0)
    rank = jnp.take_along_axis(csum, e[:, None], axis=1)[:, 0] - 1
    tiles_per_expert = (csum[-1] + tile - 1) // tile
    tile_end = jnp.cumsum(tiles_per_expert)
    pos = ((tile_end - tiles_per_expert) * tile)[e] + rank
    n_tiles = (2 * n_tok) // tile + N_EXPERTS
    n_used = tile_end[-1]
    tile_ids = jnp.arange(n_tiles, dtype=jnp.int32)
    tile_expert = jnp.minimum(jnp.searchsorted(tile_end, tile_ids, side="right"), N_EXPERTS - 1)
    tile_expert = jnp.where(tile_ids < n_used, tile_expert, tile_expert[n_used - 1])
    meta = jnp.concatenate([tile_expert, n_used[None], tile_end]).astype(jnp.int32)
    return meta, pos[:n_tok], pos[n_tok:], n_tiles


ROW_DMA_UNROLL = 8


def _row_copy(src, dst, sem, src_row, dst_row):
    return pltpu.make_async_copy(src.at[pl.ds(src_row, 1), :], dst.at[pl.ds(dst_row, 1), :], sem)


def _start_row_gather(idx_ref, src_hbm, dst, sem, n_rows):
    def issue(r, carry):
        _row_copy(src_hbm, dst, sem, idx_ref[0, 0, r], r).start()
        return carry

    lax.fori_loop(0, n_rows, issue, 0, unroll=ROW_DMA_UNROLL)


def _start_row_scatter(idx_ref, src, dst_hbm, sem, n_rows):
    def issue(r, carry):
        _row_copy(src, dst_hbm, sem, r, idx_ref[0, 0, r]).start()
        return carry

    lax.fori_loop(0, n_rows, issue, 0, unroll=ROW_DMA_UNROLL)


def _wait_rows(hbm, vmem, sem):
    pltpu.make_async_copy(hbm.at[pl.ds(0, vmem.shape[0]), :], vmem, sem).wait()


def _dispatch_kernel(meta_ref, p1_ref, p2_ref, x_ref, xs_hbm, zbuf, sem, zsem, *, tm, tile, n_tiles):
    @pl.when(pl.program_id(0) == 0)
    def _():
        zbuf[...] = jnp.zeros_like(zbuf)
        zero_tile = lambda t: pltpu.make_async_copy(zbuf, xs_hbm.at[pl.ds(t * tile, tile), :], zsem)
        for e in range(N_EXPERTS):
            zero_tile(jnp.maximum(meta_ref[n_tiles + 1 + e] - 1, 0)).start()
        for e in range(N_EXPERTS):
            zero_tile(0).wait()

        def zero_unused(t, carry):
            zero_tile(t).start()
            zero_tile(t).wait()
            return carry

        lax.fori_loop(meta_ref[n_tiles], n_tiles, zero_unused, 0)

    _start_row_scatter(p1_ref, x_ref, xs_hbm, sem, tm)
    _start_row_scatter(p2_ref, x_ref, xs_hbm, sem, tm)
    for _ in range(2):
        pltpu.make_async_copy(x_ref, xs_hbm.at[pl.ds(0, tm), :], sem).wait()


def _dispatch(meta, pos1, pos2, x, tm, tile, n_tiles):
    m = x.shape[0]
    idx = lambda: pl.BlockSpec((1, 1, tm), lambda i, meta: (i, 0, 0), memory_space=pltpu.SMEM)
    return pl.pallas_call(
        functools.partial(_dispatch_kernel, tm=tm, tile=tile, n_tiles=n_tiles),
        grid_spec=pltpu.PrefetchScalarGridSpec(
            num_scalar_prefetch=1, grid=(m // tm,),
            in_specs=[idx(), idx(), pl.BlockSpec((tm, D_MODEL), lambda i, meta: (i, 0))],
            out_specs=pl.BlockSpec(memory_space=pl.ANY),
            scratch_shapes=[pltpu.VMEM((tile, D_MODEL), F32),
                            pltpu.SemaphoreType.DMA(()), pltpu.SemaphoreType.DMA(())]),
        out_shape=jax.ShapeDtypeStruct((n_tiles * tile, D_MODEL), F32),
        compiler_params=_params(1), name="moe_dispatch",
    )(meta, pos1.reshape(m // tm, 1, tm), pos2.reshape(m // tm, 1, tm), x)


def _expert_kernel(meta_ref, xs_ref, wg_ref, wu_ref, wd_ref, o_ref, xb_ref, *, n_tiles):
    f = pl.program_id(1)
    used = pl.program_id(0) < meta_ref[n_tiles]

    @pl.when(jnp.logical_and(jnp.logical_not(used), f == 0))
    def _():
        o_ref[...] = jnp.zeros_like(o_ref)

    @pl.when(jnp.logical_and(used, f == 0))
    def _():
        xb_ref[...] = xs_ref[...].astype(BF16)

    @pl.when(used)
    def _():
        part = _swiglu_chunks(xb_ref[...], wg_ref, wu_ref, wd_ref, (0,), wd_ref.shape[1], FF_CHUNK)

        @pl.when(f == 0)
        def _():
            o_ref[...] = part

        @pl.when(f > 0)
        def _():
            o_ref[...] += part


def _experts(meta, xs, w_gu, w_down, tile, n_tiles, tf):
    d_ff = w_down.shape[1]
    nf = d_ff // tf
    fcol = lambda j, f, meta: jnp.where(j < meta[n_tiles], f, nf - 1)
    return pl.pallas_call(
        functools.partial(_expert_kernel, n_tiles=n_tiles),
        grid_spec=pltpu.PrefetchScalarGridSpec(
            num_scalar_prefetch=1, grid=(n_tiles, nf),
            in_specs=[pl.BlockSpec((tile, D_MODEL), lambda j, f, meta: (j, 0)),
                      pl.BlockSpec((1, D_MODEL, tf), lambda j, f, meta: (meta[j], 0, fcol(j, f, meta))),
                      pl.BlockSpec((1, D_MODEL, tf), lambda j, f, meta: (meta[j], 0, nf + fcol(j, f, meta))),
                      pl.BlockSpec((1, tf, D_MODEL), lambda j, f, meta: (meta[j], fcol(j, f, meta), 0))],
            out_specs=pl.BlockSpec((tile, D_MODEL), lambda j, f, meta: (j, 0)),
            scratch_shapes=[pltpu.VMEM((tile, D_MODEL), BF16)]),
        out_shape=jax.ShapeDtypeStruct((n_tiles * tile, D_MODEL), F32),
        compiler_params=_params(2), name="moe_experts",
    )(meta, xs, w_gu, w_gu, w_down)


def _combine_kernel(p1_ref, p2_ref, n1_ref, n2_ref, rows_hbm, x_ref, r_ref, g_ref, b_ref, op_ref, os_ref,
                    buf1, buf2, sem1, sem2, *, tm, prompt_tiles):
    i = pl.program_id(0)
    slot = i % 2

    def start(pa_ref, pb_ref, s):
        _start_row_gather(pa_ref, rows_hbm, buf1.at[s], sem1.at[s], tm)
        _start_row_gather(pb_ref, rows_hbm, buf2.at[s], sem2.at[s], tm)

    @pl.when(i == 0)
    def _():
        start(p1_ref, p2_ref, 0)

    @pl.when(i + 1 < pl.num_programs(0))
    def _():
        start(n1_ref, n2_ref, 1 - slot)

    _wait_rows(rows_hbm, buf1.at[slot], sem1.at[slot])
    _wait_rows(rows_hbm, buf2.at[slot], sem2.at[slot])
    route = r_ref[...]
    y = route[:, 0:1] * buf1[slot] + route[:, 1:2] * buf2[slot]
    res = _layer_norm(ALPHA * x_ref[...] + y, g_ref[...], b_ref[...])

    @pl.when(i < prompt_tiles)
    def _():
        op_ref[...] = res

    @pl.when(i >= prompt_tiles)
    def _():
        os_ref[...] = res


def _combine(pos1, pos2, rows, x, route, g, b, tm, n_prompt):
    m = x.shape[0]
    pt = n_prompt // tm
    nt = m // tm
    idx = lambda: pl.BlockSpec((1, 1, tm), lambda i: (i, 0, 0), memory_space=pltpu.SMEM)
    nxt = lambda: pl.BlockSpec((1, 1, tm), lambda i: (jnp.minimum(i + 1, nt - 1), 0, 0), memory_space=pltpu.SMEM)
    vec = pl.BlockSpec((1, D_MODEL), lambda i: (0, 0))
    p1 = pos1.reshape(nt, 1, tm)
    p2 = pos2.reshape(nt, 1, tm)
    return pl.pallas_call(
        functools.partial(_combine_kernel, tm=tm, prompt_tiles=pt), grid=(nt,),
        in_specs=[idx(), idx(), nxt(), nxt(), pl.BlockSpec(memory_space=pl.ANY),
                  pl.BlockSpec((tm, D_MODEL), lambda i: (i, 0)),
                  pl.BlockSpec((tm, LANES), lambda i: (i, 0)), vec, vec],
        out_specs=[pl.BlockSpec((tm, D_MODEL), lambda i: (jnp.minimum(i, pt - 1), 0)),
                   pl.BlockSpec((tm, D_MODEL), lambda i: (jnp.maximum(i - pt, 0), 0))],
        out_shape=[jax.ShapeDtypeStruct((n_prompt, D_MODEL), F32),
                   jax.ShapeDtypeStruct((m - n_prompt, D_MODEL), F32)],
        scratch_shapes=[pltpu.VMEM((2, tm, D_MODEL), F32), pltpu.VMEM((2, tm, D_MODEL), F32),
                        pltpu.SemaphoreType.DMA((2,)), pltpu.SemaphoreType.DMA((2,))],
        compiler_params=_params(1), name="moe_combine",
    )(p1, p2, p1, p2, rows, x, route, g, b)


def _moe(x, route, n_prompt, w_gu, w_down, g, b, tm, tile, tf):
    e1 = route[:, 2].astype(jnp.int32)
    e2 = route[:, 3].astype(jnp.int32)
    meta, pos1, pos2, n_tiles = _route_plan(e1, e2, tile)
    xs = _dispatch(meta, pos1, pos2, x, tm, tile, n_tiles)
    rows = _experts(meta, xs, w_gu, w_down, tile, n_tiles, tf)
    return _combine(pos1, pos2, rows, x, route, g, b, tm, n_prompt)


def kernel(x_prompt, x_sample, cache_kv_w128, cache_kv_w512, cache_kv_w2048, state_ret,
           ln_g, ln_b, rel_bias, w_in_dil, w_out_dil, w_in_ret, ret_gn_g, w_out_ret,
           w_gu_dense, w_down_dense, w_router, w_gu_moe, w_down_moe):
    bsz, seq, _ = x_prompt.shape
    dbsz, t_new, _ = x_sample.shape
    tp = SAMPLE_PAD
    n_p = bsz * seq
    n_s = dbsz * tp
    caches = (cache_kv_w128, cache_kv_w512, cache_kv_w2048)

    w_in_dil_b = w_in_dil.astype(BF16)
    w_out_dil_b = w_out_dil.astype(BF16)
    w_in_ret_b = w_in_ret.astype(BF16)
    w_out_ret_b = w_out_ret.astype(BF16)
    w_gu_dense_b = w_gu_dense.astype(BF16)
    w_down_dense_b = w_down_dense.astype(BF16)
    w_gu_moe_b = w_gu_moe.astype(BF16)
    w_down_moe_b = w_down_moe.astype(BF16)
    lng = ln_g.reshape(DEPTH, 2, 1, D_MODEL)
    lnb = ln_b.reshape(DEPTH, 2, 1, D_MODEL)
    gn = ret_gn_g.reshape(1, RET_V)

    hp = x_prompt.reshape(n_p, D_MODEL)
    hs = jnp.pad(x_sample, ((0, 0), (0, tp - t_new), (0, 0))).reshape(n_s, D_MODEL)

    tm = min(1024, seq)
    qkv_groups, (kv128_p, kv512_p, kv2048_p) = _proj_dil_prompt(hp, w_in_dil_b, bsz, seq, tm)
    outs, lses = [], []
    for g, (window, dil) in enumerate(DIL_GROUPS):
        qkv_g = qkv_groups[g].reshape(bsz * dil, seq // dil, G_COLS)
        tbl = _prompt_table(rel_bias, g, window, dil)
        o, l = _attn_prompt(qkv_g, tbl, min(512, seq // dil))
        outs.append(o.reshape(bsz, dil, seq // dil, A_WIDTH))
        lses.append(l.reshape(bsz, dil, seq // dil, A_WIDTH))
    hp = _merge_out(outs, lses, w_out_dil_b, hp, lng[0, 0], lnb[0, 0], 512)

    qkv_s = _matmul(hs, w_in_dil_b, G_COLS)
    qkv_s3 = qkv_s.reshape(dbsz, tp, N_GROUPS * G_COLS)
    tabs = [_sample_tables(rel_bias, g, window, dil, caches[g].shape[1], t_new)
            for g, (window, dil) in enumerate(DIL_GROUPS)]
    qkv_s5 = qkv_s.reshape(dbsz, tp, 3 * N_GROUPS, H_SLOT, HEAD_DIM)
    mixed_s = _attn_sample(qkv_s5, caches, [t[0] for t in tabs], [t[1] for t in tabs], t_new)
    hs = _mm_ln(mixed_s.reshape(n_s, A_WIDTH), w_out_dil_b, hs, lng[0, 0], lnb[0, 0], n_s)
    rows_s = []
    for g in range(N_GROUPS):
        kv = qkv_s3[:, :t_new, g * G_COLS + A_WIDTH:(g + 1) * G_COLS]
        rows_s.append(kv.reshape(dbsz, t_new, 2, H_SLOT, HEAD_DIM))

    hp = _ffn_dense(hp, w_gu_dense_b, w_down_dense_b, lng[0, 1], lnb[0, 1], 512)
    hs = _ffn_dense(hs, w_gu_dense_b, w_down_dense_b, lng[0, 1], lnb[0, 1], n_s)

    cos_p, sin_p = _rope_tables(jnp.arange(seq, dtype=jnp.int32))
    pos_s = jnp.tile(PAST_LEN + jnp.arange(tp, dtype=jnp.int32), dbsz)
    cos_s, sin_s = _rope_tables(pos_s)
    proj_p = _proj_ret(hp, w_in_ret_b, cos_p, sin_p, tm, BF16)
    y_p, ret_p = _ret_prompt(proj_p, gn, bsz, seq, min(1024, seq))
    proj_s = _proj_ret(hs, w_in_ret_b, cos_s, sin_s, n_s, F32)
    y_s, ret_s = _ret_sample(proj_s.reshape(dbsz, tp, -1), state_ret, gn, t_new)
    h_all, route = _mm_ln_pair(y_p, y_s.reshape(n_s, RET_V), w_out_ret_b, hp, hs, lng[1, 0], lnb[1, 0],
                               w_router, 512)

    out_p, out_s = _moe(h_all, route, n_p, w_gu_moe_b, w_down_moe_b, lng[1, 1], lnb[1, 1], 512, 1024,
                        w_down_moe.shape[1] // 2)

    y_prompt = out_p.reshape(bsz, seq, D_MODEL)
    y_sample = out_s.reshape(dbsz, tp, D_MODEL)[:, :t_new]
    shape5 = lambda a: a.reshape(a.shape[0], a.shape[1], 2, H_SLOT, HEAD_DIM)
    return (y_prompt, y_sample, shape5(kv128_p), shape5(kv512_p), shape5(kv2048_p), ret_p,
            rows_s[0], rows_s[1], rows_s[2], ret_s)
```

```python
import functools

import jax
import jax.numpy as jnp
import numpy as np
from jax import lax
from jax.experimental import pallas as pl
from jax.experimental.pallas import tpu as pltpu

F32 = jnp.float32
BF16 = jnp.bfloat16

DEPTH = 2
D_MODEL = 1024
PAST_LEN = 16384
DIL_GROUPS = ((128, 1), (512, 4), (2048, 16))
N_GROUPS = 3
H_SLOT = 8
HEAD_DIM = 64
A_WIDTH = H_SLOT * HEAD_DIM
G_COLS = 3 * A_WIDTH
N_BUCKETS = 32
MAX_DISTANCE = 2048
RET_HEADS = 4
RET_DK = 256
RET_DV = 512
RET_CHUNK = 128
RET_HPS = 4
ROPE_BASE = 10000.0
RET_QK = RET_HEADS * RET_DK
RET_V = RET_HEADS * RET_DV
N_EXPERTS = 8
LN_EPS = 1e-5
GN_EPS = 1e-5
ALPHA = (2 * DEPTH) ** 0.25
NEG = -1e30

LANES = 128
ATT_BLOCK = 128
PERM = 256
FF_CHUNK = 512
SAMPLE_PAD = 16
VMEM_LIMIT = 56 * 1024 * 1024

_NT = (((1,), (1,)), ((), ()))
_TN = (((0,), (0,)), ((), ()))


def _params(n_grid):
    return pltpu.CompilerParams(dimension_semantics=("arbitrary",) * n_grid,
                                vmem_limit_bytes=VMEM_LIMIT)


def _layer_norm(z, g, b):
    mu = jnp.mean(z, axis=-1, keepdims=True)
    zc = z - mu
    var = jnp.mean(zc * zc, axis=-1, keepdims=True)
    return zc * lax.rsqrt(var + LN_EPS) * g + b


def _silu(a):
    return a / (1.0 + jnp.exp(-a))


def _t5_bucket(dist):
    max_exact = N_BUCKETS // 2
    d = np.asarray(dist, dtype=np.int64)
    scaled = np.log(np.maximum(d, 1) / max_exact) / np.log(MAX_DISTANCE / max_exact)
    large = np.minimum(max_exact + (scaled * (N_BUCKETS - max_exact)).astype(np.int32), N_BUCKETS - 1)
    return np.where(d < max_exact, d, large).astype(np.int32)


def _group_bias(rel_bias, g, window, dil):
    n_keys = window // dil + 1
    buckets = jnp.asarray(_t5_bucket(np.arange(n_keys) * dil))
    return rel_bias[buckets][:, g * H_SLOT:(g + 1) * H_SLOT].T.astype(F32)


def _matmul_kernel(x_ref, w_ref, o_ref):
    o_ref[...] = jnp.dot(x_ref[...].astype(BF16), w_ref[...], preferred_element_type=F32)


def _matmul(x, w, tn):
    m, k = x.shape
    n = w.shape[1]
    return pl.pallas_call(
        _matmul_kernel,
        grid=(n // tn,),
        in_specs=[pl.BlockSpec((m, k), lambda j: (0, 0)),
                  pl.BlockSpec((k, tn), lambda j: (0, j))],
        out_specs=pl.BlockSpec((m, tn), lambda j: (0, j)),
        out_shape=jax.ShapeDtypeStruct((m, n), F32),
        compiler_params=_params(1),
        name="matmul_sample",
    )(x, w)


def _deinterleave_matrix(dil):
    p = np.zeros((PERM, PERM), np.float32)
    rows = PERM // dil
    for r in range(dil):
        for m in range(rows):
            p[r * rows + m, m * dil + r] = 1.0
    return p


def _proj_dil_kernel(x_ref, w_ref, p1_ref, p2_ref, q0_ref, q1_ref, q2_ref, kv0_ref, kv1_ref, kv2_ref, xperm, xb_ref,
                     *, tm, tpb, keeps):
    g = pl.program_id(1)
    j = pl.program_id(0) % tpb

    @pl.when(g == 0)
    def _():
        xb_ref[...] = x_ref[...].astype(BF16)
    q_refs = (q0_ref, q1_ref, q2_ref)
    kv_refs = (kv0_ref, kv1_ref, kv2_ref)
    perms = (None, p1_ref, p2_ref)

    def store_q(ref, r, row0, n_rows, acc, src0):
        ref[0, r, row0:row0 + n_rows, :A_WIDTH] = (acc[src0:src0 + n_rows, :A_WIDTH] * (HEAD_DIM ** -0.5)).astype(BF16)
        ref[0, r, row0:row0 + n_rows, A_WIDTH:] = acc[src0:src0 + n_rows, A_WIDTH:].astype(BF16)

    for gi, (_, dil) in enumerate(DIL_GROUPS):
        keep = keeps[gi]
        in_tail = (j >= tpb - keep // tm) if keep >= tm else (j == tpb - 1)
        row0 = 0 if keep >= tm else tm - keep

        @pl.when(g == gi)
        def _(gi=gi, dil=dil, in_tail=in_tail, row0=row0):
            xb = xb_ref[...]
            if dil == 1:
                acc = jnp.dot(xb, w_ref[...], preferred_element_type=F32)
                store_q(q_refs[gi], 0, 0, tm, acc, 0)
            else:
                for sub in range(tm // PERM):
                    blk = xb[sub * PERM:(sub + 1) * PERM]
                    xperm[sub * PERM:(sub + 1) * PERM, :] = jnp.dot(
                        perms[gi][...], blk, preferred_element_type=F32).astype(BF16)
                acc = jnp.dot(xperm[...], w_ref[...], preferred_element_type=F32)
                rows = PERM // dil
                for sub in range(tm // PERM):
                    for r in range(dil):
                        store_q(q_refs[gi], r, sub * rows, rows, acc, sub * PERM + r * rows)

            @pl.when(in_tail)
            def _():
                if dil == 1:
                    kv_refs[gi][0] = acc[row0:, A_WIDTH:]
                else:
                    kv_refs[gi][0] = jnp.dot(xb[row0:], w_ref[:, A_WIDTH:], preferred_element_type=F32)


def _proj_dil_prompt(x, w, bsz, seq, tm):
    tpb = seq // tm
    keeps = tuple(min(wd, seq) for wd, _ in DIL_GROUPS)

    def kv_spec(keep):
        if keep >= tm:
            first = tpb - keep // tm
            return pl.BlockSpec((1, tm, 2 * A_WIDTH),
                                lambda i, g: (i // tpb, jnp.maximum(i % tpb - first, 0), 0))
        return pl.BlockSpec((1, keep, 2 * A_WIDTH), lambda i, g: (i // tpb, 0, 0))

    perm = lambda d: jnp.asarray(_deinterleave_matrix(d), BF16)
    const = pl.BlockSpec((PERM, PERM), lambda i, g: (0, 0))
    outs = pl.pallas_call(
        functools.partial(_proj_dil_kernel, tm=tm, tpb=tpb, keeps=keeps),
        grid=(bsz * tpb, N_GROUPS),
        in_specs=[pl.BlockSpec((tm, D_MODEL), lambda i, g: (i, 0)),
                  pl.BlockSpec((D_MODEL, G_COLS), lambda i, g: (0, g)), const, const],
        out_specs=[pl.BlockSpec((1, d, tm // d, G_COLS), lambda i, g: (i // tpb, 0, i % tpb, 0))
                   for _, d in DIL_GROUPS] + [kv_spec(k) for k in keeps],
        out_shape=[jax.ShapeDtypeStruct((bsz, d, seq // d, G_COLS), BF16) for _, d in DIL_GROUPS]
        + [jax.ShapeDtypeStruct((bsz, k, 2 * A_WIDTH), F32) for k in keeps],
        scratch_shapes=[pltpu.VMEM((tm, D_MODEL), BF16), pltpu.VMEM((tm, D_MODEL), BF16)],
        compiler_params=_params(2),
        name="proj_dil_prompt",
    )(x, w, perm(DIL_GROUPS[1][1]), perm(DIL_GROUPS[2][1]))
    return outs[:3], outs[3:]


def _attn_prompt_kernel(q_ref, k_ref, v_ref, kp_ref, vp_ref, tbl_ref, o_ref, lse_ref, kbuf, vbuf, *, tq):
    n = pl.program_id(1)
    kbuf[0:ATT_BLOCK, :] = kp_ref[0]
    kbuf[ATT_BLOCK:, :] = k_ref[0]
    vbuf[0:ATT_BLOCK, :] = vp_ref[0]
    vbuf[ATT_BLOCK:, :] = v_ref[0]
    low = lax.broadcasted_iota(jnp.int32, (ATT_BLOCK, LANES), 1) < HEAD_DIM

    def body(m, carry):
        r0 = pl.multiple_of(m * ATT_BLOCK, ATT_BLOCK)
        first = jnp.where(jnp.logical_and(n == 0, m == 0), 1, 0)
        for p in range(A_WIDTH // LANES):
            cols = slice(p * LANES, (p + 1) * LANES)
            qm = q_ref[0, pl.ds(r0, ATT_BLOCK), cols]
            keys = kbuf[pl.ds(r0, 2 * ATT_BLOCK), cols]
            vals = vbuf[pl.ds(r0, 2 * ATT_BLOCK), cols]
            outs, lses = [], []
            for a in range(2):
                qa = jnp.where(low if a == 0 else jnp.logical_not(low), qm, jnp.zeros_like(qm))
                s = lax.dot_general(qa, keys, _NT, preferred_element_type=F32) + tbl_ref[first, 2 * p + a]
                mx = jnp.max(s, axis=-1, keepdims=True)
                e = jnp.exp(s - mx)
                l = jnp.sum(e, axis=-1, keepdims=True)
                o = jnp.dot(e.astype(BF16), vals, preferred_element_type=F32)
                outs.append(o / l)
                lses.append(jnp.broadcast_to(mx + jnp.log(l), (ATT_BLOCK, LANES)))
            o_ref[0, pl.ds(r0, ATT_BLOCK), cols] = jnp.where(low, outs[0], outs[1]).astype(BF16)
            lse_ref[0, pl.ds(r0, ATT_BLOCK), cols] = jnp.where(low, lses[0], lses[1])
        return carry

    lax.fori_loop(0, tq // ATT_BLOCK, body, 0)


def _attn_prompt(qkv, tbl, tq):
    nb, length, _ = qkv.shape
    sub = tq // ATT_BLOCK
    cur = lambda c: pl.BlockSpec((1, tq, A_WIDTH), lambda s, n: (s, n, c))
    prev = lambda c: pl.BlockSpec((1, ATT_BLOCK, A_WIDTH), lambda s, n: (s, jnp.maximum(n * sub - 1, 0), c))
    return pl.pallas_call(
        functools.partial(_attn_prompt_kernel, tq=tq),
        grid=(nb, length // tq),
        in_specs=[cur(0), cur(1), cur(2), prev(1), prev(2),
                  pl.BlockSpec(tbl.shape, lambda s, n: (0, 0, 0, 0))],
        out_specs=[pl.BlockSpec((1, tq, A_WIDTH), lambda s, n: (s, n, 0))] * 2,
        out_shape=[jax.ShapeDtypeStruct((nb, length, A_WIDTH), BF16),
                   jax.ShapeDtypeStruct((nb, length, A_WIDTH), F32)],
        scratch_shapes=[pltpu.VMEM((tq + ATT_BLOCK, A_WIDTH), BF16)] * 2,
        compiler_params=_params(2),
        name="attn_prompt",
    )(qkv, qkv, qkv, qkv, qkv, tbl)


def _toeplitz(u, n_rows, n_cols):
    h, n = u.shape
    assert n == n_rows + n_cols - 1
    up = jnp.pad(u, ((0, 0), (0, 1)))
    w = jnp.tile(up, (1, n_rows))[:, :n_rows * n].reshape(h, n_rows, n)
    return w[:, :, n_rows - 1:n_rows - 1 + n_cols]


def _bias_by_offset(bias, dist, valid):
    return jnp.where(valid[None], bias[:, np.clip(dist, 0, bias.shape[1] - 1)], NEG)


def _prompt_table(rel_bias, g, window, dil):
    bias = _group_bias(rel_bias, g, window, dil)
    k = np.arange(3 * ATT_BLOCK - 1)
    dist = 2 * ATT_BLOCK - 1 - k
    u = _bias_by_offset(bias, dist, (dist >= 0) & (dist <= window // dil))
    tbl = _toeplitz(u, ATT_BLOCK, 2 * ATT_BLOCK)
    c = np.arange(2 * ATT_BLOCK)[None, None, :]
    tbl_first = jnp.where(c < ATT_BLOCK, NEG, tbl)
    return jnp.stack([tbl, tbl_first], 0)


def _split3(x):
    hi = x.astype(BF16)
    r1 = x - hi.astype(F32)
    mid = r1.astype(BF16)
    lo = (r1 - mid.astype(F32)).astype(BF16)
    return hi, mid, lo


def _merge_out_kernel(o0, o1, o2, l0, l1, l2, pt1_ref, pt2_ref, w_ref, x_ref, g_ref, b_ref, out_ref, *, tm):
    pts = (None, pt1_ref, pt2_ref)

    def natural(ref, gi, exact_f32):
        dil = DIL_GROUPS[gi][1]
        if dil == 1:
            return ref[0, 0].astype(F32)
        rows = PERM // dil
        blocks = []
        for sub in range(tm // PERM):
            piece = jnp.concatenate([ref[0, r, sub * rows:(sub + 1) * rows, :] for r in range(dil)], axis=0)
            parts = _split3(piece) if exact_f32 else (piece,)
            blocks.append(sum(jnp.dot(pts[gi][...], p, preferred_element_type=F32) for p in parts))
        return jnp.concatenate(blocks, axis=0)

    os_ = [natural(r, gi, False) for gi, r in enumerate((o0, o1, o2))]
    ls = [natural(r, gi, True) for gi, r in enumerate((l0, l1, l2))]
    mx = jnp.maximum(jnp.maximum(ls[0], ls[1]), ls[2])
    es = [jnp.exp(l - mx) for l in ls]
    mixed = (es[0] * os_[0] + es[1] * os_[1] + es[2] * os_[2]) / (es[0] + es[1] + es[2])
    acc = jnp.dot(mixed.astype(BF16), w_ref[...], preferred_element_type=F32)
    out_ref[...] = _layer_norm(ALPHA * x_ref[...] + acc, g_ref[...], b_ref[...])


def _merge_out(os_, ls_, w, x, g, b, tm):
    bsz, _, seq, _ = os_[0].shape
    tpb = seq // tm
    grp = lambda d: pl.BlockSpec((1, d, tm // d, A_WIDTH), lambda i: (i // tpb, 0, i % tpb, 0))
    specs = [grp(d) for _, d in DIL_GROUPS]
    pt = lambda d: jnp.asarray(_deinterleave_matrix(d).T, BF16)
    const = pl.BlockSpec((PERM, PERM), lambda i: (0, 0))
    vec = pl.BlockSpec((1, D_MODEL), lambda i: (0, 0))
    return pl.pallas_call(
        functools.partial(_merge_out_kernel, tm=tm), grid=(bsz * tpb,),
        in_specs=specs + specs + [const, const, pl.BlockSpec((A_WIDTH, D_MODEL), lambda i: (0, 0)),
                                  pl.BlockSpec((tm, D_MODEL), lambda i: (i, 0)), vec, vec],
        out_specs=pl.BlockSpec((tm, D_MODEL), lambda i: (i, 0)),
        out_shape=jax.ShapeDtypeStruct((bsz * seq, D_MODEL), F32),
        compiler_params=_params(1), name="merge_outproj_ln",
    )(*os_, *ls_, pt(DIL_GROUPS[1][1]), pt(DIL_GROUPS[2][1]), w, x, g, b)


def _attn_sample_kernel(q_ref, c0_ref, c1_ref, c2_ref, tb0, tb1, tb2, tn0, tn1, tn2, out_ref, *scratch, t_new):
    tp = SAMPLE_PAD
    lane_t = lax.broadcasted_iota(jnp.int32, (HEAD_DIM, tp), 1)
    sets = []
    for g, (c_ref, tb, tn) in enumerate(((c0_ref, tb0, tn0), (c1_ref, tb1, tn1), (c2_ref, tb2, tn2))):
        sets.append((lambda h, c_ref=c_ref: c_ref[0, 0, h], lambda h, c_ref=c_ref: c_ref[0, 1, h], tb, g))
        sets.append((lambda h, g=g: q_ref[0, g, 1, h], lambda h, g=g: q_ref[0, g, 2, h], tn, g))
    out_ref[0] = jnp.zeros(out_ref.shape[1:], F32)

    for t in range(t_new):
        for h in range(H_SLOT):
            for (keys, _, _, g), s_ref in zip(sets, scratch):
                qcol = q_ref[0, g, 0, h][:, t:t + 1] * (HEAD_DIM ** -0.5)
                s_ref[h:h + 1, :] = jnp.sum(keys(h) * qcol, axis=0, keepdims=True)
        scores = [s_ref[...] + tab[t] for (_, _, tab, _), s_ref in zip(sets, scratch)]
        mx = functools.reduce(jnp.maximum, [jnp.max(s, axis=-1, keepdims=True) for s in scores])
        den = jnp.zeros((H_SLOT, 1), F32)
        for s, s_ref in zip(scores, scratch):
            e = jnp.exp(s - mx)
            den = den + jnp.sum(e, axis=-1, keepdims=True)
            s_ref[...] = e
        inv = 1.0 / den

        for h in range(H_SLOT):
            col = jnp.zeros((HEAD_DIM, 1), F32)
            for (_, values, _, _), s_ref in zip(sets, scratch):
                col = col + jnp.sum(values(h) * s_ref[h:h + 1, :], axis=-1, keepdims=True)
            out_ref[0, h] = jnp.where(lane_t == t, col * inv[h:h + 1, :], out_ref[0, h])


def _sample_tables(rel_bias, g, window, dil, length, t_new):
    bias = _group_bias(rel_bias, g, window, dil)
    n_keys = window // dil + 1
    tp = SAMPLE_PAD
    k = np.arange(tp + length - 1)
    dist = length + tp - 1 - k
    u = _bias_by_offset(bias, dist // dil, (dist % dil == 0) & (dist // dil < n_keys))
    tc = _toeplitz(u, tp, length)
    k2 = np.arange(2 * tp - 1)
    dist2 = tp - 1 - k2
    u2 = _bias_by_offset(bias, dist2 // dil, (dist2 >= 0) & (dist2 % dil == 0) & (dist2 // dil < n_keys))
    live_col = (np.arange(tp) < t_new)[None, None, :]
    tn = jnp.where(live_col, _toeplitz(u2, tp, tp), NEG)
    return tc.transpose(1, 0, 2)[:t_new], tn.transpose(1, 0, 2)[:t_new]


def _attn_sample(q_t, caches_t, tables_c, tables_n, t_new):
    bsz = q_t.shape[0]
    tp = SAMPLE_PAD
    const = lambda a: pl.BlockSpec(a.shape, lambda b: (0, 0, 0))
    scratch = []
    for c in caches_t:
        scratch += [pltpu.VMEM((H_SLOT, c.shape[-1]), F32), pltpu.VMEM((H_SLOT, tp), F32)]
    return pl.pallas_call(
        functools.partial(_attn_sample_kernel, t_new=t_new),
        grid=(bsz,),
        in_specs=[pl.BlockSpec((1,) + q_t.shape[1:], lambda b: (b, 0, 0, 0, 0, 0))]
        + [pl.BlockSpec((1,) + c.shape[1:], lambda b: (b, 0, 0, 0, 0)) for c in caches_t]
        + [const(t) for t in tables_c] + [const(t) for t in tables_n],
        out_specs=pl.BlockSpec((1, H_SLOT, HEAD_DIM, tp), lambda b: (b, 0, 0, 0)),
        out_shape=jax.ShapeDtypeStruct((bsz, H_SLOT, HEAD_DIM, tp), F32),
        scratch_shapes=scratch,
        compiler_params=_params(1),
        name="attn_sample",
    )(q_t, *caches_t, *tables_c, *tables_n)


def _mm_ln_kernel(y_ref, w_ref, x_ref, g_ref, b_ref, o_ref):
    acc = jnp.dot(y_ref[...].astype(BF16), w_ref[...], preferred_element_type=F32)
    o_ref[...] = _layer_norm(ALPHA * x_ref[...] + acc, g_ref[...], b_ref[...])


def _mm_ln(y, w, x, g, b, tm):
    m, k = y.shape
    vec = pl.BlockSpec((1, D_MODEL), lambda i: (0, 0))
    return pl.pallas_call(
        _mm_ln_kernel, grid=(m // tm,),
        in_specs=[pl.BlockSpec((tm, k), lambda i: (i, 0)), pl.BlockSpec((k, D_MODEL), lambda i: (0, 0)),
                  pl.BlockSpec((tm, D_MODEL), lambda i: (i, 0)), vec, vec],
        out_specs=pl.BlockSpec((tm, D_MODEL), lambda i: (i, 0)),
        out_shape=jax.ShapeDtypeStruct((m, D_MODEL), F32),
        compiler_params=_params(1), name="outproj_ln",
    )(y, w, x, g, b)


def _mm_ln_pair_kernel(yp_ref, ys_ref, w_ref, xp_ref, xs_ref, g_ref, b_ref, wr_ref, o_ref, route_ref, *, prompt_tiles):
    def run(y_ref, x_ref):
        acc = jnp.dot(y_ref[...].astype(BF16), w_ref[...], preferred_element_type=F32)
        h = _layer_norm(ALPHA * x_ref[...] + acc, g_ref[...], b_ref[...])
        o_ref[...] = h
        route_ref[...] = _top2_route(jnp.dot(h.astype(BF16), wr_ref[...], preferred_element_type=F32))

    pl.when(pl.program_id(0) < prompt_tiles)(lambda: run(yp_ref, xp_ref))
    pl.when(pl.program_id(0) >= prompt_tiles)(lambda: run(ys_ref, xs_ref))


def _mm_ln_pair(y_p, y_s, w, x_p, x_s, g, b, w_router, tm):
    n_p, k = y_p.shape
    n_s = y_s.shape[0]
    pt = n_p // tm
    first = lambda width: pl.BlockSpec((tm, width), lambda i: (jnp.minimum(i, pt - 1), 0))
    second = lambda width: pl.BlockSpec((tm, width), lambda i: (jnp.maximum(i - pt, 0), 0))
    vec = pl.BlockSpec((1, D_MODEL), lambda i: (0, 0))
    wr = jnp.pad(w_router, ((0, 0), (0, LANES - N_EXPERTS))).astype(BF16)
    return pl.pallas_call(
        functools.partial(_mm_ln_pair_kernel, prompt_tiles=pt), grid=((n_p + n_s) // tm,),
        in_specs=[first(k), second(k), pl.BlockSpec((k, D_MODEL), lambda i: (0, 0)),
                  first(D_MODEL), second(D_MODEL), vec, vec, pl.BlockSpec((D_MODEL, LANES), lambda i: (0, 0))],
        out_specs=[pl.BlockSpec((tm, D_MODEL), lambda i: (i, 0)), pl.BlockSpec((tm, LANES), lambda i: (i, 0))],
        out_shape=[jax.ShapeDtypeStruct((n_p + n_s, D_MODEL), F32),
                   jax.ShapeDtypeStruct((n_p + n_s, LANES), F32)],
        compiler_params=_params(1), name="outproj_ln_route",
    )(y_p, y_s, w, x_p, x_s, g, b, wr)


def _swiglu_chunks(xb, wg_ref, wu_ref, wd_ref, lead, width, chunk):
    acc = None
    for c0 in range(0, width, chunk):
        cs = slice(c0, min(c0 + chunk, width))
        a = jnp.dot(xb, wg_ref[lead + (slice(None), cs)], preferred_element_type=F32)
        u = jnp.dot(xb, wu_ref[lead + (slice(None), cs)], preferred_element_type=F32)
        part = jnp.dot((_silu(a) * u).astype(BF16), wd_ref[lead + (cs, slice(None))], preferred_element_type=F32)
        acc = part if acc is None else acc + part
    return acc


def _ffn_kernel(x_ref, wg_ref, wu_ref, wd_ref, g_ref, b_ref, o_ref, *, d_ff):
    x = x_ref[...]
    acc = _swiglu_chunks(x.astype(BF16), wg_ref, wu_ref, wd_ref, (), d_ff, FF_CHUNK)
    o_ref[...] = _layer_norm(ALPHA * x + acc, g_ref[...], b_ref[...])


def _ffn_dense(x, w_gu, w_down, g, b, tm):
    m = x.shape[0]
    d_ff = w_down.shape[0]
    vec = pl.BlockSpec((1, D_MODEL), lambda i: (0, 0))
    return pl.pallas_call(
        functools.partial(_ffn_kernel, d_ff=d_ff), grid=(m // tm,),
        in_specs=[pl.BlockSpec((tm, D_MODEL), lambda i: (i, 0)),
                  pl.BlockSpec((D_MODEL, d_ff), lambda i: (0, 0)),
                  pl.BlockSpec((D_MODEL, d_ff), lambda i: (0, 1)),
                  pl.BlockSpec((d_ff, D_MODEL), lambda i: (0, 0)), vec, vec],
        out_specs=pl.BlockSpec((tm, D_MODEL), lambda i: (i, 0)),
        out_shape=jax.ShapeDtypeStruct((m, D_MODEL), F32),
        compiler_params=_params(1), name="ffn_dense",
    )(x, w_gu, w_gu, w_down, g, b)


def _proj_ret_kernel(x_ref, w_ref, cos_ref, sin_ref, o_ref, xb_ref, *, scale_k):
    n = pl.program_id(1)

    @pl.when(n == 0)
    def _():
        xb_ref[...] = x_ref[...].astype(BF16)

    xb = xb_ref[...]
    half = RET_DK // 2

    @pl.when(n < 2)
    def _():
        cos = cos_ref[...]
        sin = sin_ref[...]
        scale = jnp.where(n == 1, scale_k, 1.0)
        for h in range(RET_HEADS):
            acc = jnp.dot(xb, w_ref[:, h * RET_DK:(h + 1) * RET_DK], preferred_element_type=F32)
            x1 = acc[:, :half]
            x2 = acc[:, half:]
            o_ref[:, h * RET_DK:h * RET_DK + half] = ((x1 * cos - x2 * sin) * scale).astype(o_ref.dtype)
            o_ref[:, h * RET_DK + half:(h + 1) * RET_DK] = ((x1 * sin + x2 * cos) * scale).astype(o_ref.dtype)

    @pl.when(n >= 2)
    def _():
        for h in range(RET_HEADS):
            cs = slice(h * RET_DK, (h + 1) * RET_DK)
            o_ref[:, cs] = jnp.dot(xb, w_ref[:, cs], preferred_element_type=F32).astype(o_ref.dtype)


def _proj_ret(x, w, cos, sin, tm, out_dtype):
    m = x.shape[0]
    n_cols = w.shape[1]
    tn = RET_QK
    pos_tiles = cos.shape[0] // tm
    return pl.pallas_call(
        functools.partial(_proj_ret_kernel, scale_k=RET_DK ** -0.5),
        grid=(m // tm, n_cols // tn),
        in_specs=[pl.BlockSpec((tm, D_MODEL), lambda i, n: (i, 0)),
                  pl.BlockSpec((D_MODEL, tn), lambda i, n: (0, n)),
                  pl.BlockSpec((tm, RET_DK // 2), lambda i, n: (i % pos_tiles, 0)),
                  pl.BlockSpec((tm, RET_DK // 2), lambda i, n: (i % pos_tiles, 0))],
        out_specs=pl.BlockSpec((tm, tn), lambda i, n: (i, n)),
        out_shape=jax.ShapeDtypeStruct((m, n_cols), out_dtype),
        scratch_shapes=[pltpu.VMEM((tm, D_MODEL), BF16)],
        compiler_params=_params(2), name="proj_ret",
    )(x, w, cos, sin)


def _rope_tables(pos):
    half = RET_DK // 2
    inv = 1.0 / (ROPE_BASE ** (jnp.arange(half, dtype=F32) / half))
    ang = pos.astype(F32)[:, None] * inv[None]
    return jnp.cos(ang), jnp.sin(ang)


def _log_gamma():
    return jnp.log(1.0 - 2.0 ** (-5.0 - jnp.arange(RET_HEADS, dtype=F32)))


def _decay_tables(c, rows):
    lg = _log_gamma()
    n = jnp.arange(rows, dtype=F32)
    live = n < c
    diff = n[:, None] - n[None, :]
    decay = jnp.where((diff >= 0)[None] & live[None, None, :],
                      jnp.exp(jnp.maximum(diff, 0.0)[None] * lg[:, None, None]), 0.0)
    q_decay = jnp.exp((n[None, :] + 1.0) * lg[:, None])
    k_decay = jnp.where(live[None], jnp.exp((c - 1.0 - n)[None, :] * lg[:, None]), 0.0)
    chunk_decay = jnp.exp(c * lg)
    return decay, q_decay, k_decay, chunk_decay


def _group_norm_gate(o, gate, gn):
    mu = jnp.mean(o, axis=-1, keepdims=True)
    oc = o - mu
    var = jnp.mean(oc * oc, axis=-1, keepdims=True)
    return _silu(gate) * (oc * lax.rsqrt(var + GN_EPS) * gn)


def _ret_prompt_kernel(q_ref, k_ref, v_ref, gate_ref, dec_ref, qd_ref, kd_ref, cd_ref, gn_ref,
                       y_ref, st_ref, s_ref, *, tb):
    cb = pl.program_id(2)

    @pl.when(cb == 0)
    def _():
        s_ref[...] = jnp.zeros_like(s_ref)

    def body(ci, carry):
        r0 = pl.multiple_of(ci * RET_CHUNK, RET_CHUNK)
        rows = pl.ds(r0, RET_CHUNK)
        for j in range(RET_HPS):
            kc = slice(j * RET_DK, (j + 1) * RET_DK)
            vc = slice(j * RET_DV, (j + 1) * RET_DV)
            q = q_ref[0, rows, kc]
            k = k_ref[0, rows, kc]
            v = v_ref[0, rows, vc]
            state = s_ref[j]
            scores = lax.dot_general(q, k, _NT, preferred_element_type=F32) * dec_ref[j]
            inner = jnp.dot(scores.astype(BF16), v, preferred_element_type=F32)
            cross = jnp.dot(q, state.astype(BF16), preferred_element_type=F32) * qd_ref[j]
            kd = (k.astype(F32) * kd_ref[j]).astype(BF16)
            s_ref[j] = cd_ref[j, 0:1, :] * state + lax.dot_general(kd, v, _TN, preferred_element_type=F32)
            y = _group_norm_gate(inner + cross, gate_ref[0, rows, vc].astype(F32), gn_ref[:, vc])
            y_ref[0, rows, vc] = y.astype(y_ref.dtype)
        return carry

    lax.fori_loop(0, tb // RET_CHUNK, body, 0)

    @pl.when(cb == pl.num_programs(2) - 1)
    def _():
        st_ref[0] = s_ref[...]


def _ret_prompt(proj, gn, bsz, seq, tb):
    decay, q_decay, k_decay, chunk_decay = _decay_tables(RET_CHUNK, RET_CHUNK)
    qd = jnp.broadcast_to(q_decay[:, :, None], (RET_HEADS, RET_CHUNK, RET_DV))
    kd = jnp.broadcast_to(k_decay[:, :, None], (RET_HEADS, RET_CHUNK, RET_DK))
    cd = jnp.broadcast_to(chunk_decay[:, None, None], (RET_HEADS, 8, RET_DV))
    proj = proj.reshape(bsz, seq, proj.shape[-1])
    hps = RET_HPS
    k_off = RET_QK // (hps * RET_DK)
    v_off = 2 * RET_QK // (hps * RET_DV)
    g_off = v_off + RET_HEADS // hps
    per_head = lambda shape: pl.BlockSpec((hps,) + shape, lambda b, h, c: (h, 0, 0))
    y, state = pl.pallas_call(
        functools.partial(_ret_prompt_kernel, tb=tb),
        grid=(bsz, RET_HEADS // hps, seq // tb),
        in_specs=[pl.BlockSpec((1, tb, hps * RET_DK), lambda b, h, c: (b, c, h)),
                  pl.BlockSpec((1, tb, hps * RET_DK), lambda b, h, c: (b, c, k_off + h)),
                  pl.BlockSpec((1, tb, hps * RET_DV), lambda b, h, c: (b, c, v_off + h)),
                  pl.BlockSpec((1, tb, hps * RET_DV), lambda b, h, c: (b, c, g_off + h)),
                  per_head((RET_CHUNK, RET_CHUNK)), per_head((RET_CHUNK, RET_DV)),
                  per_head((RET_CHUNK, RET_DK)), per_head((8, RET_DV)),
                  pl.BlockSpec((1, hps * RET_DV), lambda b, h, c: (0, h))],
        out_specs=[pl.BlockSpec((1, tb, hps * RET_DV), lambda b, h, c: (b, c, h)),
                   pl.BlockSpec((1, hps, RET_DK, RET_DV), lambda b, h, c: (b, h, 0, 0))],
        out_shape=[jax.ShapeDtypeStruct((bsz, seq, RET_V), BF16),
                   jax.ShapeDtypeStruct((bsz, RET_HEADS, RET_DK, RET_DV), F32)],
        scratch_shapes=[pltpu.VMEM((hps, RET_DK, RET_DV), F32)],
        compiler_params=_params(3), name="retention_prompt",
    )(proj, proj, proj, proj, decay, qd, kd, cd, gn)
    return y.reshape(bsz * seq, RET_V), state


def _ret_sample_kernel(p_ref, st_ref, dec_ref, qd_ref, kd_ref, cd_ref, gn_ref, y_ref, ns_ref):
    tp = SAMPLE_PAD
    proj = p_ref[0]
    for h in range(RET_HEADS):
        q = proj[:, h * RET_DK:(h + 1) * RET_DK].astype(BF16)
        k = proj[:, RET_QK + h * RET_DK:RET_QK + (h + 1) * RET_DK]
        v = proj[:, 2 * RET_QK + h * RET_DV:2 * RET_QK + (h + 1) * RET_DV]
        gate = proj[:, 2 * RET_QK + RET_V + h * RET_DV:2 * RET_QK + RET_V + (h + 1) * RET_DV]
        zk = jnp.zeros((LANES - tp, RET_DK), F32)
        zv = jnp.zeros((LANES - tp, RET_DV), F32)
        kp = jnp.concatenate([k, zk], axis=0).astype(BF16)
        kdp = jnp.concatenate([k * kd_ref[h], zk], axis=0).astype(BF16)
        vp = jnp.concatenate([v, zv], axis=0).astype(BF16)
        state = st_ref[0, h]
        scores = lax.dot_general(q, kp, _NT, preferred_element_type=F32) * dec_ref[h]
        inner = jnp.dot(scores.astype(BF16), vp, preferred_element_type=F32)
        cross = jnp.dot(q, state.astype(BF16), preferred_element_type=F32) * qd_ref[h]
        ns_ref[0, h] = cd_ref[h, 0:1, :] * state + lax.dot_general(kdp, vp, _TN, preferred_element_type=F32)
        y = _group_norm_gate(inner + cross, gate, gn_ref[:, h * RET_DV:(h + 1) * RET_DV])
        y_ref[0, :, h * RET_DV:(h + 1) * RET_DV] = y


def _ret_sample(proj, state, gn, t_new):
    bsz = proj.shape[0]
    tp = SAMPLE_PAD
    decay, q_decay, k_decay, chunk_decay = _decay_tables(t_new, tp)
    dec = jnp.pad(decay, ((0, 0), (0, 0), (0, LANES - tp)))
    qd = jnp.broadcast_to(q_decay[:, :, None], (RET_HEADS, tp, RET_DV))
    kd = jnp.broadcast_to(k_decay[:, :, None], (RET_HEADS, tp, RET_DK))
    cd = jnp.broadcast_to(chunk_decay[:, None, None], (RET_HEADS, 8, RET_DV))
    const = lambda a: pl.BlockSpec(a.shape, lambda b: (0,) * a.ndim)
    return pl.pallas_call(
        _ret_sample_kernel, grid=(bsz,),
        in_specs=[pl.BlockSpec((1,) + proj.shape[1:], lambda b: (b, 0, 0)),
                  pl.BlockSpec((1,) + state.shape[1:], lambda b: (b, 0, 0, 0)),
                  const(dec), const(qd), const(kd), const(cd), const(gn)],
        out_specs=[pl.BlockSpec((1, tp, RET_V), lambda b: (b, 0, 0)),
                   pl.BlockSpec((1,) + state.shape[1:], lambda b: (b, 0, 0, 0))],
        out_shape=[jax.ShapeDtypeStruct((bsz, tp, RET_V), F32),
                   jax.ShapeDtypeStruct(state.shape, F32)],
        compiler_params=_params(1), name="retention_sample",
    )(proj, state, dec, qd, kd, cd, gn)


def _top2_route(logits):
    lane = lax.broadcasted_iota(jnp.int32, logits.shape, 1)
    logits = jnp.where(lane < N_EXPERTS, logits, NEG)
    m1 = jnp.max(logits, axis=-1, keepdims=True)
    i1 = jnp.min(jnp.where(logits == m1, lane, LANES), axis=-1, keepdims=True)
    rest = jnp.where(lane == i1, NEG, logits)
    m2 = jnp.max(rest, axis=-1, keepdims=True)
    i2 = jnp.min(jnp.where(rest == m2, lane, LANES), axis=-1, keepdims=True)
    e2 = jnp.exp(m2 - m1)
    den = 1.0 + e2
    return jnp.where(lane == 0, 1.0 / den,
                     jnp.where(lane == 1, e2 / den,
                               jnp.where(lane == 2, i1.astype(F32),
                                         jnp.where(lane == 3, i2.astype(F32), 0.0))))


def _route_plan(e1, e2, tile):
    n_tok = e1.shape[0]
    e = jnp.concatenate([e1, e2])
    onehot = (e[:, None] == jnp.arange(N_EXPERTS, dtype=jnp.int32)[None]).astype(jnp.int32)
    csum = jnp.cumsum(onehot, axis=0)
    rank = jnp.take_along_axis(csum, e[:, None], axis=1)[:, 0] - 1
    tiles_per_expert = (csum[-1] + tile - 1) // tile
    tile_end = jnp.cumsum(tiles_per_expert)
    pos = ((tile_end - tiles_per_expert) * tile)[e] + rank
    n_tiles = (2 * n_tok) // tile + N_EXPERTS
    n_used = tile_end[-1]
    tile_ids = jnp.arange(n_tiles, dtype=jnp.int32)
    tile_expert = jnp.minimum(jnp.searchsorted(tile_end, tile_ids, side="right"), N_EXPERTS - 1)
    tile_expert = jnp.where(tile_ids < n_used, tile_expert, tile_expert[n_used - 1])
    meta = jnp.concatenate([tile_expert, n_used[None], tile_end]).astype(jnp.int32)
    return meta, pos[:n_tok], pos[n_tok:], n_tiles


ROW_DMA_UNROLL = 8


def _row_copy(src, dst, sem, src_row, dst_row):
    return pltpu.make_async_copy(src.at[pl.ds(src_row, 1), :], dst.at[pl.ds(dst_row, 1), :], sem)


def _start_row_gather(idx_ref, src_hbm, dst, sem, n_rows):
    def issue(r, carry):
        _row_copy(src_hbm, dst, sem, idx_ref[0, 0, r], r).start()
        return carry

    lax.fori_loop(0, n_rows, issue, 0, unroll=ROW_DMA_UNROLL)


def _start_row_scatter(idx_ref, src, dst_hbm, sem, n_rows):
    def issue(r, carry):
        _row_copy(src, dst_hbm, sem, r, idx_ref[0, 0, r]).start()
        return carry

    lax.fori_loop(0, n_rows, issue, 0, unroll=ROW_DMA_UNROLL)


def _wait_rows(hbm, vmem, sem):
    pltpu.make_async_copy(hbm.at[pl.ds(0, vmem.shape[0]), :], vmem, sem).wait()


def _dispatch_kernel(meta_ref, p1_ref, p2_ref, x_ref, xs_hbm, zbuf, sem, zsem, *, tm, tile, n_tiles):
    @pl.when(pl.program_id(0) == 0)
    def _():
        zbuf[...] = jnp.zeros_like(zbuf)
        zero_tile = lambda t: pltpu.make_async_copy(zbuf, xs_hbm.at[pl.ds(t * tile, tile), :], zsem)
        for e in range(N_EXPERTS):
            zero_tile(jnp.maximum(meta_ref[n_tiles + 1 + e] - 1, 0)).start()
        for e in range(N_EXPERTS):
            zero_tile(0).wait()

        def zero_unused(t, carry):
            zero_tile(t).start()
            zero_tile(t).wait()
            return carry

        lax.fori_loop(meta_ref[n_tiles], n_tiles, zero_unused, 0)

    _start_row_scatter(p1_ref, x_ref, xs_hbm, sem, tm)
    _start_row_scatter(p2_ref, x_ref, xs_hbm, sem, tm)
    for _ in range(2):
        pltpu.make_async_copy(x_ref, xs_hbm.at[pl.ds(0, tm), :], sem).wait()


def _dispatch(meta, pos1, pos2, x, tm, tile, n_tiles):
    m = x.shape[0]
    idx = lambda: pl.BlockSpec((1, 1, tm), lambda i, meta: (i, 0, 0), memory_space=pltpu.SMEM)
    return pl.pallas_call(
        functools.partial(_dispatch_kernel, tm=tm, tile=tile, n_tiles=n_tiles),
        grid_spec=pltpu.PrefetchScalarGridSpec(
            num_scalar_prefetch=1, grid=(m // tm,),
            in_specs=[idx(), idx(), pl.BlockSpec((tm, D_MODEL), lambda i, meta: (i, 0))],
            out_specs=pl.BlockSpec(memory_space=pl.ANY),
            scratch_shapes=[pltpu.VMEM((tile, D_MODEL), F32),
                            pltpu.SemaphoreType.DMA(()), pltpu.SemaphoreType.DMA(())]),
        out_shape=jax.ShapeDtypeStruct((n_tiles * tile, D_MODEL), F32),
        compiler_params=_params(1), name="moe_dispatch",
    )(meta, pos1.reshape(m // tm, 1, tm), pos2.reshape(m // tm, 1, tm), x)


def _expert_kernel(meta_ref, xs_ref, wg_ref, wu_ref, wd_ref, o_ref, xb_ref, *, n_tiles):
    f = pl.program_id(1)
    used = pl.program_id(0) < meta_ref[n_tiles]

    @pl.when(jnp.logical_and(jnp.logical_not(used), f == 0))
    def _():
        o_ref[...] = jnp.zeros_like(o_ref)

    @pl.when(jnp.logical_and(used, f == 0))
    def _():
        xb_ref[...] = xs_ref[...].astype(BF16)

    @pl.when(used)
    def _():
        part = _swiglu_chunks(xb_ref[...], wg_ref, wu_ref, wd_ref, (0,), wd_ref.shape[1], FF_CHUNK)

        @pl.when(f == 0)
        def _():
            o_ref[...] = part

        @pl.when(f > 0)
        def _():
            o_ref[...] += part


def _experts(meta, xs, w_gu, w_down, tile, n_tiles, tf):
    d_ff = w_down.shape[1]
    nf = d_ff // tf
    fcol = lambda j, f, meta: jnp.where(j < meta[n_tiles], f, nf - 1)
    return pl.pallas_call(
        functools.partial(_expert_kernel, n_tiles=n_tiles),
        grid_spec=pltpu.PrefetchScalarGridSpec(
            num_scalar_prefetch=1, grid=(n_tiles, nf),
            in_specs=[pl.BlockSpec((tile, D_MODEL), lambda j, f, meta: (j, 0)),
                      pl.BlockSpec((1, D_MODEL, tf), lambda j, f, meta: (meta[j], 0, fcol(j, f, meta))),
                      pl.BlockSpec((1, D_MODEL, tf), lambda j, f, meta: (meta[j], 0, nf + fcol(j, f, meta))),
                      pl.BlockSpec((1, tf, D_MODEL), lambda j, f, meta: (meta[j], fcol(j, f, meta), 0))],
            out_specs=pl.BlockSpec((tile, D_MODEL), lambda j, f, meta: (j, 0)),
            scratch_shapes=[pltpu.VMEM((tile, D_MODEL), BF16)]),
        out_shape=jax.ShapeDtypeStruct((n_tiles * tile, D_MODEL), F32),
        compiler_params=_params(2), name="moe_experts",
    )(meta, xs, w_gu, w_gu, w_down)


def _combine_kernel(p1_ref, p2_ref, n1_ref, n2_ref, rows_hbm, x_ref, r_ref, g_ref, b_ref, op_ref, os_ref,
                    buf1, buf2, sem1, sem2, *, tm, prompt_tiles):
    i = pl.program_id(0)
    slot = i % 2

    def start(pa_ref, pb_ref, s):
        _start_row_gather(pa_ref, rows_hbm, buf1.at[s], sem1.at[s], tm)
        _start_row_gather(pb_ref, rows_hbm, buf2.at[s], sem2.at[s], tm)

    @pl.when(i == 0)
    def _():
        start(p1_ref, p2_ref, 0)

    @pl.when(i + 1 < pl.num_programs(0))
    def _():
        start(n1_ref, n2_ref, 1 - slot)

    _wait_rows(rows_hbm, buf1.at[slot], sem1.at[slot])
    _wait_rows(rows_hbm, buf2.at[slot], sem2.at[slot])
    route = r_ref[...]
    y = route[:, 0:1] * buf1[slot] + route[:, 1:2] * buf2[slot]
    res = _layer_norm(ALPHA * x_ref[...] + y, g_ref[...], b_ref[...])

    @pl.when(i < prompt_tiles)
    def _():
        op_ref[...] = res

    @pl.when(i >= prompt_tiles)
    def _():
        os_ref[...] = res


def _combine(pos1, pos2, rows, x, route, g, b, tm, n_prompt):
    m = x.shape[0]
    pt = n_prompt // tm
    nt = m // tm
    idx = lambda: pl.BlockSpec((1, 1, tm), lambda i: (i, 0, 0), memory_space=pltpu.SMEM)
    nxt = lambda: pl.BlockSpec((1, 1, tm), lambda i: (jnp.minimum(i + 1, nt - 1), 0, 0), memory_space=pltpu.SMEM)
    vec = pl.BlockSpec((1, D_MODEL), lambda i: (0, 0))
    p1 = pos1.reshape(nt, 1, tm)
    p2 = pos2.reshape(nt, 1, tm)
    return pl.pallas_call(
        functools.partial(_combine_kernel, tm=tm, prompt_tiles=pt), grid=(nt,),
        in_specs=[idx(), idx(), nxt(), nxt(), pl.BlockSpec(memory_space=pl.ANY),
                  pl.BlockSpec((tm, D_MODEL), lambda i: (i, 0)),
                  pl.BlockSpec((tm, LANES), lambda i: (i, 0)), vec, vec],
        out_specs=[pl.BlockSpec((tm, D_MODEL), lambda i: (jnp.minimum(i, pt - 1), 0)),
                   pl.BlockSpec((tm, D_MODEL), lambda i: (jnp.maximum(i - pt, 0), 0))],
        out_shape=[jax.ShapeDtypeStruct((n_prompt, D_MODEL), F32),
                   jax.ShapeDtypeStruct((m - n_prompt, D_MODEL), F32)],
        scratch_shapes=[pltpu.VMEM((2, tm, D_MODEL), F32), pltpu.VMEM((2, tm, D_MODEL), F32),
                        pltpu.SemaphoreType.DMA((2,)), pltpu.SemaphoreType.DMA((2,))],
        compiler_params=_params(1), name="moe_combine",
    )(p1, p2, p1, p2, rows, x, route, g, b)


def _moe(x, route, n_prompt, w_gu, w_down, g, b, tm, tile, tf):
    e1 = route[:, 2].astype(jnp.int32)
    e2 = route[:, 3].astype(jnp.int32)
    meta, pos1, pos2, n_tiles = _route_plan(e1, e2, tile)
    xs = _dispatch(meta, pos1, pos2, x, tm, tile, n_tiles)
    rows = _experts(meta, xs, w_gu, w_down, tile, n_tiles, tf)
    return _combine(pos1, pos2, rows, x, route, g, b, tm, n_prompt)


def kernel(x_prompt, x_sample, cache_kv_w128, cache_kv_w512, cache_kv_w2048, state_ret,
           ln_g, ln_b, rel_bias, w_in_dil, w_out_dil, w_in_ret, ret_gn_g, w_out_ret,
           w_gu_dense, w_down_dense, w_router, w_gu_moe, w_down_moe):
    bsz, seq, _ = x_prompt.shape
    dbsz, t_new, _ = x_sample.shape
    tp = SAMPLE_PAD
    n_p = bsz * seq
    n_s = dbsz * tp
    caches = (cache_kv_w128, cache_kv_w512, cache_kv_w2048)

    w_in_dil_b = w_in_dil.astype(BF16)
    w_out_dil_b = w_out_dil.astype(BF16)
    w_in_ret_b = w_in_ret.astype(BF16)
    w_out_ret_b = w_out_ret.astype(BF16)
    w_gu_dense_b = w_gu_dense.astype(BF16)
    w_down_dense_b = w_down_dense.astype(BF16)
    w_gu_moe_b = w_gu_moe.astype(BF16)
    w_down_moe_b = w_down_moe.astype(BF16)
    lng = ln_g.reshape(DEPTH, 2, 1, D_MODEL)
    lnb = ln_b.reshape(DEPTH, 2, 1, D_MODEL)
    gn = ret_gn_g.reshape(1, RET_V)

    hp = x_prompt.reshape(n_p, D_MODEL)
    hs = jnp.pad(x_sample, ((0, 0), (0, tp - t_new), (0, 0))).reshape(n_s, D_MODEL)

    tm = min(1024, seq)
    qkv_groups, (kv128_p, kv512_p, kv2048_p) = _proj_dil_prompt(hp, w_in_dil_b, bsz, seq, tm)
    outs, lses = [], []
    for g, (window, dil) in enumerate(DIL_GROUPS):
        qkv_g = qkv_groups[g].reshape(bsz * dil, seq // dil, G_COLS)
        tbl = _prompt_table(rel_bias, g, window, dil)
        o, l = _attn_prompt(qkv_g, tbl, min(512, seq // dil))
        outs.append(o.reshape(bsz, dil, seq // dil, A_WIDTH))
        lses.append(l.reshape(bsz, dil, seq // dil, A_WIDTH))
    hp = _merge_out(outs, lses, w_out_dil_b, hp, lng[0, 0], lnb[0, 0], 512)

    qkv_s = _matmul(hs, w_in_dil_b, G_COLS)
    qkv_s3 = qkv_s.reshape(dbsz, tp, N_GROUPS * G_COLS)
    tabs = [_sample_tables(rel_bias, g, window, dil, caches[g].shape[1], t_new)
            for g, (window, dil) in enumerate(DIL_GROUPS)]
    q_t = qkv_s.reshape(dbsz, tp, N_GROUPS, 3, H_SLOT, HEAD_DIM).transpose(0, 2, 3, 4, 5, 1)
    caches_t = [c.transpose(0, 2, 3, 4, 1) for c in caches]
    mixed_t = _attn_sample(q_t, caches_t, [t[0] for t in tabs], [t[1] for t in tabs], t_new)
    mixed_s = mixed_t.transpose(0, 3, 1, 2)
    hs = _mm_ln(mixed_s.reshape(n_s, A_WIDTH), w_out_dil_b, hs, lng[0, 0], lnb[0, 0], n_s)
    rows_s = []
    for g in range(N_GROUPS):
        kv = qkv_s3[:, :t_new, g * G_COLS + A_WIDTH:(g + 1) * G_COLS]
        rows_s.append(kv.reshape(dbsz, t_new, 2, H_SLOT, HEAD_DIM))

    hp = _ffn_dense(hp, w_gu_dense_b, w_down_dense_b, lng[0, 1], lnb[0, 1], 512)
    hs = _ffn_dense(hs, w_gu_dense_b, w_down_dense_b, lng[0, 1], lnb[0, 1], n_s)

    cos_p, sin_p = _rope_tables(jnp.arange(seq, dtype=jnp.int32))
    pos_s = jnp.tile(PAST_LEN + jnp.arange(tp, dtype=jnp.int32), dbsz)
    cos_s, sin_s = _rope_tables(pos_s)
    proj_p = _proj_ret(hp, w_in_ret_b, cos_p, sin_p, tm, BF16)
    y_p, ret_p = _ret_prompt(proj_p, gn, bsz, seq, min(1024, seq))
    proj_s = _proj_ret(hs, w_in_ret_b, cos_s, sin_s, n_s, F32)
    y_s, ret_s = _ret_sample(proj_s.reshape(dbsz, tp, -1), state_ret, gn, t_new)
    h_all, route = _mm_ln_pair(y_p, y_s.reshape(n_s, RET_V), w_out_ret_b, hp, hs, lng[1, 0], lnb[1, 0],
                               w_router, 512)

    out_p, out_s = _moe(h_all, route, n_p, w_gu_moe_b, w_down_moe_b, lng[1, 1], lnb[1, 1], 512, 1024,
                        w_down_moe.shape[1] // 2)

    y_prompt = out_p.reshape(bsz, seq, D_MODEL)
    y_sample = out_s.reshape(dbsz, tp, D_MODEL)[:, :t_new]
    shape5 = lambda a: a.reshape(a.shape[0], a.shape[1], 2, H_SLOT, HEAD_DIM)
    return (y_prompt, y_sample, shape5(kv128_p), shape5(kv512_p), shape5(kv2048_p), ret_p,
            rows_s[0], rows_s[1], rows_s[2], ret_s)
```

```python
import functools

import jax
import jax.numpy as jnp
import numpy as np
from jax import lax
from jax.experimental import pallas as pl
from jax.experimental.pallas import tpu as pltpu

F32 = jnp.float32
BF16 = jnp.bfloat16

DEPTH = 2
D_MODEL = 1024
PAST_LEN = 16384
DIL_GROUPS = ((128, 1), (512, 4), (2048, 16))
N_GROUPS = 3
H_SLOT = 8
HEAD_DIM = 64
A_WIDTH = H_SLOT * HEAD_DIM
G_COLS = 3 * A_WIDTH
N_BUCKETS = 32
MAX_DISTANCE = 2048
RET_HEADS = 4
RET_DK = 256
RET_DV = 512
RET_CHUNK = 128
RET_HPS = 4
ROPE_BASE = 10000.0
RET_QK = RET_HEADS * RET_DK
RET_V = RET_HEADS * RET_DV
N_EXPERTS = 8
LN_EPS = 1e-5
GN_EPS = 1e-5
ALPHA = (2 * DEPTH) ** 0.25
NEG = -1e30

LANES = 128
ATT_BLOCK = 128
PERM = 256
FF_CHUNK = 512
SAMPLE_PAD = 16
VMEM_LIMIT = 56 * 1024 * 1024

_NT = (((1,), (1,)), ((), ()))
_TN = (((0,), (0,)), ((), ()))


def _params(n_grid):
    return pltpu.CompilerParams(dimension_semantics=("arbitrary",) * n_grid,
                                vmem_limit_bytes=VMEM_LIMIT)


def _layer_norm(z, g, b):
    mu = jnp.mean(z, axis=-1, keepdims=True)
    zc = z - mu
    var = jnp.mean(zc * zc, axis=-1, keepdims=True)
    return zc * lax.rsqrt(var + LN_EPS) * g + b


def _silu(a):
    return a / (1.0 + jnp.exp(-a))


def _t5_bucket(dist):
    max_exact = N_BUCKETS // 2
    d = np.asarray(dist, dtype=np.int64)
    scaled = np.log(np.maximum(d, 1) / max_exact) / np.log(MAX_DISTANCE / max_exact)
    large = np.minimum(max_exact + (scaled * (N_BUCKETS - max_exact)).astype(np.int32), N_BUCKETS - 1)
    return np.where(d < max_exact, d, large).astype(np.int32)


def _group_bias(rel_bias, g, window, dil):
    n_keys = window // dil + 1
    buckets = jnp.asarray(_t5_bucket(np.arange(n_keys) * dil))
    return rel_bias[buckets][:, g * H_SLOT:(g + 1) * H_SLOT].T.astype(F32)


def _matmul_kernel(x_ref, w_ref, o_ref):
    o_ref[...] = jnp.dot(x_ref[...].astype(BF16), w_ref[...], preferred_element_type=F32)


def _matmul(x, w, tn):
    m, k = x.shape
    n = w.shape[1]
    return pl.pallas_call(
        _matmul_kernel,
        grid=(n // tn,),
        in_specs=[pl.BlockSpec((m, k), lambda j: (0, 0)),
                  pl.BlockSpec((k, tn), lambda j: (0, j))],
        out_specs=pl.BlockSpec((m, tn), lambda j: (0, j)),
        out_shape=jax.ShapeDtypeStruct((m, n), F32),
        compiler_params=_params(1),
        name="matmul_sample",
    )(x, w)


def _deinterleave_matrix(dil):
    p = np.zeros((PERM, PERM), np.float32)
    rows = PERM // dil
    for r in range(dil):
        for m in range(rows):
            p[r * rows + m, m * dil + r] = 1.0
    return p


def _proj_dil_kernel(x_ref, w_ref, p1_ref, p2_ref, q0_ref, q1_ref, q2_ref, kv0_ref, kv1_ref, kv2_ref, xperm, xb_ref,
                     *, tm, tpb, keeps):
    g = pl.program_id(1)
    j = pl.program_id(0) % tpb

    @pl.when(g == 0)
    def _():
        xb_ref[...] = x_ref[...].astype(BF16)
    q_refs = (q0_ref, q1_ref, q2_ref)
    kv_refs = (kv0_ref, kv1_ref, kv2_ref)
    perms = (None, p1_ref, p2_ref)

    def store_q(ref, r, row0, n_rows, acc, src0):
        ref[0, r, row0:row0 + n_rows, :A_WIDTH] = (acc[src0:src0 + n_rows, :A_WIDTH] * (HEAD_DIM ** -0.5)).astype(BF16)
        ref[0, r, row0:row0 + n_rows, A_WIDTH:] = acc[src0:src0 + n_rows, A_WIDTH:].astype(BF16)

    for gi, (_, dil) in enumerate(DIL_GROUPS):
        keep = keeps[gi]
        in_tail = (j >= tpb - keep // tm) if keep >= tm else (j == tpb - 1)
        row0 = 0 if keep >= tm else tm - keep

        @pl.when(g == gi)
        def _(gi=gi, dil=dil, in_tail=in_tail, row0=row0):
            xb = xb_ref[...]
            if dil == 1:
                acc = jnp.dot(xb, w_ref[...], preferred_element_type=F32)
                store_q(q_refs[gi], 0, 0, tm, acc, 0)
            else:
                for sub in range(tm // PERM):
                    blk = xb[sub * PERM:(sub + 1) * PERM]
                    xperm[sub * PERM:(sub + 1) * PERM, :] = jnp.dot(
                        perms[gi][...], blk, preferred_element_type=F32).astype(BF16)
                acc = jnp.dot(xperm[...], w_ref[...], preferred_element_type=F32)
                rows = PERM // dil
                for sub in range(tm // PERM):
                    for r in range(dil):
                        store_q(q_refs[gi], r, sub * rows, rows, acc, sub * PERM + r * rows)

            @pl.when(in_tail)
            def _():
                if dil == 1:
                    kv_refs[gi][0] = acc[row0:, A_WIDTH:]
                else:
                    kv_refs[gi][0] = jnp.dot(xb[row0:], w_ref[:, A_WIDTH:], preferred_element_type=F32)


def _proj_dil_prompt(x, w, bsz, seq, tm):
    tpb = seq // tm
    keeps = tuple(min(wd, seq) for wd, _ in DIL_GROUPS)

    def kv_spec(keep):
        if keep >= tm:
            first = tpb - keep // tm
            return pl.BlockSpec((1, tm, 2 * A_WIDTH),
                                lambda i, g: (i // tpb, jnp.maximum(i % tpb - first, 0), 0))
        return pl.BlockSpec((1, keep, 2 * A_WIDTH), lambda i, g: (i // tpb, 0, 0))

    perm = lambda d: jnp.asarray(_deinterleave_matrix(d), BF16)
    const = pl.BlockSpec((PERM, PERM), lambda i, g: (0, 0))
    outs = pl.pallas_call(
        functools.partial(_proj_dil_kernel, tm=tm, tpb=tpb, keeps=keeps),
        grid=(bsz * tpb, N_GROUPS),
        in_specs=[pl.BlockSpec((tm, D_MODEL), lambda i, g: (i, 0)),
                  pl.BlockSpec((D_MODEL, G_COLS), lambda i, g: (0, g)), const, const],
        out_specs=[pl.BlockSpec((1, d, tm // d, G_COLS), lambda i, g: (i // tpb, 0, i % tpb, 0))
                   for _, d in DIL_GROUPS] + [kv_spec(k) for k in keeps],
        out_shape=[jax.ShapeDtypeStruct((bsz, d, seq // d, G_COLS), BF16) for _, d in DIL_GROUPS]
        + [jax.ShapeDtypeStruct((bsz, k, 2 * A_WIDTH), F32) for k in keeps],
        scratch_shapes=[pltpu.VMEM((tm, D_MODEL), BF16), pltpu.VMEM((tm, D_MODEL), BF16)],
        compiler_params=_params(2),
        name="proj_dil_prompt",
    )(x, w, perm(DIL_GROUPS[1][1]), perm(DIL_GROUPS[2][1]))
    return outs[:3], outs[3:]


def _attn_prompt_kernel(q_ref, k_ref, v_ref, kp_ref, vp_ref, tbl_ref, o_ref, lse_ref, kbuf, vbuf, *, tq):
    n = pl.program_id(1)
    kbuf[0:ATT_BLOCK, :] = kp_ref[0]
    kbuf[ATT_BLOCK:, :] = k_ref[0]
    vbuf[0:ATT_BLOCK, :] = vp_ref[0]
    vbuf[ATT_BLOCK:, :] = v_ref[0]
    low = lax.broadcasted_iota(jnp.int32, (ATT_BLOCK, LANES), 1) < HEAD_DIM

    def body(m, carry):
        r0 = pl.multiple_of(m * ATT_BLOCK, ATT_BLOCK)
        first = jnp.where(jnp.logical_and(n == 0, m == 0), 1, 0)
        for p in range(A_WIDTH // LANES):
            cols = slice(p * LANES, (p + 1) * LANES)
            qm = q_ref[0, pl.ds(r0, ATT_BLOCK), cols]
            keys = kbuf[pl.ds(r0, 2 * ATT_BLOCK), cols]
            vals = vbuf[pl.ds(r0, 2 * ATT_BLOCK), cols]
            outs, lses = [], []
            for a in range(2):
                qa = jnp.where(low if a == 0 else jnp.logical_not(low), qm, jnp.zeros_like(qm))
                s = lax.dot_general(qa, keys, _NT, preferred_element_type=F32) + tbl_ref[first, 2 * p + a]
                mx = jnp.max(s, axis=-1, keepdims=True)
                e = jnp.exp(s - mx)
                l = jnp.sum(e, axis=-1, keepdims=True)
                o = jnp.dot(e.astype(BF16), vals, preferred_element_type=F32)
                outs.append(o / l)
                lses.append(jnp.broadcast_to(mx + jnp.log(l), (ATT_BLOCK, LANES)))
            o_ref[0, pl.ds(r0, ATT_BLOCK), cols] = jnp.where(low, outs[0], outs[1]).astype(BF16)
            lse_ref[0, pl.ds(r0, ATT_BLOCK), cols] = jnp.where(low, lses[0], lses[1])
        return carry

    lax.fori_loop(0, tq // ATT_BLOCK, body, 0)


def _attn_prompt(qkv, tbl, tq):
    nb, length, _ = qkv.shape
    sub = tq // ATT_BLOCK
    cur = lambda c: pl.BlockSpec((1, tq, A_WIDTH), lambda s, n: (s, n, c))
    prev = lambda c: pl.BlockSpec((1, ATT_BLOCK, A_WIDTH), lambda s, n: (s, jnp.maximum(n * sub - 1, 0), c))
    return pl.pallas_call(
        functools.partial(_attn_prompt_kernel, tq=tq),
        grid=(nb, length // tq),
        in_specs=[cur(0), cur(1), cur(2), prev(1), prev(2),
                  pl.BlockSpec(tbl.shape, lambda s, n: (0, 0, 0, 0))],
        out_specs=[pl.BlockSpec((1, tq, A_WIDTH), lambda s, n: (s, n, 0))] * 2,
        out_shape=[jax.ShapeDtypeStruct((nb, length, A_WIDTH), BF16),
                   jax.ShapeDtypeStruct((nb, length, A_WIDTH), F32)],
        scratch_shapes=[pltpu.VMEM((tq + ATT_BLOCK, A_WIDTH), BF16)] * 2,
        compiler_params=_params(2),
        name="attn_prompt",
    )(qkv, qkv, qkv, qkv, qkv, tbl)


def _toeplitz(u, n_rows, n_cols):
    h, n = u.shape
    assert n == n_rows + n_cols - 1
    up = jnp.pad(u, ((0, 0), (0, 1)))
    w = jnp.tile(up, (1, n_rows))[:, :n_rows * n].reshape(h, n_rows, n)
    return w[:, :, n_rows - 1:n_rows - 1 + n_cols]


def _bias_by_offset(bias, dist, valid):
    return jnp.where(valid[None], bias[:, np.clip(dist, 0, bias.shape[1] - 1)], NEG)


def _prompt_table(rel_bias, g, window, dil):
    bias = _group_bias(rel_bias, g, window, dil)
    k = np.arange(3 * ATT_BLOCK - 1)
    dist = 2 * ATT_BLOCK - 1 - k
    u = _bias_by_offset(bias, dist, (dist >= 0) & (dist <= window // dil))
    tbl = _toeplitz(u, ATT_BLOCK, 2 * ATT_BLOCK)
    c = np.arange(2 * ATT_BLOCK)[None, None, :]
    tbl_first = jnp.where(c < ATT_BLOCK, NEG, tbl)
    return jnp.stack([tbl, tbl_first], 0)


def _split3(x):
    hi = x.astype(BF16)
    r1 = x - hi.astype(F32)
    mid = r1.astype(BF16)
    lo = (r1 - mid.astype(F32)).astype(BF16)
    return hi, mid, lo


def _merge_out_kernel(o0, o1, o2, l0, l1, l2, pt1_ref, pt2_ref, w_ref, x_ref, g_ref, b_ref, out_ref, *, tm):
    pts = (None, pt1_ref, pt2_ref)

    def natural(ref, gi, exact_f32):
        dil = DIL_GROUPS[gi][1]
        if dil == 1:
            return ref[0, 0].astype(F32)
        rows = PERM // dil
        blocks = []
        for sub in range(tm // PERM):
            piece = jnp.concatenate([ref[0, r, sub * rows:(sub + 1) * rows, :] for r in range(dil)], axis=0)
            parts = _split3(piece) if exact_f32 else (piece,)
            blocks.append(sum(jnp.dot(pts[gi][...], p, preferred_element_type=F32) for p in parts))
        return jnp.concatenate(blocks, axis=0)

    os_ = [natural(r, gi, False) for gi, r in enumerate((o0, o1, o2))]
    ls = [natural(r, gi, True) for gi, r in enumerate((l0, l1, l2))]
    mx = jnp.maximum(jnp.maximum(ls[0], ls[1]), ls[2])
    es = [jnp.exp(l - mx) for l in ls]
    mixed = (es[0] * os_[0] + es[1] * os_[1] + es[2] * os_[2]) / (es[0] + es[1] + es[2])
    acc = jnp.dot(mixed.astype(BF16), w_ref[...], preferred_element_type=F32)
    out_ref[...] = _layer_norm(ALPHA * x_ref[...] + acc, g_ref[...], b_ref[...])


def _merge_out(os_, ls_, w, x, g, b, tm):
    bsz, _, seq, _ = os_[0].shape
    tpb = seq // tm
    grp = lambda d: pl.BlockSpec((1, d, tm // d, A_WIDTH), lambda i: (i // tpb, 0, i % tpb, 0))
    specs = [grp(d) for _, d in DIL_GROUPS]
    pt = lambda d: jnp.asarray(_deinterleave_matrix(d).T, BF16)
    const = pl.BlockSpec((PERM, PERM), lambda i: (0, 0))
    vec = pl.BlockSpec((1, D_MODEL), lambda i: (0, 0))
    return pl.pallas_call(
        functools.partial(_merge_out_kernel, tm=tm), grid=(bsz * tpb,),
        in_specs=specs + specs + [const, const, pl.BlockSpec((A_WIDTH, D_MODEL), lambda i: (0, 0)),
                                  pl.BlockSpec((tm, D_MODEL), lambda i: (i, 0)), vec, vec],
        out_specs=pl.BlockSpec((tm, D_MODEL), lambda i: (i, 0)),
        out_shape=jax.ShapeDtypeStruct((bsz * seq, D_MODEL), F32),
        compiler_params=_params(1), name="merge_outproj_ln",
    )(*os_, *ls_, pt(DIL_GROUPS[1][1]), pt(DIL_GROUPS[2][1]), w, x, g, b)


def _attn_sample_kernel(q_ref, c0_ref, c1_ref, c2_ref, tb0, tb1, tb2, tn0, tn1, tn2, out_ref, *scratch, t_new):
    tp = SAMPLE_PAD
    lane_t = lax.broadcasted_iota(jnp.int32, (HEAD_DIM, tp), 1)
    sets = []
    for g, (c_ref, tb, tn) in enumerate(((c0_ref, tb0, tn0), (c1_ref, tb1, tn1), (c2_ref, tb2, tn2))):
        sets.append((lambda h, c_ref=c_ref: c_ref[0, 0, h], lambda h, c_ref=c_ref: c_ref[0, 1, h], tb, g))
        sets.append((lambda h, g=g: q_ref[0, g, 1, h], lambda h, g=g: q_ref[0, g, 2, h], tn, g))
    out_ref[0] = jnp.zeros(out_ref.shape[1:], F32)

    for t in range(t_new):
        for h in range(H_SLOT):
            for (keys, _, _, g), s_ref in zip(sets, scratch):
                qcol = q_ref[0, g, 0, h][:, t:t + 1] * (HEAD_DIM ** -0.5)
                s_ref[h:h + 1, :] = jnp.sum(keys(h) * qcol, axis=0, keepdims=True)
        scores = [s_ref[...] + tab[t] for (_, _, tab, _), s_ref in zip(sets, scratch)]
        mx = functools.reduce(jnp.maximum, [jnp.max(s, axis=-1, keepdims=True) for s in scores])
        den = jnp.zeros((H_SLOT, 1), F32)
        for s, s_ref in zip(scores, scratch):
            e = jnp.exp(s - mx)
            den = den + jnp.sum(e, axis=-1, keepdims=True)
            s_ref[...] = e
        inv = 1.0 / den

        for h in range(H_SLOT):
            col = jnp.zeros((HEAD_DIM, 1), F32)
            for (_, values, _, _), s_ref in zip(sets, scratch):
                col = col + jnp.sum(values(h) * s_ref[h:h + 1, :], axis=-1, keepdims=True)
            out_ref[0, h] = jnp.where(lane_t == t, col * inv[h:h + 1, :], out_ref[0, h])


def _sample_tables(rel_bias, g, window, dil, length, t_new):
    bias = _group_bias(rel_bias, g, window, dil)
    n_keys = window // dil + 1
    tp = SAMPLE_PAD
    k = np.arange(tp + length - 1)
    dist = length + tp - 1 - k
    u = _bias_by_offset(bias, dist // dil, (dist % dil == 0) & (dist // dil < n_keys))
    tc = _toeplitz(u, tp, length)
    k2 = np.arange(2 * tp - 1)
    dist2 = tp - 1 - k2
    u2 = _bias_by_offset(bias, dist2 // dil, (dist2 >= 0) & (dist2 % dil == 0) & (dist2 // dil < n_keys))
    live_col = (np.arange(tp) < t_new)[None, None, :]
    tn = jnp.where(live_col, _toeplitz(u2, tp, tp), NEG)
    return tc.transpose(1, 0, 2)[:t_new], tn.transpose(1, 0, 2)[:t_new]


def _attn_sample(q_t, caches_t, tables_c, tables_n, t_new):
    bsz = q_t.shape[0]
    tp = SAMPLE_PAD
    const = lambda a: pl.BlockSpec(a.shape, lambda b: (0, 0, 0))
    scratch = []
    for c in caches_t:
        scratch += [pltpu.VMEM((H_SLOT, c.shape[-1]), F32), pltpu.VMEM((H_SLOT, tp), F32)]
    return pl.pallas_call(
        functools.partial(_attn_sample_kernel, t_new=t_new),
        grid=(bsz,),
        in_specs=[pl.BlockSpec((1,) + q_t.shape[1:], lambda b: (b, 0, 0, 0, 0, 0))]
        + [pl.BlockSpec((1,) + c.shape[1:], lambda b: (b, 0, 0, 0, 0)) for c in caches_t]
        + [const(t) for t in tables_c] + [const(t) for t in tables_n],
        out_specs=pl.BlockSpec((1, H_SLOT, HEAD_DIM, tp), lambda b: (b, 0, 0, 0)),
        out_shape=jax.ShapeDtypeStruct((bsz, H_SLOT, HEAD_DIM, tp), F32),
        scratch_shapes=scratch,
        compiler_params=_params(1),
        name="attn_sample",
    )(q_t, *caches_t, *tables_c, *tables_n)


def _mm_ln_kernel(y_ref, w_ref, x_ref, g_ref, b_ref, o_ref):
    acc = jnp.dot(y_ref[...].astype(BF16), w_ref[...], preferred_element_type=F32)
    o_ref[...] = _layer_norm(ALPHA * x_ref[...] + acc, g_ref[...], b_ref[...])


def _mm_ln(y, w, x, g, b, tm):
    m, k = y.shape
    vec = pl.BlockSpec((1, D_MODEL), lambda i: (0, 0))
    return pl.pallas_call(
        _mm_ln_kernel, grid=(m // tm,),
        in_specs=[pl.BlockSpec((tm, k), lambda i: (i, 0)), pl.BlockSpec((k, D_MODEL), lambda i: (0, 0)),
                  pl.BlockSpec((tm, D_MODEL), lambda i: (i, 0)), vec, vec],
        out_specs=pl.BlockSpec((tm, D_MODEL), lambda i: (i, 0)),
        out_shape=jax.ShapeDtypeStruct((m, D_MODEL), F32),
        compiler_params=_params(1), name="outproj_ln",
    )(y, w, x, g, b)


def _mm_ln_pair_kernel(yp_ref, ys_ref, w_ref, xp_ref, xs_ref, g_ref, b_ref, wr_ref, o_ref, route_ref, *, prompt_tiles):
    def run(y_ref, x_ref):
        acc = jnp.dot(y_ref[...].astype(BF16), w_ref[...], preferred_element_type=F32)
        h = _layer_norm(ALPHA * x_ref[...] + acc, g_ref[...], b_ref[...])
        o_ref[...] = h
        route_ref[...] = _top2_route(jnp.dot(h.astype(BF16), wr_ref[...], preferred_element_type=F32))

    pl.when(pl.program_id(0) < prompt_tiles)(lambda: run(yp_ref, xp_ref))
    pl.when(pl.program_id(0) >= prompt_tiles)(lambda: run(ys_ref, xs_ref))


def _mm_ln_pair(y_p, y_s, w, x_p, x_s, g, b, w_router, tm):
    n_p, k = y_p.shape
    n_s = y_s.shape[0]
    pt = n_p // tm
    first = lambda width: pl.BlockSpec((tm, width), lambda i: (jnp.minimum(i, pt - 1), 0))
    second = lambda width: pl.BlockSpec((tm, width), lambda i: (jnp.maximum(i - pt, 0), 0))
    vec = pl.BlockSpec((1, D_MODEL), lambda i: (0, 0))
    wr = jnp.pad(w_router, ((0, 0), (0, LANES - N_EXPERTS))).astype(BF16)
    return pl.pallas_call(
        functools.partial(_mm_ln_pair_kernel, prompt_tiles=pt), grid=((n_p + n_s) // tm,),
        in_specs=[first(k), second(k), pl.BlockSpec((k, D_MODEL), lambda i: (0, 0)),
                  first(D_MODEL), second(D_MODEL), vec, vec, pl.BlockSpec((D_MODEL, LANES), lambda i: (0, 0))],
        out_specs=[pl.BlockSpec((tm, D_MODEL), lambda i: (i, 0)), pl.BlockSpec((tm, LANES), lambda i: (i, 0))],
        out_shape=[jax.ShapeDtypeStruct((n_p + n_s, D_MODEL), F32),
                   jax.ShapeDtypeStruct((n_p + n_s, LANES), F32)],
        compiler_params=_params(1), name="outproj_ln_route",
    )(y_p, y_s, w, x_p, x_s, g, b, wr)


def _swiglu_chunks(xb, wg_ref, wu_ref, wd_ref, lead, width, chunk):
    acc = None
    for c0 in range(0, width, chunk):
        cs = slice(c0, min(c0 + chunk, width))
        a = jnp.dot(xb, wg_ref[lead + (slice(None), cs)], preferred_element_type=F32)
        u = jnp.dot(xb, wu_ref[lead + (slice(None), cs)], preferred_element_type=F32)
        part = jnp.dot((_silu(a) * u).astype(BF16), wd_ref[lead + (cs, slice(None))], preferred_element_type=F32)
        acc = part if acc is None else acc + part
    return acc


def _ffn_kernel(x_ref, wg_ref, wu_ref, wd_ref, g_ref, b_ref, o_ref, *, d_ff):
    x = x_ref[...]
    acc = _swiglu_chunks(x.astype(BF16), wg_ref, wu_ref, wd_ref, (), d_ff, FF_CHUNK)
    o_ref[...] = _layer_norm(ALPHA * x + acc, g_ref[...], b_ref[...])


def _ffn_dense(x, w_gu, w_down, g, b, tm):
    m = x.shape[0]
    d_ff = w_down.shape[0]
    vec = pl.BlockSpec((1, D_MODEL), lambda i: (0, 0))
    return pl.pallas_call(
        functools.partial(_ffn_kernel, d_ff=d_ff), grid=(m // tm,),
        in_specs=[pl.BlockSpec((tm, D_MODEL), lambda i: (i, 0)),
                  pl.BlockSpec((D_MODEL, d_ff), lambda i: (0, 0)),
                  pl.BlockSpec((D_MODEL, d_ff), lambda i: (0, 1)),
                  pl.BlockSpec((d_ff, D_MODEL), lambda i: (0, 0)), vec, vec],
        out_specs=pl.BlockSpec((tm, D_MODEL), lambda i: (i, 0)),
        out_shape=jax.ShapeDtypeStruct((m, D_MODEL), F32),
        compiler_params=_params(1), name="ffn_dense",
    )(x, w_gu, w_gu, w_down, g, b)


def _proj_ret_kernel(x_ref, w_ref, cos_ref, sin_ref, o_ref, xb_ref, *, scale_k):
    n = pl.program_id(1)

    @pl.when(n == 0)
    def _():
        xb_ref[...] = x_ref[...].astype(BF16)

    xb = xb_ref[...]
    half = RET_DK // 2

    @pl.when(n < 2)
    def _():
        cos = cos_ref[...]
        sin = sin_ref[...]
        scale = jnp.where(n == 1, scale_k, 1.0)
        for h in range(RET_HEADS):
            acc = jnp.dot(xb, w_ref[:, h * RET_DK:(h + 1) * RET_DK], preferred_element_type=F32)
            x1 = acc[:, :half]
            x2 = acc[:, half:]
            o_ref[:, h * RET_DK:h * RET_DK + half] = ((x1 * cos - x2 * sin) * scale).astype(o_ref.dtype)
            o_ref[:, h * RET_DK + half:(h + 1) * RET_DK] = ((x1 * sin + x2 * cos) * scale).astype(o_ref.dtype)

    @pl.when(n >= 2)
    def _():
        for h in range(RET_HEADS):
            cs = slice(h * RET_DK, (h + 1) * RET_DK)
            o_ref[:, cs] = jnp.dot(xb, w_ref[:, cs], preferred_element_type=F32).astype(o_ref.dtype)


def _proj_ret(x, w, cos, sin, tm, out_dtype):
    m = x.shape[0]
    n_cols = w.shape[1]
    tn = RET_QK
    pos_tiles = cos.shape[0] // tm
    return pl.pallas_call(
        functools.partial(_proj_ret_kernel, scale_k=RET_DK ** -0.5),
        grid=(m // tm, n_cols // tn),
        in_specs=[pl.BlockSpec((tm, D_MODEL), lambda i, n: (i, 0)),
                  pl.BlockSpec((D_MODEL, tn), lambda i, n: (0, n)),
                  pl.BlockSpec((tm, RET_DK // 2), lambda i, n: (i % pos_tiles, 0)),
                  pl.BlockSpec((tm, RET_DK // 2), lambda i, n: (i % pos_tiles, 0))],
        out_specs=pl.BlockSpec((tm, tn), lambda i, n: (i, n)),
        out_shape=jax.ShapeDtypeStruct((m, n_cols), out_dtype),
        scratch_shapes=[pltpu.VMEM((tm, D_MODEL), BF16)],
        compiler_params=_params(2), name="proj_ret",
    )(x, w, cos, sin)


def _rope_tables(pos):
    half = RET_DK // 2
    inv = 1.0 / (ROPE_BASE ** (jnp.arange(half, dtype=F32) / half))
    ang = pos.astype(F32)[:, None] * inv[None]
    return jnp.cos(ang), jnp.sin(ang)


def _log_gamma():
    return jnp.log(1.0 - 2.0 ** (-5.0 - jnp.arange(RET_HEADS, dtype=F32)))


def _decay_tables(c, rows):
    lg = _log_gamma()
    n = jnp.arange(rows, dtype=F32)
    live = n < c
    diff = n[:, None] - n[None, :]
    decay = jnp.where((diff >= 0)[None] & live[None, None, :],
                      jnp.exp(jnp.maximum(diff, 0.0)[None] * lg[:, None, None]), 0.0)
    q_decay = jnp.exp((n[None, :] + 1.0) * lg[:, None])
    k_decay = jnp.where(live[None], jnp.exp((c - 1.0 - n)[None, :] * lg[:, None]), 0.0)
    chunk_decay = jnp.exp(c * lg)
    return decay, q_decay, k_decay, chunk_decay


def _group_norm_gate(o, gate, gn):
    mu = jnp.mean(o, axis=-1, keepdims=True)
    oc = o - mu
    var = jnp.mean(oc * oc, axis=-1, keepdims=True)
    return _silu(gate) * (oc * lax.rsqrt(var + GN_EPS) * gn)


def _ret_prompt_kernel(q_ref, k_ref, v_ref, gate_ref, dec_ref, qd_ref, kd_ref, cd_ref, gn_ref,
                       y_ref, st_ref, s_ref, *, tb):
    cb = pl.program_id(2)

    @pl.when(cb == 0)
    def _():
        s_ref[...] = jnp.zeros_like(s_ref)

    def body(ci, carry):
        r0 = pl.multiple_of(ci * RET_CHUNK, RET_CHUNK)
        rows = pl.ds(r0, RET_CHUNK)
        for j in range(RET_HPS):
            kc = slice(j * RET_DK, (j + 1) * RET_DK)
            vc = slice(j * RET_DV, (j + 1) * RET_DV)
            q = q_ref[0, rows, kc]
            k = k_ref[0, rows, kc]
            v = v_ref[0, rows, vc]
            state = s_ref[j]
            scores = lax.dot_general(q, k, _NT, preferred_element_type=F32) * dec_ref[j]
            inner = jnp.dot(scores.astype(BF16), v, preferred_element_type=F32)
            cross = jnp.dot(q, state.astype(BF16), preferred_element_type=F32) * qd_ref[j]
            kd = (k.astype(F32) * kd_ref[j]).astype(BF16)
            s_ref[j] = cd_ref[j, 0:1, :] * state + lax.dot_general(kd, v, _TN, preferred_element_type=F32)
            y = _group_norm_gate(inner + cross, gate_ref[0, rows, vc].astype(F32), gn_ref[:, vc])
            y_ref[0, rows, vc] = y.astype(y_ref.dtype)
        return carry

    lax.fori_loop(0, tb // RET_CHUNK, body, 0)

    @pl.when(cb == pl.num_programs(2) - 1)
    def _():
        st_ref[0] = s_ref[...]


def _ret_prompt(proj, gn, bsz, seq, tb):
    decay, q_decay, k_decay, chunk_decay = _decay_tables(RET_CHUNK, RET_CHUNK)
    qd = jnp.broadcast_to(q_decay[:, :, None], (RET_HEADS, RET_CHUNK, RET_DV))
    kd = jnp.broadcast_to(k_decay[:, :, None], (RET_HEADS, RET_CHUNK, RET_DK))
    cd = jnp.broadcast_to(chunk_decay[:, None, None], (RET_HEADS, 8, RET_DV))
    proj = proj.reshape(bsz, seq, proj.shape[-1])
    hps = RET_HPS
    k_off = RET_QK // (hps * RET_DK)
    v_off = 2 * RET_QK // (hps * RET_DV)
    g_off = v_off + RET_HEADS // hps
    per_head = lambda shape: pl.BlockSpec((hps,) + shape, lambda b, h, c: (h, 0, 0))
    y, state = pl.pallas_call(
        functools.partial(_ret_prompt_kernel, tb=tb),
        grid=(bsz, RET_HEADS // hps, seq // tb),
        in_specs=[pl.BlockSpec((1, tb, hps * RET_DK), lambda b, h, c: (b, c, h)),
                  pl.BlockSpec((1, tb, hps * RET_DK), lambda b, h, c: (b, c, k_off + h)),
                  pl.BlockSpec((1, tb, hps * RET_DV), lambda b, h, c: (b, c, v_off + h)),
                  pl.BlockSpec((1, tb, hps * RET_DV), lambda b, h, c: (b, c, g_off + h)),
                  per_head((RET_CHUNK, RET_CHUNK)), per_head((RET_CHUNK, RET_DV)),
                  per_head((RET_CHUNK, RET_DK)), per_head((8, RET_DV)),
                  pl.BlockSpec((1, hps * RET_DV), lambda b, h, c: (0, h))],
        out_specs=[pl.BlockSpec((1, tb, hps * RET_DV), lambda b, h, c: (b, c, h)),
                   pl.BlockSpec((1, hps, RET_DK, RET_DV), lambda b, h, c: (b, h, 0, 0))],
        out_shape=[jax.ShapeDtypeStruct((bsz, seq, RET_V), BF16),
                   jax.ShapeDtypeStruct((bsz, RET_HEADS, RET_DK, RET_DV), F32)],
        scratch_shapes=[pltpu.VMEM((hps, RET_DK, RET_DV), F32)],
        compiler_params=_params(3), name="retention_prompt",
    )(proj, proj, proj, proj, decay, qd, kd, cd, gn)
    return y.reshape(bsz * seq, RET_V), state


def _ret_sample_kernel(p_ref, st_ref, dec_ref, qd_ref, kd_ref, cd_ref, gn_ref, y_ref, ns_ref):
    tp = SAMPLE_PAD
    proj = p_ref[0]
    for h in range(RET_HEADS):
        q = proj[:, h * RET_DK:(h + 1) * RET_DK].astype(BF16)
        k = proj[:, RET_QK + h * RET_DK:RET_QK + (h + 1) * RET_DK]
        v = proj[:, 2 * RET_QK + h * RET_DV:2 * RET_QK + (h + 1) * RET_DV]
        gate = proj[:, 2 * RET_QK + RET_V + h * RET_DV:2 * RET_QK + RET_V + (h + 1) * RET_DV]
        zk = jnp.zeros((LANES - tp, RET_DK), F32)
        zv = jnp.zeros((LANES - tp, RET_DV), F32)
        kp = jnp.concatenate([k, zk], axis=0).astype(BF16)
        kdp = jnp.concatenate([k * kd_ref[h], zk], axis=0).astype(BF16)
        vp = jnp.concatenate([v, zv], axis=0).astype(BF16)
        state = st_ref[0, h]
        scores = lax.dot_general(q, kp, _NT, preferred_element_type=F32) * dec_ref[h]
        inner = jnp.dot(scores.astype(BF16), vp, preferred_element_type=F32)
        cross = jnp.dot(q, state.astype(BF16), preferred_element_type=F32) * qd_ref[h]
        ns_ref[0, h] = cd_ref[h, 0:1, :] * state + lax.dot_general(kdp, vp, _TN, preferred_element_type=F32)
        y = _group_norm_gate(inner + cross, gate, gn_ref[:, h * RET_DV:(h + 1) * RET_DV])
        y_ref[0, :, h * RET_DV:(h + 1) * RET_DV] = y


def _ret_sample(proj, state, gn, t_new):
    bsz = proj.shape[0]
    tp = SAMPLE_PAD
    decay, q_decay, k_decay, chunk_decay = _decay_tables(t_new, tp)
    dec = jnp.pad(decay, ((0, 0), (0, 0), (0, LANES - tp)))
    qd = jnp.broadcast_to(q_decay[:, :, None], (RET_HEADS, tp, RET_DV))
    kd = jnp.broadcast_to(k_decay[:, :, None], (RET_HEADS, tp, RET_DK))
    cd = jnp.broadcast_to(chunk_decay[:, None, None], (RET_HEADS, 8, RET_DV))
    const = lambda a: pl.BlockSpec(a.shape, lambda b: (0,) * a.ndim)
    return pl.pallas_call(
        _ret_sample_kernel, grid=(bsz,),
        in_specs=[pl.BlockSpec((1,) + proj.shape[1:], lambda b: (b, 0, 0)),
                  pl.BlockSpec((1,) + state.shape[1:], lambda b: (b, 0, 0, 0)),
                  const(dec), const(qd), const(kd), const(cd), const(gn)],
        out_specs=[pl.BlockSpec((1, tp, RET_V), lambda b: (b, 0, 0)),
                   pl.BlockSpec((1,) + state.shape[1:], lambda b: (b, 0, 0, 0))],
        out_shape=[jax.ShapeDtypeStruct((bsz, tp, RET_V), F32),
                   jax.ShapeDtypeStruct(state.shape, F32)],
        compiler_params=_params(1), name="retention_sample",
    )(proj, state, dec, qd, kd, cd, gn)


def _top2_route(logits):
    lane = lax.broadcasted_iota(jnp.int32, logits.shape, 1)
    logits = jnp.where(lane < N_EXPERTS, logits, NEG)
    m1 = jnp.max(logits, axis=-1, keepdims=True)
    i1 = jnp.min(jnp.where(logits == m1, lane, LANES), axis=-1, keepdims=True)
    rest = jnp.where(lane == i1, NEG, logits)
    m2 = jnp.max(rest, axis=-1, keepdims=True)
    i2 = jnp.min(jnp.where(rest == m2, lane, LANES), axis=-1, keepdims=True)
    e2 = jnp.exp(m2 - m1)
    den = 1.0 + e2
    return jnp.where(lane == 0, 1.0 / den,
                     jnp.where(lane == 1, e2 / den,
                               jnp.where(lane == 2, i1.astype(F32),
                                         jnp.where(lane == 3, i2.astype(F32), 0.0))))


def _route_plan(e1, e2, tile):
    n_tok = e1.shape[0]
    e = jnp.concatenate([e1, e2])
    onehot = (e[:, None] == jnp.arange(N_EXPERTS, dtype=jnp.int32)[None]).astype(jnp.int32)
    csum = jnp.cumsum(onehot, axis=0)
    rank = jnp.take_along_axis(csum, e[:, None], axis=1)[:, 0] - 1
    tiles_per_expert = (csum[-1] + tile - 1) // tile
    tile_end = jnp.cumsum(tiles_per_expert)
    pos = ((tile_end - tiles_per_expert) * tile)[e] + rank
    n_tiles = (2 * n_tok) // tile + N_EXPERTS
    n_used = tile_end[-1]
    tile_ids = jnp.arange(n_tiles, dtype=jnp.int32)
    tile_expert = jnp.minimum(jnp.searchsorted(tile_end, tile_ids, side="right"), N_EXPERTS - 1)
    tile_expert = jnp.where(tile_ids < n_used, tile_expert, tile_expert[n_used - 1])
    meta = jnp.concatenate([tile_expert, n_used[None], tile_end]).astype(jnp.int32)
    return meta, pos[:n_tok], pos[n_tok:], n_tiles


SUBLANES = 8


def _start_row_gather(idx_ref, src_hbm, dst3, sem):
    def issue(blk, carry):
        for s in range(SUBLANES):
            pltpu.make_async_copy(src_hbm.at[pl.ds(idx_ref[0, 0, blk * SUBLANES + s], 1), :],
                                  dst3.at[blk, pl.ds(s, 1), :], sem).start()
        return carry

    lax.fori_loop(0, dst3.shape[0], issue, 0)


def _start_row_scatter(idx_ref, src3, dst_hbm, sem):
    def issue(blk, carry):
        for s in range(SUBLANES):
            pltpu.make_async_copy(src3.at[blk, pl.ds(s, 1), :],
                                  dst_hbm.at[pl.ds(idx_ref[0, 0, blk * SUBLANES + s], 1), :], sem).start()
        return carry

    lax.fori_loop(0, src3.shape[0], issue, 0)


def _wait_rows(hbm, vmem3, sem):
    rows = vmem3.shape[0] * SUBLANES
    pltpu.make_async_copy(hbm.at[pl.ds(0, rows), :], hbm.at[pl.ds(0, rows), :], sem).wait()


def _dispatch_kernel(meta_ref, p1_ref, p2_ref, x_ref, xs_hbm, zbuf, sem, zsem, *, tm, tile, n_tiles):
    @pl.when(pl.program_id(0) == 0)
    def _():
        zbuf[...] = jnp.zeros_like(zbuf)
        zero_tile = lambda t: pltpu.make_async_copy(zbuf, xs_hbm.at[pl.ds(t * tile, tile), :], zsem)
        for e in range(N_EXPERTS):
            zero_tile(jnp.maximum(meta_ref[n_tiles + 1 + e] - 1, 0)).start()
        for e in range(N_EXPERTS):
            zero_tile(0).wait()

        def zero_unused(t, carry):
            zero_tile(t).start()
            zero_tile(t).wait()
            return carry

        lax.fori_loop(meta_ref[n_tiles], n_tiles, zero_unused, 0)

    _start_row_scatter(p1_ref, x_ref, xs_hbm, sem)
    _start_row_scatter(p2_ref, x_ref, xs_hbm, sem)
    for _ in range(2):
        _wait_rows(xs_hbm, x_ref, sem)


def _dispatch(meta, pos1, pos2, x, tm, tile, n_tiles):
    m = x.shape[0]
    idx = lambda: pl.BlockSpec((1, 1, tm), lambda i, meta: (i, 0, 0), memory_space=pltpu.SMEM)
    return pl.pallas_call(
        functools.partial(_dispatch_kernel, tm=tm, tile=tile, n_tiles=n_tiles),
        grid_spec=pltpu.PrefetchScalarGridSpec(
            num_scalar_prefetch=1, grid=(m // tm,),
            in_specs=[idx(), idx(),
                      pl.BlockSpec((tm // SUBLANES, SUBLANES, D_MODEL), lambda i, meta: (i, 0, 0))],
            out_specs=pl.BlockSpec(memory_space=pl.ANY),
            scratch_shapes=[pltpu.VMEM((tile, D_MODEL), F32),
                            pltpu.SemaphoreType.DMA(()), pltpu.SemaphoreType.DMA(())]),
        out_shape=jax.ShapeDtypeStruct((n_tiles * tile, D_MODEL), F32),
        compiler_params=_params(1), name="moe_dispatch",
    )(meta, pos1.reshape(m // tm, 1, tm), pos2.reshape(m // tm, 1, tm),
      x.reshape(m // SUBLANES, SUBLANES, D_MODEL))


def _expert_kernel(meta_ref, xs_ref, wg_ref, wu_ref, wd_ref, o_ref, xb_ref, *, n_tiles):
    f = pl.program_id(1)
    used = pl.program_id(0) < meta_ref[n_tiles]

    @pl.when(jnp.logical_and(jnp.logical_not(used), f == 0))
    def _():
        o_ref[...] = jnp.zeros_like(o_ref)

    @pl.when(jnp.logical_and(used, f == 0))
    def _():
        xb_ref[...] = xs_ref[...].astype(BF16)

    @pl.when(used)
    def _():
        part = _swiglu_chunks(xb_ref[...], wg_ref, wu_ref, wd_ref, (0,), wd_ref.shape[1], FF_CHUNK)

        @pl.when(f == 0)
        def _():
            o_ref[...] = part

        @pl.when(f > 0)
        def _():
            o_ref[...] += part


def _experts(meta, xs, w_gu, w_down, tile, n_tiles, tf):
    d_ff = w_down.shape[1]
    nf = d_ff // tf
    fcol = lambda j, f, meta: jnp.where(j < meta[n_tiles], f, nf - 1)
    return pl.pallas_call(
        functools.partial(_expert_kernel, n_tiles=n_tiles),
        grid_spec=pltpu.PrefetchScalarGridSpec(
            num_scalar_prefetch=1, grid=(n_tiles, nf),
            in_specs=[pl.BlockSpec((tile, D_MODEL), lambda j, f, meta: (j, 0)),
                      pl.BlockSpec((1, D_MODEL, tf), lambda j, f, meta: (meta[j], 0, fcol(j, f, meta))),
                      pl.BlockSpec((1, D_MODEL, tf), lambda j, f, meta: (meta[j], 0, nf + fcol(j, f, meta))),
                      pl.BlockSpec((1, tf, D_MODEL), lambda j, f, meta: (meta[j], fcol(j, f, meta), 0))],
            out_specs=pl.BlockSpec((tile, D_MODEL), lambda j, f, meta: (j, 0)),
            scratch_shapes=[pltpu.VMEM((tile, D_MODEL), BF16)]),
        out_shape=jax.ShapeDtypeStruct((n_tiles * tile, D_MODEL), F32),
        compiler_params=_params(2), name="moe_experts",
    )(meta, xs, w_gu, w_gu, w_down)


def _combine_kernel(p1_ref, p2_ref, n1_ref, n2_ref, rows_hbm, x_ref, r_ref, g_ref, b_ref, op_ref, os_ref,
                    buf1, buf2, sem1, sem2, *, tm, prompt_tiles):
    i = pl.program_id(0)
    slot = i % 2

    def start(pa_ref, pb_ref, s):
        _start_row_gather(pa_ref, rows_hbm, buf1.at[s], sem1.at[s])
        _start_row_gather(pb_ref, rows_hbm, buf2.at[s], sem2.at[s])

    @pl.when(i == 0)
    def _():
        start(p1_ref, p2_ref, 0)

    @pl.when(i + 1 < pl.num_programs(0))
    def _():
        start(n1_ref, n2_ref, 1 - slot)

    _wait_rows(rows_hbm, buf1.at[slot], sem1.at[slot])
    _wait_rows(rows_hbm, buf2.at[slot], sem2.at[slot])
    route = r_ref[...]
    y = (route[:, 0:1] * buf1[slot].reshape(tm, D_MODEL) + route[:, 1:2] * buf2[slot].reshape(tm, D_MODEL))
    res = _layer_norm(ALPHA * x_ref[...] + y, g_ref[...], b_ref[...])

    @pl.when(i < prompt_tiles)
    def _():
        op_ref[...] = res

    @pl.when(i >= prompt_tiles)
    def _():
        os_ref[...] = res


def _combine(pos1, pos2, rows, x, route, g, b, tm, n_prompt):
    m = x.shape[0]
    pt = n_prompt // tm
    nt = m // tm
    idx = lambda: pl.BlockSpec((1, 1, tm), lambda i: (i, 0, 0), memory_space=pltpu.SMEM)
    nxt = lambda: pl.BlockSpec((1, 1, tm), lambda i: (jnp.minimum(i + 1, nt - 1), 0, 0), memory_space=pltpu.SMEM)
    vec = pl.BlockSpec((1, D_MODEL), lambda i: (0, 0))
    p1 = pos1.reshape(nt, 1, tm)
    p2 = pos2.reshape(nt, 1, tm)
    return pl.pallas_call(
        functools.partial(_combine_kernel, tm=tm, prompt_tiles=pt), grid=(nt,),
        in_specs=[idx(), idx(), nxt(), nxt(), pl.BlockSpec(memory_space=pl.ANY),
                  pl.BlockSpec((tm, D_MODEL), lambda i: (i, 0)),
                  pl.BlockSpec((tm, LANES), lambda i: (i, 0)), vec, vec],
        out_specs=[pl.BlockSpec((tm, D_MODEL), lambda i: (jnp.minimum(i, pt - 1), 0)),
                   pl.BlockSpec((tm, D_MODEL), lambda i: (jnp.maximum(i - pt, 0), 0))],
        out_shape=[jax.ShapeDtypeStruct((n_prompt, D_MODEL), F32),
                   jax.ShapeDtypeStruct((m - n_prompt, D_MODEL), F32)],
        scratch_shapes=[pltpu.VMEM((2, tm // SUBLANES, SUBLANES, D_MODEL), F32),
                        pltpu.VMEM((2, tm // SUBLANES, SUBLANES, D_MODEL), F32),
                        pltpu.SemaphoreType.DMA((2,)), pltpu.SemaphoreType.DMA((2,))],
        compiler_params=_params(1), name="moe_combine",
    )(p1, p2, p1, p2, rows, x, route, g, b)


def _moe(x, route, n_prompt, w_gu, w_down, g, b, tm, tile, tf):
    e1 = route[:, 2].astype(jnp.int32)
    e2 = route[:, 3].astype(jnp.int32)
    meta, pos1, pos2, n_tiles = _route_plan(e1, e2, tile)
    xs = _dispatch(meta, pos1, pos2, x, tm, tile, n_tiles)
    rows = _experts(meta, xs, w_gu, w_down, tile, n_tiles, tf)
    return _combine(pos1, pos2, rows, x, route, g, b, tm, n_prompt)


def kernel(x_prompt, x_sample, cache_kv_w128, cache_kv_w512, cache_kv_w2048, state_ret,
           ln_g, ln_b, rel_bias, w_in_dil, w_out_dil, w_in_ret, ret_gn_g, w_out_ret,
           w_gu_dense, w_down_dense, w_router, w_gu_moe, w_down_moe):
    bsz, seq, _ = x_prompt.shape
    dbsz, t_new, _ = x_sample.shape
    tp = SAMPLE_PAD
    n_p = bsz * seq
    n_s = dbsz * tp
    caches = (cache_kv_w128, cache_kv_w512, cache_kv_w2048)

    w_in_dil_b = w_in_dil.astype(BF16)
    w_out_dil_b = w_out_dil.astype(BF16)
    w_in_ret_b = w_in_ret.astype(BF16)
    w_out_ret_b = w_out_ret.astype(BF16)
    w_gu_dense_b = w_gu_dense.astype(BF16)
    w_down_dense_b = w_down_dense.astype(BF16)
    w_gu_moe_b = w_gu_moe.astype(BF16)
    w_down_moe_b = w_down_moe.astype(BF16)
    lng = ln_g.reshape(DEPTH, 2, 1, D_MODEL)
    lnb = ln_b.reshape(DEPTH, 2, 1, D_MODEL)
    gn = ret_gn_g.reshape(1, RET_V)

    hp = x_prompt.reshape(n_p, D_MODEL)
    hs = jnp.pad(x_sample, ((0, 0), (0, tp - t_new), (0, 0))).reshape(n_s, D_MODEL)

    tm = min(1024, seq)
    qkv_groups, (kv128_p, kv512_p, kv2048_p) = _proj_dil_prompt(hp, w_in_dil_b, bsz, seq, tm)
    outs, lses = [], []
    for g, (window, dil) in enumerate(DIL_GROUPS):
        qkv_g = qkv_groups[g].reshape(bsz * dil, seq // dil, G_COLS)
        tbl = _prompt_table(rel_bias, g, window, dil)
        o, l = _attn_prompt(qkv_g, tbl, min(512, seq // dil))
        outs.append(o.reshape(bsz, dil, seq // dil, A_WIDTH))
        lses.append(l.reshape(bsz, dil, seq // dil, A_WIDTH))
    hp = _merge_out(outs, lses, w_out_dil_b, hp, lng[0, 0], lnb[0, 0], 512)

    qkv_s = _matmul(hs, w_in_dil_b, G_COLS)
    qkv_s3 = qkv_s.reshape(dbsz, tp, N_GROUPS * G_COLS)
    tabs = [_sample_tables(rel_bias, g, window, dil, caches[g].shape[1], t_new)
            for g, (window, dil) in enumerate(DIL_GROUPS)]
    q_t = qkv_s.reshape(dbsz, tp, N_GROUPS, 3, H_SLOT, HEAD_DIM).transpose(0, 2, 3, 4, 5, 1)
    caches_t = [c.transpose(0, 2, 3, 4, 1) for c in caches]
    mixed_t = _attn_sample(q_t, caches_t, [t[0] for t in tabs], [t[1] for t in tabs], t_new)
    mixed_s = mixed_t.transpose(0, 3, 1, 2)
    hs = _mm_ln(mixed_s.reshape(n_s, A_WIDTH), w_out_dil_b, hs, lng[0, 0], lnb[0, 0], n_s)
    rows_s = []
    for g in range(N_GROUPS):
        kv = qkv_s3[:, :t_new, g * G_COLS + A_WIDTH:(g + 1) * G_COLS]
        rows_s.append(kv.reshape(dbsz, t_new, 2, H_SLOT, HEAD_DIM))

    hp = _ffn_dense(hp, w_gu_dense_b, w_down_dense_b, lng[0, 1], lnb[0, 1], 512)
    hs = _ffn_dense(hs, w_gu_dense_b, w_down_dense_b, lng[0, 1], lnb[0, 1], n_s)

    cos_p, sin_p = _rope_tables(jnp.arange(seq, dtype=jnp.int32))
    pos_s = jnp.tile(PAST_LEN + jnp.arange(tp, dtype=jnp.int32), dbsz)
    cos_s, sin_s = _rope_tables(pos_s)
    proj_p = _proj_ret(hp, w_in_ret_b, cos_p, sin_p, tm, BF16)
    y_p, ret_p = _ret_prompt(proj_p, gn, bsz, seq, min(1024, seq))
    proj_s = _proj_ret(hs, w_in_ret_b, cos_s, sin_s, n_s, F32)
    y_s, ret_s = _ret_sample(proj_s.reshape(dbsz, tp, -1), state_ret, gn, t_new)
    h_all, route = _mm_ln_pair(y_p, y_s.reshape(n_s, RET_V), w_out_ret_b, hp, hs, lng[1, 0], lnb[1, 0],
                               w_router, 512)

    out_p, out_s = _moe(h_all, route, n_p, w_gu_moe_b, w_down_moe_b, lng[1, 1], lnb[1, 1], 512, 1024,
                        w_down_moe.shape[1] // 2)

    y_prompt = out_p.reshape(bsz, seq, D_MODEL)
    y_sample = out_s.reshape(dbsz, tp, D_MODEL)[:, :t_new]
    shape5 = lambda a: a.reshape(a.shape[0], a.shape[1], 2, H_SLOT, HEAD_DIM)
    return (y_prompt, y_sample, shape5(kv128_p), shape5(kv512_p), shape5(kv2048_p), ret_p,
            rows_s[0], rows_s[1], rows_s[2], ret_s)
```

```python
import functools

import jax
import jax.numpy as jnp
import numpy as np
from jax import lax
from jax.experimental import pallas as pl
from jax.experimental.pallas import tpu as pltpu

F32 = jnp.float32
BF16 = jnp.bfloat16

DEPTH = 2
D_MODEL = 1024
PAST_LEN = 16384
DIL_GROUPS = ((128, 1), (512, 4), (2048, 16))
N_GROUPS = 3
H_SLOT = 8
HEAD_DIM = 64
A_WIDTH = H_SLOT * HEAD_DIM
G_COLS = 3 * A_WIDTH
N_BUCKETS = 32
MAX_DISTANCE = 2048
RET_HEADS = 4
RET_DK = 256
RET_DV = 512
RET_CHUNK = 128
RET_HPS = 4
ROPE_BASE = 10000.0
RET_QK = RET_HEADS * RET_DK
RET_V = RET_HEADS * RET_DV
N_EXPERTS = 8
LN_EPS = 1e-5
GN_EPS = 1e-5
ALPHA = (2 * DEPTH) ** 0.25
NEG = -1e30

LANES = 128
ATT_BLOCK = 128
PERM = 256
FF_CHUNK = 512
SAMPLE_PAD = 16
VMEM_LIMIT = 56 * 1024 * 1024

_NT = (((1,), (1,)), ((), ()))
_TN = (((0,), (0,)), ((), ()))


def _params(n_grid):
    return pltpu.CompilerParams(dimension_semantics=("arbitrary",) * n_grid,
                                vmem_limit_bytes=VMEM_LIMIT)


def _layer_norm(z, g, b):
    mu = jnp.mean(z, axis=-1, keepdims=True)
    zc = z - mu
    var = jnp.mean(zc * zc, axis=-1, keepdims=True)
    return zc * lax.rsqrt(var + LN_EPS) * g + b


def _silu(a):
    return a / (1.0 + jnp.exp(-a))


def _t5_bucket(dist):
    max_exact = N_BUCKETS // 2
    d = np.asarray(dist, dtype=np.int64)
    scaled = np.log(np.maximum(d, 1) / max_exact) / np.log(MAX_DISTANCE / max_exact)
    large = np.minimum(max_exact + (scaled * (N_BUCKETS - max_exact)).astype(np.int32), N_BUCKETS - 1)
    return np.where(d < max_exact, d, large).astype(np.int32)


def _group_bias(rel_bias, g, window, dil):
    n_keys = window // dil + 1
    buckets = jnp.asarray(_t5_bucket(np.arange(n_keys) * dil))
    return rel_bias[buckets][:, g * H_SLOT:(g + 1) * H_SLOT].T.astype(F32)


def _matmul_kernel(x_ref, w_ref, o_ref):
    o_ref[...] = jnp.dot(x_ref[...].astype(BF16), w_ref[...], preferred_element_type=F32)


def _matmul(x, w, tn):
    m, k = x.shape
    n = w.shape[1]
    return pl.pallas_call(
        _matmul_kernel,
        grid=(n // tn,),
        in_specs=[pl.BlockSpec((m, k), lambda j: (0, 0)),
                  pl.BlockSpec((k, tn), lambda j: (0, j))],
        out_specs=pl.BlockSpec((m, tn), lambda j: (0, j)),
        out_shape=jax.ShapeDtypeStruct((m, n), F32),
        compiler_params=_params(1),
        name="matmul_sample",
    )(x, w)


def _deinterleave_matrix(dil):
    p = np.zeros((PERM, PERM), np.float32)
    rows = PERM // dil
    for r in range(dil):
        for m in range(rows):
            p[r * rows + m, m * dil + r] = 1.0
    return p


def _proj_dil_kernel(x_ref, w_ref, p1_ref, p2_ref, q0_ref, q1_ref, q2_ref, kv0_ref, kv1_ref, kv2_ref, xperm, xb_ref,
                     *, tm, tpb, keeps):
    g = pl.program_id(1)
    j = pl.program_id(0) % tpb

    @pl.when(g == 0)
    def _():
        xb_ref[...] = x_ref[...].astype(BF16)
    q_refs = (q0_ref, q1_ref, q2_ref)
    kv_refs = (kv0_ref, kv1_ref, kv2_ref)
    perms = (None, p1_ref, p2_ref)

    def store_q(ref, r, row0, n_rows, acc, src0):
        ref[0, r, row0:row0 + n_rows, :A_WIDTH] = (acc[src0:src0 + n_rows, :A_WIDTH] * (HEAD_DIM ** -0.5)).astype(BF16)
        ref[0, r, row0:row0 + n_rows, A_WIDTH:] = acc[src0:src0 + n_rows, A_WIDTH:].astype(BF16)

    for gi, (_, dil) in enumerate(DIL_GROUPS):
        keep = keeps[gi]
        in_tail = (j >= tpb - keep // tm) if keep >= tm else (j == tpb - 1)
        row0 = 0 if keep >= tm else tm - keep

        @pl.when(g == gi)
        def _(gi=gi, dil=dil, in_tail=in_tail, row0=row0):
            xb = xb_ref[...]
            if dil == 1:
                acc = jnp.dot(xb, w_ref[...], preferred_element_type=F32)
                store_q(q_refs[gi], 0, 0, tm, acc, 0)
            else:
                for sub in range(tm // PERM):
                    blk = xb[sub * PERM:(sub + 1) * PERM]
                    xperm[sub * PERM:(sub + 1) * PERM, :] = jnp.dot(
                        perms[gi][...], blk, preferred_element_type=F32).astype(BF16)
                acc = jnp.dot(xperm[...], w_ref[...], preferred_element_type=F32)
                rows = PERM // dil
                for sub in range(tm // PERM):
                    for r in range(dil):
                        store_q(q_refs[gi], r, sub * rows, rows, acc, sub * PERM + r * rows)

            @pl.when(in_tail)
            def _():
                if dil == 1:
                    kv_refs[gi][0] = acc[row0:, A_WIDTH:]
                else:
                    kv_refs[gi][0] = jnp.dot(xb[row0:], w_ref[:, A_WIDTH:], preferred_element_type=F32)


def _proj_dil_prompt(x, w, bsz, seq, tm):
    tpb = seq // tm
    keeps = tuple(min(wd, seq) for wd, _ in DIL_GROUPS)

    def kv_spec(keep):
        if keep >= tm:
            first = tpb - keep // tm
            return pl.BlockSpec((1, tm, 2 * A_WIDTH),
                                lambda i, g: (i // tpb, jnp.maximum(i % tpb - first, 0), 0))
        return pl.BlockSpec((1, keep, 2 * A_WIDTH), lambda i, g: (i // tpb, 0, 0))

    perm = lambda d: jnp.asarray(_deinterleave_matrix(d), BF16)
    const = pl.BlockSpec((PERM, PERM), lambda i, g: (0, 0))
    outs = pl.pallas_call(
        functools.partial(_proj_dil_kernel, tm=tm, tpb=tpb, keeps=keeps),
        grid=(bsz * tpb, N_GROUPS),
        in_specs=[pl.BlockSpec((tm, D_MODEL), lambda i, g: (i, 0)),
                  pl.BlockSpec((D_MODEL, G_COLS), lambda i, g: (0, g)), const, const],
        out_specs=[pl.BlockSpec((1, d, tm // d, G_COLS), lambda i, g: (i // tpb, 0, i % tpb, 0))
                   for _, d in DIL_GROUPS] + [kv_spec(k) for k in keeps],
        out_shape=[jax.ShapeDtypeStruct((bsz, d, seq // d, G_COLS), BF16) for _, d in DIL_GROUPS]
        + [jax.ShapeDtypeStruct((bsz, k, 2 * A_WIDTH), F32) for k in keeps],
        scratch_shapes=[pltpu.VMEM((tm, D_MODEL), BF16), pltpu.VMEM((tm, D_MODEL), BF16)],
        compiler_params=_params(2),
        name="proj_dil_prompt",
    )(x, w, perm(DIL_GROUPS[1][1]), perm(DIL_GROUPS[2][1]))
    return outs[:3], outs[3:]


def _attn_prompt_kernel(q_ref, k_ref, v_ref, kp_ref, vp_ref, tbl_ref, o_ref, lse_ref, kbuf, vbuf, *, tq):
    n = pl.program_id(1)
    kbuf[0:ATT_BLOCK, :] = kp_ref[0]
    kbuf[ATT_BLOCK:, :] = k_ref[0]
    vbuf[0:ATT_BLOCK, :] = vp_ref[0]
    vbuf[ATT_BLOCK:, :] = v_ref[0]
    low = lax.broadcasted_iota(jnp.int32, (ATT_BLOCK, LANES), 1) < HEAD_DIM

    def body(m, carry):
        r0 = pl.multiple_of(m * ATT_BLOCK, ATT_BLOCK)
        first = jnp.where(jnp.logical_and(n == 0, m == 0), 1, 0)
        for p in range(A_WIDTH // LANES):
            cols = slice(p * LANES, (p + 1) * LANES)
            qm = q_ref[0, pl.ds(r0, ATT_BLOCK), cols]
            keys = kbuf[pl.ds(r0, 2 * ATT_BLOCK), cols]
            vals = vbuf[pl.ds(r0, 2 * ATT_BLOCK), cols]
            outs, lses = [], []
            for a in range(2):
                qa = jnp.where(low if a == 0 else jnp.logical_not(low), qm, jnp.zeros_like(qm))
                s = lax.dot_general(qa, keys, _NT, preferred_element_type=F32) + tbl_ref[first, 2 * p + a]
                mx = jnp.max(s, axis=-1, keepdims=True)
                e = jnp.exp(s - mx)
                l = jnp.sum(e, axis=-1, keepdims=True)
                o = jnp.dot(e.astype(BF16), vals, preferred_element_type=F32)
                outs.append(o / l)
                lses.append(jnp.broadcast_to(mx + jnp.log(l), (ATT_BLOCK, LANES)))
            o_ref[0, pl.ds(r0, ATT_BLOCK), cols] = jnp.where(low, outs[0], outs[1]).astype(BF16)
            lse_ref[0, pl.ds(r0, ATT_BLOCK), cols] = jnp.where(low, lses[0], lses[1])
        return carry

    lax.fori_loop(0, tq // ATT_BLOCK, body, 0)


def _attn_prompt(qkv, tbl, tq):
    nb, length, _ = qkv.shape
    sub = tq // ATT_BLOCK
    cur = lambda c: pl.BlockSpec((1, tq, A_WIDTH), lambda s, n: (s, n, c))
    prev = lambda c: pl.BlockSpec((1, ATT_BLOCK, A_WIDTH), lambda s, n: (s, jnp.maximum(n * sub - 1, 0), c))
    return pl.pallas_call(
        functools.partial(_attn_prompt_kernel, tq=tq),
        grid=(nb, length // tq),
        in_specs=[cur(0), cur(1), cur(2), prev(1), prev(2),
                  pl.BlockSpec(tbl.shape, lambda s, n: (0, 0, 0, 0))],
        out_specs=[pl.BlockSpec((1, tq, A_WIDTH), lambda s, n: (s, n, 0))] * 2,
        out_shape=[jax.ShapeDtypeStruct((nb, length, A_WIDTH), BF16),
                   jax.ShapeDtypeStruct((nb, length, A_WIDTH), F32)],
        scratch_shapes=[pltpu.VMEM((tq + ATT_BLOCK, A_WIDTH), BF16)] * 2,
        compiler_params=_params(2),
        name="attn_prompt",
    )(qkv, qkv, qkv, qkv, qkv, tbl)


def _toeplitz(u, n_rows, n_cols):
    h, n = u.shape
    assert n == n_rows + n_cols - 1
    up = jnp.pad(u, ((0, 0), (0, 1)))
    w = jnp.tile(up, (1, n_rows))[:, :n_rows * n].reshape(h, n_rows, n)
    return w[:, :, n_rows - 1:n_rows - 1 + n_cols]


def _bias_by_offset(bias, dist, valid):
    return jnp.where(valid[None], bias[:, np.clip(dist, 0, bias.shape[1] - 1)], NEG)


def _prompt_table(rel_bias, g, window, dil):
    bias = _group_bias(rel_bias, g, window, dil)
    k = np.arange(3 * ATT_BLOCK - 1)
    dist = 2 * ATT_BLOCK - 1 - k
    u = _bias_by_offset(bias, dist, (dist >= 0) & (dist <= window // dil))
    tbl = _toeplitz(u, ATT_BLOCK, 2 * ATT_BLOCK)
    c = np.arange(2 * ATT_BLOCK)[None, None, :]
    tbl_first = jnp.where(c < ATT_BLOCK, NEG, tbl)
    return jnp.stack([tbl, tbl_first], 0)


def _split3(x):
    hi = x.astype(BF16)
    r1 = x - hi.astype(F32)
    mid = r1.astype(BF16)
    lo = (r1 - mid.astype(F32)).astype(BF16)
    return hi, mid, lo


def _merge_out_kernel(o0, o1, o2, l0, l1, l2, pt1_ref, pt2_ref, w_ref, x_ref, g_ref, b_ref, out_ref, *, tm):
    pts = (None, pt1_ref, pt2_ref)

    def natural(ref, gi, exact_f32):
        dil = DIL_GROUPS[gi][1]
        if dil == 1:
            return ref[0, 0].astype(F32)
        rows = PERM // dil
        blocks = []
        for sub in range(tm // PERM):
            piece = jnp.concatenate([ref[0, r, sub * rows:(sub + 1) * rows, :] for r in range(dil)], axis=0)
            parts = _split3(piece) if exact_f32 else (piece,)
            blocks.append(sum(jnp.dot(pts[gi][...], p, preferred_element_type=F32) for p in parts))
        return jnp.concatenate(blocks, axis=0)

    os_ = [natural(r, gi, False) for gi, r in enumerate((o0, o1, o2))]
    ls = [natural(r, gi, True) for gi, r in enumerate((l0, l1, l2))]
    mx = jnp.maximum(jnp.maximum(ls[0], ls[1]), ls[2])
    es = [jnp.exp(l - mx) for l in ls]
    mixed = (es[0] * os_[0] + es[1] * os_[1] + es[2] * os_[2]) / (es[0] + es[1] + es[2])
    acc = jnp.dot(mixed.astype(BF16), w_ref[...], preferred_element_type=F32)
    out_ref[...] = _layer_norm(ALPHA * x_ref[...] + acc, g_ref[...], b_ref[...])


def _merge_out(os_, ls_, w, x, g, b, tm):
    bsz, _, seq, _ = os_[0].shape
    tpb = seq // tm
    grp = lambda d: pl.BlockSpec((1, d, tm // d, A_WIDTH), lambda i: (i // tpb, 0, i % tpb, 0))
    specs = [grp(d) for _, d in DIL_GROUPS]
    pt = lambda d: jnp.asarray(_deinterleave_matrix(d).T, BF16)
    const = pl.BlockSpec((PERM, PERM), lambda i: (0, 0))
    vec = pl.BlockSpec((1, D_MODEL), lambda i: (0, 0))
    return pl.pallas_call(
        functools.partial(_merge_out_kernel, tm=tm), grid=(bsz * tpb,),
        in_specs=specs + specs + [const, const, pl.BlockSpec((A_WIDTH, D_MODEL), lambda i: (0, 0)),
                                  pl.BlockSpec((tm, D_MODEL), lambda i: (i, 0)), vec, vec],
        out_specs=pl.BlockSpec((tm, D_MODEL), lambda i: (i, 0)),
        out_shape=jax.ShapeDtypeStruct((bsz * seq, D_MODEL), F32),
        compiler_params=_params(1), name="merge_outproj_ln",
    )(*os_, *ls_, pt(DIL_GROUPS[1][1]), pt(DIL_GROUPS[2][1]), w, x, g, b)


def _attn_sample_kernel(q_ref, c0_ref, c1_ref, c2_ref, tb0, tb1, tb2, tn0, tn1, tn2, out_ref, *scratch, t_new):
    tp = SAMPLE_PAD
    lane_t = lax.broadcasted_iota(jnp.int32, (HEAD_DIM, tp), 1)
    sets = []
    for g, (c_ref, tb, tn) in enumerate(((c0_ref, tb0, tn0), (c1_ref, tb1, tn1), (c2_ref, tb2, tn2))):
        sets.append((lambda h, c_ref=c_ref: c_ref[0, 0, h], lambda h, c_ref=c_ref: c_ref[0, 1, h], tb, g))
        sets.append((lambda h, g=g: q_ref[0, g, 1, h], lambda h, g=g: q_ref[0, g, 2, h], tn, g))
    out_ref[0] = jnp.zeros(out_ref.shape[1:], F32)

    for t in range(t_new):
        for h in range(H_SLOT):
            for (keys, _, _, g), s_ref in zip(sets, scratch):
                qcol = q_ref[0, g, 0, h][:, t:t + 1] * (HEAD_DIM ** -0.5)
                s_ref[h:h + 1, :] = jnp.sum(keys(h) * qcol, axis=0, keepdims=True)
        scores = [s_ref[...] + tab[t] for (_, _, tab, _), s_ref in zip(sets, scratch)]
        mx = functools.reduce(jnp.maximum, [jnp.max(s, axis=-1, keepdims=True) for s in scores])
        den = jnp.zeros((H_SLOT, 1), F32)
        for s, s_ref in zip(scores, scratch):
            e = jnp.exp(s - mx)
            den = den + jnp.sum(e, axis=-1, keepdims=True)
            s_ref[...] = e
        inv = 1.0 / den

        for h in range(H_SLOT):
            col = jnp.zeros((HEAD_DIM, 1), F32)
            for (_, values, _, _), s_ref in zip(sets, scratch):
                col = col + jnp.sum(values(h) * s_ref[h:h + 1, :], axis=-1, keepdims=True)
            out_ref[0, h] = jnp.where(lane_t == t, col * inv[h:h + 1, :], out_ref[0, h])


def _sample_tables(rel_bias, g, window, dil, length, t_new):
    bias = _group_bias(rel_bias, g, window, dil)
    n_keys = window // dil + 1
    tp = SAMPLE_PAD
    k = np.arange(tp + length - 1)
    dist = length + tp - 1 - k
    u = _bias_by_offset(bias, dist // dil, (dist % dil == 0) & (dist // dil < n_keys))
    tc = _toeplitz(u, tp, length)
    k2 = np.arange(2 * tp - 1)
    dist2 = tp - 1 - k2
    u2 = _bias_by_offset(bias, dist2 // dil, (dist2 >= 0) & (dist2 % dil == 0) & (dist2 // dil < n_keys))
    live_col = (np.arange(tp) < t_new)[None, None, :]
    tn = jnp.where(live_col, _toeplitz(u2, tp, tp), NEG)
    return tc.transpose(1, 0, 2)[:t_new], tn.transpose(1, 0, 2)[:t_new]


def _attn_sample(q_t, caches_t, tables_c, tables_n, t_new):
    bsz = q_t.shape[0]
    tp = SAMPLE_PAD
    const = lambda a: pl.BlockSpec(a.shape, lambda b: (0, 0, 0))
    scratch = []
    for c in caches_t:
        scratch += [pltpu.VMEM((H_SLOT, c.shape[-1]), F32), pltpu.VMEM((H_SLOT, tp), F32)]
    return pl.pallas_call(
        functools.partial(_attn_sample_kernel, t_new=t_new),
        grid=(bsz,),
        in_specs=[pl.BlockSpec((1,) + q_t.shape[1:], lambda b: (b, 0, 0, 0, 0, 0))]
        + [pl.BlockSpec((1,) + c.shape[1:], lambda b: (b, 0, 0, 0, 0)) for c in caches_t]
        + [const(t) for t in tables_c] + [const(t) for t in tables_n],
        out_specs=pl.BlockSpec((1, H_SLOT, HEAD_DIM, tp), lambda b: (b, 0, 0, 0)),
        out_shape=jax.ShapeDtypeStruct((bsz, H_SLOT, HEAD_DIM, tp), F32),
        scratch_shapes=scratch,
        compiler_params=_params(1),
        name="attn_sample",
    )(q_t, *caches_t, *tables_c, *tables_n)


def _mm_ln_kernel(y_ref, w_ref, x_ref, g_ref, b_ref, o_ref):
    acc = jnp.dot(y_ref[...].astype(BF16), w_ref[...], preferred_element_type=F32)
    o_ref[...] = _layer_norm(ALPHA * x_ref[...] + acc, g_ref[...], b_ref[...])


def _mm_ln(y, w, x, g, b, tm):
    m, k = y.shape
    vec = pl.BlockSpec((1, D_MODEL), lambda i: (0, 0))
    return pl.pallas_call(
        _mm_ln_kernel, grid=(m // tm,),
        in_specs=[pl.BlockSpec((tm, k), lambda i: (i, 0)), pl.BlockSpec((k, D_MODEL), lambda i: (0, 0)),
                  pl.BlockSpec((tm, D_MODEL), lambda i: (i, 0)), vec, vec],
        out_specs=pl.BlockSpec((tm, D_MODEL), lambda i: (i, 0)),
        out_shape=jax.ShapeDtypeStruct((m, D_MODEL), F32),
        compiler_params=_params(1), name="outproj_ln",
    )(y, w, x, g, b)


def _mm_ln_pair_kernel(yp_ref, ys_ref, w_ref, xp_ref, xs_ref, g_ref, b_ref, wr_ref, o_ref, route_ref, *, prompt_tiles):
    def run(y_ref, x_ref):
        half = o_ref.shape[0] // 2
        for r0 in (0, half):
            rows = slice(r0, r0 + half)
            acc = jnp.dot(y_ref[rows, :].astype(BF16), w_ref[...], preferred_element_type=F32)
            h = _layer_norm(ALPHA * x_ref[rows, :] + acc, g_ref[...], b_ref[...])
            o_ref[rows, :] = h
            route_ref[rows, :] = _top2_route(jnp.dot(h.astype(BF16), wr_ref[...], preferred_element_type=F32))

    pl.when(pl.program_id(0) < prompt_tiles)(lambda: run(yp_ref, xp_ref))
    pl.when(pl.program_id(0) >= prompt_tiles)(lambda: run(ys_ref, xs_ref))


def _mm_ln_pair(y_p, y_s, w, x_p, x_s, g, b, w_router, tm):
    n_p, k = y_p.shape
    n_s = y_s.shape[0]
    pt = n_p // tm
    first = lambda width: pl.BlockSpec((tm, width), lambda i: (jnp.minimum(i, pt - 1), 0))
    second = lambda width: pl.BlockSpec((tm, width), lambda i: (jnp.maximum(i - pt, 0), 0))
    vec = pl.BlockSpec((1, D_MODEL), lambda i: (0, 0))
    wr = jnp.pad(w_router, ((0, 0), (0, LANES - N_EXPERTS))).astype(BF16)
    return pl.pallas_call(
        functools.partial(_mm_ln_pair_kernel, prompt_tiles=pt), grid=((n_p + n_s) // tm,),
        in_specs=[first(k), second(k), pl.BlockSpec((k, D_MODEL), lambda i: (0, 0)),
                  first(D_MODEL), second(D_MODEL), vec, vec, pl.BlockSpec((D_MODEL, LANES), lambda i: (0, 0))],
        out_specs=[pl.BlockSpec((tm, D_MODEL), lambda i: (i, 0)), pl.BlockSpec((tm, LANES), lambda i: (i, 0))],
        out_shape=[jax.ShapeDtypeStruct((n_p + n_s, D_MODEL), F32),
                   jax.ShapeDtypeStruct((n_p + n_s, LANES), F32)],
        compiler_params=_params(1), name="outproj_ln_route",
    )(y_p, y_s, w, x_p, x_s, g, b, wr)


def _swiglu_chunks(xb, wg_ref, wu_ref, wd_ref, lead, width, chunk):
    acc = None
    for c0 in range(0, width, chunk):
        cs = slice(c0, min(c0 + chunk, width))
        a = jnp.dot(xb, wg_ref[lead + (slice(None), cs)], preferred_element_type=F32)
        u = jnp.dot(xb, wu_ref[lead + (slice(None), cs)], preferred_element_type=F32)
        part = jnp.dot((_silu(a) * u).astype(BF16), wd_ref[lead + (cs, slice(None))], preferred_element_type=F32)
        acc = part if acc is None else acc + part
    return acc


def _ffn_kernel(x_ref, wg_ref, wu_ref, wd_ref, g_ref, b_ref, o_ref, *, d_ff):
    x = x_ref[...]
    acc = _swiglu_chunks(x.astype(BF16), wg_ref, wu_ref, wd_ref, (), d_ff, FF_CHUNK)
    o_ref[...] = _layer_norm(ALPHA * x + acc, g_ref[...], b_ref[...])


def _ffn_dense(x, w_gu, w_down, g, b, tm):
    m = x.shape[0]
    d_ff = w_down.shape[0]
    vec = pl.BlockSpec((1, D_MODEL), lambda i: (0, 0))
    return pl.pallas_call(
        functools.partial(_ffn_kernel, d_ff=d_ff), grid=(m // tm,),
        in_specs=[pl.BlockSpec((tm, D_MODEL), lambda i: (i, 0)),
                  pl.BlockSpec((D_MODEL, d_ff), lambda i: (0, 0)),
                  pl.BlockSpec((D_MODEL, d_ff), lambda i: (0, 1)),
                  pl.BlockSpec((d_ff, D_MODEL), lambda i: (0, 0)), vec, vec],
        out_specs=pl.BlockSpec((tm, D_MODEL), lambda i: (i, 0)),
        out_shape=jax.ShapeDtypeStruct((m, D_MODEL), F32),
        compiler_params=_params(1), name="ffn_dense",
    )(x, w_gu, w_gu, w_down, g, b)


def _proj_ret_kernel(x_ref, w_ref, cos_ref, sin_ref, o_ref, *, scale_k):
    xb = x_ref[...].astype(BF16)
    cos = cos_ref[...]
    sin = sin_ref[...]
    half = RET_DK // 2
    for c in range(w_ref.shape[1] // RET_DK):
        cs = slice(c * RET_DK, (c + 1) * RET_DK)
        acc = jnp.dot(xb, w_ref[:, cs], preferred_element_type=F32)
        if c < 2 * RET_HEADS:
            scale = scale_k if c >= RET_HEADS else 1.0
            x1 = acc[:, :half]
            x2 = acc[:, half:]
            o_ref[:, c * RET_DK:c * RET_DK + half] = ((x1 * cos - x2 * sin) * scale).astype(o_ref.dtype)
            o_ref[:, c * RET_DK + half:(c + 1) * RET_DK] = ((x1 * sin + x2 * cos) * scale).astype(o_ref.dtype)
        else:
            o_ref[:, cs] = acc.astype(o_ref.dtype)


def _proj_ret(x, w, cos, sin, tm, out_dtype):
    m = x.shape[0]
    n_cols = w.shape[1]
    pos_tiles = cos.shape[0] // tm
    return pl.pallas_call(
        functools.partial(_proj_ret_kernel, scale_k=RET_DK ** -0.5),
        grid=(m // tm,),
        in_specs=[pl.BlockSpec((tm, D_MODEL), lambda i: (i, 0)),
                  pl.BlockSpec((D_MODEL, n_cols), lambda i: (0, 0)),
                  pl.BlockSpec((tm, RET_DK // 2), lambda i: (i % pos_tiles, 0)),
                  pl.BlockSpec((tm, RET_DK // 2), lambda i: (i % pos_tiles, 0))],
        out_specs=pl.BlockSpec((tm, n_cols), lambda i: (i, 0)),
        out_shape=jax.ShapeDtypeStruct((m, n_cols), out_dtype),
        compiler_params=_params(1), name="proj_ret",
    )(x, w, cos, sin)


def _rope_tables(pos):
    half = RET_DK // 2
    inv = 1.0 / (ROPE_BASE ** (jnp.arange(half, dtype=F32) / half))
    ang = pos.astype(F32)[:, None] * inv[None]
    return jnp.cos(ang), jnp.sin(ang)


def _log_gamma():
    return jnp.log(1.0 - 2.0 ** (-5.0 - jnp.arange(RET_HEADS, dtype=F32)))


def _decay_tables(c, rows):
    lg = _log_gamma()
    n = jnp.arange(rows, dtype=F32)
    live = n < c
    diff = n[:, None] - n[None, :]
    decay = jnp.where((diff >= 0)[None] & live[None, None, :],
                      jnp.exp(jnp.maximum(diff, 0.0)[None] * lg[:, None, None]), 0.0)
    q_decay = jnp.exp((n[None, :] + 1.0) * lg[:, None])
    k_decay = jnp.where(live[None], jnp.exp((c - 1.0 - n)[None, :] * lg[:, None]), 0.0)
    chunk_decay = jnp.exp(c * lg)
    return decay, q_decay, k_decay, chunk_decay


def _group_norm_gate(o, gate, gn):
    mu = jnp.mean(o, axis=-1, keepdims=True)
    oc = o - mu
    var = jnp.mean(oc * oc, axis=-1, keepdims=True)
    return _silu(gate) * (oc * lax.rsqrt(var + GN_EPS) * gn)


def _ret_prompt_kernel(q_ref, k_ref, v_ref, gate_ref, dec_ref, qd_ref, kd_ref, cd_ref, gn_ref,
                       y_ref, st_ref, s_ref, *, tb):
    cb = pl.program_id(2)

    @pl.when(cb == 0)
    def _():
        s_ref[...] = jnp.zeros_like(s_ref)

    def body(ci, carry):
        r0 = pl.multiple_of(ci * RET_CHUNK, RET_CHUNK)
        rows = pl.ds(r0, RET_CHUNK)
        for j in range(RET_HPS):
            kc = slice(j * RET_DK, (j + 1) * RET_DK)
            vc = slice(j * RET_DV, (j + 1) * RET_DV)
            q = q_ref[0, rows, kc]
            k = k_ref[0, rows, kc]
            v = v_ref[0, rows, vc]
            state = s_ref[j]
            scores = lax.dot_general(q, k, _NT, preferred_element_type=F32) * dec_ref[j]
            inner = jnp.dot(scores.astype(BF16), v, preferred_element_type=F32)
            cross = jnp.dot(q, state.astype(BF16), preferred_element_type=F32) * qd_ref[j]
            kd = (k.astype(F32) * kd_ref[j]).astype(BF16)
            s_ref[j] = cd_ref[j, 0:1, :] * state + lax.dot_general(kd, v, _TN, preferred_element_type=F32)
            y = _group_norm_gate(inner + cross, gate_ref[0, rows, vc].astype(F32), gn_ref[:, vc])
            y_ref[0, rows, vc] = y.astype(y_ref.dtype)
        return carry

    lax.fori_loop(0, tb // RET_CHUNK, body, 0)

    @pl.when(cb == pl.num_programs(2) - 1)
    def _():
        st_ref[0] = s_ref[...]


def _ret_prompt(proj, gn, bsz, seq, tb):
    decay, q_decay, k_decay, chunk_decay = _decay_tables(RET_CHUNK, RET_CHUNK)
    qd = jnp.broadcast_to(q_decay[:, :, None], (RET_HEADS, RET_CHUNK, RET_DV))
    kd = jnp.broadcast_to(k_decay[:, :, None], (RET_HEADS, RET_CHUNK, RET_DK))
    cd = jnp.broadcast_to(chunk_decay[:, None, None], (RET_HEADS, 8, RET_DV))
    proj = proj.reshape(bsz, seq, proj.shape[-1])
    hps = RET_HPS
    k_off = RET_QK // (hps * RET_DK)
    v_off = 2 * RET_QK // (hps * RET_DV)
    g_off = v_off + RET_HEADS // hps
    per_head = lambda shape: pl.BlockSpec((hps,) + shape, lambda b, h, c: (h, 0, 0))
    y, state = pl.pallas_call(
        functools.partial(_ret_prompt_kernel, tb=tb),
        grid=(bsz, RET_HEADS // hps, seq // tb),
        in_specs=[pl.BlockSpec((1, tb, hps * RET_DK), lambda b, h, c: (b, c, h)),
                  pl.BlockSpec((1, tb, hps * RET_DK), lambda b, h, c: (b, c, k_off + h)),
                  pl.BlockSpec((1, tb, hps * RET_DV), lambda b, h, c: (b, c, v_off + h)),
                  pl.BlockSpec((1, tb, hps * RET_DV), lambda b, h, c: (b, c, g_off + h)),
                  per_head((RET_CHUNK, RET_CHUNK)), per_head((RET_CHUNK, RET_DV)),
                  per_head((RET_CHUNK, RET_DK)), per_head((8, RET_DV)),
                  pl.BlockSpec((1, hps * RET_DV), lambda b, h, c: (0, h))],
        out_specs=[pl.BlockSpec((1, tb, hps * RET_DV), lambda b, h, c: (b, c, h)),
                   pl.BlockSpec((1, hps, RET_DK, RET_DV), lambda b, h, c: (b, h, 0, 0))],
        out_shape=[jax.ShapeDtypeStruct((bsz, seq, RET_V), BF16),
                   jax.ShapeDtypeStruct((bsz, RET_HEADS, RET_DK, RET_DV), F32)],
        scratch_shapes=[pltpu.VMEM((hps, RET_DK, RET_DV), F32)],
        compiler_params=_params(3), name="retention_prompt",
    )(proj, proj, proj, proj, decay, qd, kd, cd, gn)
    return y.reshape(bsz * seq, RET_V), state


def _ret_sample_kernel(p_ref, st_ref, dec_ref, qd_ref, kd_ref, cd_ref, gn_ref, y_ref, ns_ref):
    tp = SAMPLE_PAD
    proj = p_ref[0]
    for h in range(RET_HEADS):
        q = proj[:, h * RET_DK:(h + 1) * RET_DK].astype(BF16)
        k = proj[:, RET_QK + h * RET_DK:RET_QK + (h + 1) * RET_DK]
        v = proj[:, 2 * RET_QK + h * RET_DV:2 * RET_QK + (h + 1) * RET_DV]
        gate = proj[:, 2 * RET_QK + RET_V + h * RET_DV:2 * RET_QK + RET_V + (h + 1) * RET_DV]
        zk = jnp.zeros((LANES - tp, RET_DK), F32)
        zv = jnp.zeros((LANES - tp, RET_DV), F32)
        kp = jnp.concatenate([k, zk], axis=0).astype(BF16)
        kdp = jnp.concatenate([k * kd_ref[h], zk], axis=0).astype(BF16)
        vp = jnp.concatenate([v, zv], axis=0).astype(BF16)
        state = st_ref[0, h]
        scores = lax.dot_general(q, kp, _NT, preferred_element_type=F32) * dec_ref[h]
        inner = jnp.dot(scores.astype(BF16), vp, preferred_element_type=F32)
        cross = jnp.dot(q, state.astype(BF16), preferred_element_type=F32) * qd_ref[h]
        ns_ref[0, h] = cd_ref[h, 0:1, :] * state + lax.dot_general(kdp, vp, _TN, preferred_element_type=F32)
        y = _group_norm_gate(inner + cross, gate, gn_ref[:, h * RET_DV:(h + 1) * RET_DV])
        y_ref[0, :, h * RET_DV:(h + 1) * RET_DV] = y


def _ret_sample(proj, state, gn, t_new):
    bsz = proj.shape[0]
    tp = SAMPLE_PAD
    decay, q_decay, k_decay, chunk_decay = _decay_tables(t_new, tp)
    dec = jnp.pad(decay, ((0, 0), (0, 0), (0, LANES - tp)))
    qd = jnp.broadcast_to(q_decay[:, :, None], (RET_HEADS, tp, RET_DV))
    kd = jnp.broadcast_to(k_decay[:, :, None], (RET_HEADS, tp, RET_DK))
    cd = jnp.broadcast_to(chunk_decay[:, None, None], (RET_HEADS, 8, RET_DV))
    const = lambda a: pl.BlockSpec(a.shape, lambda b: (0,) * a.ndim)
    return pl.pallas_call(
        _ret_sample_kernel, grid=(bsz,),
        in_specs=[pl.BlockSpec((1,) + proj.shape[1:], lambda b: (b, 0, 0)),
                  pl.BlockSpec((1,) + state.shape[1:], lambda b: (b, 0, 0, 0)),
                  const(dec), const(qd), const(kd), const(cd), const(gn)],
        out_specs=[pl.BlockSpec((1, tp, RET_V), lambda b: (b, 0, 0)),
                   pl.BlockSpec((1,) + state.shape[1:], lambda b: (b, 0, 0, 0))],
        out_shape=[jax.ShapeDtypeStruct((bsz, tp, RET_V), F32),
                   jax.ShapeDtypeStruct(state.shape, F32)],
        compiler_params=_params(1), name="retention_sample",
    )(proj, state, dec, qd, kd, cd, gn)


def _top2_route(logits):
    lane = lax.broadcasted_iota(jnp.int32, logits.shape, 1)
    logits = jnp.where(lane < N_EXPERTS, logits, NEG)
    m1 = jnp.max(logits, axis=-1, keepdims=True)
    i1 = jnp.min(jnp.where(logits == m1, lane, LANES), axis=-1, keepdims=True)
    rest = jnp.where(lane == i1, NEG, logits)
    m2 = jnp.max(rest, axis=-1, keepdims=True)
    i2 = jnp.min(jnp.where(rest == m2, lane, LANES), axis=-1, keepdims=True)
    e2 = jnp.exp(m2 - m1)
    den = 1.0 + e2
    return jnp.where(lane == 0, 1.0 / den,
                     jnp.where(lane == 1, e2 / den,
                               jnp.where(lane == 2, i1.astype(F32),
                                         jnp.where(lane == 3, i2.astype(F32), 0.0))))


def _route_plan(e1, e2, tile):
    n_tok = e1.shape[0]
    e = jnp.concatenate([e1, e2])
    onehot = (e[:, None] == jnp.arange(N_EXPERTS, dtype=jnp.int32)[None]).astype(jnp.int32)
    csum = jnp.cumsum(onehot, axis=0)
    rank = jnp.take_along_axis(csum, e[:, None], axis=1)[:, 0] - 1
    tiles_per_expert = (csum[-1] + tile - 1) // tile
    tile_end = jnp.cumsum(tiles_per_expert)
    pos = ((tile_end - tiles_per_expert) * tile)[e] + rank
    n_tiles = (2 * n_tok) // tile + N_EXPERTS
    n_used = tile_end[-1]
    tile_ids = jnp.arange(n_tiles, dtype=jnp.int32)
    tile_expert = jnp.minimum(jnp.searchsorted(tile_end, tile_ids, side="right"), N_EXPERTS - 1)
    tile_expert = jnp.where(tile_ids < n_used, tile_expert, tile_expert[n_used - 1])
    meta = jnp.concatenate([tile_expert, n_used[None], tile_end]).astype(jnp.int32)
    return meta, pos[:n_tok], pos[n_tok:], n_tiles


SUBLANES = 8


def _start_row_gather(idx_ref, src_hbm, dst3, sem):
    def issue(blk, carry):
        for s in range(SUBLANES):
            pltpu.make_async_copy(src_hbm.at[pl.ds(idx_ref[0, 0, blk * SUBLANES + s], 1), :],
                                  dst3.at[blk, pl.ds(s, 1), :], sem).start()
        return carry

    lax.fori_loop(0, dst3.shape[0], issue, 0)


def _start_row_scatter(idx_ref, src3, dst_hbm, sem):
    def issue(blk, carry):
        for s in range(SUBLANES):
            pltpu.make_async_copy(src3.at[blk, pl.ds(s, 1), :],
                                  dst_hbm.at[pl.ds(idx_ref[0, 0, blk * SUBLANES + s], 1), :], sem).start()
        return carry

    lax.fori_loop(0, src3.shape[0], issue, 0)


def _wait_rows(hbm, vmem3, sem):
    rows = vmem3.shape[0] * SUBLANES
    pltpu.make_async_copy(hbm.at[pl.ds(0, rows), :], hbm.at[pl.ds(0, rows), :], sem).wait()


def _dispatch_kernel(meta_ref, p1_ref, p2_ref, x_ref, xs_hbm, zbuf, sem, zsem, *, tm, tile, n_tiles):
    @pl.when(pl.program_id(0) == 0)
    def _():
        zbuf[...] = jnp.zeros_like(zbuf)
        zero_tile = lambda t: pltpu.make_async_copy(zbuf, xs_hbm.at[pl.ds(t * tile, tile), :], zsem)
        for e in range(N_EXPERTS):
            zero_tile(jnp.maximum(meta_ref[n_tiles + 1 + e] - 1, 0)).start()
        for e in range(N_EXPERTS):
            zero_tile(0).wait()

        def zero_unused(t, carry):
            zero_tile(t).start()
            zero_tile(t).wait()
            return carry

        lax.fori_loop(meta_ref[n_tiles], n_tiles, zero_unused, 0)

    _start_row_scatter(p1_ref, x_ref, xs_hbm, sem)
    _start_row_scatter(p2_ref, x_ref, xs_hbm, sem)
    for _ in range(2):
        _wait_rows(xs_hbm, x_ref, sem)


def _dispatch(meta, pos1, pos2, x, tm, tile, n_tiles):
    m = x.shape[0]
    idx = lambda: pl.BlockSpec((1, 1, tm), lambda i, meta: (i, 0, 0), memory_space=pltpu.SMEM)
    return pl.pallas_call(
        functools.partial(_dispatch_kernel, tm=tm, tile=tile, n_tiles=n_tiles),
        grid_spec=pltpu.PrefetchScalarGridSpec(
            num_scalar_prefetch=1, grid=(m // tm,),
            in_specs=[idx(), idx(),
                      pl.BlockSpec((tm // SUBLANES, SUBLANES, D_MODEL), lambda i, meta: (i, 0, 0))],
            out_specs=pl.BlockSpec(memory_space=pl.ANY),
            scratch_shapes=[pltpu.VMEM((tile, D_MODEL), F32),
                            pltpu.SemaphoreType.DMA(()), pltpu.SemaphoreType.DMA(())]),
        out_shape=jax.ShapeDtypeStruct((n_tiles * tile, D_MODEL), F32),
        compiler_params=_params(1), name="moe_dispatch",
    )(meta, pos1.reshape(m // tm, 1, tm), pos2.reshape(m // tm, 1, tm),
      x.reshape(m // SUBLANES, SUBLANES, D_MODEL))


def _expert_kernel(meta_ref, xs_ref, wg_ref, wu_ref, wd_ref, o_ref, xb_ref, *, n_tiles):
    f = pl.program_id(1)
    used = pl.program_id(0) < meta_ref[n_tiles]

    @pl.when(jnp.logical_and(jnp.logical_not(used), f == 0))
    def _():
        o_ref[...] = jnp.zeros_like(o_ref)

    @pl.when(jnp.logical_and(used, f == 0))
    def _():
        xb_ref[...] = xs_ref[...].astype(BF16)

    @pl.when(used)
    def _():
        part = _swiglu_chunks(xb_ref[...], wg_ref, wu_ref, wd_ref, (0,), wd_ref.shape[1], FF_CHUNK)

        @pl.when(f == 0)
        def _():
            o_ref[...] = part

        @pl.when(f > 0)
        def _():
            o_ref[...] += part


def _experts(meta, xs, w_gu, w_down, tile, n_tiles, tf):
    d_ff = w_down.shape[1]
    nf = d_ff // tf
    fcol = lambda j, f, meta: jnp.where(j < meta[n_tiles], f, nf - 1)
    return pl.pallas_call(
        functools.partial(_expert_kernel, n_tiles=n_tiles),
        grid_spec=pltpu.PrefetchScalarGridSpec(
            num_scalar_prefetch=1, grid=(n_tiles, nf),
            in_specs=[pl.BlockSpec((tile, D_MODEL), lambda j, f, meta: (j, 0)),
                      pl.BlockSpec((1, D_MODEL, tf), lambda j, f, meta: (meta[j], 0, fcol(j, f, meta))),
                      pl.BlockSpec((1, D_MODEL, tf), lambda j, f, meta: (meta[j], 0, nf + fcol(j, f, meta))),
                      pl.BlockSpec((1, tf, D_MODEL), lambda j, f, meta: (meta[j], fcol(j, f, meta), 0))],
            out_specs=pl.BlockSpec((tile, D_MODEL), lambda j, f, meta: (j, 0)),
            scratch_shapes=[pltpu.VMEM((tile, D_MODEL), BF16)]),
        out_shape=jax.ShapeDtypeStruct((n_tiles * tile, D_MODEL), F32),
        compiler_params=_params(2), name="moe_experts",
    )(meta, xs, w_gu, w_gu, w_down)


def _combine_kernel(p1_ref, p2_ref, n1_ref, n2_ref, rows_hbm, x_ref, r_ref, g_ref, b_ref, op_ref, os_ref,
                    buf1, buf2, sem1, sem2, *, tm, prompt_tiles):
    i = pl.program_id(0)
    slot = i % 2

    def start(pa_ref, pb_ref, s):
        _start_row_gather(pa_ref, rows_hbm, buf1.at[s], sem1.at[s])
        _start_row_gather(pb_ref, rows_hbm, buf2.at[s], sem2.at[s])

    @pl.when(i == 0)
    def _():
        start(p1_ref, p2_ref, 0)

    @pl.when(i + 1 < pl.num_programs(0))
    def _():
        start(n1_ref, n2_ref, 1 - slot)

    _wait_rows(rows_hbm, buf1.at[slot], sem1.at[slot])
    _wait_rows(rows_hbm, buf2.at[slot], sem2.at[slot])
    route = r_ref[...]
    y = (route[:, 0:1] * buf1[slot].reshape(tm, D_MODEL) + route[:, 1:2] * buf2[slot].reshape(tm, D_MODEL))
    res = _layer_norm(ALPHA * x_ref[...] + y, g_ref[...], b_ref[...])

    @pl.when(i < prompt_tiles)
    def _():
        op_ref[...] = res

    @pl.when(i >= prompt_tiles)
    def _():
        os_ref[...] = res


def _combine(pos1, pos2, rows, x, route, g, b, tm, n_prompt):
    m = x.shape[0]
    pt = n_prompt // tm
    nt = m // tm
    idx = lambda: pl.BlockSpec((1, 1, tm), lambda i: (i, 0, 0), memory_space=pltpu.SMEM)
    nxt = lambda: pl.BlockSpec((1, 1, tm), lambda i: (jnp.minimum(i + 1, nt - 1), 0, 0), memory_space=pltpu.SMEM)
    vec = pl.BlockSpec((1, D_MODEL), lambda i: (0, 0))
    p1 = pos1.reshape(nt, 1, tm)
    p2 = pos2.reshape(nt, 1, tm)
    return pl.pallas_call(
        functools.partial(_combine_kernel, tm=tm, prompt_tiles=pt), grid=(nt,),
        in_specs=[idx(), idx(), nxt(), nxt(), pl.BlockSpec(memory_space=pl.ANY),
                  pl.BlockSpec((tm, D_MODEL), lambda i: (i, 0)),
                  pl.BlockSpec((tm, LANES), lambda i: (i, 0)), vec, vec],
        out_specs=[pl.BlockSpec((tm, D_MODEL), lambda i: (jnp.minimum(i, pt - 1), 0)),
                   pl.BlockSpec((tm, D_MODEL), lambda i: (jnp.maximum(i - pt, 0), 0))],
        out_shape=[jax.ShapeDtypeStruct((n_prompt, D_MODEL), F32),
                   jax.ShapeDtypeStruct((m - n_prompt, D_MODEL), F32)],
        scratch_shapes=[pltpu.VMEM((2, tm // SUBLANES, SUBLANES, D_MODEL), F32),
                        pltpu.VMEM((2, tm // SUBLANES, SUBLANES, D_MODEL), F32),
                        pltpu.SemaphoreType.DMA((2,)), pltpu.SemaphoreType.DMA((2,))],
        compiler_params=_params(1), name="moe_combine",
    )(p1, p2, p1, p2, rows, x, route, g, b)


def _moe(x, route, n_prompt, w_gu, w_down, g, b, tm, tile, tf):
    e1 = route[:, 2].astype(jnp.int32)
    e2 = route[:, 3].astype(jnp.int32)
    meta, pos1, pos2, n_tiles = _route_plan(e1, e2, tile)
    xs = _dispatch(meta, pos1, pos2, x, tm, tile, n_tiles)
    rows = _experts(meta, xs, w_gu, w_down, tile, n_tiles, tf)
    return _combine(pos1, pos2, rows, x, route, g, b, tm, n_prompt)


def kernel(x_prompt, x_sample, cache_kv_w128, cache_kv_w512, cache_kv_w2048, state_ret,
           ln_g, ln_b, rel_bias, w_in_dil, w_out_dil, w_in_ret, ret_gn_g, w_out_ret,
           w_gu_dense, w_down_dense, w_router, w_gu_moe, w_down_moe):
    bsz, seq, _ = x_prompt.shape
    dbsz, t_new, _ = x_sample.shape
    tp = SAMPLE_PAD
    n_p = bsz * seq
    n_s = dbsz * tp
    caches = (cache_kv_w128, cache_kv_w512, cache_kv_w2048)

    w_in_dil_b = w_in_dil.astype(BF16)
    w_out_dil_b = w_out_dil.astype(BF16)
    w_in_ret_b = w_in_ret.astype(BF16)
    w_out_ret_b = w_out_ret.astype(BF16)
    w_gu_dense_b = w_gu_dense.astype(BF16)
    w_down_dense_b = w_down_dense.astype(BF16)
    w_gu_moe_b = w_gu_moe.astype(BF16)
    w_down_moe_b = w_down_moe.astype(BF16)
    lng = ln_g.reshape(DEPTH, 2, 1, D_MODEL)
    lnb = ln_b.reshape(DEPTH, 2, 1, D_MODEL)
    gn = ret_gn_g.reshape(1, RET_V)

    hp = x_prompt.reshape(n_p, D_MODEL)
    hs = jnp.pad(x_sample, ((0, 0), (0, tp - t_new), (0, 0))).reshape(n_s, D_MODEL)

    tm = min(1024, seq)
    qkv_groups, (kv128_p, kv512_p, kv2048_p) = _proj_dil_prompt(hp, w_in_dil_b, bsz, seq, tm)
    outs, lses = [], []
    for g, (window, dil) in enumerate(DIL_GROUPS):
        qkv_g = qkv_groups[g].reshape(bsz * dil, seq // dil, G_COLS)
        tbl = _prompt_table(rel_bias, g, window, dil)
        o, l = _attn_prompt(qkv_g, tbl, min(512, seq // dil))
        outs.append(o.reshape(bsz, dil, seq // dil, A_WIDTH))
        lses.append(l.reshape(bsz, dil, seq // dil, A_WIDTH))
    hp = _merge_out(outs, lses, w_out_dil_b, hp, lng[0, 0], lnb[0, 0], 512)

    qkv_s = _matmul(hs, w_in_dil_b, G_COLS)
    qkv_s3 = qkv_s.reshape(dbsz, tp, N_GROUPS * G_COLS)
    tabs = [_sample_tables(rel_bias, g, window, dil, caches[g].shape[1], t_new)
            for g, (window, dil) in enumerate(DIL_GROUPS)]
    q_t = qkv_s.reshape(dbsz, tp, N_GROUPS, 3, H_SLOT, HEAD_DIM).transpose(0, 2, 3, 4, 5, 1)
    caches_t = [c.transpose(0, 2, 3, 4, 1) for c in caches]
    mixed_t = _attn_sample(q_t, caches_t, [t[0] for t in tabs], [t[1] for t in tabs], t_new)
    mixed_s = mixed_t.transpose(0, 3, 1, 2)
    hs = _mm_ln(mixed_s.reshape(n_s, A_WIDTH), w_out_dil_b, hs, lng[0, 0], lnb[0, 0], n_s)
    rows_s = []
    for g in range(N_GROUPS):
        kv = qkv_s3[:, :t_new, g * G_COLS + A_WIDTH:(g + 1) * G_COLS]
        rows_s.append(kv.reshape(dbsz, t_new, 2, H_SLOT, HEAD_DIM))

    hp = _ffn_dense(hp, w_gu_dense_b, w_down_dense_b, lng[0, 1], lnb[0, 1], 512)
    hs = _ffn_dense(hs, w_gu_dense_b, w_down_dense_b, lng[0, 1], lnb[0, 1], n_s)

    cos_p, sin_p = _rope_tables(jnp.arange(seq, dtype=jnp.int32))
    pos_s = jnp.tile(PAST_LEN + jnp.arange(tp, dtype=jnp.int32), dbsz)
    cos_s, sin_s = _rope_tables(pos_s)
    proj_p = _proj_ret(hp, w_in_ret_b, cos_p, sin_p, 512, BF16)
    y_p, ret_p = _ret_prompt(proj_p, gn, bsz, seq, min(1024, seq))
    proj_s = _proj_ret(hs, w_in_ret_b, cos_s, sin_s, n_s // 2, F32)
    y_s, ret_s = _ret_sample(proj_s.reshape(dbsz, tp, -1), state_ret, gn, t_new)
    h_all, route = _mm_ln_pair(y_p, y_s.reshape(n_s, RET_V), w_out_ret_b, hp, hs, lng[1, 0], lnb[1, 0],
                               w_router, 512)

    out_p, out_s = _moe(h_all, route, n_p, w_gu_moe_b, w_down_moe_b, lng[1, 1], lnb[1, 1], 512, 1024,
                        w_down_moe.shape[1] // 2)

    y_prompt = out_p.reshape(bsz, seq, D_MODEL)
    y_sample = out_s.reshape(dbsz, tp, D_MODEL)[:, :t_new]
    shape5 = lambda a: a.reshape(a.shape[0], a.shape[1], 2, H_SLOT, HEAD_DIM)
    return (y_prompt, y_sample, shape5(kv128_p), shape5(kv512_p), shape5(kv2048_p), ret_p,
            rows_s[0], rows_s[1], rows_s[2], ret_s)
```

```python
import functools

import jax
import jax.numpy as jnp
import numpy as np
from jax import lax
from jax.experimental import pallas as pl
from jax.experimental.pallas import tpu as pltpu

F32 = jnp.float32
BF16 = jnp.bfloat16

DEPTH = 2
D_MODEL = 1024
PAST_LEN = 16384
DIL_GROUPS = ((128, 1), (512, 4), (2048, 16))
N_GROUPS = 3
H_SLOT = 8
HEAD_DIM = 64
A_WIDTH = H_SLOT * HEAD_DIM
G_COLS = 3 * A_WIDTH
N_BUCKETS = 32
MAX_DISTANCE = 2048
RET_HEADS = 4
RET_DK = 256
RET_DV = 512
RET_CHUNK = 128
RET_HPS = 4
ROPE_BASE = 10000.0
RET_QK = RET_HEADS * RET_DK
RET_V = RET_HEADS * RET_DV
N_EXPERTS = 8
LN_EPS = 1e-5
GN_EPS = 1e-5
ALPHA = (2 * DEPTH) ** 0.25
NEG = -1e30

LANES = 128
ATT_BLOCK = 128
PERM = 256
FF_CHUNK = 512
SAMPLE_PAD = 16
VMEM_LIMIT = 56 * 1024 * 1024

_NT = (((1,), (1,)), ((), ()))
_TN = (((0,), (0,)), ((), ()))


def _params(n_grid):
    return pltpu.CompilerParams(dimension_semantics=("arbitrary",) * n_grid,
                                vmem_limit_bytes=VMEM_LIMIT)


def _layer_norm(z, g, b):
    mu = jnp.mean(z, axis=-1, keepdims=True)
    zc = z - mu
    var = jnp.mean(zc * zc, axis=-1, keepdims=True)
    return zc * lax.rsqrt(var + LN_EPS) * g + b


def _silu(a):
    return a / (1.0 + jnp.exp(-a))


def _t5_bucket(dist):
    max_exact = N_BUCKETS // 2
    d = np.asarray(dist, dtype=np.int64)
    scaled = np.log(np.maximum(d, 1) / max_exact) / np.log(MAX_DISTANCE / max_exact)
    large = np.minimum(max_exact + (scaled * (N_BUCKETS - max_exact)).astype(np.int32), N_BUCKETS - 1)
    return np.where(d < max_exact, d, large).astype(np.int32)


def _group_bias(rel_bias, g, window, dil):
    n_keys = window // dil + 1
    buckets = jnp.asarray(_t5_bucket(np.arange(n_keys) * dil))
    return rel_bias[buckets][:, g * H_SLOT:(g + 1) * H_SLOT].T.astype(F32)


def _matmul_kernel(x_ref, w_ref, o_ref):
    o_ref[...] = jnp.dot(x_ref[...].astype(BF16), w_ref[...], preferred_element_type=F32)


def _matmul(x, w, tn):
    m, k = x.shape
    n = w.shape[1]
    return pl.pallas_call(
        _matmul_kernel,
        grid=(n // tn,),
        in_specs=[pl.BlockSpec((m, k), lambda j: (0, 0)),
                  pl.BlockSpec((k, tn), lambda j: (0, j))],
        out_specs=pl.BlockSpec((m, tn), lambda j: (0, j)),
        out_shape=jax.ShapeDtypeStruct((m, n), F32),
        compiler_params=_params(1),
        name="matmul_sample",
    )(x, w)


def _deinterleave_matrix(dil):
    p = np.zeros((PERM, PERM), np.float32)
    rows = PERM // dil
    for r in range(dil):
        for m in range(rows):
            p[r * rows + m, m * dil + r] = 1.0
    return p


def _proj_dil_kernel(x_ref, w_ref, p1_ref, p2_ref, q0_ref, q1_ref, q2_ref, kv0_ref, kv1_ref, kv2_ref, xperm, xb_ref,
                     *, tm, tpb, keeps):
    g = pl.program_id(1)
    j = pl.program_id(0) % tpb

    @pl.when(g == 0)
    def _():
        xb_ref[...] = x_ref[...].astype(BF16)
    q_refs = (q0_ref, q1_ref, q2_ref)
    kv_refs = (kv0_ref, kv1_ref, kv2_ref)
    perms = (None, p1_ref, p2_ref)

    def store_q(ref, r, row0, n_rows, acc, src0):
        ref[0, r, row0:row0 + n_rows, :A_WIDTH] = (acc[src0:src0 + n_rows, :A_WIDTH] * (HEAD_DIM ** -0.5)).astype(BF16)
        ref[0, r, row0:row0 + n_rows, A_WIDTH:] = acc[src0:src0 + n_rows, A_WIDTH:].astype(BF16)

    for gi, (_, dil) in enumerate(DIL_GROUPS):
        keep = keeps[gi]
        in_tail = (j >= tpb - keep // tm) if keep >= tm else (j == tpb - 1)
        row0 = 0 if keep >= tm else tm - keep

        @pl.when(g == gi)
        def _(gi=gi, dil=dil, in_tail=in_tail, row0=row0):
            xb = xb_ref[...]
            if dil == 1:
                acc = jnp.dot(xb, w_ref[...], preferred_element_type=F32)
                store_q(q_refs[gi], 0, 0, tm, acc, 0)
            else:
                for sub in range(tm // PERM):
                    blk = xb[sub * PERM:(sub + 1) * PERM]
                    xperm[sub * PERM:(sub + 1) * PERM, :] = jnp.dot(
                        perms[gi][...], blk, preferred_element_type=F32).astype(BF16)
                acc = jnp.dot(xperm[...], w_ref[...], preferred_element_type=F32)
                rows = PERM // dil
                for sub in range(tm // PERM):
                    for r in range(dil):
                        store_q(q_refs[gi], r, sub * rows, rows, acc, sub * PERM + r * rows)

            @pl.when(in_tail)
            def _():
                if dil == 1:
                    kv_refs[gi][0] = acc[row0:, A_WIDTH:]
                else:
                    kv_refs[gi][0] = jnp.dot(xb[row0:], w_ref[:, A_WIDTH:], preferred_element_type=F32)


def _proj_dil_prompt(x, w, bsz, seq, tm):
    tpb = seq // tm
    keeps = tuple(min(wd, seq) for wd, _ in DIL_GROUPS)

    def kv_spec(keep):
        if keep >= tm:
            first = tpb - keep // tm
            return pl.BlockSpec((1, tm, 2 * A_WIDTH),
                                lambda i, g: (i // tpb, jnp.maximum(i % tpb - first, 0), 0))
        return pl.BlockSpec((1, keep, 2 * A_WIDTH), lambda i, g: (i // tpb, 0, 0))

    perm = lambda d: jnp.asarray(_deinterleave_matrix(d), BF16)
    const = pl.BlockSpec((PERM, PERM), lambda i, g: (0, 0))
    outs = pl.pallas_call(
        functools.partial(_proj_dil_kernel, tm=tm, tpb=tpb, keeps=keeps),
        grid=(bsz * tpb, N_GROUPS),
        in_specs=[pl.BlockSpec((tm, D_MODEL), lambda i, g: (i, 0)),
                  pl.BlockSpec((D_MODEL, G_COLS), lambda i, g: (0, g)), const, const],
        out_specs=[pl.BlockSpec((1, d, tm // d, G_COLS), lambda i, g: (i // tpb, 0, i % tpb, 0))
                   for _, d in DIL_GROUPS] + [kv_spec(k) for k in keeps],
        out_shape=[jax.ShapeDtypeStruct((bsz, d, seq // d, G_COLS), BF16) for _, d in DIL_GROUPS]
        + [jax.ShapeDtypeStruct((bsz, k, 2 * A_WIDTH), F32) for k in keeps],
        scratch_shapes=[pltpu.VMEM((tm, D_MODEL), BF16), pltpu.VMEM((tm, D_MODEL), BF16)],
        compiler_params=_params(2),
        name="proj_dil_prompt",
    )(x, w, perm(DIL_GROUPS[1][1]), perm(DIL_GROUPS[2][1]))
    return outs[:3], outs[3:]


def _attn_prompt_kernel(q_ref, k_ref, v_ref, kp_ref, vp_ref, tbl_ref, o_ref, lse_ref, kbuf, vbuf, *, tq):
    n = pl.program_id(1)
    kbuf[0:ATT_BLOCK, :] = kp_ref[0]
    kbuf[ATT_BLOCK:, :] = k_ref[0]
    vbuf[0:ATT_BLOCK, :] = vp_ref[0]
    vbuf[ATT_BLOCK:, :] = v_ref[0]
    low = lax.broadcasted_iota(jnp.int32, (ATT_BLOCK, LANES), 1) < HEAD_DIM

    def body(m, carry):
        r0 = pl.multiple_of(m * ATT_BLOCK, ATT_BLOCK)
        first = jnp.where(jnp.logical_and(n == 0, m == 0), 1, 0)
        for p in range(A_WIDTH // LANES):
            cols = slice(p * LANES, (p + 1) * LANES)
            qm = q_ref[0, pl.ds(r0, ATT_BLOCK), cols]
            keys = kbuf[pl.ds(r0, 2 * ATT_BLOCK), cols]
            vals = vbuf[pl.ds(r0, 2 * ATT_BLOCK), cols]
            outs, lses = [], []
            for a in range(2):
                qa = jnp.where(low if a == 0 else jnp.logical_not(low), qm, jnp.zeros_like(qm))
                s = lax.dot_general(qa, keys, _NT, preferred_element_type=F32) + tbl_ref[first, 2 * p + a]
                mx = jnp.max(s, axis=-1, keepdims=True)
                e = jnp.exp(s - mx)
                l = jnp.sum(e, axis=-1, keepdims=True)
                o = jnp.dot(e.astype(BF16), vals, preferred_element_type=F32)
                outs.append(o / l)
                lses.append(jnp.broadcast_to(mx + jnp.log(l), (ATT_BLOCK, LANES)))
            o_ref[0, pl.ds(r0, ATT_BLOCK), cols] = jnp.where(low, outs[0], outs[1]).astype(BF16)
            lse_ref[0, pl.ds(r0, ATT_BLOCK), cols] = jnp.where(low, lses[0], lses[1])
        return carry

    lax.fori_loop(0, tq // ATT_BLOCK, body, 0)


def _attn_prompt(qkv, tbl, tq):
    nb, length, _ = qkv.shape
    sub = tq // ATT_BLOCK
    cur = lambda c: pl.BlockSpec((1, tq, A_WIDTH), lambda s, n: (s, n, c))
    prev = lambda c: pl.BlockSpec((1, ATT_BLOCK, A_WIDTH), lambda s, n: (s, jnp.maximum(n * sub - 1, 0), c))
    return pl.pallas_call(
        functools.partial(_attn_prompt_kernel, tq=tq),
        grid=(nb, length // tq),
        in_specs=[cur(0), cur(1), cur(2), prev(1), prev(2),
                  pl.BlockSpec(tbl.shape, lambda s, n: (0, 0, 0, 0))],
        out_specs=[pl.BlockSpec((1, tq, A_WIDTH), lambda s, n: (s, n, 0))] * 2,
        out_shape=[jax.ShapeDtypeStruct((nb, length, A_WIDTH), BF16),
                   jax.ShapeDtypeStruct((nb, length, A_WIDTH), F32)],
        scratch_shapes=[pltpu.VMEM((tq + ATT_BLOCK, A_WIDTH), BF16)] * 2,
        compiler_params=_params(2),
        name="attn_prompt",
    )(qkv, qkv, qkv, qkv, qkv, tbl)


def _toeplitz(u, n_rows, n_cols):
    h, n = u.shape
    assert n == n_rows + n_cols - 1
    up = jnp.pad(u, ((0, 0), (0, 1)))
    w = jnp.tile(up, (1, n_rows))[:, :n_rows * n].reshape(h, n_rows, n)
    return w[:, :, n_rows - 1:n_rows - 1 + n_cols]


def _bias_by_offset(bias, dist, valid):
    return jnp.where(valid[None], bias[:, np.clip(dist, 0, bias.shape[1] - 1)], NEG)


def _prompt_table(rel_bias, g, window, dil):
    bias = _group_bias(rel_bias, g, window, dil)
    k = np.arange(3 * ATT_BLOCK - 1)
    dist = 2 * ATT_BLOCK - 1 - k
    u = _bias_by_offset(bias, dist, (dist >= 0) & (dist <= window // dil))
    tbl = _toeplitz(u, ATT_BLOCK, 2 * ATT_BLOCK)
    c = np.arange(2 * ATT_BLOCK)[None, None, :]
    tbl_first = jnp.where(c < ATT_BLOCK, NEG, tbl)
    return jnp.stack([tbl, tbl_first], 0)


def _split3(x):
    hi = x.astype(BF16)
    r1 = x - hi.astype(F32)
    mid = r1.astype(BF16)
    lo = (r1 - mid.astype(F32)).astype(BF16)
    return hi, mid, lo


def _merge_out_kernel(o0, o1, o2, l0, l1, l2, pt1_ref, pt2_ref, w_ref, x_ref, g_ref, b_ref, out_ref, *, tm):
    pts = (None, pt1_ref, pt2_ref)

    def natural(ref, gi, exact_f32):
        dil = DIL_GROUPS[gi][1]
        if dil == 1:
            return ref[0, 0].astype(F32)
        rows = PERM // dil
        blocks = []
        for sub in range(tm // PERM):
            piece = jnp.concatenate([ref[0, r, sub * rows:(sub + 1) * rows, :] for r in range(dil)], axis=0)
            parts = _split3(piece) if exact_f32 else (piece,)
            blocks.append(sum(jnp.dot(pts[gi][...], p, preferred_element_type=F32) for p in parts))
        return jnp.concatenate(blocks, axis=0)

    os_ = [natural(r, gi, False) for gi, r in enumerate((o0, o1, o2))]
    ls = [natural(r, gi, True) for gi, r in enumerate((l0, l1, l2))]
    mx = jnp.maximum(jnp.maximum(ls[0], ls[1]), ls[2])
    es = [jnp.exp(l - mx) for l in ls]
    mixed = (es[0] * os_[0] + es[1] * os_[1] + es[2] * os_[2]) / (es[0] + es[1] + es[2])
    acc = jnp.dot(mixed.astype(BF16), w_ref[...], preferred_element_type=F32)
    out_ref[...] = _layer_norm(ALPHA * x_ref[...] + acc, g_ref[...], b_ref[...])


def _merge_out(os_, ls_, w, x, g, b, tm):
    bsz, _, seq, _ = os_[0].shape
    tpb = seq // tm
    grp = lambda d: pl.BlockSpec((1, d, tm // d, A_WIDTH), lambda i: (i // tpb, 0, i % tpb, 0))
    specs = [grp(d) for _, d in DIL_GROUPS]
    pt = lambda d: jnp.asarray(_deinterleave_matrix(d).T, BF16)
    const = pl.BlockSpec((PERM, PERM), lambda i: (0, 0))
    vec = pl.BlockSpec((1, D_MODEL), lambda i: (0, 0))
    return pl.pallas_call(
        functools.partial(_merge_out_kernel, tm=tm), grid=(bsz * tpb,),
        in_specs=specs + specs + [const, const, pl.BlockSpec((A_WIDTH, D_MODEL), lambda i: (0, 0)),
                                  pl.BlockSpec((tm, D_MODEL), lambda i: (i, 0)), vec, vec],
        out_specs=pl.BlockSpec((tm, D_MODEL), lambda i: (i, 0)),
        out_shape=jax.ShapeDtypeStruct((bsz * seq, D_MODEL), F32),
        compiler_params=_params(1), name="merge_outproj_ln",
    )(*os_, *ls_, pt(DIL_GROUPS[1][1]), pt(DIL_GROUPS[2][1]), w, x, g, b)


def _attn_sample_kernel(q_ref, c0_ref, c1_ref, c2_ref, tb0, tb1, tb2, tn0, tn1, tn2, out_ref, *scratch, t_new):
    tp = SAMPLE_PAD
    lane_t = lax.broadcasted_iota(jnp.int32, (HEAD_DIM, tp), 1)
    lane_k = lax.broadcasted_iota(jnp.int32, (HEAD_DIM, LANES), 1)
    qcol = lambda g, h, t: q_ref[0, g, 0, h][:, t:t + 1] * (HEAD_DIM ** -0.5)
    sets = []
    for g, (c_ref, tb, tn) in enumerate(((c0_ref, tb0, tn0), (c1_ref, tb1, tn1), (c2_ref, tb2, tn2))):
        dil = DIL_GROUPS[g][1]
        sets.append((lambda h, c_ref=c_ref: c_ref[0, 0, h], lambda h, c_ref=c_ref: c_ref[0, 1, h], tb, g,
                     dil if dil >= t_new else 0))
        sets.append((lambda h, g=g: q_ref[0, g, 1, h], lambda h, g=g: q_ref[0, g, 2, h], tn, g, 0))
    s_refs = scratch[:len(sets)]
    p_refs = dict(zip([i for i, st in enumerate(sets) if st[4]], scratch[len(sets):]))
    out_ref[0] = jnp.zeros(out_ref.shape[1:], F32)

    for i, p_ref in p_refs.items():
        keys, _, _, g, dil = sets[i]
        p_ref[...] = jnp.zeros_like(p_ref)
        for h in range(H_SLOT):
            qsel = jnp.zeros((HEAD_DIM, LANES), F32)
            for t in range(t_new):
                qsel = jnp.where(lane_k % dil == t, qcol(g, h, t), qsel)
            kt = keys(h)
            s_refs[i][h:h + 1, :] = jnp.sum(kt * jnp.tile(qsel, (1, kt.shape[1] // LANES)), axis=0, keepdims=True)

    for t in range(t_new):
        for h in range(H_SLOT):
            for (keys, _, _, g, dil), s_ref in zip(sets, s_refs):
                if not dil:
                    s_ref[h:h + 1, :] = jnp.sum(keys(h) * qcol(g, h, t), axis=0, keepdims=True)
        scores = [s_ref[...] + st[2][t] for st, s_ref in zip(sets, s_refs)]
        mx = functools.reduce(jnp.maximum, [jnp.max(s, axis=-1, keepdims=True) for s in scores])
        es = [jnp.exp(s - mx) for s in scores]
        inv = 1.0 / functools.reduce(jnp.add, [jnp.sum(e, axis=-1, keepdims=True) for e in es])
        for i, (e, s_ref) in enumerate(zip(es, s_refs)):
            if i in p_refs:
                p_refs[i][...] += e * inv
            else:
                s_ref[...] = e
        for h in range(H_SLOT):
            by_width = {}
            for (_, values, _, _, dil), s_ref in zip(sets, s_refs):
                if not dil:
                    prod = values(h) * s_ref[h:h + 1, :]
                    by_width[prod.shape[1]] = by_width[prod.shape[1]] + prod if prod.shape[1] in by_width else prod
            col = functools.reduce(jnp.add, [jnp.sum(p, axis=-1, keepdims=True) for p in by_width.values()])
            out_ref[0, h] = jnp.where(lane_t == t, col * inv[h:h + 1, :], out_ref[0, h])

    for i, p_ref in p_refs.items():
        _, values, _, _, dil = sets[i]
        for h in range(H_SLOT):
            prod = values(h) * p_ref[h:h + 1, :]
            folded = functools.reduce(jnp.add, [prod[:, j:j + LANES] for j in range(0, prod.shape[1], LANES)])
            acc = out_ref[0, h]
            for t in range(t_new):
                col = jnp.sum(jnp.where(lane_k % dil == t, folded, 0.0), axis=-1, keepdims=True)
                acc = acc + jnp.where(lane_t == t, col, 0.0)
            out_ref[0, h] = acc


def _sample_tables(rel_bias, g, window, dil, length, t_new):
    bias = _group_bias(rel_bias, g, window, dil)
    n_keys = window // dil + 1
    tp = SAMPLE_PAD
    k = np.arange(tp + length - 1)
    dist = length + tp - 1 - k
    u = _bias_by_offset(bias, dist // dil, (dist % dil == 0) & (dist // dil < n_keys))
    tc = _toeplitz(u, tp, length)
    k2 = np.arange(2 * tp - 1)
    dist2 = tp - 1 - k2
    u2 = _bias_by_offset(bias, dist2 // dil, (dist2 >= 0) & (dist2 % dil == 0) & (dist2 // dil < n_keys))
    live_col = (np.arange(tp) < t_new)[None, None, :]
    tn = jnp.where(live_col, _toeplitz(u2, tp, tp), NEG)
    return tc.transpose(1, 0, 2)[:t_new], tn.transpose(1, 0, 2)[:t_new]


def _attn_sample(q_t, caches_t, tables_c, tables_n, t_new):
    bsz = q_t.shape[0]
    tp = SAMPLE_PAD
    const = lambda a: pl.BlockSpec(a.shape, lambda b: (0, 0, 0))
    scratch = []
    for c in caches_t:
        scratch += [pltpu.VMEM((H_SLOT, c.shape[-1]), F32), pltpu.VMEM((H_SLOT, tp), F32)]
    for c, (_, dil) in zip(caches_t, DIL_GROUPS):
        if dil >= t_new:
            assert c.shape[-1] % dil == 0 and c.shape[-1] % LANES == 0 and LANES % dil == 0
            scratch.append(pltpu.VMEM((H_SLOT, c.shape[-1]), F32))
    return pl.pallas_call(
        functools.partial(_attn_sample_kernel, t_new=t_new),
        grid=(bsz,),
        in_specs=[pl.BlockSpec((1,) + q_t.shape[1:], lambda b: (b, 0, 0, 0, 0, 0))]
        + [pl.BlockSpec((1,) + c.shape[1:], lambda b: (b, 0, 0, 0, 0)) for c in caches_t]
        + [const(t) for t in tables_c] + [const(t) for t in tables_n],
        out_specs=pl.BlockSpec((1, H_SLOT, HEAD_DIM, tp), lambda b: (b, 0, 0, 0)),
        out_shape=jax.ShapeDtypeStruct((bsz, H_SLOT, HEAD_DIM, tp), F32),
        scratch_shapes=scratch,
        compiler_params=_params(1),
        name="attn_sample",
    )(q_t, *caches_t, *tables_c, *tables_n)


def _mm_ln_kernel(y_ref, w_ref, x_ref, g_ref, b_ref, o_ref):
    acc = jnp.dot(y_ref[...].astype(BF16), w_ref[...], preferred_element_type=F32)
    o_ref[...] = _layer_norm(ALPHA * x_ref[...] + acc, g_ref[...], b_ref[...])


def _mm_ln(y, w, x, g, b, tm):
    m, k = y.shape
    vec = pl.BlockSpec((1, D_MODEL), lambda i: (0, 0))
    return pl.pallas_call(
        _mm_ln_kernel, grid=(m // tm,),
        in_specs=[pl.BlockSpec((tm, k), lambda i: (i, 0)), pl.BlockSpec((k, D_MODEL), lambda i: (0, 0)),
                  pl.BlockSpec((tm, D_MODEL), lambda i: (i, 0)), vec, vec],
        out_specs=pl.BlockSpec((tm, D_MODEL), lambda i: (i, 0)),
        out_shape=jax.ShapeDtypeStruct((m, D_MODEL), F32),
        compiler_params=_params(1), name="outproj_ln",
    )(y, w, x, g, b)


def _mm_ln_pair_kernel(yp_ref, ys_ref, w_ref, xp_ref, xs_ref, g_ref, b_ref, wr_ref, o_ref, route_ref, *, prompt_tiles):
    def run(y_ref, x_ref):
        half = o_ref.shape[0] // 2
        for r0 in (0, half):
            rows = slice(r0, r0 + half)
            acc = jnp.dot(y_ref[rows, :].astype(BF16), w_ref[...], preferred_element_type=F32)
            h = _layer_norm(ALPHA * x_ref[rows, :] + acc, g_ref[...], b_ref[...])
            o_ref[rows, :] = h
            route_ref[rows, :] = _top2_route(jnp.dot(h.astype(BF16), wr_ref[...], preferred_element_type=F32))

    pl.when(pl.program_id(0) < prompt_tiles)(lambda: run(yp_ref, xp_ref))
    pl.when(pl.program_id(0) >= prompt_tiles)(lambda: run(ys_ref, xs_ref))


def _mm_ln_pair(y_p, y_s, w, x_p, x_s, g, b, w_router, tm):
    n_p, k = y_p.shape
    n_s = y_s.shape[0]
    pt = n_p // tm
    first = lambda width: pl.BlockSpec((tm, width), lambda i: (jnp.minimum(i, pt - 1), 0))
    second = lambda width: pl.BlockSpec((tm, width), lambda i: (jnp.maximum(i - pt, 0), 0))
    vec = pl.BlockSpec((1, D_MODEL), lambda i: (0, 0))
    wr = jnp.pad(w_router, ((0, 0), (0, LANES - N_EXPERTS))).astype(BF16)
    return pl.pallas_call(
        functools.partial(_mm_ln_pair_kernel, prompt_tiles=pt), grid=((n_p + n_s) // tm,),
        in_specs=[first(k), second(k), pl.BlockSpec((k, D_MODEL), lambda i: (0, 0)),
                  first(D_MODEL), second(D_MODEL), vec, vec, pl.BlockSpec((D_MODEL, LANES), lambda i: (0, 0))],
        out_specs=[pl.BlockSpec((tm, D_MODEL), lambda i: (i, 0)), pl.BlockSpec((tm, LANES), lambda i: (i, 0))],
        out_shape=[jax.ShapeDtypeStruct((n_p + n_s, D_MODEL), F32),
                   jax.ShapeDtypeStruct((n_p + n_s, LANES), F32)],
        compiler_params=_params(1), name="outproj_ln_route",
    )(y_p, y_s, w, x_p, x_s, g, b, wr)


def _swiglu_chunks(xb, wg_ref, wu_ref, wd_ref, lead, width, chunk):
    acc = None
    for c0 in range(0, width, chunk):
        cs = slice(c0, min(c0 + chunk, width))
        a = jnp.dot(xb, wg_ref[lead + (slice(None), cs)], preferred_element_type=F32)
        u = jnp.dot(xb, wu_ref[lead + (slice(None), cs)], preferred_element_type=F32)
        part = jnp.dot((_silu(a) * u).astype(BF16), wd_ref[lead + (cs, slice(None))], preferred_element_type=F32)
        acc = part if acc is None else acc + part
    return acc


def _ffn_kernel(x_ref, wg_ref, wu_ref, wd_ref, g_ref, b_ref, o_ref, *, d_ff):
    x = x_ref[...]
    acc = _swiglu_chunks(x.astype(BF16), wg_ref, wu_ref, wd_ref, (), d_ff, FF_CHUNK)
    o_ref[...] = _layer_norm(ALPHA * x + acc, g_ref[...], b_ref[...])


def _ffn_dense(x, w_gu, w_down, g, b, tm):
    m = x.shape[0]
    d_ff = w_down.shape[0]
    vec = pl.BlockSpec((1, D_MODEL), lambda i: (0, 0))
    return pl.pallas_call(
        functools.partial(_ffn_kernel, d_ff=d_ff), grid=(m // tm,),
        in_specs=[pl.BlockSpec((tm, D_MODEL), lambda i: (i, 0)),
                  pl.BlockSpec((D_MODEL, d_ff), lambda i: (0, 0)),
                  pl.BlockSpec((D_MODEL, d_ff), lambda i: (0, 1)),
                  pl.BlockSpec((d_ff, D_MODEL), lambda i: (0, 0)), vec, vec],
        out_specs=pl.BlockSpec((tm, D_MODEL), lambda i: (i, 0)),
        out_shape=jax.ShapeDtypeStruct((m, D_MODEL), F32),
        compiler_params=_params(1), name="ffn_dense",
    )(x, w_gu, w_gu, w_down, g, b)


def _proj_ret_kernel(x_ref, w_ref, cos_ref, sin_ref, o_ref, *, scale_k):
    xb = x_ref[...].astype(BF16)
    cos = cos_ref[...]
    sin = sin_ref[...]
    half = RET_DK // 2
    for c in range(w_ref.shape[1] // RET_DK):
        cs = slice(c * RET_DK, (c + 1) * RET_DK)
        acc = jnp.dot(xb, w_ref[:, cs], preferred_element_type=F32)
        if c < 2 * RET_HEADS:
            scale = scale_k if c >= RET_HEADS else 1.0
            x1 = acc[:, :half]
            x2 = acc[:, half:]
            o_ref[:, c * RET_DK:c * RET_DK + half] = ((x1 * cos - x2 * sin) * scale).astype(o_ref.dtype)
            o_ref[:, c * RET_DK + half:(c + 1) * RET_DK] = ((x1 * sin + x2 * cos) * scale).astype(o_ref.dtype)
        else:
            o_ref[:, cs] = acc.astype(o_ref.dtype)


def _proj_ret(x, w, cos, sin, tm, out_dtype):
    m = x.shape[0]
    n_cols = w.shape[1]
    pos_tiles = cos.shape[0] // tm
    return pl.pallas_call(
        functools.partial(_proj_ret_kernel, scale_k=RET_DK ** -0.5),
        grid=(m // tm,),
        in_specs=[pl.BlockSpec((tm, D_MODEL), lambda i: (i, 0)),
                  pl.BlockSpec((D_MODEL, n_cols), lambda i: (0, 0)),
                  pl.BlockSpec((tm, RET_DK // 2), lambda i: (i % pos_tiles, 0)),
                  pl.BlockSpec((tm, RET_DK // 2), lambda i: (i % pos_tiles, 0))],
        out_specs=pl.BlockSpec((tm, n_cols), lambda i: (i, 0)),
        out_shape=jax.ShapeDtypeStruct((m, n_cols), out_dtype),
        compiler_params=_params(1), name="proj_ret",
    )(x, w, cos, sin)


def _rope_tables(pos):
    half = RET_DK // 2
    inv = 1.0 / (ROPE_BASE ** (jnp.arange(half, dtype=F32) / half))
    ang = pos.astype(F32)[:, None] * inv[None]
    return jnp.cos(ang), jnp.sin(ang)


def _log_gamma():
    return jnp.log(1.0 - 2.0 ** (-5.0 - jnp.arange(RET_HEADS, dtype=F32)))


def _decay_tables(c, rows):
    lg = _log_gamma()
    n = jnp.arange(rows, dtype=F32)
    live = n < c
    diff = n[:, None] - n[None, :]
    decay = jnp.where((diff >= 0)[None] & live[None, None, :],
                      jnp.exp(jnp.maximum(diff, 0.0)[None] * lg[:, None, None]), 0.0)
    q_decay = jnp.exp((n[None, :] + 1.0) * lg[:, None])
    k_decay = jnp.where(live[None], jnp.exp((c - 1.0 - n)[None, :] * lg[:, None]), 0.0)
    chunk_decay = jnp.exp(c * lg)
    return decay, q_decay, k_decay, chunk_decay


def _group_norm_gate(o, gate, gn):
    mu = jnp.mean(o, axis=-1, keepdims=True)
    oc = o - mu
    var = jnp.mean(oc * oc, axis=-1, keepdims=True)
    return _silu(gate) * (oc * lax.rsqrt(var + GN_EPS) * gn)


def _ret_prompt_kernel(q_ref, k_ref, v_ref, gate_ref, dec_ref, qd_ref, kd_ref, cd_ref, gn_ref,
                       y_ref, st_ref, s_ref, *, tb):
    cb = pl.program_id(2)

    @pl.when(cb == 0)
    def _():
        s_ref[...] = jnp.zeros_like(s_ref)

    def body(ci, carry):
        r0 = pl.multiple_of(ci * RET_CHUNK, RET_CHUNK)
        rows = pl.ds(r0, RET_CHUNK)
        for j in range(RET_HPS):
            kc = slice(j * RET_DK, (j + 1) * RET_DK)
            vc = slice(j * RET_DV, (j + 1) * RET_DV)
            q = q_ref[0, rows, kc]
            k = k_ref[0, rows, kc]
            v = v_ref[0, rows, vc]
            state = s_ref[j]
            scores = lax.dot_general(q, k, _NT, preferred_element_type=F32) * dec_ref[j]
            inner = jnp.dot(scores.astype(BF16), v, preferred_element_type=F32)
            cross = jnp.dot(q, state.astype(BF16), preferred_element_type=F32) * qd_ref[j]
            kd = (k.astype(F32) * kd_ref[j]).astype(BF16)
            s_ref[j] = cd_ref[j, 0:1, :] * state + lax.dot_general(kd, v, _TN, preferred_element_type=F32)
            y = _group_norm_gate(inner + cross, gate_ref[0, rows, vc].astype(F32), gn_ref[:, vc])
            y_ref[0, rows, vc] = y.astype(y_ref.dtype)
        return carry

    lax.fori_loop(0, tb // RET_CHUNK, body, 0)

    @pl.when(cb == pl.num_programs(2) - 1)
    def _():
        st_ref[0] = s_ref[...]


def _ret_prompt(proj, gn, bsz, seq, tb):
    decay, q_decay, k_decay, chunk_decay = _decay_tables(RET_CHUNK, RET_CHUNK)
    qd = jnp.broadcast_to(q_decay[:, :, None], (RET_HEADS, RET_CHUNK, RET_DV))
    kd = jnp.broadcast_to(k_decay[:, :, None], (RET_HEADS, RET_CHUNK, RET_DK))
    cd = jnp.broadcast_to(chunk_decay[:, None, None], (RET_HEADS, 8, RET_DV))
    proj = proj.reshape(bsz, seq, proj.shape[-1])
    hps = RET_HPS
    k_off = RET_QK // (hps * RET_DK)
    v_off = 2 * RET_QK // (hps * RET_DV)
    g_off = v_off + RET_HEADS // hps
    per_head = lambda shape: pl.BlockSpec((hps,) + shape, lambda b, h, c: (h, 0, 0))
    y, state = pl.pallas_call(
        functools.partial(_ret_prompt_kernel, tb=tb),
        grid=(bsz, RET_HEADS // hps, seq // tb),
        in_specs=[pl.BlockSpec((1, tb, hps * RET_DK), lambda b, h, c: (b, c, h)),
                  pl.BlockSpec((1, tb, hps * RET_DK), lambda b, h, c: (b, c, k_off + h)),
                  pl.BlockSpec((1, tb, hps * RET_DV), lambda b, h, c: (b, c, v_off + h)),
                  pl.BlockSpec((1, tb, hps * RET_DV), lambda b, h, c: (b, c, g_off + h)),
                  per_head((RET_CHUNK, RET_CHUNK)), per_head((RET_CHUNK, RET_DV)),
                  per_head((RET_CHUNK, RET_DK)), per_head((8, RET_DV)),
                  pl.BlockSpec((1, hps * RET_DV), lambda b, h, c: (0, h))],
        out_specs=[pl.BlockSpec((1, tb, hps * RET_DV), lambda b, h, c: (b, c, h)),
                   pl.BlockSpec((1, hps, RET_DK, RET_DV), lambda b, h, c: (b, h, 0, 0))],
        out_shape=[jax.ShapeDtypeStruct((bsz, seq, RET_V), BF16),
                   jax.ShapeDtypeStruct((bsz, RET_HEADS, RET_DK, RET_DV), F32)],
        scratch_shapes=[pltpu.VMEM((hps, RET_DK, RET_DV), F32)],
        compiler_params=_params(3), name="retention_prompt",
    )(proj, proj, proj, proj, decay, qd, kd, cd, gn)
    return y.reshape(bsz * seq, RET_V), state


def _ret_sample_kernel(p_ref, st_ref, dec_ref, qd_ref, kd_ref, cd_ref, gn_ref, y_ref, ns_ref):
    tp = SAMPLE_PAD
    proj = p_ref[0]
    for h in range(RET_HEADS):
        q = proj[:, h * RET_DK:(h + 1) * RET_DK].astype(BF16)
        k = proj[:, RET_QK + h * RET_DK:RET_QK + (h + 1) * RET_DK]
        v = proj[:, 2 * RET_QK + h * RET_DV:2 * RET_QK + (h + 1) * RET_DV]
        gate = proj[:, 2 * RET_QK + RET_V + h * RET_DV:2 * RET_QK + RET_V + (h + 1) * RET_DV]
        zk = jnp.zeros((LANES - tp, RET_DK), F32)
        zv = jnp.zeros((LANES - tp, RET_DV), F32)
        kp = jnp.concatenate([k, zk], axis=0).astype(BF16)
        kdp = jnp.concatenate([k * kd_ref[h], zk], axis=0).astype(BF16)
        vp = jnp.concatenate([v, zv], axis=0).astype(BF16)
        state = st_ref[0, h]
        scores = lax.dot_general(q, kp, _NT, preferred_element_type=F32) * dec_ref[h]
        inner = jnp.dot(scores.astype(BF16), vp, preferred_element_type=F32)
        cross = jnp.dot(q, state.astype(BF16), preferred_element_type=F32) * qd_ref[h]
        ns_ref[0, h] = cd_ref[h, 0:1, :] * state + lax.dot_general(kdp, vp, _TN, preferred_element_type=F32)
        y = _group_norm_gate(inner + cross, gate, gn_ref[:, h * RET_DV:(h + 1) * RET_DV])
        y_ref[0, :, h * RET_DV:(h + 1) * RET_DV] = y


def _ret_sample(proj, state, gn, t_new):
    bsz = proj.shape[0]
    tp = SAMPLE_PAD
    decay, q_decay, k_decay, chunk_decay = _decay_tables(t_new, tp)
    dec = jnp.pad(decay, ((0, 0), (0, 0), (0, LANES - tp)))
    qd = jnp.broadcast_to(q_decay[:, :, None], (RET_HEADS, tp, RET_DV))
    kd = jnp.broadcast_to(k_decay[:, :, None], (RET_HEADS, tp, RET_DK))
    cd = jnp.broadcast_to(chunk_decay[:, None, None], (RET_HEADS, 8, RET_DV))
    const = lambda a: pl.BlockSpec(a.shape, lambda b: (0,) * a.ndim)
    return pl.pallas_call(
        _ret_sample_kernel, grid=(bsz,),
        in_specs=[pl.BlockSpec((1,) + proj.shape[1:], lambda b: (b, 0, 0)),
                  pl.BlockSpec((1,) + state.shape[1:], lambda b: (b, 0, 0, 0)),
                  const(dec), const(qd), const(kd), const(cd), const(gn)],
        out_specs=[pl.BlockSpec((1, tp, RET_V), lambda b: (b, 0, 0)),
                   pl.BlockSpec((1,) + state.shape[1:], lambda b: (b, 0, 0, 0))],
        out_shape=[jax.ShapeDtypeStruct((bsz, tp, RET_V), F32),
                   jax.ShapeDtypeStruct(state.shape, F32)],
        compiler_params=_params(1), name="retention_sample",
    )(proj, state, dec, qd, kd, cd, gn)


def _top2_route(logits):
    lane = lax.broadcasted_iota(jnp.int32, logits.shape, 1)
    logits = jnp.where(lane < N_EXPERTS, logits, NEG)
    m1 = jnp.max(logits, axis=-1, keepdims=True)
    i1 = jnp.min(jnp.where(logits == m1, lane, LANES), axis=-1, keepdims=True)
    rest = jnp.where(lane == i1, NEG, logits)
    m2 = jnp.max(rest, axis=-1, keepdims=True)
    i2 = jnp.min(jnp.where(rest == m2, lane, LANES), axis=-1, keepdims=True)
    e2 = jnp.exp(m2 - m1)
    den = 1.0 + e2
    return jnp.where(lane == 0, 1.0 / den,
                     jnp.where(lane == 1, e2 / den,
                               jnp.where(lane == 2, i1.astype(F32),
                                         jnp.where(lane == 3, i2.astype(F32), 0.0))))


def _route_plan(e1, e2, tile):
    n_tok = e1.shape[0]
    e = jnp.concatenate([e1, e2])
    onehot = (e[:, None] == jnp.arange(N_EXPERTS, dtype=jnp.int32)[None]).astype(jnp.int32)
    csum = jnp.cumsum(onehot, axis=0)
    rank = jnp.take_along_axis(csum, e[:, None], axis=1)[:, 0] - 1
    tiles_per_expert = (csum[-1] + tile - 1) // tile
    tile_end = jnp.cumsum(tiles_per_expert)
    pos = ((tile_end - tiles_per_expert) * tile)[e] + rank
    n_tiles = (2 * n_tok) // tile + N_EXPERTS
    n_used = tile_end[-1]
    tile_ids = jnp.arange(n_tiles, dtype=jnp.int32)
    tile_expert = jnp.minimum(jnp.searchsorted(tile_end, tile_ids, side="right"), N_EXPERTS - 1)
    tile_expert = jnp.where(tile_ids < n_used, tile_expert, tile_expert[n_used - 1])
    meta = jnp.concatenate([tile_expert, n_used[None], tile_end]).astype(jnp.int32)
    return meta, pos[:n_tok], pos[n_tok:], n_tiles


SUBLANES = 8


def _start_row_gather(idx_ref, src_hbm, dst3, sem):
    def issue(blk, carry):
        for s in range(SUBLANES):
            pltpu.make_async_copy(src_hbm.at[pl.ds(idx_ref[0, 0, blk * SUBLANES + s], 1), :],
                                  dst3.at[blk, pl.ds(s, 1), :], sem).start()
        return carry

    lax.fori_loop(0, dst3.shape[0], issue, 0)


def _start_row_scatter(idx_ref, src3, dst_hbm, sem):
    def issue(blk, carry):
        for s in range(SUBLANES):
            pltpu.make_async_copy(src3.at[blk, pl.ds(s, 1), :],
                                  dst_hbm.at[pl.ds(idx_ref[0, 0, blk * SUBLANES + s], 1), :], sem).start()
        return carry

    lax.fori_loop(0, src3.shape[0], issue, 0)


def _wait_rows(hbm, vmem3, sem):
    rows = vmem3.shape[0] * SUBLANES
    pltpu.make_async_copy(hbm.at[pl.ds(0, rows), :], hbm.at[pl.ds(0, rows), :], sem).wait()


def _dispatch_kernel(meta_ref, p1_ref, p2_ref, x_ref, xs_hbm, zbuf, sem, zsem, *, tm, tile, n_tiles):
    @pl.when(pl.program_id(0) == 0)
    def _():
        zbuf[...] = jnp.zeros_like(zbuf)
        zero_tile = lambda t: pltpu.make_async_copy(zbuf, xs_hbm.at[pl.ds(t * tile, tile), :], zsem)
        for e in range(N_EXPERTS):
            zero_tile(jnp.maximum(meta_ref[n_tiles + 1 + e] - 1, 0)).start()
        for e in range(N_EXPERTS):
            zero_tile(0).wait()

        def zero_unused(t, carry):
            zero_tile(t).start()
            zero_tile(t).wait()
            return carry

        lax.fori_loop(meta_ref[n_tiles], n_tiles, zero_unused, 0)

    _start_row_scatter(p1_ref, x_ref, xs_hbm, sem)
    _start_row_scatter(p2_ref, x_ref, xs_hbm, sem)
    for _ in range(2):
        _wait_rows(xs_hbm, x_ref, sem)


def _dispatch(meta, pos1, pos2, x, tm, tile, n_tiles):
    m = x.shape[0]
    idx = lambda: pl.BlockSpec((1, 1, tm), lambda i, meta: (i, 0, 0), memory_space=pltpu.SMEM)
    return pl.pallas_call(
        functools.partial(_dispatch_kernel, tm=tm, tile=tile, n_tiles=n_tiles),
        grid_spec=pltpu.PrefetchScalarGridSpec(
            num_scalar_prefetch=1, grid=(m // tm,),
            in_specs=[idx(), idx(),
                      pl.BlockSpec((tm // SUBLANES, SUBLANES, D_MODEL), lambda i, meta: (i, 0, 0))],
            out_specs=pl.BlockSpec(memory_space=pl.ANY),
            scratch_shapes=[pltpu.VMEM((tile, D_MODEL), F32),
                            pltpu.SemaphoreType.DMA(()), pltpu.SemaphoreType.DMA(())]),
        out_shape=jax.ShapeDtypeStruct((n_tiles * tile, D_MODEL), F32),
        compiler_params=_params(1), name="moe_dispatch",
    )(meta, pos1.reshape(m // tm, 1, tm), pos2.reshape(m // tm, 1, tm),
      x.reshape(m // SUBLANES, SUBLANES, D_MODEL))


def _expert_kernel(meta_ref, xs_ref, wg_ref, wu_ref, wd_ref, o_ref, xb_ref, *, n_tiles):
    f = pl.program_id(1)
    used = pl.program_id(0) < meta_ref[n_tiles]

    @pl.when(jnp.logical_and(jnp.logical_not(used), f == 0))
    def _():
        o_ref[...] = jnp.zeros_like(o_ref)

    @pl.when(jnp.logical_and(used, f == 0))
    def _():
        xb_ref[...] = xs_ref[...].astype(BF16)

    @pl.when(used)
    def _():
        part = _swiglu_chunks(xb_ref[...], wg_ref, wu_ref, wd_ref, (0,), wd_ref.shape[1], FF_CHUNK)

        @pl.when(f == 0)
        def _():
            o_ref[...] = part

        @pl.when(f > 0)
        def _():
            o_ref[...] += part


def _experts(meta, xs, w_gu, w_down, tile, n_tiles, tf):
    d_ff = w_down.shape[1]
    nf = d_ff // tf
    fcol = lambda j, f, meta: jnp.where(j < meta[n_tiles], f, nf - 1)
    return pl.pallas_call(
        functools.partial(_expert_kernel, n_tiles=n_tiles),
        grid_spec=pltpu.PrefetchScalarGridSpec(
            num_scalar_prefetch=1, grid=(n_tiles, nf),
            in_specs=[pl.BlockSpec((tile, D_MODEL), lambda j, f, meta: (j, 0)),
                      pl.BlockSpec((1, D_MODEL, tf), lambda j, f, meta: (meta[j], 0, fcol(j, f, meta))),
                      pl.BlockSpec((1, D_MODEL, tf), lambda j, f, meta: (meta[j], 0, nf + fcol(j, f, meta))),
                      pl.BlockSpec((1, tf, D_MODEL), lambda j, f, meta: (meta[j], fcol(j, f, meta), 0))],
            out_specs=pl.BlockSpec((tile, D_MODEL), lambda j, f, meta: (j, 0)),
            scratch_shapes=[pltpu.VMEM((tile, D_MODEL), BF16)]),
        out_shape=jax.ShapeDtypeStruct((n_tiles * tile, D_MODEL), F32),
        compiler_params=_params(2), name="moe_experts",
    )(meta, xs, w_gu, w_gu, w_down)


def _combine_kernel(p1_ref, p2_ref, n1_ref, n2_ref, rows_hbm, x_ref, r_ref, g_ref, b_ref, op_ref, os_ref,
                    buf1, buf2, sem1, sem2, *, tm, prompt_tiles):
    i = pl.program_id(0)
    slot = i % 2

    def start(pa_ref, pb_ref, s):
        _start_row_gather(pa_ref, rows_hbm, buf1.at[s], sem1.at[s])
        _start_row_gather(pb_ref, rows_hbm, buf2.at[s], sem2.at[s])

    @pl.when(i == 0)
    def _():
        start(p1_ref, p2_ref, 0)

    @pl.when(i + 1 < pl.num_programs(0))
    def _():
        start(n1_ref, n2_ref, 1 - slot)

    _wait_rows(rows_hbm, buf1.at[slot], sem1.at[slot])
    _wait_rows(rows_hbm, buf2.at[slot], sem2.at[slot])
    route = r_ref[...]
    y = (route[:, 0:1] * buf1[slot].reshape(tm, D_MODEL) + route[:, 1:2] * buf2[slot].reshape(tm, D_MODEL))
    res = _layer_norm(ALPHA * x_ref[...] + y, g_ref[...], b_ref[...])

    @pl.when(i < prompt_tiles)
    def _():
        op_ref[...] = res

    @pl.when(i >= prompt_tiles)
    def _():
        os_ref[...] = res


def _combine(pos1, pos2, rows, x, route, g, b, tm, n_prompt):
    m = x.shape[0]
    pt = n_prompt // tm
    nt = m // tm
    idx = lambda: pl.BlockSpec((1, 1, tm), lambda i: (i, 0, 0), memory_space=pltpu.SMEM)
    nxt = lambda: pl.BlockSpec((1, 1, tm), lambda i: (jnp.minimum(i + 1, nt - 1), 0, 0), memory_space=pltpu.SMEM)
    vec = pl.BlockSpec((1, D_MODEL), lambda i: (0, 0))
    p1 = pos1.reshape(nt, 1, tm)
    p2 = pos2.reshape(nt, 1, tm)
    return pl.pallas_call(
        functools.partial(_combine_kernel, tm=tm, prompt_tiles=pt), grid=(nt,),
        in_specs=[idx(), idx(), nxt(), nxt(), pl.BlockSpec(memory_space=pl.ANY),
                  pl.BlockSpec((tm, D_MODEL), lambda i: (i, 0)),
                  pl.BlockSpec((tm, LANES), lambda i: (i, 0)), vec, vec],
        out_specs=[pl.BlockSpec((tm, D_MODEL), lambda i: (jnp.minimum(i, pt - 1), 0)),
                   pl.BlockSpec((tm, D_MODEL), lambda i: (jnp.maximum(i - pt, 0), 0))],
        out_shape=[jax.ShapeDtypeStruct((n_prompt, D_MODEL), F32),
                   jax.ShapeDtypeStruct((m - n_prompt, D_MODEL), F32)],
        scratch_shapes=[pltpu.VMEM((2, tm // SUBLANES, SUBLANES, D_MODEL), F32),
                        pltpu.VMEM((2, tm // SUBLANES, SUBLANES, D_MODEL), F32),
                        pltpu.SemaphoreType.DMA((2,)), pltpu.SemaphoreType.DMA((2,))],
        compiler_params=_params(1), name="moe_combine",
    )(p1, p2, p1, p2, rows, x, route, g, b)


def _moe(x, route, n_prompt, w_gu, w_down, g, b, tm, tile, tf):
    e1 = route[:, 2].astype(jnp.int32)
    e2 = route[:, 3].astype(jnp.int32)
    meta, pos1, pos2, n_tiles = _route_plan(e1, e2, tile)
    xs = _dispatch(meta, pos1, pos2, x, tm, tile, n_tiles)
    rows = _experts(meta, xs, w_gu, w_down, tile, n_tiles, tf)
    return _combine(pos1, pos2, rows, x, route, g, b, tm, n_prompt)


def kernel(x_prompt, x_sample, cache_kv_w128, cache_kv_w512, cache_kv_w2048, state_ret,
           ln_g, ln_b, rel_bias, w_in_dil, w_out_dil, w_in_ret, ret_gn_g, w_out_ret,
           w_gu_dense, w_down_dense, w_router, w_gu_moe, w_down_moe):
    bsz, seq, _ = x_prompt.shape
    dbsz, t_new, _ = x_sample.shape
    tp = SAMPLE_PAD
    n_p = bsz * seq
    n_s = dbsz * tp
    caches = (cache_kv_w128, cache_kv_w512, cache_kv_w2048)

    w_in_dil_b = w_in_dil.astype(BF16)
    w_out_dil_b = w_out_dil.astype(BF16)
    w_in_ret_b = w_in_ret.astype(BF16)
    w_out_ret_b = w_out_ret.astype(BF16)
    w_gu_dense_b = w_gu_dense.astype(BF16)
    w_down_dense_b = w_down_dense.astype(BF16)
    w_gu_moe_b = w_gu_moe.astype(BF16)
    w_down_moe_b = w_down_moe.astype(BF16)
    lng = ln_g.reshape(DEPTH, 2, 1, D_MODEL)
    lnb = ln_b.reshape(DEPTH, 2, 1, D_MODEL)
    gn = ret_gn_g.reshape(1, RET_V)

    hp = x_prompt.reshape(n_p, D_MODEL)
    hs = jnp.pad(x_sample, ((0, 0), (0, tp - t_new), (0, 0))).reshape(n_s, D_MODEL)

    tm = min(1024, seq)
    qkv_groups, (kv128_p, kv512_p, kv2048_p) = _proj_dil_prompt(hp, w_in_dil_b, bsz, seq, tm)
    outs, lses = [], []
    for g, (window, dil) in enumerate(DIL_GROUPS):
        qkv_g = qkv_groups[g].reshape(bsz * dil, seq // dil, G_COLS)
        tbl = _prompt_table(rel_bias, g, window, dil)
        o, l = _attn_prompt(qkv_g, tbl, min(512, seq // dil))
        outs.append(o.reshape(bsz, dil, seq // dil, A_WIDTH))
        lses.append(l.reshape(bsz, dil, seq // dil, A_WIDTH))
    hp = _merge_out(outs, lses, w_out_dil_b, hp, lng[0, 0], lnb[0, 0], 512)

    qkv_s = _matmul(hs, w_in_dil_b, G_COLS)
    qkv_s3 = qkv_s.reshape(dbsz, tp, N_GROUPS * G_COLS)
    tabs = [_sample_tables(rel_bias, g, window, dil, caches[g].shape[1], t_new)
            for g, (window, dil) in enumerate(DIL_GROUPS)]
    q_t = qkv_s.reshape(dbsz, tp, N_GROUPS, 3, H_SLOT, HEAD_DIM).transpose(0, 2, 3, 4, 5, 1)
    caches_t = [c.transpose(0, 2, 3, 4, 1) for c in caches]
    mixed_t = _attn_sample(q_t, caches_t, [t[0] for t in tabs], [t[1] for t in tabs], t_new)
    mixed_s = mixed_t.transpose(0, 3, 1, 2)
    hs = _mm_ln(mixed_s.reshape(n_s, A_WIDTH), w_out_dil_b, hs, lng[0, 0], lnb[0, 0], n_s)
    rows_s = []
    for g in range(N_GROUPS):
        kv = qkv_s3[:, :t_new, g * G_COLS + A_WIDTH:(g + 1) * G_COLS]
        rows_s.append(kv.reshape(dbsz, t_new, 2, H_SLOT, HEAD_DIM))

    hp = _ffn_dense(hp, w_gu_dense_b, w_down_dense_b, lng[0, 1], lnb[0, 1], 512)
    hs = _ffn_dense(hs, w_gu_dense_b, w_down_dense_b, lng[0, 1], lnb[0, 1], n_s)

    cos_p, sin_p = _rope_tables(jnp.arange(seq, dtype=jnp.int32))
    pos_s = jnp.tile(PAST_LEN + jnp.arange(tp, dtype=jnp.int32), dbsz)
    cos_s, sin_s = _rope_tables(pos_s)
    proj_p = _proj_ret(hp, w_in_ret_b, cos_p, sin_p, 512, BF16)
    y_p, ret_p = _ret_prompt(proj_p, gn, bsz, seq, min(1024, seq))
    proj_s = _proj_ret(hs, w_in_ret_b, cos_s, sin_s, n_s // 2, F32)
    y_s, ret_s = _ret_sample(proj_s.reshape(dbsz, tp, -1), state_ret, gn, t_new)
    h_all, route = _mm_ln_pair(y_p, y_s.reshape(n_s, RET_V), w_out_ret_b, hp, hs, lng[1, 0], lnb[1, 0],
                               w_router, 512)

    out_p, out_s = _moe(h_all, route, n_p, w_gu_moe_b, w_down_moe_b, lng[1, 1], lnb[1, 1], 512, 1024,
                        w_down_moe.shape[1] // 2)

    y_prompt = out_p.reshape(bsz, seq, D_MODEL)
    y_sample = out_s.reshape(dbsz, tp, D_MODEL)[:, :t_new]
    shape5 = lambda a: a.reshape(a.shape[0], a.shape[1], 2, H_SLOT, HEAD_DIM)
    return (y_prompt, y_sample, shape5(kv128_p), shape5(kv512_p), shape5(kv2048_p), ret_p,
            rows_s[0], rows_s[1], rows_s[2], ret_s)
```

```python
import functools

import jax
import jax.numpy as jnp
import numpy as np
from jax import lax
from jax.experimental import pallas as pl
from jax.experimental.pallas import tpu as pltpu

F32 = jnp.float32
BF16 = jnp.bfloat16

DEPTH = 2
D_MODEL = 1024
PAST_LEN = 16384
DIL_GROUPS = ((128, 1), (512, 4), (2048, 16))
N_GROUPS = 3
H_SLOT = 8
HEAD_DIM = 64
A_WIDTH = H_SLOT * HEAD_DIM
G_COLS = 3 * A_WIDTH
N_BUCKETS = 32
MAX_DISTANCE = 2048
RET_HEADS = 4
RET_DK = 256
RET_DV = 512
RET_CHUNK = 128
RET_HPS = 4
ROPE_BASE = 10000.0
RET_QK = RET_HEADS * RET_DK
RET_V = RET_HEADS * RET_DV
N_EXPERTS = 8
LN_EPS = 1e-5
GN_EPS = 1e-5
ALPHA = (2 * DEPTH) ** 0.25
NEG = -1e30

LANES = 128
ATT_BLOCK = 128
PERM = 256
FF_CHUNK = 512
SAMPLE_PAD = 16
VMEM_LIMIT = 56 * 1024 * 1024

_NT = (((1,), (1,)), ((), ()))
_TN = (((0,), (0,)), ((), ()))


def _params(n_grid):
    return pltpu.CompilerParams(dimension_semantics=("arbitrary",) * n_grid,
                                vmem_limit_bytes=VMEM_LIMIT)


def _layer_norm(z, g, b):
    mu = jnp.mean(z, axis=-1, keepdims=True)
    zc = z - mu
    var = jnp.mean(zc * zc, axis=-1, keepdims=True)
    return zc * lax.rsqrt(var + LN_EPS) * g + b


def _silu(a):
    return a / (1.0 + jnp.exp(-a))


def _t5_bucket(dist):
    max_exact = N_BUCKETS // 2
    d = np.asarray(dist, dtype=np.int64)
    scaled = np.log(np.maximum(d, 1) / max_exact) / np.log(MAX_DISTANCE / max_exact)
    large = np.minimum(max_exact + (scaled * (N_BUCKETS - max_exact)).astype(np.int32), N_BUCKETS - 1)
    return np.where(d < max_exact, d, large).astype(np.int32)


def _group_bias(rel_bias, g, window, dil):
    n_keys = window // dil + 1
    buckets = jnp.asarray(_t5_bucket(np.arange(n_keys) * dil))
    return rel_bias[buckets][:, g * H_SLOT:(g + 1) * H_SLOT].T.astype(F32)


def _matmul_kernel(x_ref, w_ref, o_ref):
    o_ref[...] = jnp.dot(x_ref[...].astype(BF16), w_ref[...], preferred_element_type=F32)


def _matmul(x, w, tn):
    m, k = x.shape
    n = w.shape[1]
    return pl.pallas_call(
        _matmul_kernel,
        grid=(n // tn,),
        in_specs=[pl.BlockSpec((m, k), lambda j: (0, 0)),
                  pl.BlockSpec((k, tn), lambda j: (0, j))],
        out_specs=pl.BlockSpec((m, tn), lambda j: (0, j)),
        out_shape=jax.ShapeDtypeStruct((m, n), F32),
        compiler_params=_params(1),
        name="matmul_sample",
    )(x, w)


def _deinterleave_matrix(dil):
    p = np.zeros((PERM, PERM), np.float32)
    rows = PERM // dil
    for r in range(dil):
        for m in range(rows):
            p[r * rows + m, m * dil + r] = 1.0
    return p


def _proj_dil_kernel(x_ref, w_ref, p1_ref, p2_ref, q0_ref, q1_ref, q2_ref, kv0_ref, kv1_ref, kv2_ref, xperm, xb_ref,
                     *, tm, tpb, keeps):
    g = pl.program_id(1)
    j = pl.program_id(0) % tpb

    @pl.when(g == 0)
    def _():
        xb_ref[...] = x_ref[...].astype(BF16)
    q_refs = (q0_ref, q1_ref, q2_ref)
    kv_refs = (kv0_ref, kv1_ref, kv2_ref)
    perms = (None, p1_ref, p2_ref)

    def store_q(ref, r, row0, n_rows, acc, src0):
        ref[0, r, row0:row0 + n_rows, :A_WIDTH] = (acc[src0:src0 + n_rows, :A_WIDTH] * (HEAD_DIM ** -0.5)).astype(BF16)
        ref[0, r, row0:row0 + n_rows, A_WIDTH:] = acc[src0:src0 + n_rows, A_WIDTH:].astype(BF16)

    for gi, (_, dil) in enumerate(DIL_GROUPS):
        keep = keeps[gi]
        in_tail = (j >= tpb - keep // tm) if keep >= tm else (j == tpb - 1)
        row0 = 0 if keep >= tm else tm - keep

        @pl.when(g == gi)
        def _(gi=gi, dil=dil, in_tail=in_tail, row0=row0):
            xb = xb_ref[...]
            if dil == 1:
                acc = jnp.dot(xb, w_ref[...], preferred_element_type=F32)
                store_q(q_refs[gi], 0, 0, tm, acc, 0)
            else:
                for sub in range(tm // PERM):
                    blk = xb[sub * PERM:(sub + 1) * PERM]
                    xperm[sub * PERM:(sub + 1) * PERM, :] = jnp.dot(
                        perms[gi][...], blk, preferred_element_type=F32).astype(BF16)
                acc = jnp.dot(xperm[...], w_ref[...], preferred_element_type=F32)
                rows = PERM // dil
                for sub in range(tm // PERM):
                    for r in range(dil):
                        store_q(q_refs[gi], r, sub * rows, rows, acc, sub * PERM + r * rows)

            @pl.when(in_tail)
            def _():
                if dil == 1:
                    kv_refs[gi][0] = acc[row0:, A_WIDTH:]
                else:
                    kv_refs[gi][0] = jnp.dot(xb[row0:], w_ref[:, A_WIDTH:], preferred_element_type=F32)


def _proj_dil_prompt(x, w, bsz, seq, tm):
    tpb = seq // tm
    keeps = tuple(min(wd, seq) for wd, _ in DIL_GROUPS)

    def kv_spec(keep):
        if keep >= tm:
            first = tpb - keep // tm
            return pl.BlockSpec((1, tm, 2 * A_WIDTH),
                                lambda i, g: (i // tpb, jnp.maximum(i % tpb - first, 0), 0))
        return pl.BlockSpec((1, keep, 2 * A_WIDTH), lambda i, g: (i // tpb, 0, 0))

    perm = lambda d: jnp.asarray(_deinterleave_matrix(d), BF16)
    const = pl.BlockSpec((PERM, PERM), lambda i, g: (0, 0))
    outs = pl.pallas_call(
        functools.partial(_proj_dil_kernel, tm=tm, tpb=tpb, keeps=keeps),
        grid=(bsz * tpb, N_GROUPS),
        in_specs=[pl.BlockSpec((tm, D_MODEL), lambda i, g: (i, 0)),
                  pl.BlockSpec((D_MODEL, G_COLS), lambda i, g: (0, g)), const, const],
        out_specs=[pl.BlockSpec((1, d, tm // d, G_COLS), lambda i, g: (i // tpb, 0, i % tpb, 0))
                   for _, d in DIL_GROUPS] + [kv_spec(k) for k in keeps],
        out_shape=[jax.ShapeDtypeStruct((bsz, d, seq // d, G_COLS), BF16) for _, d in DIL_GROUPS]
        + [jax.ShapeDtypeStruct((bsz, k, 2 * A_WIDTH), F32) for k in keeps],
        scratch_shapes=[pltpu.VMEM((tm, D_MODEL), BF16), pltpu.VMEM((tm, D_MODEL), BF16)],
        compiler_params=_params(2),
        name="proj_dil_prompt",
    )(x, w, perm(DIL_GROUPS[1][1]), perm(DIL_GROUPS[2][1]))
    return outs[:3], outs[3:]


def _attn_prompt_kernel(q_ref, k_ref, v_ref, kp_ref, vp_ref, tbl_ref, o_ref, lse_ref, kbuf, vbuf, *, tq):
    n = pl.program_id(1)
    kbuf[0:ATT_BLOCK, :] = kp_ref[0]
    kbuf[ATT_BLOCK:, :] = k_ref[0]
    vbuf[0:ATT_BLOCK, :] = vp_ref[0]
    vbuf[ATT_BLOCK:, :] = v_ref[0]
    low = lax.broadcasted_iota(jnp.int32, (ATT_BLOCK, LANES), 1) < HEAD_DIM

    def body(m, carry):
        r0 = pl.multiple_of(m * ATT_BLOCK, ATT_BLOCK)
        first = jnp.where(jnp.logical_and(n == 0, m == 0), 1, 0)
        for p in range(A_WIDTH // LANES):
            cols = slice(p * LANES, (p + 1) * LANES)
            qm = q_ref[0, pl.ds(r0, ATT_BLOCK), cols]
            keys = kbuf[pl.ds(r0, 2 * ATT_BLOCK), cols]
            vals = vbuf[pl.ds(r0, 2 * ATT_BLOCK), cols]
            outs, lses = [], []
            for a in range(2):
                qa = jnp.where(low if a == 0 else jnp.logical_not(low), qm, jnp.zeros_like(qm))
                s = lax.dot_general(qa, keys, _NT, preferred_element_type=F32) + tbl_ref[first, 2 * p + a]
                mx = jnp.max(s, axis=-1, keepdims=True)
                e = jnp.exp(s - mx)
                l = jnp.sum(e, axis=-1, keepdims=True)
                o = jnp.dot(e.astype(BF16), vals, preferred_element_type=F32)
                outs.append(o / l)
                lses.append(jnp.broadcast_to(mx + jnp.log(l), (ATT_BLOCK, LANES)))
            o_ref[0, pl.ds(r0, ATT_BLOCK), cols] = jnp.where(low, outs[0], outs[1]).astype(BF16)
            lse_ref[0, pl.ds(r0, ATT_BLOCK), cols] = jnp.where(low, lses[0], lses[1])
        return carry

    lax.fori_loop(0, tq // ATT_BLOCK, body, 0)


def _attn_prompt(qkv, tbl, tq):
    nb, length, _ = qkv.shape
    sub = tq // ATT_BLOCK
    cur = lambda c: pl.BlockSpec((1, tq, A_WIDTH), lambda s, n: (s, n, c))
    prev = lambda c: pl.BlockSpec((1, ATT_BLOCK, A_WIDTH), lambda s, n: (s, jnp.maximum(n * sub - 1, 0), c))
    return pl.pallas_call(
        functools.partial(_attn_prompt_kernel, tq=tq),
        grid=(nb, length // tq),
        in_specs=[cur(0), cur(1), cur(2), prev(1), prev(2),
                  pl.BlockSpec(tbl.shape, lambda s, n: (0, 0, 0, 0))],
        out_specs=[pl.BlockSpec((1, tq, A_WIDTH), lambda s, n: (s, n, 0))] * 2,
        out_shape=[jax.ShapeDtypeStruct((nb, length, A_WIDTH), BF16),
                   jax.ShapeDtypeStruct((nb, length, A_WIDTH), F32)],
        scratch_shapes=[pltpu.VMEM((tq + ATT_BLOCK, A_WIDTH), BF16)] * 2,
        compiler_params=_params(2),
        name="attn_prompt",
    )(qkv, qkv, qkv, qkv, qkv, tbl)


def _toeplitz(u, n_rows, n_cols):
    h, n = u.shape
    assert n == n_rows + n_cols - 1
    up = jnp.pad(u, ((0, 0), (0, 1)))
    w = jnp.tile(up, (1, n_rows))[:, :n_rows * n].reshape(h, n_rows, n)
    return w[:, :, n_rows - 1:n_rows - 1 + n_cols]


def _bias_by_offset(bias, dist, valid):
    return jnp.where(valid[None], bias[:, np.clip(dist, 0, bias.shape[1] - 1)], NEG)


def _prompt_table(rel_bias, g, window, dil):
    bias = _group_bias(rel_bias, g, window, dil)
    k = np.arange(3 * ATT_BLOCK - 1)
    dist = 2 * ATT_BLOCK - 1 - k
    u = _bias_by_offset(bias, dist, (dist >= 0) & (dist <= window // dil))
    tbl = _toeplitz(u, ATT_BLOCK, 2 * ATT_BLOCK)
    c = np.arange(2 * ATT_BLOCK)[None, None, :]
    tbl_first = jnp.where(c < ATT_BLOCK, NEG, tbl)
    return jnp.stack([tbl, tbl_first], 0)


def _split3(x):
    hi = x.astype(BF16)
    r1 = x - hi.astype(F32)
    mid = r1.astype(BF16)
    lo = (r1 - mid.astype(F32)).astype(BF16)
    return hi, mid, lo


def _merge_out_kernel(o0, o1, o2, l0, l1, l2, pt1_ref, pt2_ref, w_ref, x_ref, g_ref, b_ref, out_ref, *, tm):
    pts = (None, pt1_ref, pt2_ref)

    def natural(ref, gi, exact_f32):
        dil = DIL_GROUPS[gi][1]
        if dil == 1:
            return ref[0, 0].astype(F32)
        rows = PERM // dil
        blocks = []
        for sub in range(tm // PERM):
            piece = jnp.concatenate([ref[0, r, sub * rows:(sub + 1) * rows, :] for r in range(dil)], axis=0)
            parts = _split3(piece) if exact_f32 else (piece,)
            blocks.append(sum(jnp.dot(pts[gi][...], p, preferred_element_type=F32) for p in parts))
        return jnp.concatenate(blocks, axis=0)

    os_ = [natural(r, gi, False) for gi, r in enumerate((o0, o1, o2))]
    ls = [natural(r, gi, True) for gi, r in enumerate((l0, l1, l2))]
    mx = jnp.maximum(jnp.maximum(ls[0], ls[1]), ls[2])
    es = [jnp.exp(l - mx) for l in ls]
    mixed = (es[0] * os_[0] + es[1] * os_[1] + es[2] * os_[2]) / (es[0] + es[1] + es[2])
    acc = jnp.dot(mixed.astype(BF16), w_ref[...], preferred_element_type=F32)
    out_ref[...] = _layer_norm(ALPHA * x_ref[...] + acc, g_ref[...], b_ref[...])


def _merge_out(os_, ls_, w, x, g, b, tm):
    bsz, _, seq, _ = os_[0].shape
    tpb = seq // tm
    grp = lambda d: pl.BlockSpec((1, d, tm // d, A_WIDTH), lambda i: (i // tpb, 0, i % tpb, 0))
    specs = [grp(d) for _, d in DIL_GROUPS]
    pt = lambda d: jnp.asarray(_deinterleave_matrix(d).T, BF16)
    const = pl.BlockSpec((PERM, PERM), lambda i: (0, 0))
    vec = pl.BlockSpec((1, D_MODEL), lambda i: (0, 0))
    return pl.pallas_call(
        functools.partial(_merge_out_kernel, tm=tm), grid=(bsz * tpb,),
        in_specs=specs + specs + [const, const, pl.BlockSpec((A_WIDTH, D_MODEL), lambda i: (0, 0)),
                                  pl.BlockSpec((tm, D_MODEL), lambda i: (i, 0)), vec, vec],
        out_specs=pl.BlockSpec((tm, D_MODEL), lambda i: (i, 0)),
        out_shape=jax.ShapeDtypeStruct((bsz * seq, D_MODEL), F32),
        compiler_params=_params(1), name="merge_outproj_ln",
    )(*os_, *ls_, pt(DIL_GROUPS[1][1]), pt(DIL_GROUPS[2][1]), w, x, g, b)


def _attn_sample_kernel(q_ref, c0_ref, c1_ref, c2_ref, tb0, tb1, tb2, tn0, tn1, tn2, out_ref, *scratch, t_new):
    tp = SAMPLE_PAD
    lane_t = lax.broadcasted_iota(jnp.int32, (HEAD_DIM, tp), 1)
    lane_k = lax.broadcasted_iota(jnp.int32, (HEAD_DIM, LANES), 1)
    qcol = lambda g, h, t: q_ref[0, g, 0, h][:, t:t + 1] * (HEAD_DIM ** -0.5)
    sets = []
    for g, (c_ref, tb, tn) in enumerate(((c0_ref, tb0, tn0), (c1_ref, tb1, tn1), (c2_ref, tb2, tn2))):
        dil = DIL_GROUPS[g][1]
        sets.append((lambda h, c_ref=c_ref: c_ref[0, 0, h], lambda h, c_ref=c_ref: c_ref[0, 1, h], tb, g,
                     dil if dil >= t_new else 0))
        sets.append((lambda h, g=g: q_ref[0, g, 1, h], lambda h, g=g: q_ref[0, g, 2, h], tn, g, 0))
    s_refs = scratch[:len(sets)]
    p_refs = dict(zip([i for i, st in enumerate(sets) if st[4]], scratch[len(sets):]))
    out_ref[0] = jnp.zeros(out_ref.shape[1:], F32)

    for i, p_ref in p_refs.items():
        keys, _, _, g, dil = sets[i]
        p_ref[...] = jnp.zeros_like(p_ref)
        for h in range(H_SLOT):
            qsel = jnp.zeros((HEAD_DIM, LANES), F32)
            for t in range(t_new):
                qsel = jnp.where(lane_k % dil == t, qcol(g, h, t), qsel)
            kt = keys(h)
            s_refs[i][h:h + 1, :] = jnp.sum(kt * jnp.tile(qsel, (1, kt.shape[1] // LANES)), axis=0, keepdims=True)

    for t in range(t_new):
        for h in range(H_SLOT):
            for (keys, _, _, g, dil), s_ref in zip(sets, s_refs):
                if not dil:
                    s_ref[h:h + 1, :] = jnp.sum(keys(h) * qcol(g, h, t), axis=0, keepdims=True)
        scores = [s_ref[...] + st[2][t] for st, s_ref in zip(sets, s_refs)]
        mx = functools.reduce(jnp.maximum, [jnp.max(s, axis=-1, keepdims=True) for s in scores])
        es = [jnp.exp(s - mx) for s in scores]
        inv = 1.0 / functools.reduce(jnp.add, [jnp.sum(e, axis=-1, keepdims=True) for e in es])
        for i, (e, s_ref) in enumerate(zip(es, s_refs)):
            if i in p_refs:
                p_refs[i][...] += e * inv
            else:
                s_ref[...] = e
        for h in range(H_SLOT):
            by_width = {}
            for (_, values, _, _, dil), s_ref in zip(sets, s_refs):
                if not dil:
                    prod = values(h) * s_ref[h:h + 1, :]
                    by_width[prod.shape[1]] = by_width[prod.shape[1]] + prod if prod.shape[1] in by_width else prod
            col = functools.reduce(jnp.add, [jnp.sum(p, axis=-1, keepdims=True) for p in by_width.values()])
            out_ref[0, h] = jnp.where(lane_t == t, col * inv[h:h + 1, :], out_ref[0, h])

    for i, p_ref in p_refs.items():
        _, values, _, _, dil = sets[i]
        for h in range(H_SLOT):
            prod = values(h) * p_ref[h:h + 1, :]
            folded = functools.reduce(jnp.add, [prod[:, j:j + LANES] for j in range(0, prod.shape[1], LANES)])
            acc = out_ref[0, h]
            for t in range(t_new):
                col = jnp.sum(jnp.where(lane_k % dil == t, folded, 0.0), axis=-1, keepdims=True)
                acc = acc + jnp.where(lane_t == t, col, 0.0)
            out_ref[0, h] = acc


def _sample_tables(rel_bias, g, window, dil, length, t_new):
    bias = _group_bias(rel_bias, g, window, dil)
    n_keys = window // dil + 1
    tp = SAMPLE_PAD
    k = np.arange(tp + length - 1)
    dist = length + tp - 1 - k
    u = _bias_by_offset(bias, dist // dil, (dist % dil == 0) & (dist // dil < n_keys))
    tc = _toeplitz(u, tp, length)
    k2 = np.arange(2 * tp - 1)
    dist2 = tp - 1 - k2
    u2 = _bias_by_offset(bias, dist2 // dil, (dist2 >= 0) & (dist2 % dil == 0) & (dist2 // dil < n_keys))
    live_col = (np.arange(tp) < t_new)[None, None, :]
    tn = jnp.where(live_col, _toeplitz(u2, tp, tp), NEG)
    return tc.transpose(1, 0, 2)[:t_new], tn.transpose(1, 0, 2)[:t_new]


def _attn_sample(q_t, caches_t, tables_c, tables_n, t_new):
    bsz = q_t.shape[0]
    tp = SAMPLE_PAD
    const = lambda a: pl.BlockSpec(a.shape, lambda b: (0, 0, 0))
    scratch = []
    for c in caches_t:
        scratch += [pltpu.VMEM((H_SLOT, c.shape[-1]), F32), pltpu.VMEM((H_SLOT, tp), F32)]
    for c, (_, dil) in zip(caches_t, DIL_GROUPS):
        if dil >= t_new:
            assert c.shape[-1] % dil == 0 and c.shape[-1] % LANES == 0 and LANES % dil == 0
            scratch.append(pltpu.VMEM((H_SLOT, c.shape[-1]), F32))
    return pl.pallas_call(
        functools.partial(_attn_sample_kernel, t_new=t_new),
        grid=(bsz,),
        in_specs=[pl.BlockSpec((1,) + q_t.shape[1:], lambda b: (b, 0, 0, 0, 0, 0))]
        + [pl.BlockSpec((1,) + c.shape[1:], lambda b: (b, 0, 0, 0, 0)) for c in caches_t]
        + [const(t) for t in tables_c] + [const(t) for t in tables_n],
        out_specs=pl.BlockSpec((1, H_SLOT, HEAD_DIM, tp), lambda b: (b, 0, 0, 0)),
        out_shape=jax.ShapeDtypeStruct((bsz, H_SLOT, HEAD_DIM, tp), F32),
        scratch_shapes=scratch,
        compiler_params=_params(1),
        name="attn_sample",
    )(q_t, *caches_t, *tables_c, *tables_n)


def _mm_ln_kernel(y_ref, w_ref, x_ref, g_ref, b_ref, o_ref):
    acc = jnp.dot(y_ref[...].astype(BF16), w_ref[...], preferred_element_type=F32)
    o_ref[...] = _layer_norm(ALPHA * x_ref[...] + acc, g_ref[...], b_ref[...])


def _mm_ln(y, w, x, g, b, tm):
    m, k = y.shape
    vec = pl.BlockSpec((1, D_MODEL), lambda i: (0, 0))
    return pl.pallas_call(
        _mm_ln_kernel, grid=(m // tm,),
        in_specs=[pl.BlockSpec((tm, k), lambda i: (i, 0)), pl.BlockSpec((k, D_MODEL), lambda i: (0, 0)),
                  pl.BlockSpec((tm, D_MODEL), lambda i: (i, 0)), vec, vec],
        out_specs=pl.BlockSpec((tm, D_MODEL), lambda i: (i, 0)),
        out_shape=jax.ShapeDtypeStruct((m, D_MODEL), F32),
        compiler_params=_params(1), name="outproj_ln",
    )(y, w, x, g, b)


def _mm_ln_pair_kernel(yp_ref, ys_ref, w_ref, xp_ref, xs_ref, g_ref, b_ref, wr_ref, o_ref, route_ref, *, prompt_tiles):
    def run(y_ref, x_ref):
        half = o_ref.shape[0] // 2
        for r0 in (0, half):
            rows = slice(r0, r0 + half)
            acc = jnp.dot(y_ref[rows, :].astype(BF16), w_ref[...], preferred_element_type=F32)
            h = _layer_norm(ALPHA * x_ref[rows, :] + acc, g_ref[...], b_ref[...])
            o_ref[rows, :] = h
            route_ref[rows, :] = _top2_route(jnp.dot(h.astype(BF16), wr_ref[...], preferred_element_type=F32))

    pl.when(pl.program_id(0) < prompt_tiles)(lambda: run(yp_ref, xp_ref))
    pl.when(pl.program_id(0) >= prompt_tiles)(lambda: run(ys_ref, xs_ref))


def _mm_ln_pair(y_p, y_s, w, x_p, x_s, g, b, w_router, tm):
    n_p, k = y_p.shape
    n_s = y_s.shape[0]
    pt = n_p // tm
    first = lambda width: pl.BlockSpec((tm, width), lambda i: (jnp.minimum(i, pt - 1), 0))
    second = lambda width: pl.BlockSpec((tm, width), lambda i: (jnp.maximum(i - pt, 0), 0))
    vec = pl.BlockSpec((1, D_MODEL), lambda i: (0, 0))
    wr = jnp.pad(w_router, ((0, 0), (0, LANES - N_EXPERTS))).astype(BF16)
    return pl.pallas_call(
        functools.partial(_mm_ln_pair_kernel, prompt_tiles=pt), grid=((n_p + n_s) // tm,),
        in_specs=[first(k), second(k), pl.BlockSpec((k, D_MODEL), lambda i: (0, 0)),
                  first(D_MODEL), second(D_MODEL), vec, vec, pl.BlockSpec((D_MODEL, LANES), lambda i: (0, 0))],
        out_specs=[pl.BlockSpec((tm, D_MODEL), lambda i: (i, 0)), pl.BlockSpec((tm, LANES), lambda i: (i, 0))],
        out_shape=[jax.ShapeDtypeStruct((n_p + n_s, D_MODEL), F32),
                   jax.ShapeDtypeStruct((n_p + n_s, LANES), F32)],
        compiler_params=_params(1), name="outproj_ln_route",
    )(y_p, y_s, w, x_p, x_s, g, b, wr)


def _swiglu_chunks(xb, wg_ref, wu_ref, wd_ref, lead, width, chunk):
    acc = None
    for c0 in range(0, width, chunk):
        cs = slice(c0, min(c0 + chunk, width))
        a = jnp.dot(xb, wg_ref[lead + (slice(None), cs)], preferred_element_type=F32)
        u = jnp.dot(xb, wu_ref[lead + (slice(None), cs)], preferred_element_type=F32)
        part = jnp.dot((_silu(a) * u).astype(BF16), wd_ref[lead + (cs, slice(None))], preferred_element_type=F32)
        acc = part if acc is None else acc + part
    return acc


class _CastRider:
    def __init__(self, w, n_steps):
        rows = w.shape[0] * w.shape[1]
        assert rows % n_steps == 0 and (rows // n_steps) % 16 == 0
        self.shape = w.shape
        self.view = w.reshape(n_steps, rows // n_steps, w.shape[2])
        self.spec = pl.BlockSpec((1,) + self.view.shape[1:], lambda i: (i, 0, 0))
        self.out_shape = jax.ShapeDtypeStruct(self.view.shape, BF16)


def _ride(rest):
    if len(rest) == 1:
        return rest[0]
    src_ref, o_ref, dst_ref = rest
    dst_ref[...] = src_ref[...].astype(BF16)
    return o_ref


def _ffn_kernel(x_ref, wg_ref, wu_ref, wd_ref, g_ref, b_ref, *rest, d_ff):
    o_ref = _ride(rest)
    x = x_ref[...]
    acc = _swiglu_chunks(x.astype(BF16), wg_ref, wu_ref, wd_ref, (), d_ff, FF_CHUNK)
    o_ref[...] = _layer_norm(ALPHA * x + acc, g_ref[...], b_ref[...])


def _ffn_dense(x, w_gu, w_down, g, b, tm, ride=None):
    m = x.shape[0]
    d_ff = w_down.shape[0]
    vec = pl.BlockSpec((1, D_MODEL), lambda i: (0, 0))
    rider = _CastRider(ride, m // tm) if ride is not None else None
    out = pl.pallas_call(
        functools.partial(_ffn_kernel, d_ff=d_ff), grid=(m // tm,),
        in_specs=[pl.BlockSpec((tm, D_MODEL), lambda i: (i, 0)),
                  pl.BlockSpec((D_MODEL, d_ff), lambda i: (0, 0)),
                  pl.BlockSpec((D_MODEL, d_ff), lambda i: (0, 1)),
                  pl.BlockSpec((d_ff, D_MODEL), lambda i: (0, 0)), vec, vec] + ([rider.spec] if rider else []),
        out_specs=[pl.BlockSpec((tm, D_MODEL), lambda i: (i, 0))] + ([rider.spec] if rider else []),
        out_shape=[jax.ShapeDtypeStruct((m, D_MODEL), F32)] + ([rider.out_shape] if rider else []),
        compiler_params=_params(1), name="ffn_dense",
    )(x, w_gu, w_gu, w_down, g, b, *([rider.view] if rider else []))
    return (out[0], out[1].reshape(rider.shape)) if rider else out[0]


def _proj_ret_kernel(x_ref, w_ref, cos_ref, sin_ref, *rest, scale_k):
    o_ref = _ride(rest)
    xb = x_ref[...].astype(BF16)
    cos = cos_ref[...]
    sin = sin_ref[...]
    half = RET_DK // 2
    for c in range(w_ref.shape[1] // RET_DK):
        cs = slice(c * RET_DK, (c + 1) * RET_DK)
        acc = jnp.dot(xb, w_ref[:, cs], preferred_element_type=F32)
        if c < 2 * RET_HEADS:
            scale = scale_k if c >= RET_HEADS else 1.0
            x1 = acc[:, :half]
            x2 = acc[:, half:]
            o_ref[:, c * RET_DK:c * RET_DK + half] = ((x1 * cos - x2 * sin) * scale).astype(o_ref.dtype)
            o_ref[:, c * RET_DK + half:(c + 1) * RET_DK] = ((x1 * sin + x2 * cos) * scale).astype(o_ref.dtype)
        else:
            o_ref[:, cs] = acc.astype(o_ref.dtype)


def _proj_ret(x, w, cos, sin, tm, out_dtype, ride=None):
    m = x.shape[0]
    n_cols = w.shape[1]
    pos_tiles = cos.shape[0] // tm
    rider = _CastRider(ride, m // tm) if ride is not None else None
    out = pl.pallas_call(
        functools.partial(_proj_ret_kernel, scale_k=RET_DK ** -0.5),
        grid=(m // tm,),
        in_specs=[pl.BlockSpec((tm, D_MODEL), lambda i: (i, 0)),
                  pl.BlockSpec((D_MODEL, n_cols), lambda i: (0, 0)),
                  pl.BlockSpec((tm, RET_DK // 2), lambda i: (i % pos_tiles, 0)),
                  pl.BlockSpec((tm, RET_DK // 2), lambda i: (i % pos_tiles, 0))] + ([rider.spec] if rider else []),
        out_specs=[pl.BlockSpec((tm, n_cols), lambda i: (i, 0))] + ([rider.spec] if rider else []),
        out_shape=[jax.ShapeDtypeStruct((m, n_cols), out_dtype)] + ([rider.out_shape] if rider else []),
        compiler_params=_params(1), name="proj_ret",
    )(x, w, cos, sin, *([rider.view] if rider else []))
    return (out[0], out[1].reshape(rider.shape)) if rider else out[0]


def _rope_tables(pos):
    half = RET_DK // 2
    inv = 1.0 / (ROPE_BASE ** (jnp.arange(half, dtype=F32) / half))
    ang = pos.astype(F32)[:, None] * inv[None]
    return jnp.cos(ang), jnp.sin(ang)


def _log_gamma():
    return jnp.log(1.0 - 2.0 ** (-5.0 - jnp.arange(RET_HEADS, dtype=F32)))


def _decay_tables(c, rows):
    lg = _log_gamma()
    n = jnp.arange(rows, dtype=F32)
    live = n < c
    diff = n[:, None] - n[None, :]
    decay = jnp.where((diff >= 0)[None] & live[None, None, :],
                      jnp.exp(jnp.maximum(diff, 0.0)[None] * lg[:, None, None]), 0.0)
    q_decay = jnp.exp((n[None, :] + 1.0) * lg[:, None])
    k_decay = jnp.where(live[None], jnp.exp((c - 1.0 - n)[None, :] * lg[:, None]), 0.0)
    chunk_decay = jnp.exp(c * lg)
    return decay, q_decay, k_decay, chunk_decay


def _group_norm_gate(o, gate, gn):
    mu = jnp.mean(o, axis=-1, keepdims=True)
    oc = o - mu
    var = jnp.mean(oc * oc, axis=-1, keepdims=True)
    return _silu(gate) * (oc * lax.rsqrt(var + GN_EPS) * gn)


def _ret_prompt_kernel(q_ref, k_ref, v_ref, gate_ref, dec_ref, qd_ref, kd_ref, cd_ref, gn_ref,
                       y_ref, st_ref, s_ref, *, tb):
    cb = pl.program_id(2)

    @pl.when(cb == 0)
    def _():
        s_ref[...] = jnp.zeros_like(s_ref)

    def body(ci, carry):
        r0 = pl.multiple_of(ci * RET_CHUNK, RET_CHUNK)
        rows = pl.ds(r0, RET_CHUNK)
        for j in range(RET_HPS):
            kc = slice(j * RET_DK, (j + 1) * RET_DK)
            vc = slice(j * RET_DV, (j + 1) * RET_DV)
            q = q_ref[0, rows, kc]
            k = k_ref[0, rows, kc]
            v = v_ref[0, rows, vc]
            state = s_ref[j]
            scores = lax.dot_general(q, k, _NT, preferred_element_type=F32) * dec_ref[j]
            inner = jnp.dot(scores.astype(BF16), v, preferred_element_type=F32)
            cross = jnp.dot(q, state.astype(BF16), preferred_element_type=F32) * qd_ref[j]
            kd = (k.astype(F32) * kd_ref[j]).astype(BF16)
            s_ref[j] = cd_ref[j, 0:1, :] * state + lax.dot_general(kd, v, _TN, preferred_element_type=F32)
            y = _group_norm_gate(inner + cross, gate_ref[0, rows, vc].astype(F32), gn_ref[:, vc])
            y_ref[0, rows, vc] = y.astype(y_ref.dtype)
        return carry

    lax.fori_loop(0, tb // RET_CHUNK, body, 0)

    @pl.when(cb == pl.num_programs(2) - 1)
    def _():
        st_ref[0] = s_ref[...]


def _ret_prompt(proj, gn, bsz, seq, tb):
    decay, q_decay, k_decay, chunk_decay = _decay_tables(RET_CHUNK, RET_CHUNK)
    qd = jnp.broadcast_to(q_decay[:, :, None], (RET_HEADS, RET_CHUNK, RET_DV))
    kd = jnp.broadcast_to(k_decay[:, :, None], (RET_HEADS, RET_CHUNK, RET_DK))
    cd = jnp.broadcast_to(chunk_decay[:, None, None], (RET_HEADS, 8, RET_DV))
    proj = proj.reshape(bsz, seq, proj.shape[-1])
    hps = RET_HPS
    k_off = RET_QK // (hps * RET_DK)
    v_off = 2 * RET_QK // (hps * RET_DV)
    g_off = v_off + RET_HEADS // hps
    per_head = lambda shape: pl.BlockSpec((hps,) + shape, lambda b, h, c: (h, 0, 0))
    y, state = pl.pallas_call(
        functools.partial(_ret_prompt_kernel, tb=tb),
        grid=(bsz, RET_HEADS // hps, seq // tb),
        in_specs=[pl.BlockSpec((1, tb, hps * RET_DK), lambda b, h, c: (b, c, h)),
                  pl.BlockSpec((1, tb, hps * RET_DK), lambda b, h, c: (b, c, k_off + h)),
                  pl.BlockSpec((1, tb, hps * RET_DV), lambda b, h, c: (b, c, v_off + h)),
                  pl.BlockSpec((1, tb, hps * RET_DV), lambda b, h, c: (b, c, g_off + h)),
                  per_head((RET_CHUNK, RET_CHUNK)), per_head((RET_CHUNK, RET_DV)),
                  per_head((RET_CHUNK, RET_DK)), per_head((8, RET_DV)),
                  pl.BlockSpec((1, hps * RET_DV), lambda b, h, c: (0, h))],
        out_specs=[pl.BlockSpec((1, tb, hps * RET_DV), lambda b, h, c: (b, c, h)),
                   pl.BlockSpec((1, hps, RET_DK, RET_DV), lambda b, h, c: (b, h, 0, 0))],
        out_shape=[jax.ShapeDtypeStruct((bsz, seq, RET_V), BF16),
                   jax.ShapeDtypeStruct((bsz, RET_HEADS, RET_DK, RET_DV), F32)],
        scratch_shapes=[pltpu.VMEM((hps, RET_DK, RET_DV), F32)],
        compiler_params=_params(3), name="retention_prompt",
    )(proj, proj, proj, proj, decay, qd, kd, cd, gn)
    return y.reshape(bsz * seq, RET_V), state


def _ret_sample_kernel(p_ref, st_ref, dec_ref, qd_ref, kd_ref, cd_ref, gn_ref, y_ref, ns_ref):
    tp = SAMPLE_PAD
    proj = p_ref[0]
    for h in range(RET_HEADS):
        q = proj[:, h * RET_DK:(h + 1) * RET_DK].astype(BF16)
        k = proj[:, RET_QK + h * RET_DK:RET_QK + (h + 1) * RET_DK]
        v = proj[:, 2 * RET_QK + h * RET_DV:2 * RET_QK + (h + 1) * RET_DV]
        gate = proj[:, 2 * RET_QK + RET_V + h * RET_DV:2 * RET_QK + RET_V + (h + 1) * RET_DV]
        zk = jnp.zeros((LANES - tp, RET_DK), F32)
        zv = jnp.zeros((LANES - tp, RET_DV), F32)
        kp = jnp.concatenate([k, zk], axis=0).astype(BF16)
        kdp = jnp.concatenate([k * kd_ref[h], zk], axis=0).astype(BF16)
        vp = jnp.concatenate([v, zv], axis=0).astype(BF16)
        state = st_ref[0, h]
        scores = lax.dot_general(q, kp, _NT, preferred_element_type=F32) * dec_ref[h]
        inner = jnp.dot(scores.astype(BF16), vp, preferred_element_type=F32)
        cross = jnp.dot(q, state.astype(BF16), preferred_element_type=F32) * qd_ref[h]
        ns_ref[0, h] = cd_ref[h, 0:1, :] * state + lax.dot_general(kdp, vp, _TN, preferred_element_type=F32)
        y = _group_norm_gate(inner + cross, gate, gn_ref[:, h * RET_DV:(h + 1) * RET_DV])
        y_ref[0, :, h * RET_DV:(h + 1) * RET_DV] = y


def _ret_sample(proj, state, gn, t_new):
    bsz = proj.shape[0]
    tp = SAMPLE_PAD
    decay, q_decay, k_decay, chunk_decay = _decay_tables(t_new, tp)
    dec = jnp.pad(decay, ((0, 0), (0, 0), (0, LANES - tp)))
    qd = jnp.broadcast_to(q_decay[:, :, None], (RET_HEADS, tp, RET_DV))
    kd = jnp.broadcast_to(k_decay[:, :, None], (RET_HEADS, tp, RET_DK))
    cd = jnp.broadcast_to(chunk_decay[:, None, None], (RET_HEADS, 8, RET_DV))
    const = lambda a: pl.BlockSpec(a.shape, lambda b: (0,) * a.ndim)
    return pl.pallas_call(
        _ret_sample_kernel, grid=(bsz,),
        in_specs=[pl.BlockSpec((1,) + proj.shape[1:], lambda b: (b, 0, 0)),
                  pl.BlockSpec((1,) + state.shape[1:], lambda b: (b, 0, 0, 0)),
                  const(dec), const(qd), const(kd), const(cd), const(gn)],
        out_specs=[pl.BlockSpec((1, tp, RET_V), lambda b: (b, 0, 0)),
                   pl.BlockSpec((1,) + state.shape[1:], lambda b: (b, 0, 0, 0))],
        out_shape=[jax.ShapeDtypeStruct((bsz, tp, RET_V), F32),
                   jax.ShapeDtypeStruct(state.shape, F32)],
        compiler_params=_params(1), name="retention_sample",
    )(proj, state, dec, qd, kd, cd, gn)


def _top2_route(logits):
    lane = lax.broadcasted_iota(jnp.int32, logits.shape, 1)
    logits = jnp.where(lane < N_EXPERTS, logits, NEG)
    m1 = jnp.max(logits, axis=-1, keepdims=True)
    i1 = jnp.min(jnp.where(logits == m1, lane, LANES), axis=-1, keepdims=True)
    rest = jnp.where(lane == i1, NEG, logits)
    m2 = jnp.max(rest, axis=-1, keepdims=True)
    i2 = jnp.min(jnp.where(rest == m2, lane, LANES), axis=-1, keepdims=True)
    e2 = jnp.exp(m2 - m1)
    den = 1.0 + e2
    return jnp.where(lane == 0, 1.0 / den,
                     jnp.where(lane == 1, e2 / den,
                               jnp.where(lane == 2, i1.astype(F32),
                                         jnp.where(lane == 3, i2.astype(F32), 0.0))))


def _route_plan(e1, e2, tile):
    n_tok = e1.shape[0]
    e = jnp.concatenate([e1, e2])
    onehot = (e[:, None] == jnp.arange(N_EXPERTS, dtype=jnp.int32)[None]).astype(jnp.int32)
    csum = jnp.cumsum(onehot, axis=0)
    rank = jnp.take_along_axis(csum, e[:, None], axis=1)[:, 0] - 1
    tiles_per_expert = (csum[-1] + tile - 1) // tile
    tile_end = jnp.cumsum(tiles_per_expert)
    pos = ((tile_end - tiles_per_expert) * tile)[e] + rank
    n_tiles = (2 * n_tok) // tile + N_EXPERTS
    n_used = tile_end[-1]
    tile_ids = jnp.arange(n_tiles, dtype=jnp.int32)
    tile_expert = jnp.minimum(jnp.searchsorted(tile_end, tile_ids, side="right"), N_EXPERTS - 1)
    tile_expert = jnp.where(tile_ids < n_used, tile_expert, tile_expert[n_used - 1])
    meta = jnp.concatenate([tile_expert, n_used[None], tile_end]).astype(jnp.int32)
    return meta, pos[:n_tok], pos[n_tok:], n_tiles


SUBLANES = 8


def _start_row_gather(idx_ref, src_hbm, dst3, sem):
    def issue(blk, carry):
        for s in range(SUBLANES):
            pltpu.make_async_copy(src_hbm.at[pl.ds(idx_ref[0, 0, blk * SUBLANES + s], 1), :],
                                  dst3.at[blk, pl.ds(s, 1), :], sem).start()
        return carry

    lax.fori_loop(0, dst3.shape[0], issue, 0)


def _start_row_scatter(idx_ref, src3, dst_hbm, sem):
    def issue(blk, carry):
        for s in range(SUBLANES):
            pltpu.make_async_copy(src3.at[blk, pl.ds(s, 1), :],
                                  dst_hbm.at[pl.ds(idx_ref[0, 0, blk * SUBLANES + s], 1), :], sem).start()
        return carry

    lax.fori_loop(0, src3.shape[0], issue, 0)


def _wait_rows(hbm, vmem3, sem):
    rows = vmem3.shape[0] * SUBLANES
    pltpu.make_async_copy(hbm.at[pl.ds(0, rows), :], hbm.at[pl.ds(0, rows), :], sem).wait()


def _dispatch_kernel(meta_ref, p1_ref, p2_ref, x_ref, xs_hbm, zbuf, sem, zsem, *, tm, tile, n_tiles):
    @pl.when(pl.program_id(0) == 0)
    def _():
        zbuf[...] = jnp.zeros_like(zbuf)
        zero_tile = lambda t: pltpu.make_async_copy(zbuf, xs_hbm.at[pl.ds(t * tile, tile), :], zsem)
        for e in range(N_EXPERTS):
            zero_tile(jnp.maximum(meta_ref[n_tiles + 1 + e] - 1, 0)).start()
        for e in range(N_EXPERTS):
            zero_tile(0).wait()

        def zero_unused(t, carry):
            zero_tile(t).start()
            zero_tile(t).wait()
            return carry

        lax.fori_loop(meta_ref[n_tiles], n_tiles, zero_unused, 0)

    _start_row_scatter(p1_ref, x_ref, xs_hbm, sem)
    _start_row_scatter(p2_ref, x_ref, xs_hbm, sem)
    for _ in range(2):
        _wait_rows(xs_hbm, x_ref, sem)


def _dispatch(meta, pos1, pos2, x, tm, tile, n_tiles):
    m = x.shape[0]
    idx = lambda: pl.BlockSpec((1, 1, tm), lambda i, meta: (i, 0, 0), memory_space=pltpu.SMEM)
    return pl.pallas_call(
        functools.partial(_dispatch_kernel, tm=tm, tile=tile, n_tiles=n_tiles),
        grid_spec=pltpu.PrefetchScalarGridSpec(
            num_scalar_prefetch=1, grid=(m // tm,),
            in_specs=[idx(), idx(),
                      pl.BlockSpec((tm // SUBLANES, SUBLANES, D_MODEL), lambda i, meta: (i, 0, 0))],
            out_specs=pl.BlockSpec(memory_space=pl.ANY),
            scratch_shapes=[pltpu.VMEM((tile, D_MODEL), F32),
                            pltpu.SemaphoreType.DMA(()), pltpu.SemaphoreType.DMA(())]),
        out_shape=jax.ShapeDtypeStruct((n_tiles * tile, D_MODEL), F32),
        compiler_params=_params(1), name="moe_dispatch",
    )(meta, pos1.reshape(m // tm, 1, tm), pos2.reshape(m // tm, 1, tm),
      x.reshape(m // SUBLANES, SUBLANES, D_MODEL))


def _expert_kernel(meta_ref, xs_ref, wg_ref, wu_ref, wd_ref, o_ref, xb_ref, *, n_tiles):
    f = pl.program_id(1)
    used = pl.program_id(0) < meta_ref[n_tiles]

    @pl.when(jnp.logical_and(jnp.logical_not(used), f == 0))
    def _():
        o_ref[...] = jnp.zeros_like(o_ref)

    @pl.when(jnp.logical_and(used, f == 0))
    def _():
        xb_ref[...] = xs_ref[...].astype(BF16)

    @pl.when(used)
    def _():
        part = _swiglu_chunks(xb_ref[...], wg_ref, wu_ref, wd_ref, (0,), wd_ref.shape[1], FF_CHUNK)

        @pl.when(f == 0)
        def _():
            o_ref[...] = part

        @pl.when(f > 0)
        def _():
            o_ref[...] += part


def _experts(meta, xs, w_gu, w_down, tile, n_tiles, tf):
    d_ff = w_down.shape[1]
    nf = d_ff // tf
    fcol = lambda j, f, meta: jnp.where(j < meta[n_tiles], f, nf - 1)
    return pl.pallas_call(
        functools.partial(_expert_kernel, n_tiles=n_tiles),
        grid_spec=pltpu.PrefetchScalarGridSpec(
            num_scalar_prefetch=1, grid=(n_tiles, nf),
            in_specs=[pl.BlockSpec((tile, D_MODEL), lambda j, f, meta: (j, 0)),
                      pl.BlockSpec((1, D_MODEL, tf), lambda j, f, meta: (meta[j], 0, fcol(j, f, meta))),
                      pl.BlockSpec((1, D_MODEL, tf), lambda j, f, meta: (meta[j], 0, nf + fcol(j, f, meta))),
                      pl.BlockSpec((1, tf, D_MODEL), lambda j, f, meta: (meta[j], fcol(j, f, meta), 0))],
            out_specs=pl.BlockSpec((tile, D_MODEL), lambda j, f, meta: (j, 0)),
            scratch_shapes=[pltpu.VMEM((tile, D_MODEL), BF16)]),
        out_shape=jax.ShapeDtypeStruct((n_tiles * tile, D_MODEL), F32),
        compiler_params=_params(2), name="moe_experts",
    )(meta, xs, w_gu, w_gu, w_down)


def _combine_kernel(p1_ref, p2_ref, n1_ref, n2_ref, rows_hbm, x_ref, r_ref, g_ref, b_ref, op_ref, os_ref,
                    buf1, buf2, sem1, sem2, *, tm, prompt_tiles):
    i = pl.program_id(0)
    slot = i % 2

    def start(pa_ref, pb_ref, s):
        _start_row_gather(pa_ref, rows_hbm, buf1.at[s], sem1.at[s])
        _start_row_gather(pb_ref, rows_hbm, buf2.at[s], sem2.at[s])

    @pl.when(i == 0)
    def _():
        start(p1_ref, p2_ref, 0)

    @pl.when(i + 1 < pl.num_programs(0))
    def _():
        start(n1_ref, n2_ref, 1 - slot)

    _wait_rows(rows_hbm, buf1.at[slot], sem1.at[slot])
    _wait_rows(rows_hbm, buf2.at[slot], sem2.at[slot])
    route = r_ref[...]
    y = (route[:, 0:1] * buf1[slot].reshape(tm, D_MODEL) + route[:, 1:2] * buf2[slot].reshape(tm, D_MODEL))
    res = _layer_norm(ALPHA * x_ref[...] + y, g_ref[...], b_ref[...])

    @pl.when(i < prompt_tiles)
    def _():
        op_ref[...] = res

    @pl.when(i >= prompt_tiles)
    def _():
        os_ref[...] = res


def _combine(pos1, pos2, rows, x, route, g, b, tm, n_prompt):
    m = x.shape[0]
    pt = n_prompt // tm
    nt = m // tm
    idx = lambda: pl.BlockSpec((1, 1, tm), lambda i: (i, 0, 0), memory_space=pltpu.SMEM)
    nxt = lambda: pl.BlockSpec((1, 1, tm), lambda i: (jnp.minimum(i + 1, nt - 1), 0, 0), memory_space=pltpu.SMEM)
    vec = pl.BlockSpec((1, D_MODEL), lambda i: (0, 0))
    p1 = pos1.reshape(nt, 1, tm)
    p2 = pos2.reshape(nt, 1, tm)
    return pl.pallas_call(
        functools.partial(_combine_kernel, tm=tm, prompt_tiles=pt), grid=(nt,),
        in_specs=[idx(), idx(), nxt(), nxt(), pl.BlockSpec(memory_space=pl.ANY),
                  pl.BlockSpec((tm, D_MODEL), lambda i: (i, 0)),
                  pl.BlockSpec((tm, LANES), lambda i: (i, 0)), vec, vec],
        out_specs=[pl.BlockSpec((tm, D_MODEL), lambda i: (jnp.minimum(i, pt - 1), 0)),
                   pl.BlockSpec((tm, D_MODEL), lambda i: (jnp.maximum(i - pt, 0), 0))],
        out_shape=[jax.ShapeDtypeStruct((n_prompt, D_MODEL), F32),
                   jax.ShapeDtypeStruct((m - n_prompt, D_MODEL), F32)],
        scratch_shapes=[pltpu.VMEM((2, tm // SUBLANES, SUBLANES, D_MODEL), F32),
                        pltpu.VMEM((2, tm // SUBLANES, SUBLANES, D_MODEL), F32),
                        pltpu.SemaphoreType.DMA((2,)), pltpu.SemaphoreType.DMA((2,))],
        compiler_params=_params(1), name="moe_combine",
    )(p1, p2, p1, p2, rows, x, route, g, b)


def _moe(x, route, n_prompt, w_gu, w_down, g, b, tm, tile, tf):
    e1 = route[:, 2].astype(jnp.int32)
    e2 = route[:, 3].astype(jnp.int32)
    meta, pos1, pos2, n_tiles = _route_plan(e1, e2, tile)
    xs = _dispatch(meta, pos1, pos2, x, tm, tile, n_tiles)
    rows = _experts(meta, xs, w_gu, w_down, tile, n_tiles, tf)
    return _combine(pos1, pos2, rows, x, route, g, b, tm, n_prompt)


def kernel(x_prompt, x_sample, cache_kv_w128, cache_kv_w512, cache_kv_w2048, state_ret,
           ln_g, ln_b, rel_bias, w_in_dil, w_out_dil, w_in_ret, ret_gn_g, w_out_ret,
           w_gu_dense, w_down_dense, w_router, w_gu_moe, w_down_moe):
    bsz, seq, _ = x_prompt.shape
    dbsz, t_new, _ = x_sample.shape
    tp = SAMPLE_PAD
    n_p = bsz * seq
    n_s = dbsz * tp
    caches = (cache_kv_w128, cache_kv_w512, cache_kv_w2048)

    w_in_dil_b = w_in_dil.astype(BF16)
    w_out_dil_b = w_out_dil.astype(BF16)
    w_in_ret_b = w_in_ret.astype(BF16)
    w_out_ret_b = w_out_ret.astype(BF16)
    w_gu_dense_b = w_gu_dense.astype(BF16)
    w_down_dense_b = w_down_dense.astype(BF16)
    lng = ln_g.reshape(DEPTH, 2, 1, D_MODEL)
    lnb = ln_b.reshape(DEPTH, 2, 1, D_MODEL)
    gn = ret_gn_g.reshape(1, RET_V)

    hp = x_prompt.reshape(n_p, D_MODEL)
    hs = jnp.pad(x_sample, ((0, 0), (0, tp - t_new), (0, 0))).reshape(n_s, D_MODEL)

    tm = min(1024, seq)
    qkv_groups, (kv128_p, kv512_p, kv2048_p) = _proj_dil_prompt(hp, w_in_dil_b, bsz, seq, tm)
    outs, lses = [], []
    for g, (window, dil) in enumerate(DIL_GROUPS):
        qkv_g = qkv_groups[g].reshape(bsz * dil, seq // dil, G_COLS)
        tbl = _prompt_table(rel_bias, g, window, dil)
        o, l = _attn_prompt(qkv_g, tbl, min(512, seq // dil))
        outs.append(o.reshape(bsz, dil, seq // dil, A_WIDTH))
        lses.append(l.reshape(bsz, dil, seq // dil, A_WIDTH))
    hp = _merge_out(outs, lses, w_out_dil_b, hp, lng[0, 0], lnb[0, 0], 512)

    qkv_s = _matmul(hs, w_in_dil_b, G_COLS)
    qkv_s3 = qkv_s.reshape(dbsz, tp, N_GROUPS * G_COLS)
    tabs = [_sample_tables(rel_bias, g, window, dil, caches[g].shape[1], t_new)
            for g, (window, dil) in enumerate(DIL_GROUPS)]
    q_t = qkv_s.reshape(dbsz, tp, N_GROUPS, 3, H_SLOT, HEAD_DIM).transpose(0, 2, 3, 4, 5, 1)
    caches_t = [c.transpose(0, 2, 3, 4, 1) for c in caches]
    mixed_t = _attn_sample(q_t, caches_t, [t[0] for t in tabs], [t[1] for t in tabs], t_new)
    mixed_s = mixed_t.transpose(0, 3, 1, 2)
    hs = _mm_ln(mixed_s.reshape(n_s, A_WIDTH), w_out_dil_b, hs, lng[0, 0], lnb[0, 0], n_s)
    rows_s = []
    for g in range(N_GROUPS):
        kv = qkv_s3[:, :t_new, g * G_COLS + A_WIDTH:(g + 1) * G_COLS]
        rows_s.append(kv.reshape(dbsz, t_new, 2, H_SLOT, HEAD_DIM))

    hp, w_down_moe_b = _ffn_dense(hp, w_gu_dense_b, w_down_dense_b, lng[0, 1], lnb[0, 1], 512, ride=w_down_moe)
    hs = _ffn_dense(hs, w_gu_dense_b, w_down_dense_b, lng[0, 1], lnb[0, 1], n_s)

    cos_p, sin_p = _rope_tables(jnp.arange(seq, dtype=jnp.int32))
    pos_s = jnp.tile(PAST_LEN + jnp.arange(tp, dtype=jnp.int32), dbsz)
    cos_s, sin_s = _rope_tables(pos_s)
    proj_p, w_gu_moe_b = _proj_ret(hp, w_in_ret_b, cos_p, sin_p, 512, BF16, ride=w_gu_moe)
    y_p, ret_p = _ret_prompt(proj_p, gn, bsz, seq, min(1024, seq))
    proj_s = _proj_ret(hs, w_in_ret_b, cos_s, sin_s, n_s // 2, F32)
    y_s, ret_s = _ret_sample(proj_s.reshape(dbsz, tp, -1), state_ret, gn, t_new)
    h_all, route = _mm_ln_pair(y_p, y_s.reshape(n_s, RET_V), w_out_ret_b, hp, hs, lng[1, 0], lnb[1, 0],
                               w_router, 512)

    out_p, out_s = _moe(h_all, route, n_p, w_gu_moe_b, w_down_moe_b, lng[1, 1], lnb[1, 1], 512, 1024,
                        w_down_moe.shape[1] // 2)

    y_prompt = out_p.reshape(bsz, seq, D_MODEL)
    y_sample = out_s.reshape(dbsz, tp, D_MODEL)[:, :t_new]
    shape5 = lambda a: a.reshape(a.shape[0], a.shape[1], 2, H_SLOT, HEAD_DIM)
    return (y_prompt, y_sample, shape5(kv128_p), shape5(kv512_p), shape5(kv2048_p), ret_p,
            rows_s[0], rows_s[1], rows_s[2], ret_s)
```

```python
import functools

import jax
import jax.numpy as jnp
import numpy as np
from jax import lax
from jax.experimental import pallas as pl
from jax.experimental.pallas import tpu as pltpu

F32 = jnp.float32
BF16 = jnp.bfloat16

DEPTH = 2
D_MODEL = 1024
PAST_LEN = 16384
DIL_GROUPS = ((128, 1), (512, 4), (2048, 16))
N_GROUPS = 3
H_SLOT = 8
HEAD_DIM = 64
A_WIDTH = H_SLOT * HEAD_DIM
G_COLS = 3 * A_WIDTH
N_BUCKETS = 32
MAX_DISTANCE = 2048
RET_HEADS = 4
RET_DK = 256
RET_DV = 512
RET_CHUNK = 128
RET_HPS = 4
ROPE_BASE = 10000.0
RET_QK = RET_HEADS * RET_DK
RET_V = RET_HEADS * RET_DV
N_EXPERTS = 8
LN_EPS = 1e-5
GN_EPS = 1e-5
ALPHA = (2 * DEPTH) ** 0.25
NEG = -1e30

LANES = 128
ATT_BLOCK = 128
PERM = 256
FF_CHUNK = 512
SAMPLE_PAD = 16
VMEM_LIMIT = 56 * 1024 * 1024

_NT = (((1,), (1,)), ((), ()))
_TN = (((0,), (0,)), ((), ()))


def _params(n_grid):
    return pltpu.CompilerParams(dimension_semantics=("arbitrary",) * n_grid,
                                vmem_limit_bytes=VMEM_LIMIT)


def _layer_norm(z, g, b):
    mu = jnp.mean(z, axis=-1, keepdims=True)
    zc = z - mu
    var = jnp.mean(zc * zc, axis=-1, keepdims=True)
    return zc * lax.rsqrt(var + LN_EPS) * g + b


def _silu(a):
    return a / (1.0 + jnp.exp(-a))


def _t5_bucket(dist):
    max_exact = N_BUCKETS // 2
    d = np.asarray(dist, dtype=np.int64)
    scaled = np.log(np.maximum(d, 1) / max_exact) / np.log(MAX_DISTANCE / max_exact)
    large = np.minimum(max_exact + (scaled * (N_BUCKETS - max_exact)).astype(np.int32), N_BUCKETS - 1)
    return np.where(d < max_exact, d, large).astype(np.int32)


def _group_bias(rel_bias, g, window, dil):
    n_keys = window // dil + 1
    buckets = jnp.asarray(_t5_bucket(np.arange(n_keys) * dil))
    return rel_bias[buckets][:, g * H_SLOT:(g + 1) * H_SLOT].T.astype(F32)


def _matmul_kernel(x_ref, w_ref, o_ref):
    o_ref[...] = jnp.dot(x_ref[...].astype(BF16), w_ref[...], preferred_element_type=F32)


def _matmul(x, w, tn):
    m, k = x.shape
    n = w.shape[1]
    return pl.pallas_call(
        _matmul_kernel,
        grid=(n // tn,),
        in_specs=[pl.BlockSpec((m, k), lambda j: (0, 0)),
                  pl.BlockSpec((k, tn), lambda j: (0, j))],
        out_specs=pl.BlockSpec((m, tn), lambda j: (0, j)),
        out_shape=jax.ShapeDtypeStruct((m, n), F32),
        compiler_params=_params(1),
        name="matmul_sample",
    )(x, w)


def _deinterleave_matrix(dil):
    p = np.zeros((PERM, PERM), np.float32)
    rows = PERM // dil
    for r in range(dil):
        for m in range(rows):
            p[r * rows + m, m * dil + r] = 1.0
    return p


def _proj_dil_kernel(x_ref, w_ref, p1_ref, p2_ref, q0_ref, q1_ref, q2_ref, kv0_ref, kv1_ref, kv2_ref, xperm, xb_ref,
                     *, tm, tpb, keeps):
    g = pl.program_id(1)
    j = pl.program_id(0) % tpb

    @pl.when(g == 0)
    def _():
        xb_ref[...] = x_ref[...].astype(BF16)
    q_refs = (q0_ref, q1_ref, q2_ref)
    kv_refs = (kv0_ref, kv1_ref, kv2_ref)
    perms = (None, p1_ref, p2_ref)

    def store_q(ref, r, row0, n_rows, acc, src0):
        ref[0, r, row0:row0 + n_rows, :A_WIDTH] = (acc[src0:src0 + n_rows, :A_WIDTH] * (HEAD_DIM ** -0.5)).astype(BF16)
        ref[0, r, row0:row0 + n_rows, A_WIDTH:] = acc[src0:src0 + n_rows, A_WIDTH:].astype(BF16)

    for gi, (_, dil) in enumerate(DIL_GROUPS):
        keep = keeps[gi]
        in_tail = (j >= tpb - keep // tm) if keep >= tm else (j == tpb - 1)
        row0 = 0 if keep >= tm else tm - keep

        @pl.when(g == gi)
        def _(gi=gi, dil=dil, in_tail=in_tail, row0=row0):
            xb = xb_ref[...]
            if dil == 1:
                acc = jnp.dot(xb, w_ref[...], preferred_element_type=F32)
                store_q(q_refs[gi], 0, 0, tm, acc, 0)
            else:
                for sub in range(tm // PERM):
                    blk = xb[sub * PERM:(sub + 1) * PERM]
                    xperm[sub * PERM:(sub + 1) * PERM, :] = jnp.dot(
                        perms[gi][...], blk, preferred_element_type=F32).astype(BF16)
                acc = jnp.dot(xperm[...], w_ref[...], preferred_element_type=F32)
                rows = PERM // dil
                for sub in range(tm // PERM):
                    for r in range(dil):
                        store_q(q_refs[gi], r, sub * rows, rows, acc, sub * PERM + r * rows)

            @pl.when(in_tail)
            def _():
                if dil == 1:
                    kv_refs[gi][0] = acc[row0:, A_WIDTH:]
                else:
                    kv_refs[gi][0] = jnp.dot(xb[row0:], w_ref[:, A_WIDTH:], preferred_element_type=F32)


def _proj_dil_prompt(x, w, bsz, seq, tm):
    tpb = seq // tm
    keeps = tuple(min(wd, seq) for wd, _ in DIL_GROUPS)

    def kv_spec(keep):
        if keep >= tm:
            first = tpb - keep // tm
            return pl.BlockSpec((1, tm, 2 * A_WIDTH),
                                lambda i, g: (i // tpb, jnp.maximum(i % tpb - first, 0), 0))
        return pl.BlockSpec((1, keep, 2 * A_WIDTH), lambda i, g: (i // tpb, 0, 0))

    perm = lambda d: jnp.asarray(_deinterleave_matrix(d), BF16)
    const = pl.BlockSpec((PERM, PERM), lambda i, g: (0, 0))
    outs = pl.pallas_call(
        functools.partial(_proj_dil_kernel, tm=tm, tpb=tpb, keeps=keeps),
        grid=(bsz * tpb, N_GROUPS),
        in_specs=[pl.BlockSpec((tm, D_MODEL), lambda i, g: (i, 0)),
                  pl.BlockSpec((D_MODEL, G_COLS), lambda i, g: (0, g)), const, const],
        out_specs=[pl.BlockSpec((1, d, tm // d, G_COLS), lambda i, g: (i // tpb, 0, i % tpb, 0))
                   for _, d in DIL_GROUPS] + [kv_spec(k) for k in keeps],
        out_shape=[jax.ShapeDtypeStruct((bsz, d, seq // d, G_COLS), BF16) for _, d in DIL_GROUPS]
        + [jax.ShapeDtypeStruct((bsz, k, 2 * A_WIDTH), F32) for k in keeps],
        scratch_shapes=[pltpu.VMEM((tm, D_MODEL), BF16), pltpu.VMEM((tm, D_MODEL), BF16)],
        compiler_params=_params(2),
        name="proj_dil_prompt",
    )(x, w, perm(DIL_GROUPS[1][1]), perm(DIL_GROUPS[2][1]))
    return outs[:3], outs[3:]


def _attn_prompt_kernel(q_ref, k_ref, v_ref, kp_ref, vp_ref, tbl_ref, o_ref, lse_ref, kbuf, vbuf, *, tq):
    n = pl.program_id(1)
    kbuf[0:ATT_BLOCK, :] = kp_ref[0]
    kbuf[ATT_BLOCK:, :] = k_ref[0]
    vbuf[0:ATT_BLOCK, :] = vp_ref[0]
    vbuf[ATT_BLOCK:, :] = v_ref[0]
    low = lax.broadcasted_iota(jnp.int32, (ATT_BLOCK, LANES), 1) < HEAD_DIM

    def body(m, carry):
        r0 = pl.multiple_of(m * ATT_BLOCK, ATT_BLOCK)
        first = jnp.where(jnp.logical_and(n == 0, m == 0), 1, 0)
        for p in range(A_WIDTH // LANES):
            cols = slice(p * LANES, (p + 1) * LANES)
            qm = q_ref[0, pl.ds(r0, ATT_BLOCK), cols]
            keys = kbuf[pl.ds(r0, 2 * ATT_BLOCK), cols]
            vals = vbuf[pl.ds(r0, 2 * ATT_BLOCK), cols]
            outs, lses = [], []
            for a in range(2):
                qa = jnp.where(low if a == 0 else jnp.logical_not(low), qm, jnp.zeros_like(qm))
                s = lax.dot_general(qa, keys, _NT, preferred_element_type=F32) + tbl_ref[first, 2 * p + a]
                mx = jnp.max(s, axis=-1, keepdims=True)
                e = jnp.exp(s - mx)
                l = jnp.sum(e, axis=-1, keepdims=True)
                o = jnp.dot(e.astype(BF16), vals, preferred_element_type=F32)
                outs.append(o / l)
                lses.append(jnp.broadcast_to(mx + jnp.log(l), (ATT_BLOCK, LANES)))
            o_ref[0, pl.ds(r0, ATT_BLOCK), cols] = jnp.where(low, outs[0], outs[1]).astype(BF16)
            lse_ref[0, pl.ds(r0, ATT_BLOCK), cols] = jnp.where(low, lses[0], lses[1])
        return carry

    lax.fori_loop(0, tq // ATT_BLOCK, body, 0)


def _attn_prompt(qkv, tbl, tq):
    nb, length, _ = qkv.shape
    sub = tq // ATT_BLOCK
    cur = lambda c: pl.BlockSpec((1, tq, A_WIDTH), lambda s, n: (s, n, c))
    prev = lambda c: pl.BlockSpec((1, ATT_BLOCK, A_WIDTH), lambda s, n: (s, jnp.maximum(n * sub - 1, 0), c))
    return pl.pallas_call(
        functools.partial(_attn_prompt_kernel, tq=tq),
        grid=(nb, length // tq),
        in_specs=[cur(0), cur(1), cur(2), prev(1), prev(2),
                  pl.BlockSpec(tbl.shape, lambda s, n: (0, 0, 0, 0))],
        out_specs=[pl.BlockSpec((1, tq, A_WIDTH), lambda s, n: (s, n, 0))] * 2,
        out_shape=[jax.ShapeDtypeStruct((nb, length, A_WIDTH), BF16),
                   jax.ShapeDtypeStruct((nb, length, A_WIDTH), F32)],
        scratch_shapes=[pltpu.VMEM((tq + ATT_BLOCK, A_WIDTH), BF16)] * 2,
        compiler_params=_params(2),
        name="attn_prompt",
    )(qkv, qkv, qkv, qkv, qkv, tbl)


def _toeplitz(u, n_rows, n_cols):
    h, n = u.shape
    assert n == n_rows + n_cols - 1
    up = jnp.pad(u, ((0, 0), (0, 1)))
    w = jnp.tile(up, (1, n_rows))[:, :n_rows * n].reshape(h, n_rows, n)
    return w[:, :, n_rows - 1:n_rows - 1 + n_cols]


def _bias_by_offset(bias, dist, valid):
    return jnp.where(valid[None], bias[:, np.clip(dist, 0, bias.shape[1] - 1)], NEG)


def _prompt_table(rel_bias, g, window, dil):
    bias = _group_bias(rel_bias, g, window, dil)
    k = np.arange(3 * ATT_BLOCK - 1)
    dist = 2 * ATT_BLOCK - 1 - k
    u = _bias_by_offset(bias, dist, (dist >= 0) & (dist <= window // dil))
    tbl = _toeplitz(u, ATT_BLOCK, 2 * ATT_BLOCK)
    c = np.arange(2 * ATT_BLOCK)[None, None, :]
    tbl_first = jnp.where(c < ATT_BLOCK, NEG, tbl)
    return jnp.stack([tbl, tbl_first], 0)


def _split3(x):
    hi = x.astype(BF16)
    r1 = x - hi.astype(F32)
    mid = r1.astype(BF16)
    lo = (r1 - mid.astype(F32)).astype(BF16)
    return hi, mid, lo


def _merge_out_kernel(o0, o1, o2, l0, l1, l2, pt1_ref, pt2_ref, w_ref, x_ref, g_ref, b_ref, out_ref, *, tm):
    pts = (None, pt1_ref, pt2_ref)

    def natural(ref, gi, exact_f32):
        dil = DIL_GROUPS[gi][1]
        if dil == 1:
            return ref[0, 0].astype(F32)
        rows = PERM // dil
        blocks = []
        for sub in range(tm // PERM):
            piece = jnp.concatenate([ref[0, r, sub * rows:(sub + 1) * rows, :] for r in range(dil)], axis=0)
            parts = _split3(piece) if exact_f32 else (piece,)
            blocks.append(sum(jnp.dot(pts[gi][...], p, preferred_element_type=F32) for p in parts))
        return jnp.concatenate(blocks, axis=0)

    os_ = [natural(r, gi, False) for gi, r in enumerate((o0, o1, o2))]
    ls = [natural(r, gi, True) for gi, r in enumerate((l0, l1, l2))]
    mx = jnp.maximum(jnp.maximum(ls[0], ls[1]), ls[2])
    es = [jnp.exp(l - mx) for l in ls]
    mixed = (es[0] * os_[0] + es[1] * os_[1] + es[2] * os_[2]) / (es[0] + es[1] + es[2])
    acc = jnp.dot(mixed.astype(BF16), w_ref[...], preferred_element_type=F32)
    out_ref[...] = _layer_norm(ALPHA * x_ref[...] + acc, g_ref[...], b_ref[...])


def _merge_out(os_, ls_, w, x, g, b, tm):
    bsz, _, seq, _ = os_[0].shape
    tpb = seq // tm
    grp = lambda d: pl.BlockSpec((1, d, tm // d, A_WIDTH), lambda i: (i // tpb, 0, i % tpb, 0))
    specs = [grp(d) for _, d in DIL_GROUPS]
    pt = lambda d: jnp.asarray(_deinterleave_matrix(d).T, BF16)
    const = pl.BlockSpec((PERM, PERM), lambda i: (0, 0))
    vec = pl.BlockSpec((1, D_MODEL), lambda i: (0, 0))
    return pl.pallas_call(
        functools.partial(_merge_out_kernel, tm=tm), grid=(bsz * tpb,),
        in_specs=specs + specs + [const, const, pl.BlockSpec((A_WIDTH, D_MODEL), lambda i: (0, 0)),
                                  pl.BlockSpec((tm, D_MODEL), lambda i: (i, 0)), vec, vec],
        out_specs=pl.BlockSpec((tm, D_MODEL), lambda i: (i, 0)),
        out_shape=jax.ShapeDtypeStruct((bsz * seq, D_MODEL), F32),
        compiler_params=_params(1), name="merge_outproj_ln",
    )(*os_, *ls_, pt(DIL_GROUPS[1][1]), pt(DIL_GROUPS[2][1]), w, x, g, b)


def _attn_sample_kernel(q_ref, c0_ref, c1_ref, c2_ref, tb0, tb1, tb2, tn0, tn1, tn2, out_ref, *scratch, t_new):
    tp = SAMPLE_PAD
    lane_t = lax.broadcasted_iota(jnp.int32, (HEAD_DIM, tp), 1)
    lane_k = lax.broadcasted_iota(jnp.int32, (HEAD_DIM, LANES), 1)
    qcol = lambda g, h, t: q_ref[0, g, 0, h][:, t:t + 1] * (HEAD_DIM ** -0.5)
    sets = []
    for g, (c_ref, tb, tn) in enumerate(((c0_ref, tb0, tn0), (c1_ref, tb1, tn1), (c2_ref, tb2, tn2))):
        dil = DIL_GROUPS[g][1]
        sets.append((lambda h, c_ref=c_ref: c_ref[0, 0, h], lambda h, c_ref=c_ref: c_ref[0, 1, h], tb, g,
                     dil if dil >= t_new else 0))
        sets.append((lambda h, g=g: q_ref[0, g, 1, h], lambda h, g=g: q_ref[0, g, 2, h], tn, g, 0))
    s_refs = scratch[:len(sets)]
    p_refs = dict(zip([i for i, st in enumerate(sets) if st[4]], scratch[len(sets):]))
    out_ref[0] = jnp.zeros(out_ref.shape[1:], F32)

    for i, p_ref in p_refs.items():
        keys, _, _, g, dil = sets[i]
        p_ref[...] = jnp.zeros_like(p_ref)
        for h in range(H_SLOT):
            qsel = jnp.zeros((HEAD_DIM, LANES), F32)
            for t in range(t_new):
                qsel = jnp.where(lane_k % dil == t, qcol(g, h, t), qsel)
            kt = keys(h)
            s_refs[i][h:h + 1, :] = jnp.sum(kt * jnp.tile(qsel, (1, kt.shape[1] // LANES)), axis=0, keepdims=True)

    for t in range(t_new):
        for h in range(H_SLOT):
            for (keys, _, _, g, dil), s_ref in zip(sets, s_refs):
                if not dil:
                    s_ref[h:h + 1, :] = jnp.sum(keys(h) * qcol(g, h, t), axis=0, keepdims=True)
        scores = [s_ref[...] + st[2][t] for st, s_ref in zip(sets, s_refs)]
        mx = functools.reduce(jnp.maximum, [jnp.max(s, axis=-1, keepdims=True) for s in scores])
        es = [jnp.exp(s - mx) for s in scores]
        inv = 1.0 / functools.reduce(jnp.add, [jnp.sum(e, axis=-1, keepdims=True) for e in es])
        for i, (e, s_ref) in enumerate(zip(es, s_refs)):
            if i in p_refs:
                p_refs[i][...] += e * inv
            else:
                s_ref[...] = e
        for h in range(H_SLOT):
            by_width = {}
            for (_, values, _, _, dil), s_ref in zip(sets, s_refs):
                if not dil:
                    prod = values(h) * s_ref[h:h + 1, :]
                    by_width[prod.shape[1]] = by_width[prod.shape[1]] + prod if prod.shape[1] in by_width else prod
            col = functools.reduce(jnp.add, [jnp.sum(p, axis=-1, keepdims=True) for p in by_width.values()])
            out_ref[0, h] = jnp.where(lane_t == t, col * inv[h:h + 1, :], out_ref[0, h])

    for i, p_ref in p_refs.items():
        _, values, _, _, dil = sets[i]
        for h in range(H_SLOT):
            prod = values(h) * p_ref[h:h + 1, :]
            folded = functools.reduce(jnp.add, [prod[:, j:j + LANES] for j in range(0, prod.shape[1], LANES)])
            acc = out_ref[0, h]
            for t in range(t_new):
                col = jnp.sum(jnp.where(lane_k % dil == t, folded, 0.0), axis=-1, keepdims=True)
                acc = acc + jnp.where(lane_t == t, col, 0.0)
            out_ref[0, h] = acc


def _sample_tables(rel_bias, g, window, dil, length, t_new):
    bias = _group_bias(rel_bias, g, window, dil)
    n_keys = window // dil + 1
    tp = SAMPLE_PAD
    k = np.arange(tp + length - 1)
    dist = length + tp - 1 - k
    u = _bias_by_offset(bias, dist // dil, (dist % dil == 0) & (dist // dil < n_keys))
    tc = _toeplitz(u, tp, length)
    k2 = np.arange(2 * tp - 1)
    dist2 = tp - 1 - k2
    u2 = _bias_by_offset(bias, dist2 // dil, (dist2 >= 0) & (dist2 % dil == 0) & (dist2 // dil < n_keys))
    live_col = (np.arange(tp) < t_new)[None, None, :]
    tn = jnp.where(live_col, _toeplitz(u2, tp, tp), NEG)
    return tc.transpose(1, 0, 2)[:t_new], tn.transpose(1, 0, 2)[:t_new]


def _attn_sample(q_t, caches_t, tables_c, tables_n, t_new):
    bsz = q_t.shape[0]
    tp = SAMPLE_PAD
    const = lambda a: pl.BlockSpec(a.shape, lambda b: (0, 0, 0))
    scratch = []
    for c in caches_t:
        scratch += [pltpu.VMEM((H_SLOT, c.shape[-1]), F32), pltpu.VMEM((H_SLOT, tp), F32)]
    for c, (_, dil) in zip(caches_t, DIL_GROUPS):
        if dil >= t_new:
            assert c.shape[-1] % dil == 0 and c.shape[-1] % LANES == 0 and LANES % dil == 0
            scratch.append(pltpu.VMEM((H_SLOT, c.shape[-1]), F32))
    return pl.pallas_call(
        functools.partial(_attn_sample_kernel, t_new=t_new),
        grid=(bsz,),
        in_specs=[pl.BlockSpec((1,) + q_t.shape[1:], lambda b: (b, 0, 0, 0, 0, 0))]
        + [pl.BlockSpec((1,) + c.shape[1:], lambda b: (b, 0, 0, 0, 0)) for c in caches_t]
        + [const(t) for t in tables_c] + [const(t) for t in tables_n],
        out_specs=pl.BlockSpec((1, H_SLOT, HEAD_DIM, tp), lambda b: (b, 0, 0, 0)),
        out_shape=jax.ShapeDtypeStruct((bsz, H_SLOT, HEAD_DIM, tp), F32),
        scratch_shapes=scratch,
        compiler_params=_params(1),
        name="attn_sample",
    )(q_t, *caches_t, *tables_c, *tables_n)


def _mm_ln_kernel(y_ref, w_ref, x_ref, g_ref, b_ref, o_ref):
    acc = jnp.dot(y_ref[...].astype(BF16), w_ref[...], preferred_element_type=F32)
    o_ref[...] = _layer_norm(ALPHA * x_ref[...] + acc, g_ref[...], b_ref[...])


def _mm_ln(y, w, x, g, b, tm):
    m, k = y.shape
    vec = pl.BlockSpec((1, D_MODEL), lambda i: (0, 0))
    return pl.pallas_call(
        _mm_ln_kernel, grid=(m // tm,),
        in_specs=[pl.BlockSpec((tm, k), lambda i: (i, 0)), pl.BlockSpec((k, D_MODEL), lambda i: (0, 0)),
                  pl.BlockSpec((tm, D_MODEL), lambda i: (i, 0)), vec, vec],
        out_specs=pl.BlockSpec((tm, D_MODEL), lambda i: (i, 0)),
        out_shape=jax.ShapeDtypeStruct((m, D_MODEL), F32),
        compiler_params=_params(1), name="outproj_ln",
    )(y, w, x, g, b)


def _mm_ln_pair_kernel(yp_ref, ys_ref, w_ref, xp_ref, xs_ref, g_ref, b_ref, wr_ref, o_ref, route_ref, *, prompt_tiles):
    def run(y_ref, x_ref):
        half = o_ref.shape[0] // 2
        for r0 in (0, half):
            rows = slice(r0, r0 + half)
            acc = jnp.dot(y_ref[rows, :].astype(BF16), w_ref[...], preferred_element_type=F32)
            h = _layer_norm(ALPHA * x_ref[rows, :] + acc, g_ref[...], b_ref[...])
            o_ref[rows, :] = h
            route_ref[rows, :] = _top2_route(jnp.dot(h.astype(BF16), wr_ref[...], preferred_element_type=F32))

    pl.when(pl.program_id(0) < prompt_tiles)(lambda: run(yp_ref, xp_ref))
    pl.when(pl.program_id(0) >= prompt_tiles)(lambda: run(ys_ref, xs_ref))


def _mm_ln_pair(y_p, y_s, w, x_p, x_s, g, b, w_router, tm):
    n_p, k = y_p.shape
    n_s = y_s.shape[0]
    pt = n_p // tm
    first = lambda width: pl.BlockSpec((tm, width), lambda i: (jnp.minimum(i, pt - 1), 0))
    second = lambda width: pl.BlockSpec((tm, width), lambda i: (jnp.maximum(i - pt, 0), 0))
    vec = pl.BlockSpec((1, D_MODEL), lambda i: (0, 0))
    wr = jnp.pad(w_router, ((0, 0), (0, LANES - N_EXPERTS))).astype(BF16)
    return pl.pallas_call(
        functools.partial(_mm_ln_pair_kernel, prompt_tiles=pt), grid=((n_p + n_s) // tm,),
        in_specs=[first(k), second(k), pl.BlockSpec((k, D_MODEL), lambda i: (0, 0)),
                  first(D_MODEL), second(D_MODEL), vec, vec, pl.BlockSpec((D_MODEL, LANES), lambda i: (0, 0))],
        out_specs=[pl.BlockSpec((tm, D_MODEL), lambda i: (i, 0)), pl.BlockSpec((tm, LANES), lambda i: (i, 0))],
        out_shape=[jax.ShapeDtypeStruct((n_p + n_s, D_MODEL), F32),
                   jax.ShapeDtypeStruct((n_p + n_s, LANES), F32)],
        compiler_params=_params(1), name="outproj_ln_route",
    )(y_p, y_s, w, x_p, x_s, g, b, wr)


def _swiglu_chunks(xb, wg_ref, wu_ref, wd_ref, lead, width, chunk):
    acc = None
    for c0 in range(0, width, chunk):
        cs = slice(c0, min(c0 + chunk, width))
        a = jnp.dot(xb, wg_ref[lead + (slice(None), cs)], preferred_element_type=F32)
        u = jnp.dot(xb, wu_ref[lead + (slice(None), cs)], preferred_element_type=F32)
        part = jnp.dot((_silu(a) * u).astype(BF16), wd_ref[lead + (cs, slice(None))], preferred_element_type=F32)
        acc = part if acc is None else acc + part
    return acc


class _CastRider:
    def __init__(self, w, n_steps):
        rows = w.shape[0] * w.shape[1]
        assert rows % n_steps == 0 and (rows // n_steps) % 16 == 0
        self.shape = w.shape
        self.view = w.reshape(n_steps, rows // n_steps, w.shape[2])
        self.spec = pl.BlockSpec((1,) + self.view.shape[1:], lambda i: (i, 0, 0))
        self.out_shape = jax.ShapeDtypeStruct(self.view.shape, BF16)


def _ride(rest):
    if len(rest) == 1:
        return rest[0]
    src_ref, o_ref, dst_ref = rest
    dst_ref[...] = src_ref[...].astype(BF16)
    return o_ref


def _ffn_kernel(x_ref, wg_ref, wu_ref, wd_ref, g_ref, b_ref, *rest, d_ff):
    o_ref = _ride(rest)
    x = x_ref[...]
    acc = _swiglu_chunks(x.astype(BF16), wg_ref, wu_ref, wd_ref, (), d_ff, FF_CHUNK)
    o_ref[...] = _layer_norm(ALPHA * x + acc, g_ref[...], b_ref[...])


def _ffn_dense(x, w_gu, w_down, g, b, tm, ride=None):
    m = x.shape[0]
    d_ff = w_down.shape[0]
    vec = pl.BlockSpec((1, D_MODEL), lambda i: (0, 0))
    rider = _CastRider(ride, m // tm) if ride is not None else None
    out = pl.pallas_call(
        functools.partial(_ffn_kernel, d_ff=d_ff), grid=(m // tm,),
        in_specs=[pl.BlockSpec((tm, D_MODEL), lambda i: (i, 0)),
                  pl.BlockSpec((D_MODEL, d_ff), lambda i: (0, 0)),
                  pl.BlockSpec((D_MODEL, d_ff), lambda i: (0, 1)),
                  pl.BlockSpec((d_ff, D_MODEL), lambda i: (0, 0)), vec, vec] + ([rider.spec] if rider else []),
        out_specs=[pl.BlockSpec((tm, D_MODEL), lambda i: (i, 0))] + ([rider.spec] if rider else []),
        out_shape=[jax.ShapeDtypeStruct((m, D_MODEL), F32)] + ([rider.out_shape] if rider else []),
        compiler_params=_params(1), name="ffn_dense",
    )(x, w_gu, w_gu, w_down, g, b, *([rider.view] if rider else []))
    return (out[0], out[1].reshape(rider.shape)) if rider else out[0]


def _proj_ret_kernel(x_ref, w_ref, cos_ref, sin_ref, *rest, scale_k):
    o_ref = _ride(rest)
    xb = x_ref[...].astype(BF16)
    cos = cos_ref[...]
    sin = sin_ref[...]
    half = RET_DK // 2
    for c in range(w_ref.shape[1] // RET_DK):
        cs = slice(c * RET_DK, (c + 1) * RET_DK)
        acc = jnp.dot(xb, w_ref[:, cs], preferred_element_type=F32)
        if c < 2 * RET_HEADS:
            scale = scale_k if c >= RET_HEADS else 1.0
            x1 = acc[:, :half]
            x2 = acc[:, half:]
            o_ref[:, c * RET_DK:c * RET_DK + half] = ((x1 * cos - x2 * sin) * scale).astype(o_ref.dtype)
            o_ref[:, c * RET_DK + half:(c + 1) * RET_DK] = ((x1 * sin + x2 * cos) * scale).astype(o_ref.dtype)
        else:
            o_ref[:, cs] = acc.astype(o_ref.dtype)


def _proj_ret(x, w, cos, sin, tm, out_dtype, ride=None):
    m = x.shape[0]
    n_cols = w.shape[1]
    pos_tiles = cos.shape[0] // tm
    rider = _CastRider(ride, m // tm) if ride is not None else None
    out = pl.pallas_call(
        functools.partial(_proj_ret_kernel, scale_k=RET_DK ** -0.5),
        grid=(m // tm,),
        in_specs=[pl.BlockSpec((tm, D_MODEL), lambda i: (i, 0)),
                  pl.BlockSpec((D_MODEL, n_cols), lambda i: (0, 0)),
                  pl.BlockSpec((tm, RET_DK // 2), lambda i: (i % pos_tiles, 0)),
                  pl.BlockSpec((tm, RET_DK // 2), lambda i: (i % pos_tiles, 0))] + ([rider.spec] if rider else []),
        out_specs=[pl.BlockSpec((tm, n_cols), lambda i: (i, 0))] + ([rider.spec] if rider else []),
        out_shape=[jax.ShapeDtypeStruct((m, n_cols), out_dtype)] + ([rider.out_shape] if rider else []),
        compiler_params=_params(1), name="proj_ret",
    )(x, w, cos, sin, *([rider.view] if rider else []))
    return (out[0], out[1].reshape(rider.shape)) if rider else out[0]


def _rope_tables(pos):
    half = RET_DK // 2
    inv = 1.0 / (ROPE_BASE ** (jnp.arange(half, dtype=F32) / half))
    ang = pos.astype(F32)[:, None] * inv[None]
    return jnp.cos(ang), jnp.sin(ang)


def _log_gamma():
    return jnp.log(1.0 - 2.0 ** (-5.0 - jnp.arange(RET_HEADS, dtype=F32)))


def _decay_tables(c, rows):
    lg = _log_gamma()
    n = jnp.arange(rows, dtype=F32)
    live = n < c
    diff = n[:, None] - n[None, :]
    decay = jnp.where((diff >= 0)[None] & live[None, None, :],
                      jnp.exp(jnp.maximum(diff, 0.0)[None] * lg[:, None, None]), 0.0)
    q_decay = jnp.exp((n[None, :] + 1.0) * lg[:, None])
    k_decay = jnp.where(live[None], jnp.exp((c - 1.0 - n)[None, :] * lg[:, None]), 0.0)
    chunk_decay = jnp.exp(c * lg)
    return decay, q_decay, k_decay, chunk_decay


def _group_norm_gate(o, gate, gn):
    mu = jnp.mean(o, axis=-1, keepdims=True)
    oc = o - mu
    var = jnp.mean(oc * oc, axis=-1, keepdims=True)
    return _silu(gate) * (oc * lax.rsqrt(var + GN_EPS) * gn)


def _ret_prompt_kernel(q_ref, k_ref, v_ref, gate_ref, dec_ref, qd_ref, kd_ref, cd_ref, gn_ref,
                       y_ref, st_ref, s_ref, *, tb):
    cb = pl.program_id(2)

    @pl.when(cb == 0)
    def _():
        s_ref[...] = jnp.zeros_like(s_ref)

    def body(ci, carry):
        r0 = pl.multiple_of(ci * RET_CHUNK, RET_CHUNK)
        rows = pl.ds(r0, RET_CHUNK)
        for j in range(RET_HPS):
            kc = slice(j * RET_DK, (j + 1) * RET_DK)
            vc = slice(j * RET_DV, (j + 1) * RET_DV)
            q = q_ref[0, rows, kc]
            k = k_ref[0, rows, kc]
            v = v_ref[0, rows, vc]
            state = s_ref[j]
            scores = lax.dot_general(q, k, _NT, preferred_element_type=F32) * dec_ref[j]
            inner = jnp.dot(scores.astype(BF16), v, preferred_element_type=F32)
            cross = jnp.dot(q, state.astype(BF16), preferred_element_type=F32) * qd_ref[j]
            kd = (k.astype(F32) * kd_ref[j]).astype(BF16)
            s_ref[j] = cd_ref[j, 0:1, :] * state + lax.dot_general(kd, v, _TN, preferred_element_type=F32)
            y = _group_norm_gate(inner + cross, gate_ref[0, rows, vc].astype(F32), gn_ref[:, vc])
            y_ref[0, rows, vc] = y.astype(y_ref.dtype)
        return carry

    lax.fori_loop(0, tb // RET_CHUNK, body, 0)

    @pl.when(cb == pl.num_programs(2) - 1)
    def _():
        st_ref[0] = s_ref[...]


def _ret_prompt(proj, gn, bsz, seq, tb):
    decay, q_decay, k_decay, chunk_decay = _decay_tables(RET_CHUNK, RET_CHUNK)
    qd = jnp.broadcast_to(q_decay[:, :, None], (RET_HEADS, RET_CHUNK, RET_DV))
    kd = jnp.broadcast_to(k_decay[:, :, None], (RET_HEADS, RET_CHUNK, RET_DK))
    cd = jnp.broadcast_to(chunk_decay[:, None, None], (RET_HEADS, 8, RET_DV))
    proj = proj.reshape(bsz, seq, proj.shape[-1])
    hps = RET_HPS
    k_off = RET_QK // (hps * RET_DK)
    v_off = 2 * RET_QK // (hps * RET_DV)
    g_off = v_off + RET_HEADS // hps
    per_head = lambda shape: pl.BlockSpec((hps,) + shape, lambda b, h, c: (h, 0, 0))
    y, state = pl.pallas_call(
        functools.partial(_ret_prompt_kernel, tb=tb),
        grid=(bsz, RET_HEADS // hps, seq // tb),
        in_specs=[pl.BlockSpec((1, tb, hps * RET_DK), lambda b, h, c: (b, c, h)),
                  pl.BlockSpec((1, tb, hps * RET_DK), lambda b, h, c: (b, c, k_off + h)),
                  pl.BlockSpec((1, tb, hps * RET_DV), lambda b, h, c: (b, c, v_off + h)),
                  pl.BlockSpec((1, tb, hps * RET_DV), lambda b, h, c: (b, c, g_off + h)),
                  per_head((RET_CHUNK, RET_CHUNK)), per_head((RET_CHUNK, RET_DV)),
                  per_head((RET_CHUNK, RET_DK)), per_head((8, RET_DV)),
                  pl.BlockSpec((1, hps * RET_DV), lambda b, h, c: (0, h))],
        out_specs=[pl.BlockSpec((1, tb, hps * RET_DV), lambda b, h, c: (b, c, h)),
                   pl.BlockSpec((1, hps, RET_DK, RET_DV), lambda b, h, c: (b, h, 0, 0))],
        out_shape=[jax.ShapeDtypeStruct((bsz, seq, RET_V), BF16),
                   jax.ShapeDtypeStruct((bsz, RET_HEADS, RET_DK, RET_DV), F32)],
        scratch_shapes=[pltpu.VMEM((hps, RET_DK, RET_DV), F32)],
        compiler_params=_params(3), name="retention_prompt",
    )(proj, proj, proj, proj, decay, qd, kd, cd, gn)
    return y.reshape(bsz * seq, RET_V), state


def _ret_sample_kernel(p_ref, st_ref, dec_ref, qd_ref, kd_ref, cd_ref, gn_ref, y_ref, ns_ref):
    tp = SAMPLE_PAD
    proj = p_ref[0]
    for h in range(RET_HEADS):
        q = proj[:, h * RET_DK:(h + 1) * RET_DK].astype(BF16)
        k = proj[:, RET_QK + h * RET_DK:RET_QK + (h + 1) * RET_DK]
        v = proj[:, 2 * RET_QK + h * RET_DV:2 * RET_QK + (h + 1) * RET_DV]
        gate = proj[:, 2 * RET_QK + RET_V + h * RET_DV:2 * RET_QK + RET_V + (h + 1) * RET_DV]
        zk = jnp.zeros((LANES - tp, RET_DK), F32)
        zv = jnp.zeros((LANES - tp, RET_DV), F32)
        kp = jnp.concatenate([k, zk], axis=0).astype(BF16)
        kdp = jnp.concatenate([k * kd_ref[h], zk], axis=0).astype(BF16)
        vp = jnp.concatenate([v, zv], axis=0).astype(BF16)
        state = st_ref[0, h]
        scores = lax.dot_general(q, kp, _NT, preferred_element_type=F32) * dec_ref[h]
        inner = jnp.dot(scores.astype(BF16), vp, preferred_element_type=F32)
        cross = jnp.dot(q, state.astype(BF16), preferred_element_type=F32) * qd_ref[h]
        ns_ref[0, h] = cd_ref[h, 0:1, :] * state + lax.dot_general(kdp, vp, _TN, preferred_element_type=F32)
        y = _group_norm_gate(inner + cross, gate, gn_ref[:, h * RET_DV:(h + 1) * RET_DV])
        y_ref[0, :, h * RET_DV:(h + 1) * RET_DV] = y


def _ret_sample(proj, state, gn, t_new):
    bsz = proj.shape[0]
    tp = SAMPLE_PAD
    decay, q_decay, k_decay, chunk_decay = _decay_tables(t_new, tp)
    dec = jnp.pad(decay, ((0, 0), (0, 0), (0, LANES - tp)))
    qd = jnp.broadcast_to(q_decay[:, :, None], (RET_HEADS, tp, RET_DV))
    kd = jnp.broadcast_to(k_decay[:, :, None], (RET_HEADS, tp, RET_DK))
    cd = jnp.broadcast_to(chunk_decay[:, None, None], (RET_HEADS, 8, RET_DV))
    const = lambda a: pl.BlockSpec(a.shape, lambda b: (0,) * a.ndim)
    return pl.pallas_call(
        _ret_sample_kernel, grid=(bsz,),
        in_specs=[pl.BlockSpec((1,) + proj.shape[1:], lambda b: (b, 0, 0)),
                  pl.BlockSpec((1,) + state.shape[1:], lambda b: (b, 0, 0, 0)),
                  const(dec), const(qd), const(kd), const(cd), const(gn)],
        out_specs=[pl.BlockSpec((1, tp, RET_V), lambda b: (b, 0, 0)),
                   pl.BlockSpec((1,) + state.shape[1:], lambda b: (b, 0, 0, 0))],
        out_shape=[jax.ShapeDtypeStruct((bsz, tp, RET_V), F32),
                   jax.ShapeDtypeStruct(state.shape, F32)],
        compiler_params=_params(1), name="retention_sample",
    )(proj, state, dec, qd, kd, cd, gn)


def _top2_route(logits):
    lane = lax.broadcasted_iota(jnp.int32, logits.shape, 1)
    logits = jnp.where(lane < N_EXPERTS, logits, NEG)
    m1 = jnp.max(logits, axis=-1, keepdims=True)
    i1 = jnp.min(jnp.where(logits == m1, lane, LANES), axis=-1, keepdims=True)
    rest = jnp.where(lane == i1, NEG, logits)
    m2 = jnp.max(rest, axis=-1, keepdims=True)
    i2 = jnp.min(jnp.where(rest == m2, lane, LANES), axis=-1, keepdims=True)
    e2 = jnp.exp(m2 - m1)
    den = 1.0 + e2
    return jnp.where(lane == 0, 1.0 / den,
                     jnp.where(lane == 1, e2 / den,
                               jnp.where(lane == 2, i1.astype(F32),
                                         jnp.where(lane == 3, i2.astype(F32), 0.0))))


def _route_plan(e1, e2, tile):
    n_tok = e1.shape[0]
    e = jnp.concatenate([e1, e2])
    onehot = (e[:, None] == jnp.arange(N_EXPERTS, dtype=jnp.int32)[None]).astype(jnp.int32)
    csum = jnp.cumsum(onehot, axis=0)
    rank = jnp.take_along_axis(csum, e[:, None], axis=1)[:, 0] - 1
    tiles_per_expert = (csum[-1] + tile - 1) // tile
    tile_end = jnp.cumsum(tiles_per_expert)
    pos = ((tile_end - tiles_per_expert) * tile)[e] + rank
    n_tiles = (2 * n_tok) // tile + N_EXPERTS
    n_used = tile_end[-1]
    tile_ids = jnp.arange(n_tiles, dtype=jnp.int32)
    tile_expert = jnp.minimum(jnp.searchsorted(tile_end, tile_ids, side="right"), N_EXPERTS - 1)
    tile_expert = jnp.where(tile_ids < n_used, tile_expert, tile_expert[n_used - 1])
    meta = jnp.concatenate([tile_expert, n_used[None], tile_end]).astype(jnp.int32)
    return meta, pos[:n_tok], pos[n_tok:], n_tiles


SUBLANES = 8


def _start_row_gather(idx_ref, src_hbm, dst3, sem):
    def issue(blk, carry):
        for s in range(SUBLANES):
            pltpu.make_async_copy(src_hbm.at[pl.ds(idx_ref[0, 0, blk * SUBLANES + s], 1), :],
                                  dst3.at[blk, pl.ds(s, 1), :], sem).start()
        return carry

    lax.fori_loop(0, dst3.shape[0], issue, 0)


def _start_row_scatter(idx_ref, src3, dst_hbm, sem):
    def issue(blk, carry):
        for s in range(SUBLANES):
            pltpu.make_async_copy(src3.at[blk, pl.ds(s, 1), :],
                                  dst_hbm.at[pl.ds(idx_ref[0, 0, blk * SUBLANES + s], 1), :], sem).start()
        return carry

    lax.fori_loop(0, src3.shape[0], issue, 0)


def _wait_rows(hbm, vmem3, sem):
    rows = vmem3.shape[0] * SUBLANES
    pltpu.make_async_copy(hbm.at[pl.ds(0, rows), :], hbm.at[pl.ds(0, rows), :], sem).wait()


def _dispatch_kernel(meta_ref, p1_ref, p2_ref, x_ref, xs_hbm, xbuf, zbuf, sems, zsem, *, tm, tile, n_tiles, n_steps):
    i = pl.program_id(0)
    slot = i % 2

    def wait_slot(s):
        for _ in range(2):
            _wait_rows(xs_hbm, xbuf.at[s], sems.at[s])

    @pl.when(i == 0)
    def _():
        zbuf[...] = jnp.zeros_like(zbuf)
        zero_tile = lambda t: pltpu.make_async_copy(zbuf, xs_hbm.at[pl.ds(t * tile, tile), :], zsem)
        for e in range(N_EXPERTS):
            zero_tile(jnp.maximum(meta_ref[n_tiles + 1 + e] - 1, 0)).start()
        for e in range(N_EXPERTS):
            zero_tile(0).wait()

        def zero_unused(t, carry):
            zero_tile(t).start()
            zero_tile(t).wait()
            return carry

        lax.fori_loop(meta_ref[n_tiles], n_tiles, zero_unused, 0)

    @pl.when(i >= 2)
    def _():
        wait_slot(slot)

    xbuf[slot] = x_ref[...]
    _start_row_scatter(p1_ref, xbuf.at[slot], xs_hbm, sems.at[slot])
    _start_row_scatter(p2_ref, xbuf.at[slot], xs_hbm, sems.at[slot])

    @pl.when(i == n_steps - 1)
    def _():
        wait_slot(slot)
        if n_steps > 1:
            wait_slot(1 - slot)


def _dispatch(meta, pos1, pos2, x, tm, tile, n_tiles):
    m = x.shape[0]
    idx = lambda: pl.BlockSpec((1, 1, tm), lambda i, meta: (i, 0, 0), memory_space=pltpu.SMEM)
    return pl.pallas_call(
        functools.partial(_dispatch_kernel, tm=tm, tile=tile, n_tiles=n_tiles, n_steps=m // tm),
        grid_spec=pltpu.PrefetchScalarGridSpec(
            num_scalar_prefetch=1, grid=(m // tm,),
            in_specs=[idx(), idx(),
                      pl.BlockSpec((tm // SUBLANES, SUBLANES, D_MODEL), lambda i, meta: (i, 0, 0))],
            out_specs=pl.BlockSpec(memory_space=pl.ANY),
            scratch_shapes=[pltpu.VMEM((2, tm // SUBLANES, SUBLANES, D_MODEL), F32),
                            pltpu.VMEM((tile, D_MODEL), F32),
                            pltpu.SemaphoreType.DMA((2,)), pltpu.SemaphoreType.DMA(())]),
        out_shape=jax.ShapeDtypeStruct((n_tiles * tile, D_MODEL), F32),
        compiler_params=_params(1), name="moe_dispatch",
    )(meta, pos1.reshape(m // tm, 1, tm), pos2.reshape(m // tm, 1, tm),
      x.reshape(m // SUBLANES, SUBLANES, D_MODEL))


def _expert_kernel(meta_ref, xs_ref, wg_ref, wu_ref, wd_ref, o_ref, xb_ref, *, n_tiles):
    f = pl.program_id(1)
    used = pl.program_id(0) < meta_ref[n_tiles]

    @pl.when(jnp.logical_and(jnp.logical_not(used), f == 0))
    def _():
        o_ref[...] = jnp.zeros_like(o_ref)

    @pl.when(jnp.logical_and(used, f == 0))
    def _():
        xb_ref[...] = xs_ref[...].astype(BF16)

    @pl.when(used)
    def _():
        part = _swiglu_chunks(xb_ref[...], wg_ref, wu_ref, wd_ref, (0,), wd_ref.shape[1], FF_CHUNK)

        @pl.when(f == 0)
        def _():
            o_ref[...] = part

        @pl.when(f > 0)
        def _():
            o_ref[...] += part


def _experts(meta, xs, w_gu, w_down, tile, n_tiles, tf):
    d_ff = w_down.shape[1]
    nf = d_ff // tf
    fcol = lambda j, f, meta: jnp.where(j < meta[n_tiles], f, nf - 1)
    return pl.pallas_call(
        functools.partial(_expert_kernel, n_tiles=n_tiles),
        grid_spec=pltpu.PrefetchScalarGridSpec(
            num_scalar_prefetch=1, grid=(n_tiles, nf),
            in_specs=[pl.BlockSpec((tile, D_MODEL), lambda j, f, meta: (j, 0)),
                      pl.BlockSpec((1, D_MODEL, tf), lambda j, f, meta: (meta[j], 0, fcol(j, f, meta))),
                      pl.BlockSpec((1, D_MODEL, tf), lambda j, f, meta: (meta[j], 0, nf + fcol(j, f, meta))),
                      pl.BlockSpec((1, tf, D_MODEL), lambda j, f, meta: (meta[j], fcol(j, f, meta), 0))],
            out_specs=pl.BlockSpec((tile, D_MODEL), lambda j, f, meta: (j, 0)),
            scratch_shapes=[pltpu.VMEM((tile, D_MODEL), BF16)]),
        out_shape=jax.ShapeDtypeStruct((n_tiles * tile, D_MODEL), F32),
        compiler_params=_params(2), name="moe_experts",
    )(meta, xs, w_gu, w_gu, w_down)


def _combine_kernel(p1_ref, p2_ref, n1_ref, n2_ref, rows_hbm, x_ref, r_ref, g_ref, b_ref, op_ref, os_ref,
                    buf1, buf2, sem1, sem2, *, tm, prompt_tiles):
    i = pl.program_id(0)
    slot = i % 2

    def start(pa_ref, pb_ref, s):
        _start_row_gather(pa_ref, rows_hbm, buf1.at[s], sem1.at[s])
        _start_row_gather(pb_ref, rows_hbm, buf2.at[s], sem2.at[s])

    @pl.when(i == 0)
    def _():
        start(p1_ref, p2_ref, 0)

    @pl.when(i + 1 < pl.num_programs(0))
    def _():
        start(n1_ref, n2_ref, 1 - slot)

    _wait_rows(rows_hbm, buf1.at[slot], sem1.at[slot])
    _wait_rows(rows_hbm, buf2.at[slot], sem2.at[slot])
    route = r_ref[...]
    y = (route[:, 0:1] * buf1[slot].reshape(tm, D_MODEL) + route[:, 1:2] * buf2[slot].reshape(tm, D_MODEL))
    res = _layer_norm(ALPHA * x_ref[...] + y, g_ref[...], b_ref[...])

    @pl.when(i < prompt_tiles)
    def _():
        op_ref[...] = res

    @pl.when(i >= prompt_tiles)
    def _():
        os_ref[...] = res


def _combine(pos1, pos2, rows, x, route, g, b, tm, n_prompt):
    m = x.shape[0]
    pt = n_prompt // tm
    nt = m // tm
    idx = lambda: pl.BlockSpec((1, 1, tm), lambda i: (i, 0, 0), memory_space=pltpu.SMEM)
    nxt = lambda: pl.BlockSpec((1, 1, tm), lambda i: (jnp.minimum(i + 1, nt - 1), 0, 0), memory_space=pltpu.SMEM)
    vec = pl.BlockSpec((1, D_MODEL), lambda i: (0, 0))
    p1 = pos1.reshape(nt, 1, tm)
    p2 = pos2.reshape(nt, 1, tm)
    return pl.pallas_call(
        functools.partial(_combine_kernel, tm=tm, prompt_tiles=pt), grid=(nt,),
        in_specs=[idx(), idx(), nxt(), nxt(), pl.BlockSpec(memory_space=pl.ANY),
                  pl.BlockSpec((tm, D_MODEL), lambda i: (i, 0)),
                  pl.BlockSpec((tm, LANES), lambda i: (i, 0)), vec, vec],
        out_specs=[pl.BlockSpec((tm, D_MODEL), lambda i: (jnp.minimum(i, pt - 1), 0)),
                   pl.BlockSpec((tm, D_MODEL), lambda i: (jnp.maximum(i - pt, 0), 0))],
        out_shape=[jax.ShapeDtypeStruct((n_prompt, D_MODEL), F32),
                   jax.ShapeDtypeStruct((m - n_prompt, D_MODEL), F32)],
        scratch_shapes=[pltpu.VMEM((2, tm // SUBLANES, SUBLANES, D_MODEL), F32),
                        pltpu.VMEM((2, tm // SUBLANES, SUBLANES, D_MODEL), F32),
                        pltpu.SemaphoreType.DMA((2,)), pltpu.SemaphoreType.DMA((2,))],
        compiler_params=_params(1), name="moe_combine",
    )(p1, p2, p1, p2, rows, x, route, g, b)


def _moe(x, route, n_prompt, w_gu, w_down, g, b, tm, tile, tf):
    e1 = route[:, 2].astype(jnp.int32)
    e2 = route[:, 3].astype(jnp.int32)
    meta, pos1, pos2, n_tiles = _route_plan(e1, e2, tile)
    xs = _dispatch(meta, pos1, pos2, x, tm, tile, n_tiles)
    rows = _experts(meta, xs, w_gu, w_down, tile, n_tiles, tf)
    return _combine(pos1, pos2, rows, x, route, g, b, tm, n_prompt)


def kernel(x_prompt, x_sample, cache_kv_w128, cache_kv_w512, cache_kv_w2048, state_ret,
           ln_g, ln_b, rel_bias, w_in_dil, w_out_dil, w_in_ret, ret_gn_g, w_out_ret,
           w_gu_dense, w_down_dense, w_router, w_gu_moe, w_down_moe):
    bsz, seq, _ = x_prompt.shape
    dbsz, t_new, _ = x_sample.shape
    tp = SAMPLE_PAD
    n_p = bsz * seq
    n_s = dbsz * tp
    caches = (cache_kv_w128, cache_kv_w512, cache_kv_w2048)

    w_in_dil_b = w_in_dil.astype(BF16)
    w_out_dil_b = w_out_dil.astype(BF16)
    w_in_ret_b = w_in_ret.astype(BF16)
    w_out_ret_b = w_out_ret.astype(BF16)
    w_gu_dense_b = w_gu_dense.astype(BF16)
    w_down_dense_b = w_down_dense.astype(BF16)
    lng = ln_g.reshape(DEPTH, 2, 1, D_MODEL)
    lnb = ln_b.reshape(DEPTH, 2, 1, D_MODEL)
    gn = ret_gn_g.reshape(1, RET_V)

    hp = x_prompt.reshape(n_p, D_MODEL)
    hs = jnp.pad(x_sample, ((0, 0), (0, tp - t_new), (0, 0))).reshape(n_s, D_MODEL)

    tm = min(1024, seq)
    qkv_groups, (kv128_p, kv512_p, kv2048_p) = _proj_dil_prompt(hp, w_in_dil_b, bsz, seq, tm)
    outs, lses = [], []
    for g, (window, dil) in enumerate(DIL_GROUPS):
        qkv_g = qkv_groups[g].reshape(bsz * dil, seq // dil, G_COLS)
        tbl = _prompt_table(rel_bias, g, window, dil)
        o, l = _attn_prompt(qkv_g, tbl, min(512, seq // dil))
        outs.append(o.reshape(bsz, dil, seq // dil, A_WIDTH))
        lses.append(l.reshape(bsz, dil, seq // dil, A_WIDTH))
    hp = _merge_out(outs, lses, w_out_dil_b, hp, lng[0, 0], lnb[0, 0], 512)

    qkv_s = _matmul(hs, w_in_dil_b, G_COLS)
    qkv_s3 = qkv_s.reshape(dbsz, tp, N_GROUPS * G_COLS)
    tabs = [_sample_tables(rel_bias, g, window, dil, caches[g].shape[1], t_new)
            for g, (window, dil) in enumerate(DIL_GROUPS)]
    q_t = qkv_s.reshape(dbsz, tp, N_GROUPS, 3, H_SLOT, HEAD_DIM).transpose(0, 2, 3, 4, 5, 1)
    caches_t = [c.transpose(0, 2, 3, 4, 1) for c in caches]
    mixed_t = _attn_sample(q_t, caches_t, [t[0] for t in tabs], [t[1] for t in tabs], t_new)
    mixed_s = mixed_t.transpose(0, 3, 1, 2)
    hs = _mm_ln(mixed_s.reshape(n_s, A_WIDTH), w_out_dil_b, hs, lng[0, 0], lnb[0, 0], n_s)
    rows_s = []
    for g in range(N_GROUPS):
        kv = qkv_s3[:, :t_new, g * G_COLS + A_WIDTH:(g + 1) * G_COLS]
        rows_s.append(kv.reshape(dbsz, t_new, 2, H_SLOT, HEAD_DIM))

    hp, w_down_moe_b = _ffn_dense(hp, w_gu_dense_b, w_down_dense_b, lng[0, 1], lnb[0, 1], 512, ride=w_down_moe)
    hs = _ffn_dense(hs, w_gu_dense_b, w_down_dense_b, lng[0, 1], lnb[0, 1], n_s)

    cos_p, sin_p = _rope_tables(jnp.arange(seq, dtype=jnp.int32))
    pos_s = jnp.tile(PAST_LEN + jnp.arange(tp, dtype=jnp.int32), dbsz)
    cos_s, sin_s = _rope_tables(pos_s)
    proj_p, w_gu_moe_b = _proj_ret(hp, w_in_ret_b, cos_p, sin_p, 512, BF16, ride=w_gu_moe)
    y_p, ret_p = _ret_prompt(proj_p, gn, bsz, seq, min(1024, seq))
    proj_s = _proj_ret(hs, w_in_ret_b, cos_s, sin_s, n_s // 2, F32)
    y_s, ret_s = _ret_sample(proj_s.reshape(dbsz, tp, -1), state_ret, gn, t_new)
    h_all, route = _mm_ln_pair(y_p, y_s.reshape(n_s, RET_V), w_out_ret_b, hp, hs, lng[1, 0], lnb[1, 0],
                               w_router, 512)

    out_p, out_s = _moe(h_all, route, n_p, w_gu_moe_b, w_down_moe_b, lng[1, 1], lnb[1, 1], 512, 1024,
                        w_down_moe.shape[1] // 2)

    y_prompt = out_p.reshape(bsz, seq, D_MODEL)
    y_sample = out_s.reshape(dbsz, tp, D_MODEL)[:, :t_new]
    shape5 = lambda a: a.reshape(a.shape[0], a.shape[1], 2, H_SLOT, HEAD_DIM)
    return (y_prompt, y_sample, shape5(kv128_p), shape5(kv512_p), shape5(kv2048_p), ret_p,
            rows_s[0], rows_s[1], rows_s[2], ret_s)
```

```python
import functools

import jax
import jax.numpy as jnp
import numpy as np
from jax import lax
from jax.experimental import pallas as pl
from jax.experimental.pallas import tpu as pltpu

F32 = jnp.float32
BF16 = jnp.bfloat16

DEPTH = 2
D_MODEL = 1024
PAST_LEN = 16384
DIL_GROUPS = ((128, 1), (512, 4), (2048, 16))
N_GROUPS = 3
H_SLOT = 8
HEAD_DIM = 64
A_WIDTH = H_SLOT * HEAD_DIM
G_COLS = 3 * A_WIDTH
N_BUCKETS = 32
MAX_DISTANCE = 2048
RET_HEADS = 4
RET_DK = 256
RET_DV = 512
RET_CHUNK = 128
RET_HPS = 4
ROPE_BASE = 10000.0
RET_QK = RET_HEADS * RET_DK
RET_V = RET_HEADS * RET_DV
N_EXPERTS = 8
LN_EPS = 1e-5
GN_EPS = 1e-5
ALPHA = (2 * DEPTH) ** 0.25
NEG = -1e30

LANES = 128
ATT_BLOCK = 128
PERM = 256
FF_CHUNK = 512
SAMPLE_PAD = 16
VMEM_LIMIT = 56 * 1024 * 1024

_NT = (((1,), (1,)), ((), ()))
_TN = (((0,), (0,)), ((), ()))


def _params(n_grid):
    return pltpu.CompilerParams(dimension_semantics=("arbitrary",) * n_grid,
                                vmem_limit_bytes=VMEM_LIMIT)


def _layer_norm(z, g, b):
    mu = jnp.mean(z, axis=-1, keepdims=True)
    zc = z - mu
    var = jnp.mean(zc * zc, axis=-1, keepdims=True)
    return zc * lax.rsqrt(var + LN_EPS) * g + b


def _silu(a):
    return a / (1.0 + jnp.exp(-a))


def _t5_bucket(dist):
    max_exact = N_BUCKETS // 2
    d = np.asarray(dist, dtype=np.int64)
    scaled = np.log(np.maximum(d, 1) / max_exact) / np.log(MAX_DISTANCE / max_exact)
    large = np.minimum(max_exact + (scaled * (N_BUCKETS - max_exact)).astype(np.int32), N_BUCKETS - 1)
    return np.where(d < max_exact, d, large).astype(np.int32)


def _group_bias(rel_bias, g, window, dil):
    n_keys = window // dil + 1
    buckets = jnp.asarray(_t5_bucket(np.arange(n_keys) * dil))
    return rel_bias[buckets][:, g * H_SLOT:(g + 1) * H_SLOT].T.astype(F32)


def _matmul_kernel(x_ref, w_ref, o_ref):
    o_ref[...] = jnp.dot(x_ref[...].astype(BF16), w_ref[...], preferred_element_type=F32)


def _matmul(x, w, tn):
    m, k = x.shape
    n = w.shape[1]
    return pl.pallas_call(
        _matmul_kernel,
        grid=(n // tn,),
        in_specs=[pl.BlockSpec((m, k), lambda j: (0, 0)),
                  pl.BlockSpec((k, tn), lambda j: (0, j))],
        out_specs=pl.BlockSpec((m, tn), lambda j: (0, j)),
        out_shape=jax.ShapeDtypeStruct((m, n), F32),
        compiler_params=_params(1),
        name="matmul_sample",
    )(x, w)


def _deinterleave_matrix(dil):
    p = np.zeros((PERM, PERM), np.float32)
    rows = PERM // dil
    for r in range(dil):
        for m in range(rows):
            p[r * rows + m, m * dil + r] = 1.0
    return p


def _proj_dil_kernel(x_ref, w_ref, p1_ref, p2_ref, q0_ref, q1_ref, q2_ref, kv0_ref, kv1_ref, kv2_ref, xperm, xb_ref,
                     *, tm, tpb, keeps):
    g = pl.program_id(1)
    j = pl.program_id(0) % tpb

    @pl.when(g == 0)
    def _():
        xb_ref[...] = x_ref[...].astype(BF16)
    q_refs = (q0_ref, q1_ref, q2_ref)
    kv_refs = (kv0_ref, kv1_ref, kv2_ref)
    perms = (None, p1_ref, p2_ref)

    def store_q(ref, r, row0, n_rows, acc, src0):
        ref[0, r, row0:row0 + n_rows, :A_WIDTH] = (acc[src0:src0 + n_rows, :A_WIDTH] * (HEAD_DIM ** -0.5)).astype(BF16)
        ref[0, r, row0:row0 + n_rows, A_WIDTH:] = acc[src0:src0 + n_rows, A_WIDTH:].astype(BF16)

    for gi, (_, dil) in enumerate(DIL_GROUPS):
        keep = keeps[gi]
        in_tail = (j >= tpb - keep // tm) if keep >= tm else (j == tpb - 1)
        row0 = 0 if keep >= tm else tm - keep

        @pl.when(g == gi)
        def _(gi=gi, dil=dil, in_tail=in_tail, row0=row0):
            xb = xb_ref[...]
            if dil == 1:
                acc = jnp.dot(xb, w_ref[...], preferred_element_type=F32)
                store_q(q_refs[gi], 0, 0, tm, acc, 0)
            else:
                for sub in range(tm // PERM):
                    blk = xb[sub * PERM:(sub + 1) * PERM]
                    xperm[sub * PERM:(sub + 1) * PERM, :] = jnp.dot(
                        perms[gi][...], blk, preferred_element_type=F32).astype(BF16)
                acc = jnp.dot(xperm[...], w_ref[...], preferred_element_type=F32)
                rows = PERM // dil
                for sub in range(tm // PERM):
                    for r in range(dil):
                        store_q(q_refs[gi], r, sub * rows, rows, acc, sub * PERM + r * rows)

            @pl.when(in_tail)
            def _():
                if dil == 1:
                    kv_refs[gi][0] = acc[row0:, A_WIDTH:]
                else:
                    kv_refs[gi][0] = jnp.dot(xb[row0:], w_ref[:, A_WIDTH:], preferred_element_type=F32)


def _proj_dil_prompt(x, w, bsz, seq, tm):
    tpb = seq // tm
    keeps = tuple(min(wd, seq) for wd, _ in DIL_GROUPS)

    def kv_spec(keep):
        if keep >= tm:
            first = tpb - keep // tm
            return pl.BlockSpec((1, tm, 2 * A_WIDTH),
                                lambda i, g: (i // tpb, jnp.maximum(i % tpb - first, 0), 0))
        return pl.BlockSpec((1, keep, 2 * A_WIDTH), lambda i, g: (i // tpb, 0, 0))

    perm = lambda d: jnp.asarray(_deinterleave_matrix(d), BF16)
    const = pl.BlockSpec((PERM, PERM), lambda i, g: (0, 0))
    outs = pl.pallas_call(
        functools.partial(_proj_dil_kernel, tm=tm, tpb=tpb, keeps=keeps),
        grid=(bsz * tpb, N_GROUPS),
        in_specs=[pl.BlockSpec((tm, D_MODEL), lambda i, g: (i, 0)),
                  pl.BlockSpec((D_MODEL, G_COLS), lambda i, g: (0, g)), const, const],
        out_specs=[pl.BlockSpec((1, d, tm // d, G_COLS), lambda i, g: (i // tpb, 0, i % tpb, 0))
                   for _, d in DIL_GROUPS] + [kv_spec(k) for k in keeps],
        out_shape=[jax.ShapeDtypeStruct((bsz, d, seq // d, G_COLS), BF16) for _, d in DIL_GROUPS]
        + [jax.ShapeDtypeStruct((bsz, k, 2 * A_WIDTH), F32) for k in keeps],
        scratch_shapes=[pltpu.VMEM((tm, D_MODEL), BF16), pltpu.VMEM((tm, D_MODEL), BF16)],
        compiler_params=_params(2),
        name="proj_dil_prompt",
    )(x, w, perm(DIL_GROUPS[1][1]), perm(DIL_GROUPS[2][1]))
    return outs[:3], outs[3:]


def _attn_prompt_kernel(q_ref, k_ref, v_ref, kp_ref, vp_ref, tbl_ref, o_ref, lse_ref, kbuf, vbuf, *, tq):
    n = pl.program_id(1)
    kbuf[0:ATT_BLOCK, :] = kp_ref[0]
    kbuf[ATT_BLOCK:, :] = k_ref[0]
    vbuf[0:ATT_BLOCK, :] = vp_ref[0]
    vbuf[ATT_BLOCK:, :] = v_ref[0]
    low = lax.broadcasted_iota(jnp.int32, (ATT_BLOCK, LANES), 1) < HEAD_DIM

    def body(m, carry):
        r0 = pl.multiple_of(m * ATT_BLOCK, ATT_BLOCK)
        first = jnp.where(jnp.logical_and(n == 0, m == 0), 1, 0)
        for p in range(A_WIDTH // LANES):
            cols = slice(p * LANES, (p + 1) * LANES)
            qm = q_ref[0, pl.ds(r0, ATT_BLOCK), cols]
            keys = kbuf[pl.ds(r0, 2 * ATT_BLOCK), cols]
            vals = vbuf[pl.ds(r0, 2 * ATT_BLOCK), cols]
            outs, lses = [], []
            for a in range(2):
                qa = jnp.where(low if a == 0 else jnp.logical_not(low), qm, jnp.zeros_like(qm))
                s = lax.dot_general(qa, keys, _NT, preferred_element_type=F32) + tbl_ref[first, 2 * p + a]
                mx = jnp.max(s, axis=-1, keepdims=True)
                e = jnp.exp(s - mx)
                l = jnp.sum(e, axis=-1, keepdims=True)
                o = jnp.dot(e.astype(BF16), vals, preferred_element_type=F32)
                outs.append(o / l)
                lses.append(jnp.broadcast_to(mx + jnp.log(l), (ATT_BLOCK, LANES)))
            o_ref[0, pl.ds(r0, ATT_BLOCK), cols] = jnp.where(low, outs[0], outs[1]).astype(BF16)
            lse_ref[0, pl.ds(r0, ATT_BLOCK), cols] = jnp.where(low, lses[0], lses[1])
        return carry

    lax.fori_loop(0, tq // ATT_BLOCK, body, 0)


def _attn_prompt(qkv, tbl, tq):
    nb, length, _ = qkv.shape
    sub = tq // ATT_BLOCK
    cur = lambda c: pl.BlockSpec((1, tq, A_WIDTH), lambda s, n: (s, n, c))
    prev = lambda c: pl.BlockSpec((1, ATT_BLOCK, A_WIDTH), lambda s, n: (s, jnp.maximum(n * sub - 1, 0), c))
    return pl.pallas_call(
        functools.partial(_attn_prompt_kernel, tq=tq),
        grid=(nb, length // tq),
        in_specs=[cur(0), cur(1), cur(2), prev(1), prev(2),
                  pl.BlockSpec(tbl.shape, lambda s, n: (0, 0, 0, 0))],
        out_specs=[pl.BlockSpec((1, tq, A_WIDTH), lambda s, n: (s, n, 0))] * 2,
        out_shape=[jax.ShapeDtypeStruct((nb, length, A_WIDTH), BF16),
                   jax.ShapeDtypeStruct((nb, length, A_WIDTH), F32)],
        scratch_shapes=[pltpu.VMEM((tq + ATT_BLOCK, A_WIDTH), BF16)] * 2,
        compiler_params=_params(2),
        name="attn_prompt",
    )(qkv, qkv, qkv, qkv, qkv, tbl)


def _toeplitz(u, n_rows, n_cols):
    h, n = u.shape
    assert n == n_rows + n_cols - 1
    up = jnp.pad(u, ((0, 0), (0, 1)))
    w = jnp.tile(up, (1, n_rows))[:, :n_rows * n].reshape(h, n_rows, n)
    return w[:, :, n_rows - 1:n_rows - 1 + n_cols]


def _bias_by_offset(bias, dist, valid):
    return jnp.where(valid[None], bias[:, np.clip(dist, 0, bias.shape[1] - 1)], NEG)


def _prompt_table(rel_bias, g, window, dil):
    bias = _group_bias(rel_bias, g, window, dil)
    k = np.arange(3 * ATT_BLOCK - 1)
    dist = 2 * ATT_BLOCK - 1 - k
    u = _bias_by_offset(bias, dist, (dist >= 0) & (dist <= window // dil))
    tbl = _toeplitz(u, ATT_BLOCK, 2 * ATT_BLOCK)
    c = np.arange(2 * ATT_BLOCK)[None, None, :]
    tbl_first = jnp.where(c < ATT_BLOCK, NEG, tbl)
    return jnp.stack([tbl, tbl_first], 0)


def _split3(x):
    hi = x.astype(BF16)
    r1 = x - hi.astype(F32)
    mid = r1.astype(BF16)
    lo = (r1 - mid.astype(F32)).astype(BF16)
    return hi, mid, lo


def _merge_out_kernel(o0, o1, o2, l0, l1, l2, pt1_ref, pt2_ref, w_ref, x_ref, g_ref, b_ref, out_ref, *, tm):
    pts = (None, pt1_ref, pt2_ref)

    def natural(ref, gi, exact_f32):
        dil = DIL_GROUPS[gi][1]
        if dil == 1:
            return ref[0, 0].astype(F32)
        rows = PERM // dil
        blocks = []
        for sub in range(tm // PERM):
            piece = jnp.concatenate([ref[0, r, sub * rows:(sub + 1) * rows, :] for r in range(dil)], axis=0)
            parts = _split3(piece) if exact_f32 else (piece,)
            blocks.append(sum(jnp.dot(pts[gi][...], p, preferred_element_type=F32) for p in parts))
        return jnp.concatenate(blocks, axis=0)

    os_ = [natural(r, gi, False) for gi, r in enumerate((o0, o1, o2))]
    ls = [natural(r, gi, True) for gi, r in enumerate((l0, l1, l2))]
    mx = jnp.maximum(jnp.maximum(ls[0], ls[1]), ls[2])
    es = [jnp.exp(l - mx) for l in ls]
    mixed = (es[0] * os_[0] + es[1] * os_[1] + es[2] * os_[2]) / (es[0] + es[1] + es[2])
    acc = jnp.dot(mixed.astype(BF16), w_ref[...], preferred_element_type=F32)
    out_ref[...] = _layer_norm(ALPHA * x_ref[...] + acc, g_ref[...], b_ref[...])


def _merge_out(os_, ls_, w, x, g, b, tm):
    bsz, _, seq, _ = os_[0].shape
    tpb = seq // tm
    grp = lambda d: pl.BlockSpec((1, d, tm // d, A_WIDTH), lambda i: (i // tpb, 0, i % tpb, 0))
    specs = [grp(d) for _, d in DIL_GROUPS]
    pt = lambda d: jnp.asarray(_deinterleave_matrix(d).T, BF16)
    const = pl.BlockSpec((PERM, PERM), lambda i: (0, 0))
    vec = pl.BlockSpec((1, D_MODEL), lambda i: (0, 0))
    return pl.pallas_call(
        functools.partial(_merge_out_kernel, tm=tm), grid=(bsz * tpb,),
        in_specs=specs + specs + [const, const, pl.BlockSpec((A_WIDTH, D_MODEL), lambda i: (0, 0)),
                                  pl.BlockSpec((tm, D_MODEL), lambda i: (i, 0)), vec, vec],
        out_specs=pl.BlockSpec((tm, D_MODEL), lambda i: (i, 0)),
        out_shape=jax.ShapeDtypeStruct((bsz * seq, D_MODEL), F32),
        compiler_params=_params(1), name="merge_outproj_ln",
    )(*os_, *ls_, pt(DIL_GROUPS[1][1]), pt(DIL_GROUPS[2][1]), w, x, g, b)


def _attn_sample_kernel(q_ref, c0_ref, c1_ref, c2_ref, tb0, tb1, tb2, tn0, tn1, tn2, out_ref, *scratch, t_new):
    tp = SAMPLE_PAD
    lane_t = lax.broadcasted_iota(jnp.int32, (HEAD_DIM, tp), 1)
    lane_k = lax.broadcasted_iota(jnp.int32, (HEAD_DIM, LANES), 1)
    qcol = lambda g, h, t: q_ref[0, g, 0, h][:, t:t + 1] * (HEAD_DIM ** -0.5)
    sets = []
    for g, (c_ref, tb, tn) in enumerate(((c0_ref, tb0, tn0), (c1_ref, tb1, tn1), (c2_ref, tb2, tn2))):
        dil = DIL_GROUPS[g][1]
        sets.append((lambda h, c_ref=c_ref: c_ref[0, 0, h], lambda h, c_ref=c_ref: c_ref[0, 1, h], tb, g,
                     dil if dil >= t_new else 0))
        sets.append((lambda h, g=g: q_ref[0, g, 1, h], lambda h, g=g: q_ref[0, g, 2, h], tn, g, 0))
    s_refs = scratch[:len(sets)]
    p_refs = dict(zip([i for i, st in enumerate(sets) if st[4]], scratch[len(sets):]))
    out_ref[0] = jnp.zeros(out_ref.shape[1:], F32)

    for i, p_ref in p_refs.items():
        keys, _, _, g, dil = sets[i]
        p_ref[...] = jnp.zeros_like(p_ref)
        for h in range(H_SLOT):
            qsel = jnp.zeros((HEAD_DIM, LANES), F32)
            for t in range(t_new):
                qsel = jnp.where(lane_k % dil == t, qcol(g, h, t), qsel)
            kt = keys(h)
            s_refs[i][h:h + 1, :] = jnp.sum(kt * jnp.tile(qsel, (1, kt.shape[1] // LANES)), axis=0, keepdims=True)

    for t in range(t_new):
        for h in range(H_SLOT):
            for (keys, _, _, g, dil), s_ref in zip(sets, s_refs):
                if not dil:
                    s_ref[h:h + 1, :] = jnp.sum(keys(h) * qcol(g, h, t), axis=0, keepdims=True)
        scores = [s_ref[...] + st[2][t] for st, s_ref in zip(sets, s_refs)]
        mx = functools.reduce(jnp.maximum, [jnp.max(s, axis=-1, keepdims=True) for s in scores])
        es = [jnp.exp(s - mx) for s in scores]
        inv = 1.0 / functools.reduce(jnp.add, [jnp.sum(e, axis=-1, keepdims=True) for e in es])
        for i, (e, s_ref) in enumerate(zip(es, s_refs)):
            if i in p_refs:
                p_refs[i][...] += e * inv
            else:
                s_ref[...] = e
        for h in range(H_SLOT):
            by_width = {}
            for (_, values, _, _, dil), s_ref in zip(sets, s_refs):
                if not dil:
                    prod = values(h) * s_ref[h:h + 1, :]
                    by_width[prod.shape[1]] = by_width[prod.shape[1]] + prod if prod.shape[1] in by_width else prod
            col = functools.reduce(jnp.add, [jnp.sum(p, axis=-1, keepdims=True) for p in by_width.values()])
            out_ref[0, h] = jnp.where(lane_t == t, col * inv[h:h + 1, :], out_ref[0, h])

    for i, p_ref in p_refs.items():
        _, values, _, _, dil = sets[i]
        for h in range(H_SLOT):
            prod = values(h) * p_ref[h:h + 1, :]
            folded = functools.reduce(jnp.add, [prod[:, j:j + LANES] for j in range(0, prod.shape[1], LANES)])
            acc = out_ref[0, h]
            for t in range(t_new):
                col = jnp.sum(jnp.where(lane_k % dil == t, folded, 0.0), axis=-1, keepdims=True)
                acc = acc + jnp.where(lane_t == t, col, 0.0)
            out_ref[0, h] = acc


def _sample_tables(rel_bias, g, window, dil, length, t_new):
    bias = _group_bias(rel_bias, g, window, dil)
    n_keys = window // dil + 1
    tp = SAMPLE_PAD
    k = np.arange(tp + length - 1)
    dist = length + tp - 1 - k
    u = _bias_by_offset(bias, dist // dil, (dist % dil == 0) & (dist // dil < n_keys))
    tc = _toeplitz(u, tp, length)
    k2 = np.arange(2 * tp - 1)
    dist2 = tp - 1 - k2
    u2 = _bias_by_offset(bias, dist2 // dil, (dist2 >= 0) & (dist2 % dil == 0) & (dist2 // dil < n_keys))
    live_col = (np.arange(tp) < t_new)[None, None, :]
    tn = jnp.where(live_col, _toeplitz(u2, tp, tp), NEG)
    return tc.transpose(1, 0, 2)[:t_new], tn.transpose(1, 0, 2)[:t_new]


def _attn_sample(q_t, caches_t, tables_c, tables_n, t_new):
    bsz = q_t.shape[0]
    tp = SAMPLE_PAD
    const = lambda a: pl.BlockSpec(a.shape, lambda b: (0, 0, 0))
    scratch = []
    for c in caches_t:
        scratch += [pltpu.VMEM((H_SLOT, c.shape[-1]), F32), pltpu.VMEM((H_SLOT, tp), F32)]
    for c, (_, dil) in zip(caches_t, DIL_GROUPS):
        if dil >= t_new:
            assert c.shape[-1] % dil == 0 and c.shape[-1] % LANES == 0 and LANES % dil == 0
            scratch.append(pltpu.VMEM((H_SLOT, c.shape[-1]), F32))
    return pl.pallas_call(
        functools.partial(_attn_sample_kernel, t_new=t_new),
        grid=(bsz,),
        in_specs=[pl.BlockSpec((1,) + q_t.shape[1:], lambda b: (b, 0, 0, 0, 0, 0))]
        + [pl.BlockSpec((1,) + c.shape[1:], lambda b: (b, 0, 0, 0, 0)) for c in caches_t]
        + [const(t) for t in tables_c] + [const(t) for t in tables_n],
        out_specs=pl.BlockSpec((1, H_SLOT, HEAD_DIM, tp), lambda b: (b, 0, 0, 0)),
        out_shape=jax.ShapeDtypeStruct((bsz, H_SLOT, HEAD_DIM, tp), F32),
        scratch_shapes=scratch,
        compiler_params=_params(1),
        name="attn_sample",
    )(q_t, *caches_t, *tables_c, *tables_n)


def _mm_ln_kernel(y_ref, w_ref, x_ref, g_ref, b_ref, o_ref):
    acc = jnp.dot(y_ref[...].astype(BF16), w_ref[...], preferred_element_type=F32)
    o_ref[...] = _layer_norm(ALPHA * x_ref[...] + acc, g_ref[...], b_ref[...])


def _mm_ln(y, w, x, g, b, tm):
    m, k = y.shape
    vec = pl.BlockSpec((1, D_MODEL), lambda i: (0, 0))
    return pl.pallas_call(
        _mm_ln_kernel, grid=(m // tm,),
        in_specs=[pl.BlockSpec((tm, k), lambda i: (i, 0)), pl.BlockSpec((k, D_MODEL), lambda i: (0, 0)),
                  pl.BlockSpec((tm, D_MODEL), lambda i: (i, 0)), vec, vec],
        out_specs=pl.BlockSpec((tm, D_MODEL), lambda i: (i, 0)),
        out_shape=jax.ShapeDtypeStruct((m, D_MODEL), F32),
        compiler_params=_params(1), name="outproj_ln",
    )(y, w, x, g, b)


def _mm_ln_pair_kernel(yp_ref, ys_ref, w_ref, xp_ref, xs_ref, g_ref, b_ref, wr_ref, o_ref, route_ref, *, prompt_tiles):
    def run(y_ref, x_ref):
        half = o_ref.shape[0] // 2
        for r0 in (0, half):
            rows = slice(r0, r0 + half)
            acc = jnp.dot(y_ref[rows, :].astype(BF16), w_ref[...], preferred_element_type=F32)
            h = _layer_norm(ALPHA * x_ref[rows, :] + acc, g_ref[...], b_ref[...])
            o_ref[rows, :] = h
            route_ref[rows, :] = _top2_route(jnp.dot(h.astype(BF16), wr_ref[...], preferred_element_type=F32))

    pl.when(pl.program_id(0) < prompt_tiles)(lambda: run(yp_ref, xp_ref))
    pl.when(pl.program_id(0) >= prompt_tiles)(lambda: run(ys_ref, xs_ref))


def _mm_ln_pair(y_p, y_s, w, x_p, x_s, g, b, w_router, tm):
    n_p, k = y_p.shape
    n_s = y_s.shape[0]
    pt = n_p // tm
    first = lambda width: pl.BlockSpec((tm, width), lambda i: (jnp.minimum(i, pt - 1), 0))
    second = lambda width: pl.BlockSpec((tm, width), lambda i: (jnp.maximum(i - pt, 0), 0))
    vec = pl.BlockSpec((1, D_MODEL), lambda i: (0, 0))
    wr = jnp.pad(w_router, ((0, 0), (0, LANES - N_EXPERTS))).astype(BF16)
    return pl.pallas_call(
        functools.partial(_mm_ln_pair_kernel, prompt_tiles=pt), grid=((n_p + n_s) // tm,),
        in_specs=[first(k), second(k), pl.BlockSpec((k, D_MODEL), lambda i: (0, 0)),
                  first(D_MODEL), second(D_MODEL), vec, vec, pl.BlockSpec((D_MODEL, LANES), lambda i: (0, 0))],
        out_specs=[pl.BlockSpec((tm, D_MODEL), lambda i: (i, 0)), pl.BlockSpec((tm, LANES), lambda i: (i, 0))],
        out_shape=[jax.ShapeDtypeStruct((n_p + n_s, D_MODEL), F32),
                   jax.ShapeDtypeStruct((n_p + n_s, LANES), F32)],
        compiler_params=_params(1), name="outproj_ln_route",
    )(y_p, y_s, w, x_p, x_s, g, b, wr)


def _swiglu_chunks(xb, wg_ref, wu_ref, wd_ref, lead, width, chunk):
    acc = None
    for c0 in range(0, width, chunk):
        cs = slice(c0, min(c0 + chunk, width))
        a = jnp.dot(xb, wg_ref[lead + (slice(None), cs)], preferred_element_type=F32)
        u = jnp.dot(xb, wu_ref[lead + (slice(None), cs)], preferred_element_type=F32)
        part = jnp.dot((_silu(a) * u).astype(BF16), wd_ref[lead + (cs, slice(None))], preferred_element_type=F32)
        acc = part if acc is None else acc + part
    return acc


class _CastRider:
    def __init__(self, w, n_steps):
        rows = w.shape[0] * w.shape[1]
        assert rows % n_steps == 0 and (rows // n_steps) % 16 == 0
        self.shape = w.shape
        self.view = w.reshape(n_steps, rows // n_steps, w.shape[2])
        self.spec = pl.BlockSpec((1,) + self.view.shape[1:], lambda i: (i, 0, 0))
        self.out_shape = jax.ShapeDtypeStruct(self.view.shape, BF16)


def _ride(rest):
    if len(rest) == 1:
        return rest[0]
    src_ref, o_ref, dst_ref = rest
    dst_ref[...] = src_ref[...].astype(BF16)
    return o_ref


def _ffn_kernel(x_ref, wg_ref, wu_ref, wd_ref, g_ref, b_ref, *rest, d_ff):
    o_ref = _ride(rest)
    x = x_ref[...]
    acc = _swiglu_chunks(x.astype(BF16), wg_ref, wu_ref, wd_ref, (), d_ff, FF_CHUNK)
    o_ref[...] = _layer_norm(ALPHA * x + acc, g_ref[...], b_ref[...])


def _ffn_dense(x, w_gu, w_down, g, b, tm, ride=None):
    m = x.shape[0]
    d_ff = w_down.shape[0]
    vec = pl.BlockSpec((1, D_MODEL), lambda i: (0, 0))
    rider = _CastRider(ride, m // tm) if ride is not None else None
    out = pl.pallas_call(
        functools.partial(_ffn_kernel, d_ff=d_ff), grid=(m // tm,),
        in_specs=[pl.BlockSpec((tm, D_MODEL), lambda i: (i, 0)),
                  pl.BlockSpec((D_MODEL, d_ff), lambda i: (0, 0)),
                  pl.BlockSpec((D_MODEL, d_ff), lambda i: (0, 1)),
                  pl.BlockSpec((d_ff, D_MODEL), lambda i: (0, 0)), vec, vec] + ([rider.spec] if rider else []),
        out_specs=[pl.BlockSpec((tm, D_MODEL), lambda i: (i, 0))] + ([rider.spec] if rider else []),
        out_shape=[jax.ShapeDtypeStruct((m, D_MODEL), F32)] + ([rider.out_shape] if rider else []),
        compiler_params=_params(1), name="ffn_dense",
    )(x, w_gu, w_gu, w_down, g, b, *([rider.view] if rider else []))
    return (out[0], out[1].reshape(rider.shape)) if rider else out[0]


def _proj_ret_kernel(x_ref, w_ref, cos_ref, sin_ref, *rest, scale_k):
    o_ref = _ride(rest)
    xb = x_ref[...].astype(BF16)
    cos = cos_ref[...]
    sin = sin_ref[...]
    half = RET_DK // 2
    for c in range(w_ref.shape[1] // RET_DK):
        cs = slice(c * RET_DK, (c + 1) * RET_DK)
        acc = jnp.dot(xb, w_ref[:, cs], preferred_element_type=F32)
        if c < 2 * RET_HEADS:
            scale = scale_k if c >= RET_HEADS else 1.0
            x1 = acc[:, :half]
            x2 = acc[:, half:]
            o_ref[:, c * RET_DK:c * RET_DK + half] = ((x1 * cos - x2 * sin) * scale).astype(o_ref.dtype)
            o_ref[:, c * RET_DK + half:(c + 1) * RET_DK] = ((x1 * sin + x2 * cos) * scale).astype(o_ref.dtype)
        else:
            o_ref[:, cs] = acc.astype(o_ref.dtype)


def _proj_ret(x, w, cos, sin, tm, out_dtype, ride=None):
    m = x.shape[0]
    n_cols = w.shape[1]
    pos_tiles = cos.shape[0] // tm
    rider = _CastRider(ride, m // tm) if ride is not None else None
    out = pl.pallas_call(
        functools.partial(_proj_ret_kernel, scale_k=RET_DK ** -0.5),
        grid=(m // tm,),
        in_specs=[pl.BlockSpec((tm, D_MODEL), lambda i: (i, 0)),
                  pl.BlockSpec((D_MODEL, n_cols), lambda i: (0, 0)),
                  pl.BlockSpec((tm, RET_DK // 2), lambda i: (i % pos_tiles, 0)),
                  pl.BlockSpec((tm, RET_DK // 2), lambda i: (i % pos_tiles, 0))] + ([rider.spec] if rider else []),
        out_specs=[pl.BlockSpec((tm, n_cols), lambda i: (i, 0))] + ([rider.spec] if rider else []),
        out_shape=[jax.ShapeDtypeStruct((m, n_cols), out_dtype)] + ([rider.out_shape] if rider else []),
        compiler_params=_params(1), name="proj_ret",
    )(x, w, cos, sin, *([rider.view] if rider else []))
    return (out[0], out[1].reshape(rider.shape)) if rider else out[0]


def _rope_tables(pos):
    half = RET_DK // 2
    inv = 1.0 / (ROPE_BASE ** (jnp.arange(half, dtype=F32) / half))
    ang = pos.astype(F32)[:, None] * inv[None]
    return jnp.cos(ang), jnp.sin(ang)


def _log_gamma():
    return jnp.log(1.0 - 2.0 ** (-5.0 - jnp.arange(RET_HEADS, dtype=F32)))


def _decay_tables(c, rows):
    lg = _log_gamma()
    n = jnp.arange(rows, dtype=F32)
    live = n < c
    diff = n[:, None] - n[None, :]
    decay = jnp.where((diff >= 0)[None] & live[None, None, :],
                      jnp.exp(jnp.maximum(diff, 0.0)[None] * lg[:, None, None]), 0.0)
    q_decay = jnp.exp((n[None, :] + 1.0) * lg[:, None])
    k_decay = jnp.where(live[None], jnp.exp((c - 1.0 - n)[None, :] * lg[:, None]), 0.0)
    chunk_decay = jnp.exp(c * lg)
    return decay, q_decay, k_decay, chunk_decay


def _group_norm_gate(o, gate, gn):
    mu = jnp.mean(o, axis=-1, keepdims=True)
    oc = o - mu
    var = jnp.mean(oc * oc, axis=-1, keepdims=True)
    return _silu(gate) * (oc * lax.rsqrt(var + GN_EPS) * gn)


def _ret_prompt_kernel(q_ref, k_ref, v_ref, gate_ref, dec_ref, qd_ref, kd_ref, cd_ref, gn_ref,
                       y_ref, st_ref, s_ref, *, tb):
    cb = pl.program_id(2)

    @pl.when(cb == 0)
    def _():
        s_ref[...] = jnp.zeros_like(s_ref)

    def body(ci, carry):
        r0 = pl.multiple_of(ci * RET_CHUNK, RET_CHUNK)
        rows = pl.ds(r0, RET_CHUNK)
        for j in range(RET_HPS):
            kc = slice(j * RET_DK, (j + 1) * RET_DK)
            vc = slice(j * RET_DV, (j + 1) * RET_DV)
            q = q_ref[0, rows, kc]
            k = k_ref[0, rows, kc]
            v = v_ref[0, rows, vc]
            state = s_ref[j]
            scores = lax.dot_general(q, k, _NT, preferred_element_type=F32) * dec_ref[j]
            inner = jnp.dot(scores.astype(BF16), v, preferred_element_type=F32)
            cross = jnp.dot(q, state.astype(BF16), preferred_element_type=F32) * qd_ref[j]
            kd = (k.astype(F32) * kd_ref[j]).astype(BF16)
            s_ref[j] = cd_ref[j, 0:1, :] * state + lax.dot_general(kd, v, _TN, preferred_element_type=F32)
            y = _group_norm_gate(inner + cross, gate_ref[0, rows, vc].astype(F32), gn_ref[:, vc])
            y_ref[0, rows, vc] = y.astype(y_ref.dtype)
        return carry

    lax.fori_loop(0, tb // RET_CHUNK, body, 0)

    @pl.when(cb == pl.num_programs(2) - 1)
    def _():
        st_ref[0] = s_ref[...]


def _ret_prompt(proj, gn, bsz, seq, tb):
    decay, q_decay, k_decay, chunk_decay = _decay_tables(RET_CHUNK, RET_CHUNK)
    qd = jnp.broadcast_to(q_decay[:, :, None], (RET_HEADS, RET_CHUNK, RET_DV))
    kd = jnp.broadcast_to(k_decay[:, :, None], (RET_HEADS, RET_CHUNK, RET_DK))
    cd = jnp.broadcast_to(chunk_decay[:, None, None], (RET_HEADS, 8, RET_DV))
    proj = proj.reshape(bsz, seq, proj.shape[-1])
    hps = RET_HPS
    k_off = RET_QK // (hps * RET_DK)
    v_off = 2 * RET_QK // (hps * RET_DV)
    g_off = v_off + RET_HEADS // hps
    per_head = lambda shape: pl.BlockSpec((hps,) + shape, lambda b, h, c: (h, 0, 0))
    y, state = pl.pallas_call(
        functools.partial(_ret_prompt_kernel, tb=tb),
        grid=(bsz, RET_HEADS // hps, seq // tb),
        in_specs=[pl.BlockSpec((1, tb, hps * RET_DK), lambda b, h, c: (b, c, h)),
                  pl.BlockSpec((1, tb, hps * RET_DK), lambda b, h, c: (b, c, k_off + h)),
                  pl.BlockSpec((1, tb, hps * RET_DV), lambda b, h, c: (b, c, v_off + h)),
                  pl.BlockSpec((1, tb, hps * RET_DV), lambda b, h, c: (b, c, g_off + h)),
                  per_head((RET_CHUNK, RET_CHUNK)), per_head((RET_CHUNK, RET_DV)),
                  per_head((RET_CHUNK, RET_DK)), per_head((8, RET_DV)),
                  pl.BlockSpec((1, hps * RET_DV), lambda b, h, c: (0, h))],
        out_specs=[pl.BlockSpec((1, tb, hps * RET_DV), lambda b, h, c: (b, c, h)),
                   pl.BlockSpec((1, hps, RET_DK, RET_DV), lambda b, h, c: (b, h, 0, 0))],
        out_shape=[jax.ShapeDtypeStruct((bsz, seq, RET_V), BF16),
                   jax.ShapeDtypeStruct((bsz, RET_HEADS, RET_DK, RET_DV), F32)],
        scratch_shapes=[pltpu.VMEM((hps, RET_DK, RET_DV), F32)],
        compiler_params=_params(3), name="retention_prompt",
    )(proj, proj, proj, proj, decay, qd, kd, cd, gn)
    return y.reshape(bsz * seq, RET_V), state


def _ret_sample_kernel(p_ref, st_ref, dec_ref, qd_ref, kd_ref, cd_ref, gn_ref, y_ref, ns_ref):
    tp = SAMPLE_PAD
    proj = p_ref[0]
    for h in range(RET_HEADS):
        q = proj[:, h * RET_DK:(h + 1) * RET_DK].astype(BF16)
        k = proj[:, RET_QK + h * RET_DK:RET_QK + (h + 1) * RET_DK]
        v = proj[:, 2 * RET_QK + h * RET_DV:2 * RET_QK + (h + 1) * RET_DV]
        gate = proj[:, 2 * RET_QK + RET_V + h * RET_DV:2 * RET_QK + RET_V + (h + 1) * RET_DV]
        zk = jnp.zeros((LANES - tp, RET_DK), F32)
        zv = jnp.zeros((LANES - tp, RET_DV), F32)
        kp = jnp.concatenate([k, zk], axis=0).astype(BF16)
        kdp = jnp.concatenate([k * kd_ref[h], zk], axis=0).astype(BF16)
        vp = jnp.concatenate([v, zv], axis=0).astype(BF16)
        state = st_ref[0, h]
        scores = lax.dot_general(q, kp, _NT, preferred_element_type=F32) * dec_ref[h]
        inner = jnp.dot(scores.astype(BF16), vp, preferred_element_type=F32)
        cross = jnp.dot(q, state.astype(BF16), preferred_element_type=F32) * qd_ref[h]
        ns_ref[0, h] = cd_ref[h, 0:1, :] * state + lax.dot_general(kdp, vp, _TN, preferred_element_type=F32)
        y = _group_norm_gate(inner + cross, gate, gn_ref[:, h * RET_DV:(h + 1) * RET_DV])
        y_ref[0, :, h * RET_DV:(h + 1) * RET_DV] = y


def _ret_sample(proj, state, gn, t_new):
    bsz = proj.shape[0]
    tp = SAMPLE_PAD
    decay, q_decay, k_decay, chunk_decay = _decay_tables(t_new, tp)
    dec = jnp.pad(decay, ((0, 0), (0, 0), (0, LANES - tp)))
    qd = jnp.broadcast_to(q_decay[:, :, None], (RET_HEADS, tp, RET_DV))
    kd = jnp.broadcast_to(k_decay[:, :, None], (RET_HEADS, tp, RET_DK))
    cd = jnp.broadcast_to(chunk_decay[:, None, None], (RET_HEADS, 8, RET_DV))
    const = lambda a: pl.BlockSpec(a.shape, lambda b: (0,) * a.ndim)
    return pl.pallas_call(
        _ret_sample_kernel, grid=(bsz,),
        in_specs=[pl.BlockSpec((1,) + proj.shape[1:], lambda b: (b, 0, 0)),
                  pl.BlockSpec((1,) + state.shape[1:], lambda b: (b, 0, 0, 0)),
                  const(dec), const(qd), const(kd), const(cd), const(gn)],
        out_specs=[pl.BlockSpec((1, tp, RET_V), lambda b: (b, 0, 0)),
                   pl.BlockSpec((1,) + state.shape[1:], lambda b: (b, 0, 0, 0))],
        out_shape=[jax.ShapeDtypeStruct((bsz, tp, RET_V), F32),
                   jax.ShapeDtypeStruct(state.shape, F32)],
        compiler_params=_params(1), name="retention_sample",
    )(proj, state, dec, qd, kd, cd, gn)


def _top2_route(logits):
    lane = lax.broadcasted_iota(jnp.int32, logits.shape, 1)
    logits = jnp.where(lane < N_EXPERTS, logits, NEG)
    m1 = jnp.max(logits, axis=-1, keepdims=True)
    i1 = jnp.min(jnp.where(logits == m1, lane, LANES), axis=-1, keepdims=True)
    rest = jnp.where(lane == i1, NEG, logits)
    m2 = jnp.max(rest, axis=-1, keepdims=True)
    i2 = jnp.min(jnp.where(rest == m2, lane, LANES), axis=-1, keepdims=True)
    e2 = jnp.exp(m2 - m1)
    den = 1.0 + e2
    return jnp.where(lane == 0, 1.0 / den,
                     jnp.where(lane == 1, e2 / den,
                               jnp.where(lane == 2, i1.astype(F32),
                                         jnp.where(lane == 3, i2.astype(F32), 0.0))))


def _route_plan(e1, e2, tile):
    n_tok = e1.shape[0]
    e = jnp.concatenate([e1, e2])
    onehot = (e[:, None] == jnp.arange(N_EXPERTS, dtype=jnp.int32)[None]).astype(jnp.int32)
    csum = jnp.cumsum(onehot, axis=0)
    rank = jnp.take_along_axis(csum, e[:, None], axis=1)[:, 0] - 1
    tiles_per_expert = (csum[-1] + tile - 1) // tile
    tile_end = jnp.cumsum(tiles_per_expert)
    pos = ((tile_end - tiles_per_expert) * tile)[e] + rank
    n_tiles = (2 * n_tok) // tile + N_EXPERTS
    n_used = tile_end[-1]
    tile_ids = jnp.arange(n_tiles, dtype=jnp.int32)
    tile_expert = jnp.minimum(jnp.searchsorted(tile_end, tile_ids, side="right"), N_EXPERTS - 1)
    tile_expert = jnp.where(tile_ids < n_used, tile_expert, tile_expert[n_used - 1])
    meta = jnp.concatenate([tile_expert, n_used[None], tile_end]).astype(jnp.int32)
    return meta, pos[:n_tok], pos[n_tok:], n_tiles


SUBLANES = 8


def _start_row_gather(idx_ref, src_hbm, dst3, sem):
    def issue(blk, carry):
        for s in range(SUBLANES):
            pltpu.make_async_copy(src_hbm.at[pl.ds(idx_ref[0, 0, blk * SUBLANES + s], 1), :],
                                  dst3.at[blk, pl.ds(s, 1), :], sem).start(priority=s % 2)
        return carry

    lax.fori_loop(0, dst3.shape[0], issue, 0)


def _start_row_scatter(idx_ref, src3, dst_hbm, sem):
    def issue(blk, carry):
        for s in range(SUBLANES):
            pltpu.make_async_copy(src3.at[blk, pl.ds(s, 1), :],
                                  dst_hbm.at[pl.ds(idx_ref[0, 0, blk * SUBLANES + s], 1), :], sem).start(priority=s % 2)
        return carry

    lax.fori_loop(0, src3.shape[0], issue, 0)


def _wait_rows(hbm, vmem3, sem):
    rows = vmem3.shape[0] * SUBLANES
    pltpu.make_async_copy(hbm.at[pl.ds(0, rows), :], hbm.at[pl.ds(0, rows), :], sem).wait()


def _dispatch_kernel(meta_ref, p1_ref, p2_ref, x_ref, xs_hbm, xbuf, zbuf, sems, zsem, *, tm, tile, n_tiles, n_steps):
    i = pl.program_id(0)
    slot = i % 2

    def wait_slot(s):
        for _ in range(2):
            _wait_rows(xs_hbm, xbuf.at[s], sems.at[s])

    @pl.when(i == 0)
    def _():
        zbuf[...] = jnp.zeros_like(zbuf)
        zero_tile = lambda t: pltpu.make_async_copy(zbuf, xs_hbm.at[pl.ds(t * tile, tile), :], zsem)
        for e in range(N_EXPERTS):
            zero_tile(jnp.maximum(meta_ref[n_tiles + 1 + e] - 1, 0)).start()
        for e in range(N_EXPERTS):
            zero_tile(0).wait()

        def zero_unused(t, carry):
            zero_tile(t).start()
            zero_tile(t).wait()
            return carry

        lax.fori_loop(meta_ref[n_tiles], n_tiles, zero_unused, 0)

    @pl.when(i >= 2)
    def _():
        wait_slot(slot)

    xbuf[slot] = x_ref[...]
    _start_row_scatter(p1_ref, xbuf.at[slot], xs_hbm, sems.at[slot])
    _start_row_scatter(p2_ref, xbuf.at[slot], xs_hbm, sems.at[slot])

    @pl.when(i == n_steps - 1)
    def _():
        wait_slot(slot)
        if n_steps > 1:
            wait_slot(1 - slot)


def _dispatch(meta, pos1, pos2, x, tm, tile, n_tiles):
    m = x.shape[0]
    idx = lambda: pl.BlockSpec((1, 1, tm), lambda i, meta: (i, 0, 0), memory_space=pltpu.SMEM)
    return pl.pallas_call(
        functools.partial(_dispatch_kernel, tm=tm, tile=tile, n_tiles=n_tiles, n_steps=m // tm),
        grid_spec=pltpu.PrefetchScalarGridSpec(
            num_scalar_prefetch=1, grid=(m // tm,),
            in_specs=[idx(), idx(),
                      pl.BlockSpec((tm // SUBLANES, SUBLANES, D_MODEL), lambda i, meta: (i, 0, 0))],
            out_specs=pl.BlockSpec(memory_space=pl.ANY),
            scratch_shapes=[pltpu.VMEM((2, tm // SUBLANES, SUBLANES, D_MODEL), F32),
                            pltpu.VMEM((tile, D_MODEL), F32),
                            pltpu.SemaphoreType.DMA((2,)), pltpu.SemaphoreType.DMA(())]),
        out_shape=jax.ShapeDtypeStruct((n_tiles * tile, D_MODEL), F32),
        compiler_params=_params(1), name="moe_dispatch",
    )(meta, pos1.reshape(m // tm, 1, tm), pos2.reshape(m // tm, 1, tm),
      x.reshape(m // SUBLANES, SUBLANES, D_MODEL))


def _expert_kernel(meta_ref, xs_ref, wg_ref, wu_ref, wd_ref, o_ref, xb_ref, *, n_tiles):
    f = pl.program_id(1)
    used = pl.program_id(0) < meta_ref[n_tiles]

    @pl.when(jnp.logical_and(jnp.logical_not(used), f == 0))
    def _():
        o_ref[...] = jnp.zeros_like(o_ref)

    @pl.when(jnp.logical_and(used, f == 0))
    def _():
        xb_ref[...] = xs_ref[...].astype(BF16)

    @pl.when(used)
    def _():
        part = _swiglu_chunks(xb_ref[...], wg_ref, wu_ref, wd_ref, (0,), wd_ref.shape[1], FF_CHUNK)

        @pl.when(f == 0)
        def _():
            o_ref[...] = part

        @pl.when(f > 0)
        def _():
            o_ref[...] += part


def _experts(meta, xs, w_gu, w_down, tile, n_tiles, tf):
    d_ff = w_down.shape[1]
    nf = d_ff // tf
    fcol = lambda j, f, meta: jnp.where(j < meta[n_tiles], f, nf - 1)
    return pl.pallas_call(
        functools.partial(_expert_kernel, n_tiles=n_tiles),
        grid_spec=pltpu.PrefetchScalarGridSpec(
            num_scalar_prefetch=1, grid=(n_tiles, nf),
            in_specs=[pl.BlockSpec((tile, D_MODEL), lambda j, f, meta: (j, 0)),
                      pl.BlockSpec((1, D_MODEL, tf), lambda j, f, meta: (meta[j], 0, fcol(j, f, meta))),
                      pl.BlockSpec((1, D_MODEL, tf), lambda j, f, meta: (meta[j], 0, nf + fcol(j, f, meta))),
                      pl.BlockSpec((1, tf, D_MODEL), lambda j, f, meta: (meta[j], fcol(j, f, meta), 0))],
            out_specs=pl.BlockSpec((tile, D_MODEL), lambda j, f, meta: (j, 0)),
            scratch_shapes=[pltpu.VMEM((tile, D_MODEL), BF16)]),
        out_shape=jax.ShapeDtypeStruct((n_tiles * tile, D_MODEL), F32),
        compiler_params=_params(2), name="moe_experts",
    )(meta, xs, w_gu, w_gu, w_down)


def _combine_kernel(p1_ref, p2_ref, n1_ref, n2_ref, rows_hbm, x_ref, r_ref, g_ref, b_ref, op_ref, os_ref,
                    buf1, buf2, sem1, sem2, *, tm, prompt_tiles):
    i = pl.program_id(0)
    slot = i % 2

    def start(pa_ref, pb_ref, s):
        _start_row_gather(pa_ref, rows_hbm, buf1.at[s], sem1.at[s])
        _start_row_gather(pb_ref, rows_hbm, buf2.at[s], sem2.at[s])

    @pl.when(i == 0)
    def _():
        start(p1_ref, p2_ref, 0)

    @pl.when(i + 1 < pl.num_programs(0))
    def _():
        start(n1_ref, n2_ref, 1 - slot)

    _wait_rows(rows_hbm, buf1.at[slot], sem1.at[slot])
    _wait_rows(rows_hbm, buf2.at[slot], sem2.at[slot])
    route = r_ref[...]
    y = (route[:, 0:1] * buf1[slot].reshape(tm, D_MODEL) + route[:, 1:2] * buf2[slot].reshape(tm, D_MODEL))
    res = _layer_norm(ALPHA * x_ref[...] + y, g_ref[...], b_ref[...])

    @pl.when(i < prompt_tiles)
    def _():
        op_ref[...] = res

    @pl.when(i >= prompt_tiles)
    def _():
        os_ref[...] = res


def _combine(pos1, pos2, rows, x, route, g, b, tm, n_prompt):
    m = x.shape[0]
    pt = n_prompt // tm
    nt = m // tm
    idx = lambda: pl.BlockSpec((1, 1, tm), lambda i: (i, 0, 0), memory_space=pltpu.SMEM)
    nxt = lambda: pl.BlockSpec((1, 1, tm), lambda i: (jnp.minimum(i + 1, nt - 1), 0, 0), memory_space=pltpu.SMEM)
    vec = pl.BlockSpec((1, D_MODEL), lambda i: (0, 0))
    p1 = pos1.reshape(nt, 1, tm)
    p2 = pos2.reshape(nt, 1, tm)
    return pl.pallas_call(
        functools.partial(_combine_kernel, tm=tm, prompt_tiles=pt), grid=(nt,),
        in_specs=[idx(), idx(), nxt(), nxt(), pl.BlockSpec(memory_space=pl.ANY),
                  pl.BlockSpec((tm, D_MODEL), lambda i: (i, 0)),
                  pl.BlockSpec((tm, LANES), lambda i: (i, 0)), vec, vec],
        out_specs=[pl.BlockSpec((tm, D_MODEL), lambda i: (jnp.minimum(i, pt - 1), 0)),
                   pl.BlockSpec((tm, D_MODEL), lambda i: (jnp.maximum(i - pt, 0), 0))],
        out_shape=[jax.ShapeDtypeStruct((n_prompt, D_MODEL), F32),
                   jax.ShapeDtypeStruct((m - n_prompt, D_MODEL), F32)],
        scratch_shapes=[pltpu.VMEM((2, tm // SUBLANES, SUBLANES, D_MODEL), F32),
                        pltpu.VMEM((2, tm // SUBLANES, SUBLANES, D_MODEL), F32),
                        pltpu.SemaphoreType.DMA((2,)), pltpu.SemaphoreType.DMA((2,))],
        compiler_params=_params(1), name="moe_combine",
    )(p1, p2, p1, p2, rows, x, route, g, b)


def _moe(x, route, n_prompt, w_gu, w_down, g, b, tm, tile, tf):
    e1 = route[:, 2].astype(jnp.int32)
    e2 = route[:, 3].astype(jnp.int32)
    meta, pos1, pos2, n_tiles = _route_plan(e1, e2, tile)
    xs = _dispatch(meta, pos1, pos2, x, tm, tile, n_tiles)
    rows = _experts(meta, xs, w_gu, w_down, tile, n_tiles, tf)
    return _combine(pos1, pos2, rows, x, route, g, b, tm, n_prompt)


def kernel(x_prompt, x_sample, cache_kv_w128, cache_kv_w512, cache_kv_w2048, state_ret,
           ln_g, ln_b, rel_bias, w_in_dil, w_out_dil, w_in_ret, ret_gn_g, w_out_ret,
           w_gu_dense, w_down_dense, w_router, w_gu_moe, w_down_moe):
    bsz, seq, _ = x_prompt.shape
    dbsz, t_new, _ = x_sample.shape
    tp = SAMPLE_PAD
    n_p = bsz * seq
    n_s = dbsz * tp
    caches = (cache_kv_w128, cache_kv_w512, cache_kv_w2048)

    w_in_dil_b = w_in_dil.astype(BF16)
    w_out_dil_b = w_out_dil.astype(BF16)
    w_in_ret_b = w_in_ret.astype(BF16)
    w_out_ret_b = w_out_ret.astype(BF16)
    w_gu_dense_b = w_gu_dense.astype(BF16)
    w_down_dense_b = w_down_dense.astype(BF16)
    lng = ln_g.reshape(DEPTH, 2, 1, D_MODEL)
    lnb = ln_b.reshape(DEPTH, 2, 1, D_MODEL)
    gn = ret_gn_g.reshape(1, RET_V)

    hp = x_prompt.reshape(n_p, D_MODEL)
    hs = jnp.pad(x_sample, ((0, 0), (0, tp - t_new), (0, 0))).reshape(n_s, D_MODEL)

    tm = min(1024, seq)
    qkv_groups, (kv128_p, kv512_p, kv2048_p) = _proj_dil_prompt(hp, w_in_dil_b, bsz, seq, tm)
    outs, lses = [], []
    for g, (window, dil) in enumerate(DIL_GROUPS):
        qkv_g = qkv_groups[g].reshape(bsz * dil, seq // dil, G_COLS)
        tbl = _prompt_table(rel_bias, g, window, dil)
        o, l = _attn_prompt(qkv_g, tbl, min(512, seq // dil))
        outs.append(o.reshape(bsz, dil, seq // dil, A_WIDTH))
        lses.append(l.reshape(bsz, dil, seq // dil, A_WIDTH))
    hp = _merge_out(outs, lses, w_out_dil_b, hp, lng[0, 0], lnb[0, 0], 512)

    qkv_s = _matmul(hs, w_in_dil_b, G_COLS)
    qkv_s3 = qkv_s.reshape(dbsz, tp, N_GROUPS * G_COLS)
    tabs = [_sample_tables(rel_bias, g, window, dil, caches[g].shape[1], t_new)
            for g, (window, dil) in enumerate(DIL_GROUPS)]
    q_t = qkv_s.reshape(dbsz, tp, N_GROUPS, 3, H_SLOT, HEAD_DIM).transpose(0, 2, 3, 4, 5, 1)
    caches_t = [c.transpose(0, 2, 3, 4, 1) for c in caches]
    mixed_t = _attn_sample(q_t, caches_t, [t[0] for t in tabs], [t[1] for t in tabs], t_new)
    mixed_s = mixed_t.transpose(0, 3, 1, 2)
    hs = _mm_ln(mixed_s.reshape(n_s, A_WIDTH), w_out_dil_b, hs, lng[0, 0], lnb[0, 0], n_s)
    rows_s = []
    for g in range(N_GROUPS):
        kv = qkv_s3[:, :t_new, g * G_COLS + A_WIDTH:(g + 1) * G_COLS]
        rows_s.append(kv.reshape(dbsz, t_new, 2, H_SLOT, HEAD_DIM))

    hp, w_down_moe_b = _ffn_dense(hp, w_gu_dense_b, w_down_dense_b, lng[0, 1], lnb[0, 1], 512, ride=w_down_moe)
    hs = _ffn_dense(hs, w_gu_dense_b, w_down_dense_b, lng[0, 1], lnb[0, 1], n_s)

    cos_p, sin_p = _rope_tables(jnp.arange(seq, dtype=jnp.int32))
    pos_s = jnp.tile(PAST_LEN + jnp.arange(tp, dtype=jnp.int32), dbsz)
    cos_s, sin_s = _rope_tables(pos_s)
    proj_p, w_gu_moe_b = _proj_ret(hp, w_in_ret_b, cos_p, sin_p, 512, BF16, ride=w_gu_moe)
    y_p, ret_p = _ret_prompt(proj_p, gn, bsz, seq, min(1024, seq))
    proj_s = _proj_ret(hs, w_in_ret_b, cos_s, sin_s, n_s // 2, F32)
    y_s, ret_s = _ret_sample(proj_s.reshape(dbsz, tp, -1), state_ret, gn, t_new)
    h_all, route = _mm_ln_pair(y_p, y_s.reshape(n_s, RET_V), w_out_ret_b, hp, hs, lng[1, 0], lnb[1, 0],
                               w_router, 512)

    out_p, out_s = _moe(h_all, route, n_p, w_gu_moe_b, w_down_moe_b, lng[1, 1], lnb[1, 1], 512, 1024,
                        w_down_moe.shape[1] // 2)

    y_prompt = out_p.reshape(bsz, seq, D_MODEL)
    y_sample = out_s.reshape(dbsz, tp, D_MODEL)[:, :t_new]
    shape5 = lambda a: a.reshape(a.shape[0], a.shape[1], 2, H_SLOT, HEAD_DIM)
    return (y_prompt, y_sample, shape5(kv128_p), shape5(kv512_p), shape5(kv2048_p), ret_p,
            rows_s[0], rows_s[1], rows_s[2], ret_s)
```
